```python
import math
import functools
import jax
import jax.numpy as jnp
from jax import lax
import numpy as np

D_MODEL = 1024
BATCH = 4
SEQ = 4096
DEPTH = 2

CTX_LEN = 256
GRID_W = 64
EPS = 1e-6
N_DIR = 2
CHUNK = 64

GDN_HEADS = D_MODEL // 256
GDN_DK = 128
GDN_DV = 128
CONV_W = 3

MLSTM_HEADS = D_MODEL // 256
MLSTM_DK = 128
MLSTM_DV = 128

RET_HEADS = D_MODEL // 128
RET_DK = 128
RET_DV = 256
ROPE_BASE = 10000.0

N_EXPERTS = 32
TOP_K = 4
D_FF = D_MODEL
SWIGLU_ALPHA = 1.702
SWIGLU_LIMIT = 7.0
MOE_BLOCK = 256

GDN_QK_W = GDN_HEADS * GDN_DK
GDN_V_W = GDN_HEADS * GDN_DV
GDN_CONV_CH = 2 * GDN_QK_W + GDN_V_W
ML_QK_W = MLSTM_HEADS * MLSTM_DK
ML_V_W = MLSTM_HEADS * MLSTM_DV
EVEN_SPLITS = (GDN_CONV_CH, GDN_V_W, N_DIR * GDN_HEADS, N_DIR * GDN_HEADS,
               ML_QK_W, ML_QK_W, ML_V_W, ML_V_W, N_DIR * MLSTM_HEADS, N_DIR * MLSTM_HEADS)
EVEN_IN = sum(EVEN_SPLITS)
EVEN_OUT = GDN_V_W + ML_V_W
RET_QK_W = RET_HEADS * RET_DK
RET_V_W = RET_HEADS * RET_DV
ODD_SPLITS = (RET_QK_W, RET_QK_W, RET_V_W, RET_V_W)
ODD_IN = sum(ODD_SPLITS)
N_EVEN = (DEPTH + 1) // 2
N_ODD = DEPTH // 2

kernel_name = 'hybrid_gdn_mlstm_retention_moe_dit'


def rms_norm(x, g):
    xf = x.astype(jnp.float32)
    y = xf * lax.rsqrt(jnp.mean(xf * xf, axis=-1, keepdims=True) + EPS)
    return (y * g.astype(jnp.float32)).astype(x.dtype)


def modulate(h, shift, scale):
    return h * (1.0 + scale) + shift


def ada_modulation(cond, w_mod, b_mod):
    m = jax.nn.silu(cond) @ w_mod + b_mod
    return jnp.split(m[:, None, :], 6, axis=-1)


def split_cols(t, sizes):
    bounds = []
    acc = 0
    for s_ in sizes[:-1]:
        acc += s_
        bounds.append(acc)
    return jnp.split(t, bounds, axis=-1)


def to_heads(t, heads):
    b, n, _ = t.shape
    return jnp.swapaxes(t.reshape(b, n, heads, -1), 1, 2).astype(jnp.float32)


def dir_gates(t, heads):
    b, n, _ = t.shape
    return jnp.transpose(t.reshape(b, n, N_DIR, heads), (2, 0, 3, 1)).astype(jnp.float32)


def l2_normalize(t):
    return t * lax.rsqrt(jnp.sum(t * t, axis=-1, keepdims=True) + EPS)


def headwise_norm(o, gain, center=False):
    o = jnp.swapaxes(o, 1, 2)
    if center:
        o = o - jnp.mean(o, axis=-1, keepdims=True)
    o = o * lax.rsqrt(jnp.mean(o * o, axis=-1, keepdims=True) + EPS) * gain.astype(jnp.float32)
    return o.reshape(o.shape[0], o.shape[1], -1)


def centred_short_conv(u, w, on_grid):
    b, t, ch = u.shape
    if on_grid:
        rows = t // GRID_W
        u = u.reshape(b, rows, GRID_W, ch)
    n = u.shape[-2]
    pad = CONV_W // 2
    up = jnp.pad(u, [(0, 0)] * (u.ndim - 2) + [(pad, pad), (0, 0)])
    y = up[..., 0:n, :] * w[0]
    for j in range(1, CONV_W):
        y = y + up[..., j:j + n, :] * w[j]
    return jax.nn.silu(y).reshape(b, t, ch)


def rope(t, pos):
    half = t.shape[-1] // 2
    freqs = ROPE_BASE ** (-jnp.arange(half, dtype=jnp.float32) / half)
    ang = pos[:, None] * freqs[None, :]
    cos, sin = jnp.cos(ang), jnp.sin(ang)
    t1, t2 = t[..., :half], t[..., half:]
    return jnp.concatenate([t1 * cos - t2 * sin, t1 * sin + t2 * cos], axis=-1)


def to_chunks(t):
    b, h, n = t.shape[:3]
    t = t.reshape(b, h, n // CHUNK, CHUNK, *t.shape[3:])
    return jnp.moveaxis(t, 2, 0)


def from_chunks(o):
    o = jnp.moveaxis(o, 0, 2)
    return o.reshape(o.shape[0], o.shape[1], -1, o.shape[-1])


def gdn_chunk_scan(q, k, v, log_alpha, beta, state):
    q, k, v, log_alpha, beta = (to_chunks(t) for t in (q, k, v, log_alpha, beta))
    dv = v.shape[-1]
    g = jnp.cumsum(log_alpha, axis=-1)
    idx = jnp.arange(CHUNK)
    incl = idx[:, None] >= idx[None, :]
    strict = idx[:, None] > idx[None, :]
    decay = jnp.exp(jnp.where(incl, g[..., :, None] - g[..., None, :], -jnp.inf))
    k_beta = k * beta[..., None]
    lower = jnp.where(strict, jnp.einsum('nbhcd,nbhsd->nbhcs', k_beta, k) * decay, 0.0)
    rhs = jnp.concatenate([v * beta[..., None], k_beta * jnp.exp(g)[..., None]], axis=-1)
    sol = lax.linalg.triangular_solve(lower, rhs, left_side=True, lower=True, unit_diagonal=True)
    u, w = sol[..., :dv], sol[..., dv:]
    qk = jnp.where(incl, jnp.einsum('nbhcd,nbhsd->nbhcs', q, k) * decay, 0.0)
    q_dec = q * jnp.exp(g)[..., None]
    k_dec = k * jnp.exp(g[..., -1:] - g)[..., None]
    chunk_decay = jnp.exp(g[..., -1])

    def step(s, xs):
        u_c, w_c, qk_c, qd_c, kd_c, cd_c = xs
        v_new = u_c - jnp.einsum('bhcd,bhde->bhce', w_c, s)
        o = jnp.einsum('bhcd,bhde->bhce', qd_c, s) + jnp.einsum('bhcs,bhse->bhce', qk_c, v_new)
        s = s * cd_c[..., None, None] + jnp.einsum('bhcd,bhce->bhde', kd_c, v_new)
        return s, o

    s_fin, o = lax.scan(step, state, (u, w, qk, q_dec, k_dec, chunk_decay))
    return from_chunks(o), s_fin


def mlstm_chunk_scan(q, k, v, log_i, log_f, state):
    q, k, v, log_i, log_f = (to_chunks(t) for t in (q, k, v, log_i, log_f))
    idx = jnp.arange(CHUNK)
    incl = idx[:, None] >= idx[None, :]

    def step(carry, xs):
        c, n, m = carry
        qc, kc, vc, ic, fc = xs
        b = jnp.cumsum(fc, axis=-1)
        d_in = jnp.where(incl, b[..., :, None] - b[..., None, :] + ic[..., None, :], -jnp.inf)
        d_carry = b + m[..., None]
        m_t = jnp.maximum(d_carry, jnp.max(d_in, axis=-1))
        s = jnp.einsum('bhcd,bhsd->bhcs', qc, kc) * jnp.exp(d_in - m_t[..., None])
        w_carry = jnp.exp(d_carry - m_t)
        num = w_carry[..., None] * jnp.einsum('bhcd,bhde->bhce', qc, c) + jnp.einsum('bhcs,bhse->bhce', s, vc)
        den = w_carry * jnp.einsum('bhcd,bhd->bhc', qc, n) + jnp.sum(s, axis=-1)
        h = num / jnp.maximum(jnp.abs(den), jnp.exp(-m_t))[..., None]
        d_end = b[..., -1:] - b + ic
        carry_end = b[..., -1] + m
        m_new = jnp.maximum(carry_end, jnp.max(d_end, axis=-1))
        w_end = jnp.exp(d_end - m_new[..., None])
        f_end = jnp.exp(carry_end - m_new)
        c = f_end[..., None, None] * c + jnp.einsum('bhs,bhsd,bhse->bhde', w_end, kc, vc)
        n = f_end[..., None] * n + jnp.einsum('bhs,bhsd->bhd', w_end, kc)
        return (c, n, m_new), h

    state, h = lax.scan(step, state, (q, k, v, log_i, log_f))
    return from_chunks(h), state


def retention_chunk_scan(q, k, v, state, log_gamma):
    q, k, v = (to_chunks(t) for t in (q, k, v))
    pos = jnp.arange(CHUNK, dtype=jnp.float32)
    lg = log_gamma[:, None]
    diff = pos[:, None] - pos[None, :]
    intra = jnp.exp(jnp.where(diff >= 0, diff * lg[..., None], -jnp.inf))
    cross = jnp.exp((pos + 1.0) * lg)
    tail = jnp.exp((CHUNK - 1.0 - pos) * lg)
    chunk_decay = jnp.exp(CHUNK * log_gamma)

    def step(s, xs):
        qc, kc, vc = xs
        scores = jnp.einsum('bhcd,bhsd->bhcs', qc, kc) * intra
        o = jnp.einsum('bhcs,bhse->bhce', scores, vc) + jnp.einsum('bhcd,bhde->bhce', qc, s) * cross[..., None]
        s = chunk_decay[:, None, None] * s + jnp.einsum('bhsd,bhse->bhde', kc * tail[..., None], vc)
        return s, o

    s_fin, o = lax.scan(step, state, (q, k, v))
    return from_chunks(o), s_fin


def retention_log_decay(reverse):
    expo = 5.0 + jnp.arange(RET_HEADS, dtype=jnp.float32)
    if reverse:
        expo = expo[::-1]
    return jnp.log1p(-jnp.exp2(-expo))


def bidirectional_scan(scan_f, scan_b, ctx_f, lat_f, ctx_b, lat_b, state0):
    def run(fn, ctx_seq, lat_seq):
        o_ctx, s_ctx = fn(*ctx_seq, state0)
        o_lat, _ = fn(*lat_seq, s_ctx)
        return o_ctx, o_lat

    def rev(seq):
        return tuple(jnp.flip(t, axis=2) for t in seq)

    cf, lf = run(scan_f, ctx_f, lat_f)
    cb, lb = run(scan_b, rev(ctx_b), rev(lat_b))
    return cf + jnp.flip(cb, axis=2), lf + jnp.flip(lb, axis=2)


def gdn_mlstm_mixer(h_ctx, h_lat, w_in, conv_w, a_log, dt_bias, gdn_g, i_bias, f_bias, ml_g, w_out, last):
    f32 = jnp.float32
    a_rate = jnp.exp(a_log.astype(f32))[:, None, :, None]
    dt_b = dt_bias.astype(f32)[:, None, :, None]
    i_b = i_bias.astype(f32)[:, None, :, None]
    f_b = f_bias.astype(f32)[:, None, :, None]

    def prepare(h, on_grid):
        qkv, z, a_gate, b_gate, mq, mk, mv, mo, i_gate, f_gate = split_cols(h @ w_in, EVEN_SPLITS)
        qkv = centred_short_conv(qkv, conv_w, on_grid)
        aq, ak, av = split_cols(qkv, (GDN_QK_W, GDN_QK_W, GDN_V_W))
        aq = l2_normalize(to_heads(aq, GDN_HEADS)) * GDN_DK ** -0.5
        ak = l2_normalize(to_heads(ak, GDN_HEADS))
        av = to_heads(av, GDN_HEADS)
        log_alpha = -a_rate * jax.nn.softplus(dir_gates(a_gate, GDN_HEADS) + dt_b)
        beta = jax.nn.sigmoid(dir_gates(b_gate, GDN_HEADS))
        mq = to_heads(mq, MLSTM_HEADS)
        mk = to_heads(mk, MLSTM_HEADS) * MLSTM_DK ** -0.5
        mv = to_heads(mv, MLSTM_HEADS)
        log_i = dir_gates(i_gate, MLSTM_HEADS) + i_b
        log_f = jax.nn.log_sigmoid(dir_gates(f_gate, MLSTM_HEADS) + f_b)
        gdn_seq = tuple((aq, ak, av, log_alpha[d], beta[d]) for d in range(N_DIR))
        ml_seq = tuple((mq, mk, mv, log_i[d], log_f[d]) for d in range(N_DIR))
        return gdn_seq, ml_seq, z, mo

    ctx_gdn, ctx_ml, ctx_z, ctx_o = prepare(h_ctx, False)
    lat_gdn, lat_ml, lat_z, lat_o = prepare(h_lat, True)
    b = h_lat.shape[0]
    gdn_s0 = jnp.zeros((b, GDN_HEADS, GDN_DK, GDN_DV), f32)
    ml_s0 = (jnp.zeros((b, MLSTM_HEADS, MLSTM_DK, MLSTM_DV), f32),
             jnp.zeros((b, MLSTM_HEADS, MLSTM_DK), f32),
             jnp.zeros((b, MLSTM_HEADS), f32))
    gdn_c, gdn_l = bidirectional_scan(gdn_chunk_scan, gdn_chunk_scan,
                                      ctx_gdn[0], lat_gdn[0], ctx_gdn[1], lat_gdn[1], gdn_s0)
    ml_c, ml_l = bidirectional_scan(mlstm_chunk_scan, mlstm_chunk_scan,
                                    ctx_ml[0], lat_ml[0], ctx_ml[1], lat_ml[1], ml_s0)

    def merge(o_gdn, o_ml, z, og, dtype):
        a = headwise_norm(o_gdn, gdn_g) * jax.nn.silu(z.astype(f32))
        m = headwise_norm(o_ml, ml_g.reshape(MLSTM_HEADS, MLSTM_DV)) * jax.nn.sigmoid(og.astype(f32))
        return jnp.concatenate([a, m], axis=-1).astype(dtype) @ w_out

    y_lat = merge(gdn_l, ml_l, lat_z, lat_o, h_lat.dtype)
    y_ctx = None if last else merge(gdn_c, ml_c, ctx_z, ctx_o, h_ctx.dtype)
    return y_ctx, y_lat


def retention_mixer(h_ctx, h_lat, pos_ctx, pos_lat, w_in, norm_g, w_out, last):
    def prepare(h, pos):
        q, k, v, g = split_cols(h @ w_in, ODD_SPLITS)
        q = rope(to_heads(q, RET_HEADS), pos)
        k = rope(to_heads(k, RET_HEADS), pos) * RET_DK ** -0.5
        return (q, k, to_heads(v, RET_HEADS)), g

    ctx_seq, ctx_g = prepare(h_ctx, pos_ctx)
    lat_seq, lat_g = prepare(h_lat, pos_lat)
    s0 = jnp.zeros((h_lat.shape[0], RET_HEADS, RET_DK, RET_DV), jnp.float32)
    scan_f = functools.partial(retention_chunk_scan, log_gamma=retention_log_decay(False))
    scan_b = functools.partial(retention_chunk_scan, log_gamma=retention_log_decay(True))
    o_ctx, o_lat = bidirectional_scan(scan_f, scan_b, ctx_seq, lat_seq, ctx_seq, lat_seq, s0)

    def merge(o, g, dtype):
        y = headwise_norm(o, norm_g.reshape(RET_HEADS, RET_DV), center=True) * jax.nn.silu(g.astype(jnp.float32))
        return y.astype(dtype) @ w_out

    y_lat = merge(o_lat, lat_g, h_lat.dtype)
    y_ctx = None if last else merge(o_ctx, ctx_g, h_ctx.dtype)
    return y_ctx, y_lat


def moe_ffn(h, router_w, router_b, w_gu, b_gu, w_dn, b_dn):
    n, d = h.shape
    logits = (h @ router_w + router_b).astype(jnp.float32)
    top_val, top_idx = lax.top_k(logits, TOP_K)
    top_w = jax.nn.softmax(top_val, axis=-1)
    n_assign = n * TOP_K
    n_blocks = -(-n_assign // MOE_BLOCK) + N_EXPERTS
    flat_e = top_idx.reshape(-1)
    order = jnp.argsort(flat_e)
    sorted_e = flat_e[order]
    counts = jnp.bincount(flat_e, length=N_EXPERTS)
    padded = -(-counts // MOE_BLOCK) * MOE_BLOCK
    ends = jnp.cumsum(padded)
    dest = (ends - padded)[sorted_e] + jnp.arange(n_assign) - (jnp.cumsum(counts) - counts)[sorted_e]
    rows = n_blocks * MOE_BLOCK
    row_tok = jnp.zeros((rows,), jnp.int32).at[dest].set((order // TOP_K).astype(jnp.int32))
    row_w = jnp.zeros((rows,), jnp.float32).at[dest].set(top_w.reshape(-1)[order])
    block_e = jnp.minimum(jnp.searchsorted(ends, jnp.arange(n_blocks) * MOE_BLOCK, side='right'), N_EXPERTS - 1)

    def block_step(acc, xs):
        tok, wt, e = xs
        gu = h[tok] @ w_gu[e] + b_gu[e]
        gate = jnp.minimum(gu[:, 0::2], SWIGLU_LIMIT)
        up = jnp.clip(gu[:, 1::2], -SWIGLU_LIMIT, SWIGLU_LIMIT)
        act = (up + 1.0) * gate * jax.nn.sigmoid(SWIGLU_ALPHA * gate)
        y = act @ w_dn[e] + b_dn[e]
        return acc.at[tok].add(y * wt[:, None].astype(y.dtype)), None

    out, _ = lax.scan(block_step, jnp.zeros_like(h),
                      (row_tok.reshape(n_blocks, MOE_BLOCK), row_w.reshape(n_blocks, MOE_BLOCK), block_e))
    return out


def setup_inputs(seed: int = 0) -> dict:
    key = jax.random.key(seed)
    ks = iter(jax.random.split(key, 32))
    f32 = jnp.float32

    def nrm(shape, scale):
        return jax.random.normal(next(ks), shape, f32) * scale

    x = nrm((BATCH, SEQ, D_MODEL), 1.0)
    c = nrm((BATCH, D_MODEL), 1.0)
    ctx = nrm((BATCH, CTX_LEN, D_MODEL), 1.0)
    c_ctx = nrm((D_MODEL,), 1.0)
    mod_w = nrm((DEPTH, D_MODEL, 6 * D_MODEL), 0.5 * D_MODEL ** -0.5)
    mod_b = nrm((DEPTH, 6 * D_MODEL), 0.02)
    norm1_g = 1.0 + nrm((DEPTH, D_MODEL), 0.02)
    norm2_g = 1.0 + nrm((DEPTH, D_MODEL), 0.02)
    ev_w_in = nrm((N_EVEN, D_MODEL, EVEN_IN), D_MODEL ** -0.5)
    ev_conv_w = nrm((N_EVEN, CONV_W, GDN_CONV_CH), CONV_W ** -0.5)
    gdn_a_log = jnp.log(jax.random.uniform(next(ks), (N_EVEN, N_DIR, GDN_HEADS), f32, 1.0, 16.0))
    dt = jnp.exp(jax.random.uniform(next(ks), (N_EVEN, N_DIR, GDN_HEADS), f32, math.log(1e-3), math.log(1e-1)))
    gdn_dt_bias = dt + jnp.log(-jnp.expm1(-dt))
    gdn_norm_g = 1.0 + nrm((N_EVEN, GDN_DV), 0.02)
    ml_i_bias = nrm((N_EVEN, N_DIR, MLSTM_HEADS), 0.1)
    ml_f_bias = jnp.linspace(3.0, 6.0, MLSTM_HEADS, dtype=f32) + nrm((N_EVEN, N_DIR, MLSTM_HEADS), 0.1)
    ml_norm_g = 1.0 + nrm((N_EVEN, ML_V_W), 0.02)
    ev_w_out = nrm((N_EVEN, EVEN_OUT, D_MODEL), EVEN_OUT ** -0.5)
    od_w_in = nrm((N_ODD, D_MODEL, ODD_IN), D_MODEL ** -0.5)
    ret_norm_g = 1.0 + nrm((N_ODD, RET_V_W), 0.02)
    od_w_out = nrm((N_ODD, RET_V_W, D_MODEL), RET_V_W ** -0.5)
    router_w = nrm((DEPTH, D_MODEL, N_EXPERTS), D_MODEL ** -0.5)
    router_b = nrm((DEPTH, N_EXPERTS), 0.01)
    moe_w_gu = nrm((DEPTH, N_EXPERTS, D_MODEL, 2 * D_FF), D_MODEL ** -0.5)
    moe_b_gu = nrm((DEPTH, N_EXPERTS, 2 * D_FF), 0.02)
    moe_w_dn = nrm((DEPTH, N_EXPERTS, D_FF, D_MODEL), D_FF ** -0.5)
    moe_b_dn = nrm((DEPTH, N_EXPERTS, D_MODEL), 0.02)
    final_g = 1.0 + nrm((D_MODEL,), 0.02)
    return {'x': x, 'c': c, 'ctx': ctx, 'c_ctx': c_ctx, 'mod_w': mod_w, 'mod_b': mod_b,
            'norm1_g': norm1_g, 'norm2_g': norm2_g, 'ev_w_in': ev_w_in, 'ev_conv_w': ev_conv_w,
            'gdn_a_log': gdn_a_log, 'gdn_dt_bias': gdn_dt_bias, 'gdn_norm_g': gdn_norm_g,
            'ml_i_bias': ml_i_bias, 'ml_f_bias': ml_f_bias, 'ml_norm_g': ml_norm_g, 'ev_w_out': ev_w_out,
            'od_w_in': od_w_in, 'ret_norm_g': ret_norm_g, 'od_w_out': od_w_out,
            'router_w': router_w, 'router_b': router_b, 'moe_w_gu': moe_w_gu, 'moe_b_gu': moe_b_gu,
            'moe_w_dn': moe_w_dn, 'moe_b_dn': moe_b_dn, 'final_g': final_g}


def reference(x, c, ctx, c_ctx, mod_w, mod_b, norm1_g, norm2_g, ev_w_in, ev_conv_w, gdn_a_log, gdn_dt_bias,
              gdn_norm_g, ml_i_bias, ml_f_bias, ml_norm_g, ev_w_out, od_w_in, ret_norm_g, od_w_out,
              router_w, router_b, moe_w_gu, moe_b_gu, moe_w_dn, moe_b_dn, final_g):
    b, s, d = x.shape
    n_ctx_tok = ctx.shape[1]
    pos_ctx = jnp.arange(n_ctx_tok, dtype=jnp.float32)
    pos_lat = n_ctx_tok + jnp.arange(s, dtype=jnp.float32)
    x_lat, x_ctx = x, ctx
    for layer in range(DEPTH):
        last = layer == DEPTH - 1
        j = layer // 2
        sh1, sc1, g1, sh2, sc2, g2 = ada_modulation(c, mod_w[layer], mod_b[layer])
        csh1, csc1, cg1, csh2, csc2, cg2 = ada_modulation(c_ctx[None, :], mod_w[layer], mod_b[layer])
        h_lat = modulate(rms_norm(x_lat, norm1_g[layer]), sh1, sc1)
        h_ctx = modulate(rms_norm(x_ctx, norm1_g[layer]), csh1, csc1)
        if layer % 2 == 0:
            y_ctx, y_lat = gdn_mlstm_mixer(h_ctx, h_lat, ev_w_in[j], ev_conv_w[j], gdn_a_log[j], gdn_dt_bias[j],
                                           gdn_norm_g[j], ml_i_bias[j], ml_f_bias[j], ml_norm_g[j],
                                           ev_w_out[j], last)
        else:
            y_ctx, y_lat = retention_mixer(h_ctx, h_lat, pos_ctx, pos_lat, od_w_in[j], ret_norm_g[j],
                                           od_w_out[j], last)
        x_lat = x_lat + g1 * y_lat
        h_lat = modulate(rms_norm(x_lat, norm2_g[layer]), sh2, sc2)
        moe_params = (router_w[layer], router_b[layer], moe_w_gu[layer], moe_b_gu[layer],
                      moe_w_dn[layer], moe_b_dn[layer])
        if last:
            x_lat = x_lat + g2 * moe_ffn(h_lat.reshape(-1, d), *moe_params).reshape(b, s, d)
        else:
            x_ctx = x_ctx + cg1 * y_ctx
            h_ctx = modulate(rms_norm(x_ctx, norm2_g[layer]), csh2, csc2)
            tokens = jnp.concatenate([h_ctx.reshape(-1, d), h_lat.reshape(-1, d)], axis=0)
            out = moe_ffn(tokens, *moe_params)
            n_c = b * n_ctx_tok
            x_ctx = x_ctx + cg2 * out[:n_c].reshape(b, n_ctx_tok, d)
            x_lat = x_lat + g2 * out[n_c:].reshape(b, s, d)
    return rms_norm(x_lat, final_g)
```

```python
import functools
import math

import jax
import jax.numpy as jnp
import numpy as np
from jax import lax
from jax.experimental import pallas as pl
from jax.experimental.pallas import tpu as pltpu

F32 = jnp.float32
BF16 = jnp.bfloat16
U32 = jnp.uint32
HIGHEST = lax.Precision.HIGHEST

EPS = 1e-6
CH = 64
TM = 256
CPB = TM // CH
HD = 128
N_DIR = 2
GDN_H = 4
ML_H = 4
RET_H = 8
RET_DV = 256
CONV_W = 3
N_EXPERTS = 32
TOP_K = 4
SWIGLU_ALPHA = 1.702
SWIGLU_LIMIT = 7.0
MOE_BLOCK = 256
ROPE_BASE = 10000.0
NEG = -1e30
VMEM_LIMIT = 56 * 1024 * 1024


def _cparams(sem):
    return pltpu.CompilerParams(dimension_semantics=sem, vmem_limit_bytes=VMEM_LIMIT)


def _dot(a, b, precision=None):
    return jnp.dot(a, b, preferred_element_type=F32, precision=precision)


def _dot_nt(a, b):
    return lax.dot_general(a, b, (((1,), (1,)), ((), ())), preferred_element_type=F32)


def _dot_tn(a, b):
    return lax.dot_general(a, b, (((0,), (0,)), ((), ())), preferred_element_type=F32)


def _sigmoid(x):
    return 1.0 / (1.0 + jnp.exp(-x))


def _silu(x):
    return x * _sigmoid(x)


def _group_sum(x, ones_blk):
    w = ones_blk.shape[0]
    hi = x.astype(BF16)
    lo = (x - hi.astype(F32)).astype(BF16)
    outs = []
    for j in range(x.shape[1] // w):
        sl = slice(j * w, (j + 1) * w)
        outs.append(_dot(hi[:, sl], ones_blk) + _dot(lo[:, sl], ones_blk))
    return outs[0] if len(outs) == 1 else jnp.concatenate(outs, axis=1)


def _rms_mod(x, g, sh, sc):
    ms = jnp.mean(x * x, axis=-1, keepdims=True)
    return (x * lax.rsqrt(ms + EPS) * g) * (1.0 + sc) + sh


def _adaln_kernel(c_ref, w_ref, b_ref, o_ref):
    c = c_ref[...]
    o_ref[...] = _dot(_silu(c), w_ref[...], precision=HIGHEST) + b_ref[...]


def _adaln(cond, mod_w, mod_b):
    depth, d, d6 = mod_w.shape
    n = d6 // d
    return pl.pallas_call(
        _adaln_kernel,
        out_shape=jax.ShapeDtypeStruct((depth, cond.shape[0], d6), F32),
        grid=(depth, n),
        in_specs=[pl.BlockSpec(cond.shape, lambda l, j: (0, 0)),
                  pl.BlockSpec((None, d, d), lambda l, j: (l, 0, j)),
                  pl.BlockSpec((None, 1, d), lambda l, j: (l, 0, j))],
        out_specs=pl.BlockSpec((None, cond.shape[0], d), lambda l, j: (l, 0, j)),
        compiler_params=_cparams(("arbitrary", "arbitrary")),
        name="adaln",
    )(cond, mod_w, mod_b.reshape(depth, 1, d6))


def _proj_even_kernel(x_ref, g_ref, sh_ref, sc_ref, w_ref, wg_ref, cw_ref, rate_ref, gb_ref,
                      main_ref, gates_ref):
    t = pl.program_id(1)
    h = _rms_mod(x_ref[...], g_ref[...], sh_ref[...], sc_ref[...])
    hb = h.astype(BF16)
    qk_w = GDN_H * HD
    ones_blk = jnp.ones((HD, HD), BF16)

    row = lax.broadcasted_iota(jnp.int32, (TM, 1), 0)
    pos = jnp.where(t > 0, row & (CH - 1), row)
    last = jnp.where(t > 0, CH - 1, TM - 1)
    left_ok = pos != 0
    right_ok = pos != last
    for seg in range(3):
        sl = slice(seg * qk_w, (seg + 1) * qk_w)
        u = _dot(hb, w_ref[:, sl])
        um = jnp.where(left_ok, pltpu.roll(u, 1, 0), 0.0)
        up = jnp.where(right_ok, pltpu.roll(u, TM - 1, 0), 0.0)
        cv = _silu(um * cw_ref[0:1, sl] + u * cw_ref[1:2, sl] + up * cw_ref[2:3, sl])
        if seg < 2:
            ss = _group_sum(cv * cv, ones_blk)
            cv = cv * lax.rsqrt(ss + EPS)
            if seg == 0:
                cv = cv * (HD ** -0.5)
        main_ref[:, sl] = cv.astype(BF16)
    for seg in range(3, 8):
        sl = slice(seg * qk_w, (seg + 1) * qk_w)
        u = _dot(hb, w_ref[:, sl])
        if seg == 5:
            u = u * (HD ** -0.5)
        main_ref[:, sl] = u.astype(BF16)

    z = _dot(hb, wg_ref[...]) + gb_ref[...]
    tl = jnp.log(1.0 + jnp.exp(-jnp.abs(z)))
    sp_pos = jnp.maximum(z, 0.0) + tl
    sp_neg = jnp.maximum(-z, 0.0) + tl
    lane = lax.broadcasted_iota(jnp.int32, z.shape, 1)
    ng = N_DIR * GDN_H
    res = jnp.where(lane < ng, -rate_ref[...] * sp_pos,
                    jnp.where(lane < 2 * ng, _sigmoid(z),
                              jnp.where(lane < 3 * ng, z, -sp_neg)))
    gates_ref[...] = res[:, :gates_ref.shape[-1]]


def _tile_mod_spec(d):
    return pl.BlockSpec((None, None, 1, d), lambda b, t: (b, jnp.minimum(t, 1), 0, 0))


def _proj_even(x, g, sh, sc, w_main, w_gate, conv_w, rate, gbias):
    b, t, d = x.shape
    n = w_main.shape[1]
    ngl = 4 * N_DIR * GDN_H
    const = lambda shape: pl.BlockSpec(shape, lambda b_, t_: (0,) * len(shape))
    return pl.pallas_call(
        _proj_even_kernel,
        out_shape=(jax.ShapeDtypeStruct((b, t, n), BF16), jax.ShapeDtypeStruct((b, t, ngl), F32)),
        grid=(b, t // TM),
        in_specs=[pl.BlockSpec((None, TM, d), lambda b_, t_: (b_, t_, 0)),
                  const((1, d)), _tile_mod_spec(d), _tile_mod_spec(d),
                  const(w_main.shape), const(w_gate.shape), const(conv_w.shape),
                  const(rate.shape), const(gbias.shape)],
        out_specs=(pl.BlockSpec((None, TM, n), lambda b_, t_: (b_, t_, 0)),
                   pl.BlockSpec((None, TM, ngl), lambda b_, t_: (b_, t_, 0))),
        compiler_params=_cparams(("arbitrary", "arbitrary")),
        name="proj_even",
    )(x, g, sh, sc, w_main, w_gate, conv_w, rate, gbias)


def _rev_tile(i, nt):
    return jnp.where(i == 0, 0, nt - i)


def _tri_masks():
    r = lax.broadcasted_iota(jnp.int32, (CH, CH), 0)
    c = lax.broadcasted_iota(jnp.int32, (CH, CH), 1)
    return r >= c, r > c, r <= c, r < c


def _cumsum_both(gc, gr, d, lower, upper):
    lo = lower.astype(F32)
    up = upper.astype(F32)
    if d == 0:
        return _dot(lo, gc, precision=HIGHEST), _dot(gr, up, precision=HIGHEST)
    return _dot(up, gc, precision=HIGHEST), _dot(gr, lo, precision=HIGHEST)


def _gdn_kernel(qf, kf, vf, qb, kb, vb, gcf, grf, gcb, grb, of, ob, s_ref):
    i = pl.program_id(1)

    @pl.when(i == 0)
    def _():
        s_ref[...] = jnp.zeros_like(s_ref)

    lower, lstrict, upper, ustrict = _tri_masks()
    refs = ((qf, kf, vf, gcf, grf, of), (qb, kb, vb, gcb, grb, ob))
    ng = N_DIR * GDN_H

    def chunk_body(cc, carry):
        for d in range(N_DIR):
            q_ref, k_ref, v_ref, gc_ref, gr_ref, o_ref = refs[d]
            c = cc if d == 0 else CPB - 1 - cc
            rows = pl.ds(pl.multiple_of(c * CH, CH), CH)
            gc = gc_ref[c]
            gr = gr_ref[c]
            cs_c, cs_r = _cumsum_both(gc, gr, d, lower, upper)
            incl = lower if d == 0 else upper
            strict = lstrict if d == 0 else ustrict
            for hh in range(GDN_H):
                ci = d * GDN_H + hh
                cols = slice(hh * HD, (hh + 1) * HD)
                g_col = cs_c[:, ci:ci + 1]
                g_row = cs_r[ci:ci + 1, :]
                beta = gc[:, ng + ci:ng + ci + 1]
                tot = g_col[CH - 1:CH, :] if d == 0 else g_col[0:1, :]
                decay = jnp.where(incl, jnp.exp(jnp.where(incl, g_col - g_row, 0.0)), 0.0)
                q = q_ref[rows, cols].astype(F32)
                k = k_ref[rows, cols].astype(F32)
                v = v_ref[rows, cols].astype(F32)
                kbeta = k * beta
                kbf = k.astype(BF16)
                a = jnp.where(strict, _dot_nt(kbeta.astype(BF16), kbf) * decay, 0.0)
                eg = jnp.exp(g_col)
                x = jnp.concatenate([v * beta, kbeta * eg], axis=1)
                p = -a
                for j in range(6):
                    pb = p.astype(BF16)
                    x = x + _dot(pb, x.astype(BF16))
                    if j < 5:
                        p = _dot(pb, pb)
                u = x[:, :HD]
                w = x[:, HD:]
                qk = _dot_nt(q.astype(BF16), kbf) * decay
                s = s_ref[ci]
                sb = s.astype(BF16)
                v_new = u - _dot(w.astype(BF16), sb)
                vnb = v_new.astype(BF16)
                o = _dot((q * eg).astype(BF16), sb) + _dot(qk.astype(BF16), vnb)
                k_dec = k * jnp.exp(tot - g_col)
                s_ref[ci] = s * jnp.exp(tot) + _dot_tn(k_dec.astype(BF16), vnb)
                o_ref[rows, cols] = o.astype(o_ref.dtype)
        return carry

    lax.fori_loop(0, CPB, chunk_body, 0)


def _scan_specs(nt, width, col):
    fwd = pl.BlockSpec((None, TM, width), lambda b, i: (b, i, col))
    bwd = pl.BlockSpec((None, TM, width), lambda b, i: (b, _rev_tile(i, nt), col))
    return fwd, bwd


def _gate_specs(nt, shape):
    fwd = pl.BlockSpec((None, CPB) + shape, lambda b, i: (b, i, 0, 0))
    bwd = pl.BlockSpec((None, CPB) + shape, lambda b, i: (b, _rev_tile(i, nt), 0, 0))
    return fwd, bwd


def _gdn_scan(main, gates_c, gates_r):
    b, t, _ = main.shape
    nt = t // TM
    w = GDN_H * HD
    ngl = gates_c.shape[-1]
    qf, qb = _scan_specs(nt, w, 0)
    kf, kb = _scan_specs(nt, w, 1)
    vf, vb = _scan_specs(nt, w, 2)
    gcf, gcb = _gate_specs(nt, (CH, ngl))
    grf, grb = _gate_specs(nt, (ngl, CH))
    of, ob = _scan_specs(nt, w, 0)
    return pl.pallas_call(
        _gdn_kernel,
        out_shape=(jax.ShapeDtypeStruct((b, t, w), BF16), jax.ShapeDtypeStruct((b, t, w), BF16)),
        grid=(b, nt),
        in_specs=[qf, kf, vf, qb, kb, vb, gcf, grf, gcb, grb],
        out_specs=(of, ob),
        scratch_shapes=[pltpu.VMEM((N_DIR * GDN_H, HD, HD), F32)],
        compiler_params=_cparams(("arbitrary", "arbitrary")),
        name="gdn_scan",
    )(main, main, main, main, main, main, gates_c, gates_r, gates_c, gates_r)


def _mlstm_kernel(qf, kf, vf, qb, kb, vb, gcf, grf, gcb, grb, of, ob, c_ref, m_ref):
    i = pl.program_id(1)

    @pl.when(i == 0)
    def _():
        c_ref[...] = jnp.zeros_like(c_ref)
        m_ref[...] = jnp.zeros_like(m_ref)

    lower, _, upper, _ = _tri_masks()
    refs = ((qf, kf, vf, gcf, grf, of), (qb, kb, vb, gcb, grb, ob))
    ng = N_DIR * GDN_H
    i_off = 2 * ng
    f_off = 2 * ng + N_DIR * ML_H
    lane = lax.broadcasted_iota(jnp.int32, (CH, HD), 1)
    ones_col = jnp.where(lane == 0, 1.0, 0.0).astype(BF16)

    def chunk_body(cc, carry):
        for d in range(N_DIR):
            q_ref, k_ref, v_ref, gc_ref, gr_ref, o_ref = refs[d]
            c = cc if d == 0 else CPB - 1 - cc
            rows = pl.ds(pl.multiple_of(c * CH, CH), CH)
            gc = gc_ref[c]
            gr = gr_ref[c]
            cs_c, cs_r = _cumsum_both(gc, gr, d, lower, upper)
            incl = lower if d == 0 else upper
            for hh in range(ML_H):
                ci = d * ML_H + hh
                cols = slice(hh * HD, (hh + 1) * HD)
                b_col = cs_c[:, f_off + ci:f_off + ci + 1]
                b_row = cs_r[f_off + ci:f_off + ci + 1, :]
                i_col = gc[:, i_off + ci:i_off + ci + 1]
                i_row = gr[i_off + ci:i_off + ci + 1, :]
                b_tot = b_col[CH - 1:CH, :] if d == 0 else b_col[0:1, :]
                m_row = m_ref[ci]
                m = m_row[:, 0:1]
                d_in = jnp.where(incl, b_col - b_row + i_row, NEG)
                d_carry = b_col + m
                m_t = jnp.maximum(d_carry, jnp.max(d_in, axis=-1, keepdims=True))
                q = q_ref[rows, cols]
                k = k_ref[rows, cols]
                v = v_ref[rows, cols]
                v_aug = jnp.concatenate([v, ones_col], axis=1)
                s = _dot_nt(q, k) * jnp.exp(d_in - m_t)
                w_carry = jnp.exp(d_carry - m_t)
                cst = c_ref[ci]
                num_den = w_carry * _dot(q, cst.astype(BF16)) + _dot(s.astype(BF16), v_aug)
                den = num_den[:, HD:HD + 1]
                hout = num_den[:, :HD] / jnp.maximum(jnp.abs(den), jnp.exp(-m_t))
                d_end_c = b_tot - b_col + i_col
                d_end_r = b_tot - b_row + i_row
                carry_end = b_tot + m
                m_new = jnp.maximum(carry_end, jnp.max(d_end_r, axis=-1, keepdims=True))
                w_end = jnp.exp(d_end_c - m_new)
                f_end = jnp.exp(carry_end - m_new)
                kw = (k.astype(F32) * w_end).astype(BF16)
                c_ref[ci] = f_end * cst + _dot_tn(kw, v_aug)
                m_ref[ci] = jnp.broadcast_to(m_new, m_row.shape)
                o_ref[rows, cols] = hout.astype(o_ref.dtype)
        return carry

    lax.fori_loop(0, CPB, chunk_body, 0)


def _mlstm_scan(main, gates_c, gates_r):
    b, t, _ = main.shape
    nt = t // TM
    w = ML_H * HD
    ngl = gates_c.shape[-1]
    qf, qb = _scan_specs(nt, w, 4)
    kf, kb = _scan_specs(nt, w, 5)
    vf, vb = _scan_specs(nt, w, 6)
    gcf, gcb = _gate_specs(nt, (CH, ngl))
    grf, grb = _gate_specs(nt, (ngl, CH))
    of, ob = _scan_specs(nt, w, 0)
    return pl.pallas_call(
        _mlstm_kernel,
        out_shape=(jax.ShapeDtypeStruct((b, t, w), BF16), jax.ShapeDtypeStruct((b, t, w), BF16)),
        grid=(b, nt),
        in_specs=[qf, kf, vf, qb, kb, vb, gcf, grf, gcb, grb],
        out_specs=(of, ob),
        scratch_shapes=[pltpu.VMEM((N_DIR * ML_H, HD, 2 * HD), F32),
                        pltpu.VMEM((N_DIR * ML_H, 1, HD), F32)],
        compiler_params=_cparams(("arbitrary", "arbitrary")),
        name="mlstm_scan",
    )(main, main, main, main, main, main, gates_c, gates_r, gates_c, gates_r)


def _ret_kernel(qf, kf, vf, qb, kb, vb, intra_ref, cross_ref, tail_ref, cd_ref, of, ob, s_ref):
    i = pl.program_id(1)

    @pl.when(i == 0)
    def _():
        s_ref[...] = jnp.zeros_like(s_ref)

    refs = ((qf, kf, vf, of), (qb, kb, vb, ob))

    def chunk_body(cc, carry):
        for d in range(N_DIR):
            q_ref, k_ref, v_ref, o_ref = refs[d]
            c = cc if d == 0 else CPB - 1 - cc
            rows = pl.ds(pl.multiple_of(c * CH, CH), CH)
            for hh in range(RET_H):
                si = d * RET_H + hh
                kcols = slice(hh * HD, (hh + 1) * HD)
                vcols = slice(hh * RET_DV, (hh + 1) * RET_DV)
                q = q_ref[rows, kcols]
                k = k_ref[rows, kcols]
                v = v_ref[rows, vcols]
                s = s_ref[si]
                scores = _dot_nt(q, k) * intra_ref[si]
                qc = (q.astype(F32) * cross_ref[si]).astype(BF16)
                o = _dot(scores.astype(BF16), v) + _dot(qc, s.astype(BF16))
                kt = (k.astype(F32) * tail_ref[si]).astype(BF16)
                s_ref[si] = cd_ref[si] * s + _dot_tn(kt, v)
                o_ref[rows, vcols] = o.astype(o_ref.dtype)
        return carry

    lax.fori_loop(0, CPB, chunk_body, 0)


def _ret_tables():
    pos = np.arange(CH, dtype=np.float64)
    intra, cross, tail, cd = [], [], [], []
    for d in range(N_DIR):
        expo = 5.0 + np.arange(RET_H, dtype=np.float64)
        if d == 1:
            expo = expo[::-1]
        lg = np.log1p(-np.exp2(-expo))
        p = pos if d == 0 else (CH - 1.0 - pos)
        diff = p[:, None] - p[None, :]
        for hh in range(RET_H):
            intra.append(np.where(diff >= 0, np.exp(np.where(diff >= 0, diff, 0.0) * lg[hh]), 0.0))
            cross.append(np.broadcast_to(np.exp((p + 1.0) * lg[hh])[:, None], (CH, HD)))
            tail.append(np.broadcast_to(np.exp((CH - 1.0 - p) * lg[hh])[:, None], (CH, HD)))
            cd.append(np.full((1, RET_DV), np.exp(CH * lg[hh])))
    f = lambda a: jnp.asarray(np.stack(a), F32)
    return f(intra), f(cross), f(tail), f(cd)


def _ret_scan(main):
    b, t, _ = main.shape
    nt = t // TM
    qw = RET_H * HD
    vw = RET_H * RET_DV
    qf, qb = _scan_specs(nt, qw, 0)
    kf, kb = _scan_specs(nt, qw, 1)
    vf, vb = _scan_specs(nt, vw, 1)
    of, ob = _scan_specs(nt, vw, 0)
    tabs = _ret_tables()
    const = lambda a: pl.BlockSpec(a.shape, lambda b_, i_: (0,) * a.ndim)
    return pl.pallas_call(
        _ret_kernel,
        out_shape=(jax.ShapeDtypeStruct((b, t, vw), BF16), jax.ShapeDtypeStruct((b, t, vw), BF16)),
        grid=(b, nt),
        in_specs=[qf, kf, vf, qb, kb, vb] + [const(a) for a in tabs],
        out_specs=(of, ob),
        scratch_shapes=[pltpu.VMEM((N_DIR * RET_H, HD, RET_DV), F32)],
        compiler_params=_cparams(("arbitrary", "arbitrary")),
        name="ret_scan",
    )(main, main, main, main, main, main, *tabs)


def _route_and_pack(h2, rw_ref, rb_ref, h2p_ref, idx_ref, wgt_ref):
    h2b = h2.astype(BF16)
    logits = _dot(h2b, rw_ref[...]) + rb_ref[...]
    lane = lax.broadcasted_iota(jnp.int32, logits.shape, 1)
    lane_f = lane.astype(F32)
    vals, idxs = [], []
    cur = logits
    for _ in range(TOP_K):
        m = jnp.max(cur, axis=-1, keepdims=True)
        ix = jnp.min(jnp.where(cur == m, lane_f, float(logits.shape[1])), axis=-1, keepdims=True)
        vals.append(m)
        idxs.append(ix)
        cur = jnp.where(lane_f == ix, NEG, cur)
    es = [jnp.exp(v - vals[0]) for v in vals]
    tot = es[0] + es[1] + es[2] + es[3]
    iout = jnp.zeros(logits.shape, F32)
    wout = jnp.zeros(logits.shape, F32)
    for j in range(TOP_K):
        iout = jnp.where(lane == j, idxs[j], iout)
        wout = jnp.where(lane == j, es[j] / tot, wout)
    idx_ref[...] = iout.astype(jnp.int32)
    wgt_ref[...] = wout
    half = h2.shape[1] // 2
    r = h2b.astype(F32)
    lo = lax.shift_right_logical(pltpu.bitcast(r[:, :half], U32), jnp.uint32(16))
    hi = pltpu.bitcast(r[:, half:], U32) & jnp.uint32(0xFFFF0000)
    h2p_ref[...] = hi | lo


def _merge_even_kernel(ogf, ogb, omf, omb, z_ref, mo_ref, gg_ref, mg_ref, wo_ref, x_ref, g1_ref,
                       n2_ref, sh_ref, sc_ref, rw_ref, rb_ref, x1_ref, h2p_ref, idx_ref, wgt_ref):
    ones_blk = jnp.ones((HD, HD), BF16)
    og = ogf[...].astype(F32) + ogb[...].astype(F32)
    ms = _group_sum(og * og, ones_blk) * (1.0 / HD)
    a = og * lax.rsqrt(ms + EPS) * gg_ref[...] * _silu(z_ref[...].astype(F32))
    om = omf[...].astype(F32) + omb[...].astype(F32)
    ms = _group_sum(om * om, ones_blk) * (1.0 / HD)
    m = om * lax.rsqrt(ms + EPS) * mg_ref[...] * _sigmoid(mo_ref[...].astype(F32))
    cat = jnp.concatenate([a, m], axis=1).astype(BF16)
    y = _dot(cat, wo_ref[...])
    x1 = x_ref[...] + g1_ref[...] * y
    x1_ref[...] = x1
    h2 = _rms_mod(x1, n2_ref[...], sh_ref[...], sc_ref[...])
    _route_and_pack(h2, rw_ref, rb_ref, h2p_ref, idx_ref, wgt_ref)


def _merge_odd_kernel(of, ob, gate_ref, ng_ref, wo_ref, x_ref, g1_ref,
                      n2_ref, sh_ref, sc_ref, rw_ref, rb_ref, x1_ref, h2p_ref, idx_ref, wgt_ref):
    ones_blk = jnp.ones((RET_DV, RET_DV), BF16)
    o = of[...].astype(F32) + ob[...].astype(F32)
    o = o - _group_sum(o, ones_blk) * (1.0 / RET_DV)
    ms = _group_sum(o * o, ones_blk) * (1.0 / RET_DV)
    y = o * lax.rsqrt(ms + EPS) * ng_ref[...] * _silu(gate_ref[...].astype(F32))
    y = _dot(y.astype(BF16), wo_ref[...])
    x1 = x_ref[...] + g1_ref[...] * y
    x1_ref[...] = x1
    h2 = _rms_mod(x1, n2_ref[...], sh_ref[...], sc_ref[...])
    _route_and_pack(h2, rw_ref, rb_ref, h2p_ref, idx_ref, wgt_ref)


def _merge_out(b, t, d):
    shapes = (jax.ShapeDtypeStruct((b, t, d), F32), jax.ShapeDtypeStruct((b, t, d // 2), U32),
              jax.ShapeDtypeStruct((b, t, 128), jnp.int32), jax.ShapeDtypeStruct((b, t, 128), F32))
    specs = tuple(pl.BlockSpec((None, TM, s.shape[-1]), lambda b_, t_: (b_, t_, 0)) for s in shapes)
    return shapes, specs


def _merge_even(og_f, og_b, om_f, om_b, main, gdn_g, ml_g, w_out, x, g1, n2, sh2, sc2, rw, rb):
    b, t, d = x.shape
    w = GDN_H * HD
    tile = lambda width, col: pl.BlockSpec((None, TM, width), lambda b_, t_: (b_, t_, col))
    const = lambda a: pl.BlockSpec(a.shape, lambda b_, t_: (0,) * a.ndim)
    shapes, specs = _merge_out(b, t, d)
    return pl.pallas_call(
        _merge_even_kernel,
        out_shape=shapes,
        grid=(b, t // TM),
        in_specs=[tile(w, 0), tile(w, 0), tile(w, 0), tile(w, 0), tile(w, 3), tile(w, 7),
                  const(gdn_g), const(ml_g), const(w_out), tile(d, 0), _tile_mod_spec(d),
                  const(n2), _tile_mod_spec(d), _tile_mod_spec(d), const(rw), const(rb)],
        out_specs=specs,
        compiler_params=_cparams(("arbitrary", "arbitrary")),
        name="merge_even",
    )(og_f, og_b, om_f, om_b, main, main, gdn_g, ml_g, w_out, x, g1, n2, sh2, sc2, rw, rb)


def _merge_odd(o_f, o_b, main, ret_g, w_out, x, g1, n2, sh2, sc2, rw, rb):
    b, t, d = x.shape
    vw = RET_H * RET_DV
    tile = lambda width, col: pl.BlockSpec((None, TM, width), lambda b_, t_: (b_, t_, col))
    const = lambda a: pl.BlockSpec(a.shape, lambda b_, t_: (0,) * a.ndim)
    shapes, specs = _merge_out(b, t, d)
    return pl.pallas_call(
        _merge_odd_kernel,
        out_shape=shapes,
        grid=(b, t // TM),
        in_specs=[tile(vw, 0), tile(vw, 0), tile(vw, 2), const(ret_g), const(w_out), tile(d, 0),
                  _tile_mod_spec(d), const(n2), _tile_mod_spec(d), _tile_mod_spec(d), const(rw), const(rb)],
        out_specs=specs,
        compiler_params=_cparams(("arbitrary", "arbitrary")),
        name="merge_odd",
    )(o_f, o_b, main, ret_g, w_out, x, g1, n2, sh2, sc2, rw, rb)


def _proj_odd_kernel(x_ref, acc_ref, g2_ref, g_ref, sh_ref, sc_ref, w_ref, cos_ref, sin_ref,
                     x2_ref, main_ref):
    x2 = x_ref[...] + g2_ref[...] * acc_ref[...]
    x2_ref[...] = x2
    hb = _rms_mod(x2, g_ref[...], sh_ref[...], sc_ref[...]).astype(BF16)
    qk_w = RET_H * HD
    cos = cos_ref[...]
    sin = sin_ref[...]
    n_seg = w_ref.shape[1] // qk_w
    for seg in range(n_seg):
        sl = slice(seg * qk_w, (seg + 1) * qk_w)
        u = _dot(hb, w_ref[:, sl])
        if seg < 2:
            parts = []
            for hh in range(RET_H):
                uh = u[:, hh * HD:(hh + 1) * HD]
                parts.append(uh * cos + pltpu.roll(uh, HD // 2, 1) * sin)
            u = jnp.concatenate(parts, axis=1)
            if seg == 1:
                u = u * (HD ** -0.5)
        main_ref[:, sl] = u.astype(BF16)


def _rope_tables(t):
    half = HD // 2
    freqs = ROPE_BASE ** (-jnp.arange(half, dtype=F32) / half)
    ang = jnp.arange(t, dtype=F32)[:, None] * freqs[None, :]
    cos, sin = jnp.cos(ang), jnp.sin(ang)
    return jnp.concatenate([cos, cos], axis=1), jnp.concatenate([-sin, sin], axis=1)


def _proj_odd(x, acc, g2, g, sh, sc, w):
    b, t, d = x.shape
    n = w.shape[1]
    cos, sin = _rope_tables(t)
    const = lambda a: pl.BlockSpec(a.shape, lambda b_, t_: (0,) * a.ndim)
    tile = lambda width: pl.BlockSpec((None, TM, width), lambda b_, t_: (b_, t_, 0))
    rope = pl.BlockSpec((TM, HD), lambda b_, t_: (t_, 0))
    return pl.pallas_call(
        _proj_odd_kernel,
        out_shape=(jax.ShapeDtypeStruct((b, t, d), F32), jax.ShapeDtypeStruct((b, t, n), BF16)),
        grid=(b, t // TM),
        in_specs=[tile(d), tile(d), _tile_mod_spec(d), const(g), _tile_mod_spec(d), _tile_mod_spec(d),
                  const(w), rope, rope],
        out_specs=(tile(d), tile(n)),
        compiler_params=_cparams(("arbitrary", "arbitrary")),
        name="proj_odd",
    )(x, acc, g2, g, sh, sc, w, cos, sin)


ROWS_PER_STEP = 8


def _moe_gather_kernel(src_ref, h_ref, xs_ref):
    def body(r8, carry):
        base = pl.multiple_of(r8 * ROWS_PER_STEP, ROWS_PER_STEP)
        rows = [h_ref[pl.ds(src_ref[0, base + j], 1), :] for j in range(ROWS_PER_STEP)]
        xs_ref[pl.ds(base, ROWS_PER_STEP), :] = jnp.concatenate(rows, axis=0)
        return carry

    lax.fori_loop(0, MOE_BLOCK // ROWS_PER_STEP, body, 0)


def _moe_gather(row_src, h2p, n_blocks):
    n, half = h2p.shape
    return pl.pallas_call(
        _moe_gather_kernel,
        out_shape=jax.ShapeDtypeStruct((n_blocks * MOE_BLOCK, half), U32),
        grid=(n_blocks,),
        in_specs=[pl.BlockSpec((None, 1, MOE_BLOCK), lambda i: (i, 0, 0), memory_space=pltpu.SMEM),
                  pl.BlockSpec((n, half), lambda i: (0, 0))],
        out_specs=pl.BlockSpec((MOE_BLOCK, half), lambda i: (i, 0)),
        compiler_params=_cparams(("arbitrary",)),
        name="moe_gather",
    )(row_src.reshape(n_blocks, 1, MOE_BLOCK), h2p)


def _moe_mm_kernel(be_ref, xs_ref, rw_ref, wg_ref, bg_ref, wu_ref, bu_ref, wd_ref, bd_ref, ys_ref):
    xu = xs_ref[...]
    half = xu.shape[1]
    lo = pltpu.bitcast(lax.shift_left(xu, jnp.uint32(16)), F32).astype(BF16)
    hi = pltpu.bitcast(xu & jnp.uint32(0xFFFF0000), F32).astype(BF16)
    gate = _dot(lo, wg_ref[:half, :]) + _dot(hi, wg_ref[half:, :]) + bg_ref[...]
    up = _dot(lo, wu_ref[:half, :]) + _dot(hi, wu_ref[half:, :]) + bu_ref[...]
    gate = jnp.minimum(gate, SWIGLU_LIMIT)
    up = jnp.clip(up, -SWIGLU_LIMIT, SWIGLU_LIMIT)
    act = (up + 1.0) * gate * _sigmoid(SWIGLU_ALPHA * gate)
    y = _dot(act.astype(BF16), wd_ref[...]) + bd_ref[...]
    ys_ref[...] = y * rw_ref[...]


def _moe_mm(block_e, xs, row_w, w_gate, b_gate, w_up, b_up, w_dn, b_dn):
    rows, half = xs.shape
    n_blocks = rows // MOE_BLOCK
    _, d, f = w_gate.shape
    ew = lambda shape: pl.BlockSpec((None,) + shape, lambda i, be: (be[i], 0, 0))
    grid_spec = pltpu.PrefetchScalarGridSpec(
        num_scalar_prefetch=1,
        grid=(n_blocks,),
        in_specs=[pl.BlockSpec((MOE_BLOCK, half), lambda i, be: (i, 0)),
                  pl.BlockSpec((MOE_BLOCK, 1), lambda i, be: (i, 0)),
                  ew((d, f)), ew((1, f)), ew((d, f)), ew((1, f)), ew((f, d)), ew((1, d))],
        out_specs=pl.BlockSpec((MOE_BLOCK, d), lambda i, be: (i, 0)),
    )
    return pl.pallas_call(
        _moe_mm_kernel,
        out_shape=jax.ShapeDtypeStruct((rows, d), F32),
        grid_spec=grid_spec,
        compiler_params=_cparams(("arbitrary",)),
        name="moe_mm",
    )(block_e, xs, row_w.reshape(rows, 1), w_gate, b_gate, w_up, b_up, w_dn, b_dn)


COMBINE_COLS = 256


def _moe_combine_kernel(dst_ref, ys_ref, acc_ref):
    i = pl.program_id(1)

    @pl.when(i == 0)
    def _():
        acc_ref[...] = jnp.zeros_like(acc_ref)

    def body(r8, carry):
        base = pl.multiple_of(r8 * ROWS_PER_STEP, ROWS_PER_STEP)
        y8 = ys_ref[pl.ds(base, ROWS_PER_STEP), :]
        toks = [dst_ref[0, base + j] for j in range(ROWS_PER_STEP)]
        cur = [acc_ref[pl.ds(tk, 1), :] for tk in toks]
        for j in range(ROWS_PER_STEP):
            acc_ref[pl.ds(toks[j], 1), :] = cur[j] + y8[j:j + 1, :]
        return carry

    lax.fori_loop(0, MOE_BLOCK // ROWS_PER_STEP, body, 0)


def _moe_combine(row_dst, ys, n_rows_out):
    rows, d = ys.shape
    n_blocks = rows // MOE_BLOCK
    return pl.pallas_call(
        _moe_combine_kernel,
        out_shape=jax.ShapeDtypeStruct((n_rows_out, d), F32),
        grid=(d // COMBINE_COLS, n_blocks),
        in_specs=[pl.BlockSpec((None, 1, MOE_BLOCK), lambda j, i: (i, 0, 0), memory_space=pltpu.SMEM),
                  pl.BlockSpec((MOE_BLOCK, COMBINE_COLS), lambda j, i: (i, j))],
        out_specs=pl.BlockSpec((n_rows_out, COMBINE_COLS), lambda j, i: (0, j)),
        compiler_params=_cparams(("arbitrary", "arbitrary")),
        name="moe_combine",
    )(row_dst.reshape(n_blocks, 1, MOE_BLOCK), ys)


def _moe(h2p, top_idx, top_w, w_gate, b_gate, w_up, b_up, w_dn, b_dn):
    n = h2p.shape[0]
    n_assign = n * TOP_K
    n_blocks = -(-n_assign // MOE_BLOCK) + N_EXPERTS
    rows = n_blocks * MOE_BLOCK
    flat_e = top_idx.reshape(-1)
    order = jnp.argsort(flat_e)
    sorted_e = flat_e[order]
    counts = jnp.bincount(flat_e, length=N_EXPERTS)
    padded = -(-counts // MOE_BLOCK) * MOE_BLOCK
    ends = jnp.cumsum(padded)
    dest = (ends - padded)[sorted_e] + jnp.arange(n_assign) - (jnp.cumsum(counts) - counts)[sorted_e]
    tok = (order // TOP_K).astype(jnp.int32)
    row_src = jnp.zeros((rows,), jnp.int32).at[dest].set(tok)
    row_dst = jnp.full((rows,), n, jnp.int32).at[dest].set(tok)
    row_w = jnp.zeros((rows,), F32).at[dest].set(top_w.reshape(-1)[order])
    block_e = jnp.minimum(jnp.searchsorted(ends, jnp.arange(n_blocks) * MOE_BLOCK, side='right'),
                          N_EXPERTS - 1).astype(jnp.int32)
    xs = _moe_gather(row_src, h2p, n_blocks)
    ys = _moe_mm(block_e, xs, row_w, w_gate, b_gate, w_up, b_up, w_dn, b_dn)
    return _moe_combine(row_dst, ys, n + 8)


def _final_kernel(x_ref, acc_ref, g2_ref, g_ref, o_ref):
    x = x_ref[...] + g2_ref[...] * acc_ref[...]
    ms = jnp.mean(x * x, axis=-1, keepdims=True)
    o_ref[...] = x * lax.rsqrt(ms + EPS) * g_ref[...]


def _final(x, acc, g2, g, n_ctx_tiles):
    b, t, d = x.shape
    nt = t // TM - n_ctx_tiles
    tile_in = pl.BlockSpec((None, TM, d), lambda b_, t_: (b_, t_ + n_ctx_tiles, 0))
    return pl.pallas_call(
        _final_kernel,
        out_shape=jax.ShapeDtypeStruct((b, nt * TM, d), F32),
        grid=(b, nt),
        in_specs=[tile_in, tile_in,
                  pl.BlockSpec((None, None, 1, d), lambda b_, t_: (b_, 1, 0, 0)),
                  pl.BlockSpec((1, d), lambda b_, t_: (0, 0))],
        out_specs=pl.BlockSpec((None, TM, d), lambda b_, t_: (b_, t_, 0)),
        compiler_params=_cparams(("arbitrary", "arbitrary")),
        name="final_norm",
    )(x, acc, g2, g)


def _mod_tables(mod, b, d):
    outs = []
    for j in range(6):
        m = mod[:, j * d:(j + 1) * d]
        lat = m[:b]
        ctx = jnp.broadcast_to(m[b:b + 1], (b, d))
        outs.append(jnp.stack([ctx, lat], axis=1)[:, :, None, :])
    return outs


def _router_params(router_w, router_b):
    d, e = router_w.shape
    rw = jnp.zeros((d, 128), F32).at[:, :e].set(router_w).astype(BF16)
    rb = jnp.full((1, 128), NEG, F32).at[0, :e].set(router_b)
    return rw, rb


def _expert_params(w_gu, b_gu, w_dn, b_dn):
    e = w_gu.shape[0]
    return (w_gu[:, :, 0::2].astype(BF16), b_gu[:, 0::2].reshape(e, 1, -1),
            w_gu[:, :, 1::2].astype(BF16), b_gu[:, 1::2].reshape(e, 1, -1),
            w_dn.astype(BF16), b_dn.reshape(e, 1, -1))


def kernel(x, c, ctx, c_ctx, mod_w, mod_b, norm1_g, norm2_g, ev_w_in, ev_conv_w, gdn_a_log, gdn_dt_bias,
           gdn_norm_g, ml_i_bias, ml_f_bias, ml_norm_g, ev_w_out, od_w_in, ret_norm_g, od_w_out,
           router_w, router_b, moe_w_gu, moe_b_gu, moe_w_dn, moe_b_dn, final_g):
    b, s, d = x.shape
    n_ctx = ctx.shape[1]
    depth = mod_w.shape[0]
    assert n_ctx == TM and s % TM == 0 and depth == 2
    t = n_ctx + s
    n_tok = b * t

    cond = jnp.concatenate([c, c_ctx[None, :], jnp.zeros((8 - b - 1, d), F32)], axis=0)
    mod = _adaln(cond, mod_w, mod_b)
    xa = jnp.concatenate([ctx, x], axis=1)

    sh1, sc1, g1, sh2, sc2, g2 = _mod_tables(mod[0], b, d)
    qk_w = GDN_H * HD
    conv_ch = 3 * qk_w
    ng = N_DIR * GDN_H
    w_in = ev_w_in[0]
    o_z = conv_ch
    o_a = o_z + qk_w
    o_mq = o_a + 2 * ng
    o_i = o_mq + 4 * qk_w
    w_main = jnp.concatenate([w_in[:, :o_a], w_in[:, o_mq:o_i]], axis=1).astype(BF16)
    w_gate = jnp.concatenate([w_in[:, o_a:o_mq], w_in[:, o_i:o_i + 2 * ng],
                              jnp.zeros((d, 128 - 4 * ng), F32)], axis=1).astype(BF16)
    zeros_g = jnp.zeros((ng,), F32)
    rate = jnp.concatenate([jnp.exp(gdn_a_log[0].astype(F32)).reshape(-1), jnp.zeros((128 - ng,), F32)])[None, :]
    gbias = jnp.concatenate([gdn_dt_bias[0].reshape(-1), zeros_g, ml_i_bias[0].reshape(-1),
                             ml_f_bias[0].reshape(-1), jnp.zeros((128 - 4 * ng,), F32)])[None, :].astype(F32)
    main, gates = _proj_even(xa, norm1_g[0][None, :], sh1, sc1, w_main, w_gate, ev_conv_w[0], rate, gbias)
    gates_c = gates.reshape(b, t // CH, CH, 4 * ng)
    gates_r = jnp.swapaxes(gates_c, 2, 3)
    og_f, og_b = _gdn_scan(main, gates_c, gates_r)
    om_f, om_b = _mlstm_scan(main, gates_c, gates_r)
    rw, rb = _router_params(router_w[0], router_b[0])
    gdn_g = jnp.tile(gdn_norm_g[0], GDN_H)[None, :]
    x1, h2p, top_idx, top_w = _merge_even(og_f, og_b, om_f, om_b, main, gdn_g, ml_norm_g[0][None, :],
                                          ev_w_out[0].astype(BF16), xa, g1, norm2_g[0][None, :], sh2, sc2, rw, rb)
    acc = _moe(h2p.reshape(n_tok, d // 2), top_idx.reshape(n_tok, 128)[:, :TOP_K],
               top_w.reshape(n_tok, 128)[:, :TOP_K],
               *_expert_params(moe_w_gu[0], moe_b_gu[0], moe_w_dn[0], moe_b_dn[0]))
    acc = acc[:n_tok].reshape(b, t, d)
    g2_prev = g2

    sh1, sc1, g1, sh2, sc2, g2 = _mod_tables(mod[1], b, d)
    x2, main_o = _proj_odd(x1, acc, g2_prev, norm1_g[1][None, :], sh1, sc1, od_w_in[0].astype(BF16))
    o_f, o_b = _ret_scan(main_o)
    rw, rb = _router_params(router_w[1], router_b[1])
    x3, h2p, top_idx, top_w = _merge_odd(o_f, o_b, main_o, ret_norm_g[0][None, :], od_w_out[0].astype(BF16),
                                         x2, g1, norm2_g[1][None, :], sh2, sc2, rw, rb)
    acc = _moe(h2p.reshape(n_tok, d // 2), top_idx.reshape(n_tok, 128)[:, :TOP_K],
               top_w.reshape(n_tok, 128)[:, :TOP_K],
               *_expert_params(moe_w_gu[1], moe_b_gu[1], moe_w_dn[1], moe_b_dn[1]))
    acc = acc[:n_tok].reshape(b, t, d)
    return _final(x3, acc, g2, final_g[None, :], n_ctx // TM)
```

```python
import functools
import math

import jax
import jax.numpy as jnp
import numpy as np
from jax import lax
from jax.experimental import pallas as pl
from jax.experimental.pallas import tpu as pltpu

F32 = jnp.float32
BF16 = jnp.bfloat16
U32 = jnp.uint32
HIGHEST = lax.Precision.HIGHEST

EPS = 1e-6
CH = 64
TM = 256
CPB = TM // CH
HD = 128
N_DIR = 2
GDN_H = 4
ML_H = 4
RET_H = 8
RET_DV = 256
CONV_W = 3
N_EXPERTS = 32
TOP_K = 4
SWIGLU_ALPHA = 1.702
SWIGLU_LIMIT = 7.0
MOE_BLOCK = 256
ROPE_BASE = 10000.0
NEG = -1e30
VMEM_LIMIT = 56 * 1024 * 1024


def _cparams(sem):
    return pltpu.CompilerParams(dimension_semantics=sem, vmem_limit_bytes=VMEM_LIMIT)


def _dot(a, b, precision=None):
    return jnp.dot(a, b, preferred_element_type=F32, precision=precision)


def _dot_nt(a, b):
    return lax.dot_general(a, b, (((1,), (1,)), ((), ())), preferred_element_type=F32)


def _dot_tn(a, b):
    return lax.dot_general(a, b, (((0,), (0,)), ((), ())), preferred_element_type=F32)


def _sigmoid(x):
    return 1.0 / (1.0 + jnp.exp(-x))


def _silu(x):
    return x * _sigmoid(x)


def _group_sum(x, ones_blk):
    w = ones_blk.shape[0]
    hi = x.astype(BF16)
    lo = (x - hi.astype(F32)).astype(BF16)
    outs = []
    for j in range(x.shape[1] // w):
        sl = slice(j * w, (j + 1) * w)
        outs.append(_dot(hi[:, sl], ones_blk) + _dot(lo[:, sl], ones_blk))
    return outs[0] if len(outs) == 1 else jnp.concatenate(outs, axis=1)


def _rms_mod(x, g, sh, sc):
    ms = jnp.mean(x * x, axis=-1, keepdims=True)
    return (x * lax.rsqrt(ms + EPS) * g) * (1.0 + sc) + sh


def _adaln_kernel(c_ref, w_ref, b_ref, o_ref):
    c = c_ref[...]
    o_ref[...] = _dot(_silu(c), w_ref[...], precision=HIGHEST) + b_ref[...]


def _adaln(cond, mod_w, mod_b):
    depth, d, d6 = mod_w.shape
    n = d6 // d
    return pl.pallas_call(
        _adaln_kernel,
        out_shape=jax.ShapeDtypeStruct((depth, cond.shape[0], d6), F32),
        grid=(depth, n),
        in_specs=[pl.BlockSpec(cond.shape, lambda l, j: (0, 0)),
                  pl.BlockSpec((None, d, d), lambda l, j: (l, 0, j)),
                  pl.BlockSpec((None, 1, d), lambda l, j: (l, 0, j))],
        out_specs=pl.BlockSpec((None, cond.shape[0], d), lambda l, j: (l, 0, j)),
        compiler_params=_cparams(("arbitrary", "arbitrary")),
        name="adaln",
    )(cond, mod_w, mod_b.reshape(depth, 1, d6))


def _proj_even_kernel(x_ref, g_ref, sh_ref, sc_ref, w_ref, wg_ref, cw_ref, rate_ref, gb_ref,
                      main_ref, gates_ref):
    t = pl.program_id(1)
    h = _rms_mod(x_ref[...], g_ref[...], sh_ref[...], sc_ref[...])
    hb = h.astype(BF16)
    qk_w = GDN_H * HD
    ones_blk = jnp.ones((HD, HD), BF16)

    row = lax.broadcasted_iota(jnp.int32, (TM, 1), 0)
    pos = jnp.where(t > 0, row & (CH - 1), row)
    last = jnp.where(t > 0, CH - 1, TM - 1)
    left_ok = pos != 0
    right_ok = pos != last
    for seg in range(3):
        sl = slice(seg * qk_w, (seg + 1) * qk_w)
        u = _dot(hb, w_ref[:, sl])
        um = jnp.where(left_ok, pltpu.roll(u, 1, 0), 0.0)
        up = jnp.where(right_ok, pltpu.roll(u, TM - 1, 0), 0.0)
        cv = _silu(um * cw_ref[0:1, sl] + u * cw_ref[1:2, sl] + up * cw_ref[2:3, sl])
        if seg < 2:
            ss = _group_sum(cv * cv, ones_blk)
            cv = cv * lax.rsqrt(ss + EPS)
            if seg == 0:
                cv = cv * (HD ** -0.5)
        main_ref[:, sl] = cv.astype(BF16)
    for seg in range(3, 8):
        sl = slice(seg * qk_w, (seg + 1) * qk_w)
        u = _dot(hb, w_ref[:, sl])
        if seg == 5:
            u = u * (HD ** -0.5)
        main_ref[:, sl] = u.astype(BF16)

    z = _dot(hb, wg_ref[...]) + gb_ref[...]
    tl = jnp.log(1.0 + jnp.exp(-jnp.abs(z)))
    sp_pos = jnp.maximum(z, 0.0) + tl
    sp_neg = jnp.maximum(-z, 0.0) + tl
    lane = lax.broadcasted_iota(jnp.int32, z.shape, 1)
    ng = N_DIR * GDN_H
    res = jnp.where(lane < ng, -rate_ref[...] * sp_pos,
                    jnp.where(lane < 2 * ng, _sigmoid(z),
                              jnp.where(lane < 3 * ng, z, -sp_neg)))
    gates_ref[...] = res[:, :gates_ref.shape[-1]]


def _tile_mod_spec(d):
    return pl.BlockSpec((None, None, 1, d), lambda b, t: (b, jnp.minimum(t, 1), 0, 0))


def _proj_even(x, g, sh, sc, w_main, w_gate, conv_w, rate, gbias):
    b, t, d = x.shape
    n = w_main.shape[1]
    ngl = 4 * N_DIR * GDN_H
    const = lambda shape: pl.BlockSpec(shape, lambda b_, t_: (0,) * len(shape))
    return pl.pallas_call(
        _proj_even_kernel,
        out_shape=(jax.ShapeDtypeStruct((b, t, n), BF16), jax.ShapeDtypeStruct((b, t, ngl), F32)),
        grid=(b, t // TM),
        in_specs=[pl.BlockSpec((None, TM, d), lambda b_, t_: (b_, t_, 0)),
                  const((1, d)), _tile_mod_spec(d), _tile_mod_spec(d),
                  const(w_main.shape), const(w_gate.shape), const(conv_w.shape),
                  const(rate.shape), const(gbias.shape)],
        out_specs=(pl.BlockSpec((None, TM, n), lambda b_, t_: (b_, t_, 0)),
                   pl.BlockSpec((None, TM, ngl), lambda b_, t_: (b_, t_, 0))),
        compiler_params=_cparams(("arbitrary", "arbitrary")),
        name="proj_even",
    )(x, g, sh, sc, w_main, w_gate, conv_w, rate, gbias)


def _rev_tile(i, nt):
    return jnp.where(i == 0, 0, nt - i)


def _tri_masks():
    r = lax.broadcasted_iota(jnp.int32, (CH, CH), 0)
    c = lax.broadcasted_iota(jnp.int32, (CH, CH), 1)
    return r >= c, r > c, r <= c, r < c


def _cumsum_both(gc, gr, d, lower, upper):
    lo = lower.astype(F32)
    up = upper.astype(F32)
    if d == 0:
        return _dot(lo, gc, precision=HIGHEST), _dot(gr, up, precision=HIGHEST)
    return _dot(up, gc, precision=HIGHEST), _dot(gr, lo, precision=HIGHEST)


def _gdn_kernel(qf, kf, vf, qb, kb, vb, gcf, grf, gcb, grb, of, ob, s_ref):
    i = pl.program_id(1)

    @pl.when(i == 0)
    def _():
        s_ref[...] = jnp.zeros_like(s_ref)

    lower, lstrict, upper, ustrict = _tri_masks()
    refs = ((qf, kf, vf, gcf, grf, of), (qb, kb, vb, gcb, grb, ob))
    ng = N_DIR * GDN_H

    def chunk_body(cc, carry):
        outs, states, row_sl = [], [], []
        for d in range(N_DIR):
            q_ref, k_ref, v_ref, gc_ref, gr_ref, o_ref = refs[d]
            c = cc if d == 0 else CPB - 1 - cc
            rows = pl.ds(pl.multiple_of(c * CH, CH), CH)
            row_sl.append(rows)
            gc = gc_ref[c]
            gr = gr_ref[c]
            cs_c, cs_r = _cumsum_both(gc, gr, d, lower, upper)
            incl = lower if d == 0 else upper
            strict = lstrict if d == 0 else ustrict
            heads = []
            for hh in range(GDN_H):
                ci = d * GDN_H + hh
                cols = slice(hh * HD, (hh + 1) * HD)
                g_col = cs_c[:, ci:ci + 1]
                g_row = cs_r[ci:ci + 1, :]
                beta = gc[:, ng + ci:ng + ci + 1]
                tot = g_col[CH - 1:CH, :] if d == 0 else g_col[0:1, :]
                decay = jnp.where(incl, jnp.exp(jnp.where(incl, g_col - g_row, 0.0)), 0.0)
                q = q_ref[rows, cols].astype(F32)
                k = k_ref[rows, cols].astype(F32)
                v = v_ref[rows, cols].astype(F32)
                kbeta = k * beta
                kbf = k.astype(BF16)
                a = jnp.where(strict, _dot_nt(kbeta.astype(BF16), kbf) * decay, 0.0)
                eg = jnp.exp(g_col)
                x = jnp.concatenate([v * beta, kbeta * eg], axis=1)
                p = -a
                for j in range(6):
                    pb = p.astype(BF16)
                    x = x + _dot(pb, x.astype(BF16))
                    if j < 5:
                        p = _dot(pb, pb)
                u = x[:, :HD]
                w = x[:, HD:]
                qk = _dot_nt(q.astype(BF16), kbf) * decay
                s = s_ref[ci]
                sb = s.astype(BF16)
                v_new = u - _dot(w.astype(BF16), sb)
                vnb = v_new.astype(BF16)
                o = _dot((q * eg).astype(BF16), sb) + _dot(qk.astype(BF16), vnb)
                k_dec = k * jnp.exp(tot - g_col)
                states.append(s * jnp.exp(tot) + _dot_tn(k_dec.astype(BF16), vnb))
                heads.append(o.astype(o_ref.dtype))
            outs.append(jnp.concatenate(heads, axis=1))
        for ci, s_new in enumerate(states):
            s_ref[ci] = s_new
        for d in range(N_DIR):
            refs[d][-1][row_sl[d], :] = outs[d]
        return carry

    lax.fori_loop(0, CPB, chunk_body, 0)


def _scan_specs(nt, width, col):
    fwd = pl.BlockSpec((None, TM, width), lambda b, i: (b, i, col))
    bwd = pl.BlockSpec((None, TM, width), lambda b, i: (b, _rev_tile(i, nt), col))
    return fwd, bwd


def _gate_specs(nt, shape):
    fwd = pl.BlockSpec((None, CPB) + shape, lambda b, i: (b, i, 0, 0))
    bwd = pl.BlockSpec((None, CPB) + shape, lambda b, i: (b, _rev_tile(i, nt), 0, 0))
    return fwd, bwd


def _gdn_scan(main, gates_c, gates_r):
    b, t, _ = main.shape
    nt = t // TM
    w = GDN_H * HD
    ngl = gates_c.shape[-1]
    qf, qb = _scan_specs(nt, w, 0)
    kf, kb = _scan_specs(nt, w, 1)
    vf, vb = _scan_specs(nt, w, 2)
    gcf, gcb = _gate_specs(nt, (CH, ngl))
    grf, grb = _gate_specs(nt, (ngl, CH))
    of, ob = _scan_specs(nt, w, 0)
    return pl.pallas_call(
        _gdn_kernel,
        out_shape=(jax.ShapeDtypeStruct((b, t, w), BF16), jax.ShapeDtypeStruct((b, t, w), BF16)),
        grid=(b, nt),
        in_specs=[qf, kf, vf, qb, kb, vb, gcf, grf, gcb, grb],
        out_specs=(of, ob),
        scratch_shapes=[pltpu.VMEM((N_DIR * GDN_H, HD, HD), F32)],
        compiler_params=_cparams(("arbitrary", "arbitrary")),
        name="gdn_scan",
    )(main, main, main, main, main, main, gates_c, gates_r, gates_c, gates_r)


def _mlstm_kernel(qf, kf, vf, qb, kb, vb, gcf, grf, gcb, grb, of, ob, c_ref, m_ref):
    i = pl.program_id(1)

    @pl.when(i == 0)
    def _():
        c_ref[...] = jnp.zeros_like(c_ref)
        m_ref[...] = jnp.zeros_like(m_ref)

    lower, _, upper, _ = _tri_masks()
    refs = ((qf, kf, vf, gcf, grf, of), (qb, kb, vb, gcb, grb, ob))
    ng = N_DIR * GDN_H
    i_off = 2 * ng
    f_off = 2 * ng + N_DIR * ML_H
    lane = lax.broadcasted_iota(jnp.int32, (CH, HD), 1)
    ones_col = jnp.where(lane == 0, 1.0, 0.0).astype(BF16)

    def chunk_body(cc, carry):
        for d in range(N_DIR):
            q_ref, k_ref, v_ref, gc_ref, gr_ref, o_ref = refs[d]
            c = cc if d == 0 else CPB - 1 - cc
            rows = pl.ds(pl.multiple_of(c * CH, CH), CH)
            gc = gc_ref[c]
            gr = gr_ref[c]
            cs_c, cs_r = _cumsum_both(gc, gr, d, lower, upper)
            incl = lower if d == 0 else upper
            for hh in range(ML_H):
                ci = d * ML_H + hh
                cols = slice(hh * HD, (hh + 1) * HD)
                b_col = cs_c[:, f_off + ci:f_off + ci + 1]
                b_row = cs_r[f_off + ci:f_off + ci + 1, :]
                i_col = gc[:, i_off + ci:i_off + ci + 1]
                i_row = gr[i_off + ci:i_off + ci + 1, :]
                b_tot = b_col[CH - 1:CH, :] if d == 0 else b_col[0:1, :]
                m_row = m_ref[ci]
                m = m_row[:, 0:1]
                d_in = jnp.where(incl, b_col - b_row + i_row, NEG)
                d_carry = b_col + m
                m_t = jnp.maximum(d_carry, jnp.max(d_in, axis=-1, keepdims=True))
                q = q_ref[rows, cols]
                k = k_ref[rows, cols]
                v = v_ref[rows, cols]
                v_aug = jnp.concatenate([v, ones_col], axis=1)
                s = _dot_nt(q, k) * jnp.exp(d_in - m_t)
                w_carry = jnp.exp(d_carry - m_t)
                cst = c_ref[ci]
                num_den = w_carry * _dot(q, cst.astype(BF16)) + _dot(s.astype(BF16), v_aug)
                den = num_den[:, HD:HD + 1]
                hout = num_den[:, :HD] / jnp.maximum(jnp.abs(den), jnp.exp(-m_t))
                d_end_c = b_tot - b_col + i_col
                d_end_r = b_tot - b_row + i_row
                carry_end = b_tot + m
                m_new = jnp.maximum(carry_end, jnp.max(d_end_r, axis=-1, keepdims=True))
                w_end = jnp.exp(d_end_c - m_new)
                f_end = jnp.exp(carry_end - m_new)
                kw = (k.astype(F32) * w_end).astype(BF16)
                c_ref[ci] = f_end * cst + _dot_tn(kw, v_aug)
                m_ref[ci] = jnp.broadcast_to(m_new, m_row.shape)
                o_ref[rows, cols] = hout.astype(o_ref.dtype)
        return carry

    lax.fori_loop(0, CPB, chunk_body, 0)


def _mlstm_scan(main, gates_c, gates_r):
    b, t, _ = main.shape
    nt = t // TM
    w = ML_H * HD
    ngl = gates_c.shape[-1]
    qf, qb = _scan_specs(nt, w, 4)
    kf, kb = _scan_specs(nt, w, 5)
    vf, vb = _scan_specs(nt, w, 6)
    gcf, gcb = _gate_specs(nt, (CH, ngl))
    grf, grb = _gate_specs(nt, (ngl, CH))
    of, ob = _scan_specs(nt, w, 0)
    return pl.pallas_call(
        _mlstm_kernel,
        out_shape=(jax.ShapeDtypeStruct((b, t, w), BF16), jax.ShapeDtypeStruct((b, t, w), BF16)),
        grid=(b, nt),
        in_specs=[qf, kf, vf, qb, kb, vb, gcf, grf, gcb, grb],
        out_specs=(of, ob),
        scratch_shapes=[pltpu.VMEM((N_DIR * ML_H, HD, 2 * HD), F32),
                        pltpu.VMEM((N_DIR * ML_H, 1, HD), F32)],
        compiler_params=_cparams(("arbitrary", "arbitrary")),
        name="mlstm_scan",
    )(main, main, main, main, main, main, gates_c, gates_r, gates_c, gates_r)


def _ret_kernel(qf, kf, vf, qb, kb, vb, intra_ref, cross_ref, tail_ref, cd_ref, of, ob, s_ref):
    i = pl.program_id(1)

    @pl.when(i == 0)
    def _():
        s_ref[...] = jnp.zeros_like(s_ref)

    refs = ((qf, kf, vf, of), (qb, kb, vb, ob))

    def chunk_body(cc, carry):
        for d in range(N_DIR):
            q_ref, k_ref, v_ref, o_ref = refs[d]
            c = cc if d == 0 else CPB - 1 - cc
            rows = pl.ds(pl.multiple_of(c * CH, CH), CH)
            for hh in range(RET_H):
                si = d * RET_H + hh
                kcols = slice(hh * HD, (hh + 1) * HD)
                vcols = slice(hh * RET_DV, (hh + 1) * RET_DV)
                q = q_ref[rows, kcols]
                k = k_ref[rows, kcols]
                v = v_ref[rows, vcols]
                s = s_ref[si]
                scores = _dot_nt(q, k) * intra_ref[si]
                qc = (q.astype(F32) * cross_ref[si]).astype(BF16)
                o = _dot(scores.astype(BF16), v) + _dot(qc, s.astype(BF16))
                kt = (k.astype(F32) * tail_ref[si]).astype(BF16)
                s_ref[si] = cd_ref[si] * s + _dot_tn(kt, v)
                o_ref[rows, vcols] = o.astype(o_ref.dtype)
        return carry

    lax.fori_loop(0, CPB, chunk_body, 0)


def _ret_tables():
    pos = np.arange(CH, dtype=np.float64)
    intra, cross, tail, cd = [], [], [], []
    for d in range(N_DIR):
        expo = 5.0 + np.arange(RET_H, dtype=np.float64)
        if d == 1:
            expo = expo[::-1]
        lg = np.log1p(-np.exp2(-expo))
        p = pos if d == 0 else (CH - 1.0 - pos)
        diff = p[:, None] - p[None, :]
        for hh in range(RET_H):
            intra.append(np.where(diff >= 0, np.exp(np.where(diff >= 0, diff, 0.0) * lg[hh]), 0.0))
            cross.append(np.broadcast_to(np.exp((p + 1.0) * lg[hh])[:, None], (CH, HD)))
            tail.append(np.broadcast_to(np.exp((CH - 1.0 - p) * lg[hh])[:, None], (CH, HD)))
            cd.append(np.full((1, RET_DV), np.exp(CH * lg[hh])))
    f = lambda a: jnp.asarray(np.stack(a), F32)
    return f(intra), f(cross), f(tail), f(cd)


def _ret_scan(main):
    b, t, _ = main.shape
    nt = t // TM
    qw = RET_H * HD
    vw = RET_H * RET_DV
    qf, qb = _scan_specs(nt, qw, 0)
    kf, kb = _scan_specs(nt, qw, 1)
    vf, vb = _scan_specs(nt, vw, 1)
    of, ob = _scan_specs(nt, vw, 0)
    tabs = _ret_tables()
    const = lambda a: pl.BlockSpec(a.shape, lambda b_, i_: (0,) * a.ndim)
    return pl.pallas_call(
        _ret_kernel,
        out_shape=(jax.ShapeDtypeStruct((b, t, vw), BF16), jax.ShapeDtypeStruct((b, t, vw), BF16)),
        grid=(b, nt),
        in_specs=[qf, kf, vf, qb, kb, vb] + [const(a) for a in tabs],
        out_specs=(of, ob),
        scratch_shapes=[pltpu.VMEM((N_DIR * RET_H, HD, RET_DV), F32)],
        compiler_params=_cparams(("arbitrary", "arbitrary")),
        name="ret_scan",
    )(main, main, main, main, main, main, *tabs)


def _route_and_pack(h2, rw_ref, rb_ref, h2p_ref, idx_ref, wgt_ref):
    h2b = h2.astype(BF16)
    logits = _dot(h2b, rw_ref[...]) + rb_ref[...]
    lane = lax.broadcasted_iota(jnp.int32, logits.shape, 1)
    lane_f = lane.astype(F32)
    vals, idxs = [], []
    cur = logits
    for _ in range(TOP_K):
        m = jnp.max(cur, axis=-1, keepdims=True)
        ix = jnp.min(jnp.where(cur == m, lane_f, float(logits.shape[1])), axis=-1, keepdims=True)
        vals.append(m)
        idxs.append(ix)
        cur = jnp.where(lane_f == ix, NEG, cur)
    es = [jnp.exp(v - vals[0]) for v in vals]
    tot = es[0] + es[1] + es[2] + es[3]
    iout = jnp.zeros(logits.shape, F32)
    wout = jnp.zeros(logits.shape, F32)
    for j in range(TOP_K):
        iout = jnp.where(lane == j, idxs[j], iout)
        wout = jnp.where(lane == j, es[j] / tot, wout)
    idx_ref[...] = iout.astype(jnp.int32)
    wgt_ref[...] = wout
    half = h2.shape[1] // 2
    r = h2b.astype(F32)
    lo = lax.shift_right_logical(pltpu.bitcast(r[:, :half], U32), jnp.uint32(16))
    hi = pltpu.bitcast(r[:, half:], U32) & jnp.uint32(0xFFFF0000)
    h2p_ref[...] = hi | lo


def _merge_even_kernel(ogf, ogb, omf, omb, z_ref, mo_ref, gg_ref, mg_ref, wo_ref, x_ref, g1_ref,
                       n2_ref, sh_ref, sc_ref, rw_ref, rb_ref, x1_ref, h2p_ref, idx_ref, wgt_ref):
    ones_blk = jnp.ones((HD, HD), BF16)
    og = ogf[...].astype(F32) + ogb[...].astype(F32)
    ms = _group_sum(og * og, ones_blk) * (1.0 / HD)
    a = og * lax.rsqrt(ms + EPS) * gg_ref[...] * _silu(z_ref[...].astype(F32))
    om = omf[...].astype(F32) + omb[...].astype(F32)
    ms = _group_sum(om * om, ones_blk) * (1.0 / HD)
    m = om * lax.rsqrt(ms + EPS) * mg_ref[...] * _sigmoid(mo_ref[...].astype(F32))
    cat = jnp.concatenate([a, m], axis=1).astype(BF16)
    y = _dot(cat, wo_ref[...])
    x1 = x_ref[...] + g1_ref[...] * y
    x1_ref[...] = x1
    h2 = _rms_mod(x1, n2_ref[...], sh_ref[...], sc_ref[...])
    _route_and_pack(h2, rw_ref, rb_ref, h2p_ref, idx_ref, wgt_ref)


def _merge_odd_kernel(of, ob, gate_ref, ng_ref, wo_ref, x_ref, g1_ref,
                      n2_ref, sh_ref, sc_ref, rw_ref, rb_ref, x1_ref, h2p_ref, idx_ref, wgt_ref):
    ones_blk = jnp.ones((RET_DV, RET_DV), BF16)
    o = of[...].astype(F32) + ob[...].astype(F32)
    o = o - _group_sum(o, ones_blk) * (1.0 / RET_DV)
    ms = _group_sum(o * o, ones_blk) * (1.0 / RET_DV)
    y = o * lax.rsqrt(ms + EPS) * ng_ref[...] * _silu(gate_ref[...].astype(F32))
    y = _dot(y.astype(BF16), wo_ref[...])
    x1 = x_ref[...] + g1_ref[...] * y
    x1_ref[...] = x1
    h2 = _rms_mod(x1, n2_ref[...], sh_ref[...], sc_ref[...])
    _route_and_pack(h2, rw_ref, rb_ref, h2p_ref, idx_ref, wgt_ref)


def _merge_out(b, t, d):
    shapes = (jax.ShapeDtypeStruct((b, t, d), F32), jax.ShapeDtypeStruct((b, t, d // 2), U32),
              jax.ShapeDtypeStruct((b, t, 128), jnp.int32), jax.ShapeDtypeStruct((b, t, 128), F32))
    specs = tuple(pl.BlockSpec((None, TM, s.shape[-1]), lambda b_, t_: (b_, t_, 0)) for s in shapes)
    return shapes, specs


def _merge_even(og_f, og_b, om_f, om_b, main, gdn_g, ml_g, w_out, x, g1, n2, sh2, sc2, rw, rb):
    b, t, d = x.shape
    w = GDN_H * HD
    tile = lambda width, col: pl.BlockSpec((None, TM, width), lambda b_, t_: (b_, t_, col))
    const = lambda a: pl.BlockSpec(a.shape, lambda b_, t_: (0,) * a.ndim)
    shapes, specs = _merge_out(b, t, d)
    return pl.pallas_call(
        _merge_even_kernel,
        out_shape=shapes,
        grid=(b, t // TM),
        in_specs=[tile(w, 0), tile(w, 0), tile(w, 0), tile(w, 0), tile(w, 3), tile(w, 7),
                  const(gdn_g), const(ml_g), const(w_out), tile(d, 0), _tile_mod_spec(d),
                  const(n2), _tile_mod_spec(d), _tile_mod_spec(d), const(rw), const(rb)],
        out_specs=specs,
        compiler_params=_cparams(("arbitrary", "arbitrary")),
        name="merge_even",
    )(og_f, og_b, om_f, om_b, main, main, gdn_g, ml_g, w_out, x, g1, n2, sh2, sc2, rw, rb)


def _merge_odd(o_f, o_b, main, ret_g, w_out, x, g1, n2, sh2, sc2, rw, rb):
    b, t, d = x.shape
    vw = RET_H * RET_DV
    tile = lambda width, col: pl.BlockSpec((None, TM, width), lambda b_, t_: (b_, t_, col))
    const = lambda a: pl.BlockSpec(a.shape, lambda b_, t_: (0,) * a.ndim)
    shapes, specs = _merge_out(b, t, d)
    return pl.pallas_call(
        _merge_odd_kernel,
        out_shape=shapes,
        grid=(b, t // TM),
        in_specs=[tile(vw, 0), tile(vw, 0), tile(vw, 2), const(ret_g), const(w_out), tile(d, 0),
                  _tile_mod_spec(d), const(n2), _tile_mod_spec(d), _tile_mod_spec(d), const(rw), const(rb)],
        out_specs=specs,
        compiler_params=_cparams(("arbitrary", "arbitrary")),
        name="merge_odd",
    )(o_f, o_b, main, ret_g, w_out, x, g1, n2, sh2, sc2, rw, rb)


def _proj_odd_kernel(x_ref, acc_ref, g2_ref, g_ref, sh_ref, sc_ref, w_ref, cos_ref, sin_ref,
                     x2_ref, main_ref):
    x2 = x_ref[...] + g2_ref[...] * acc_ref[...]
    x2_ref[...] = x2
    hb = _rms_mod(x2, g_ref[...], sh_ref[...], sc_ref[...]).astype(BF16)
    qk_w = RET_H * HD
    cos = cos_ref[...]
    sin = sin_ref[...]
    n_seg = w_ref.shape[1] // qk_w
    for seg in range(n_seg):
        sl = slice(seg * qk_w, (seg + 1) * qk_w)
        u = _dot(hb, w_ref[:, sl])
        if seg < 2:
            parts = []
            for hh in range(RET_H):
                uh = u[:, hh * HD:(hh + 1) * HD]
                parts.append(uh * cos + pltpu.roll(uh, HD // 2, 1) * sin)
            u = jnp.concatenate(parts, axis=1)
            if seg == 1:
                u = u * (HD ** -0.5)
        main_ref[:, sl] = u.astype(BF16)


def _rope_tables(t):
    half = HD // 2
    freqs = ROPE_BASE ** (-jnp.arange(half, dtype=F32) / half)
    ang = jnp.arange(t, dtype=F32)[:, None] * freqs[None, :]
    cos, sin = jnp.cos(ang), jnp.sin(ang)
    return jnp.concatenate([cos, cos], axis=1), jnp.concatenate([-sin, sin], axis=1)


def _proj_odd(x, acc, g2, g, sh, sc, w):
    b, t, d = x.shape
    n = w.shape[1]
    cos, sin = _rope_tables(t)
    const = lambda a: pl.BlockSpec(a.shape, lambda b_, t_: (0,) * a.ndim)
    tile = lambda width: pl.BlockSpec((None, TM, width), lambda b_, t_: (b_, t_, 0))
    rope = pl.BlockSpec((TM, HD), lambda b_, t_: (t_, 0))
    return pl.pallas_call(
        _proj_odd_kernel,
        out_shape=(jax.ShapeDtypeStruct((b, t, d), F32), jax.ShapeDtypeStruct((b, t, n), BF16)),
        grid=(b, t // TM),
        in_specs=[tile(d), tile(d), _tile_mod_spec(d), const(g), _tile_mod_spec(d), _tile_mod_spec(d),
                  const(w), rope, rope],
        out_specs=(tile(d), tile(n)),
        compiler_params=_cparams(("arbitrary", "arbitrary")),
        name="proj_odd",
    )(x, acc, g2, g, sh, sc, w, cos, sin)


ROWS_PER_STEP = 8


def _moe_gather_kernel(src_ref, h_ref, xs_ref):
    def body(r8, carry):
        base = pl.multiple_of(r8 * ROWS_PER_STEP, ROWS_PER_STEP)
        rows = [h_ref[pl.ds(src_ref[0, base + j], 1), :] for j in range(ROWS_PER_STEP)]
        xs_ref[pl.ds(base, ROWS_PER_STEP), :] = jnp.concatenate(rows, axis=0)
        return carry

    lax.fori_loop(0, MOE_BLOCK // ROWS_PER_STEP, body, 0)


def _moe_gather(row_src, h2p, n_blocks):
    n, half = h2p.shape
    return pl.pallas_call(
        _moe_gather_kernel,
        out_shape=jax.ShapeDtypeStruct((n_blocks * MOE_BLOCK, half), U32),
        grid=(n_blocks,),
        in_specs=[pl.BlockSpec((None, 1, MOE_BLOCK), lambda i: (i, 0, 0), memory_space=pltpu.SMEM),
                  pl.BlockSpec((n, half), lambda i: (0, 0))],
        out_specs=pl.BlockSpec((MOE_BLOCK, half), lambda i: (i, 0)),
        compiler_params=_cparams(("arbitrary",)),
        name="moe_gather",
    )(row_src.reshape(n_blocks, 1, MOE_BLOCK), h2p)


def _moe_mm_kernel(be_ref, xs_ref, rw_ref, wgu_ref, bgu_ref, wdn_ref, bdn_ref, ys_ref, wgu_s, wdn_s):
    i = pl.program_id(0)
    e = be_ref[i]
    e_prev = be_ref[jnp.maximum(i - 1, 0)]

    @pl.when((i == 0) | (e != e_prev))
    def _():
        wgu_s[...] = wgu_ref[...].astype(BF16)
        bits = pltpu.bitcast(wdn_ref[...].astype(BF16).astype(F32), U32)
        both = (bits & jnp.uint32(0xFFFF0000)) | lax.shift_right_logical(bits, jnp.uint32(16))
        wdn_s[...] = pltpu.bitcast(both, BF16)

    xu = xs_ref[...]
    half = xu.shape[1]
    lo = pltpu.bitcast(lax.shift_left(xu, jnp.uint32(16)), F32).astype(BF16)
    hi = pltpu.bitcast(xu & jnp.uint32(0xFFFF0000), F32).astype(BF16)
    gu = _dot(lo, wgu_s[:half, :]) + _dot(hi, wgu_s[half:, :]) + bgu_ref[...]
    nxt = jnp.concatenate([pltpu.roll(gu[:, c * 128:(c + 1) * 128], 127, 1)
                           for c in range(gu.shape[1] // 128)], axis=1)
    gate = jnp.minimum(gu, SWIGLU_LIMIT)
    up = jnp.clip(nxt, -SWIGLU_LIMIT, SWIGLU_LIMIT)
    act = (up + 1.0) * gate * _sigmoid(SWIGLU_ALPHA * gate)
    lane = lax.broadcasted_iota(jnp.int32, act.shape, 1)
    act = jnp.where((lane & 1) == 0, act, 0.0)
    y = _dot(act.astype(BF16), wdn_s[...]) + bdn_ref[...]
    ys_ref[...] = y * rw_ref[...]


def _moe_mm(block_e, xs, row_w, w_gu, b_gu, w_dn, b_dn):
    rows, half = xs.shape
    n_blocks = rows // MOE_BLOCK
    n_e, d, f2 = w_gu.shape
    ew = lambda shape: pl.BlockSpec((None,) + shape, lambda i, be: (be[i], 0, 0))
    grid_spec = pltpu.PrefetchScalarGridSpec(
        num_scalar_prefetch=1,
        grid=(n_blocks,),
        in_specs=[pl.BlockSpec((MOE_BLOCK, half), lambda i, be: (i, 0)),
                  pl.BlockSpec((MOE_BLOCK, 1), lambda i, be: (i, 0)),
                  ew((d, f2)), ew((1, f2)), ew((f2 // 2, d)), ew((1, d))],
        out_specs=pl.BlockSpec((MOE_BLOCK, d), lambda i, be: (i, 0)),
        scratch_shapes=[pltpu.VMEM((d, f2), BF16), pltpu.VMEM((f2, d), BF16)],
    )
    return pl.pallas_call(
        _moe_mm_kernel,
        out_shape=jax.ShapeDtypeStruct((rows, d), F32),
        grid_spec=grid_spec,
        compiler_params=_cparams(("arbitrary",)),
        name="moe_mm",
    )(block_e, xs, row_w.reshape(rows, 1), w_gu, b_gu.reshape(n_e, 1, f2), w_dn, b_dn.reshape(n_e, 1, d))


COMBINE_COLS = 256


def _moe_combine_kernel(dst_ref, ys_ref, acc_ref):
    i = pl.program_id(1)

    @pl.when(i == 0)
    def _():
        acc_ref[...] = jnp.zeros_like(acc_ref)

    def body(r8, carry):
        base = pl.multiple_of(r8 * ROWS_PER_STEP, ROWS_PER_STEP)
        y8 = ys_ref[pl.ds(base, ROWS_PER_STEP), :]
        toks = [dst_ref[0, base + j] for j in range(ROWS_PER_STEP)]
        cur = [acc_ref[pl.ds(tk, 1), :] for tk in toks]
        for j in range(ROWS_PER_STEP):
            acc_ref[pl.ds(toks[j], 1), :] = cur[j] + y8[j:j + 1, :]
        return carry

    lax.fori_loop(0, MOE_BLOCK // ROWS_PER_STEP, body, 0)


def _moe_combine(row_dst, ys, n_rows_out):
    rows, d = ys.shape
    n_blocks = rows // MOE_BLOCK
    return pl.pallas_call(
        _moe_combine_kernel,
        out_shape=jax.ShapeDtypeStruct((n_rows_out, d), F32),
        grid=(d // COMBINE_COLS, n_blocks),
        in_specs=[pl.BlockSpec((None, 1, MOE_BLOCK), lambda j, i: (i, 0, 0), memory_space=pltpu.SMEM),
                  pl.BlockSpec((MOE_BLOCK, COMBINE_COLS), lambda j, i: (i, j))],
        out_specs=pl.BlockSpec((n_rows_out, COMBINE_COLS), lambda j, i: (0, j)),
        compiler_params=_cparams(("arbitrary", "arbitrary")),
        name="moe_combine",
    )(row_dst.reshape(n_blocks, 1, MOE_BLOCK), ys)


def _moe(h2p, top_idx, top_w, w_gu, b_gu, w_dn, b_dn):
    n = h2p.shape[0]
    n_assign = n * TOP_K
    n_blocks = -(-n_assign // MOE_BLOCK) + N_EXPERTS
    rows = n_blocks * MOE_BLOCK
    flat_e = top_idx.reshape(-1)
    order = jnp.argsort(flat_e)
    sorted_e = flat_e[order]
    counts = jnp.bincount(flat_e, length=N_EXPERTS)
    padded = -(-counts // MOE_BLOCK) * MOE_BLOCK
    ends = jnp.cumsum(padded)
    dest = (ends - padded)[sorted_e] + jnp.arange(n_assign) - (jnp.cumsum(counts) - counts)[sorted_e]
    tok = (order // TOP_K).astype(jnp.int32)
    row_src = jnp.zeros((rows,), jnp.int32).at[dest].set(tok)
    row_dst = jnp.full((rows,), n, jnp.int32).at[dest].set(tok)
    row_w = jnp.zeros((rows,), F32).at[dest].set(top_w.reshape(-1)[order])
    block_e = jnp.minimum(jnp.searchsorted(ends, jnp.arange(n_blocks) * MOE_BLOCK, side='right'),
                          N_EXPERTS - 1).astype(jnp.int32)
    xs = _moe_gather(row_src, h2p, n_blocks)
    ys = _moe_mm(block_e, xs, row_w, w_gu, b_gu, w_dn, b_dn)
    return _moe_combine(row_dst, ys, n + 8)


def _final_kernel(x_ref, acc_ref, g2_ref, g_ref, o_ref):
    x = x_ref[...] + g2_ref[...] * acc_ref[...]
    ms = jnp.mean(x * x, axis=-1, keepdims=True)
    o_ref[...] = x * lax.rsqrt(ms + EPS) * g_ref[...]


def _final(x, acc, g2, g, n_ctx_tiles):
    b, t, d = x.shape
    nt = t // TM - n_ctx_tiles
    tile_in = pl.BlockSpec((None, TM, d), lambda b_, t_: (b_, t_ + n_ctx_tiles, 0))
    return pl.pallas_call(
        _final_kernel,
        out_shape=jax.ShapeDtypeStruct((b, nt * TM, d), F32),
        grid=(b, nt),
        in_specs=[tile_in, tile_in,
                  pl.BlockSpec((None, None, 1, d), lambda b_, t_: (b_, 1, 0, 0)),
                  pl.BlockSpec((1, d), lambda b_, t_: (0, 0))],
        out_specs=pl.BlockSpec((None, TM, d), lambda b_, t_: (b_, t_, 0)),
        compiler_params=_cparams(("arbitrary", "arbitrary")),
        name="final_norm",
    )(x, acc, g2, g)


def _mod_tables(mod, b, d):
    outs = []
    for j in range(6):
        m = mod[:, j * d:(j + 1) * d]
        lat = m[:b]
        ctx = jnp.broadcast_to(m[b:b + 1], (b, d))
        outs.append(jnp.stack([ctx, lat], axis=1)[:, :, None, :])
    return outs


def _router_params(router_w, router_b):
    d, e = router_w.shape
    rw = jnp.zeros((d, 128), F32).at[:, :e].set(router_w).astype(BF16)
    rb = jnp.full((1, 128), NEG, F32).at[0, :e].set(router_b)
    return rw, rb


def kernel(x, c, ctx, c_ctx, mod_w, mod_b, norm1_g, norm2_g, ev_w_in, ev_conv_w, gdn_a_log, gdn_dt_bias,
           gdn_norm_g, ml_i_bias, ml_f_bias, ml_norm_g, ev_w_out, od_w_in, ret_norm_g, od_w_out,
           router_w, router_b, moe_w_gu, moe_b_gu, moe_w_dn, moe_b_dn, final_g):
    b, s, d = x.shape
    n_ctx = ctx.shape[1]
    depth = mod_w.shape[0]
    assert n_ctx == TM and s % TM == 0 and depth == 2
    t = n_ctx + s
    n_tok = b * t

    cond = jnp.concatenate([c, c_ctx[None, :], jnp.zeros((8 - b - 1, d), F32)], axis=0)
    mod = _adaln(cond, mod_w, mod_b)
    xa = jnp.concatenate([ctx, x], axis=1)

    sh1, sc1, g1, sh2, sc2, g2 = _mod_tables(mod[0], b, d)
    qk_w = GDN_H * HD
    conv_ch = 3 * qk_w
    ng = N_DIR * GDN_H
    w_in = ev_w_in[0]
    o_z = conv_ch
    o_a = o_z + qk_w
    o_mq = o_a + 2 * ng
    o_i = o_mq + 4 * qk_w
    w_main = jnp.concatenate([w_in[:, :o_a], w_in[:, o_mq:o_i]], axis=1).astype(BF16)
    w_gate = jnp.concatenate([w_in[:, o_a:o_mq], w_in[:, o_i:o_i + 2 * ng],
                              jnp.zeros((d, 128 - 4 * ng), F32)], axis=1).astype(BF16)
    zeros_g = jnp.zeros((ng,), F32)
    rate = jnp.concatenate([jnp.exp(gdn_a_log[0].astype(F32)).reshape(-1), jnp.zeros((128 - ng,), F32)])[None, :]
    gbias = jnp.concatenate([gdn_dt_bias[0].reshape(-1), zeros_g, ml_i_bias[0].reshape(-1),
                             ml_f_bias[0].reshape(-1), jnp.zeros((128 - 4 * ng,), F32)])[None, :].astype(F32)
    main, gates = _proj_even(xa, norm1_g[0][None, :], sh1, sc1, w_main, w_gate, ev_conv_w[0], rate, gbias)
    gates_c = gates.reshape(b, t // CH, CH, 4 * ng)
    gates_r = jnp.swapaxes(gates_c, 2, 3)
    og_f, og_b = _gdn_scan(main, gates_c, gates_r)
    om_f, om_b = _mlstm_scan(main, gates_c, gates_r)
    rw, rb = _router_params(router_w[0], router_b[0])
    gdn_g = jnp.tile(gdn_norm_g[0], GDN_H)[None, :]
    x1, h2p, top_idx, top_w = _merge_even(og_f, og_b, om_f, om_b, main, gdn_g, ml_norm_g[0][None, :],
                                          ev_w_out[0].astype(BF16), xa, g1, norm2_g[0][None, :], sh2, sc2, rw, rb)
    acc = _moe(h2p.reshape(n_tok, d // 2), top_idx.reshape(n_tok, 128)[:, :TOP_K],
               top_w.reshape(n_tok, 128)[:, :TOP_K],
               moe_w_gu[0], moe_b_gu[0], moe_w_dn[0], moe_b_dn[0])
    acc = acc[:n_tok].reshape(b, t, d)
    g2_prev = g2

    sh1, sc1, g1, sh2, sc2, g2 = _mod_tables(mod[1], b, d)
    x2, main_o = _proj_odd(x1, acc, g2_prev, norm1_g[1][None, :], sh1, sc1, od_w_in[0].astype(BF16))
    o_f, o_b = _ret_scan(main_o)
    rw, rb = _router_params(router_w[1], router_b[1])
    x3, h2p, top_idx, top_w = _merge_odd(o_f, o_b, main_o, ret_norm_g[0][None, :], od_w_out[0].astype(BF16),
                                         x2, g1, norm2_g[1][None, :], sh2, sc2, rw, rb)
    acc = _moe(h2p.reshape(n_tok, d // 2), top_idx.reshape(n_tok, 128)[:, :TOP_K],
               top_w.reshape(n_tok, 128)[:, :TOP_K],
               moe_w_gu[1], moe_b_gu[1], moe_w_dn[1], moe_b_dn[1])
    acc = acc[:n_tok].reshape(b, t, d)
    return _final(x3, acc, g2, final_g[None, :], n_ctx // TM)
```

```python
import functools
import math

import jax
import jax.numpy as jnp
import numpy as np
from jax import lax
from jax.experimental import pallas as pl
from jax.experimental.pallas import tpu as pltpu

F32 = jnp.float32
BF16 = jnp.bfloat16
U32 = jnp.uint32
HIGHEST = lax.Precision.HIGHEST

EPS = 1e-6
CH = 64
TM = 256
CPB = TM // CH
HD = 128
N_DIR = 2
GDN_H = 4
ML_H = 4
RET_H = 8
RET_DV = 256
CONV_W = 3
N_EXPERTS = 32
TOP_K = 4
SWIGLU_ALPHA = 1.702
SWIGLU_LIMIT = 7.0
MOE_BLOCK = 256
ROPE_BASE = 10000.0
NEG = -1e30
VMEM_LIMIT = 56 * 1024 * 1024


def _cparams(sem):
    return pltpu.CompilerParams(dimension_semantics=sem, vmem_limit_bytes=VMEM_LIMIT)


def _dot(a, b, precision=None):
    return jnp.dot(a, b, preferred_element_type=F32, precision=precision)


def _dot_nt(a, b):
    return lax.dot_general(a, b, (((1,), (1,)), ((), ())), preferred_element_type=F32)


def _dot_tn(a, b):
    return lax.dot_general(a, b, (((0,), (0,)), ((), ())), preferred_element_type=F32)


def _sigmoid(x):
    return 1.0 / (1.0 + jnp.exp(-x))


def _silu(x):
    return x * _sigmoid(x)


def _group_sum(x, ones_blk):
    w = ones_blk.shape[0]
    hi = x.astype(BF16)
    lo = (x - hi.astype(F32)).astype(BF16)
    outs = []
    for j in range(x.shape[1] // w):
        sl = slice(j * w, (j + 1) * w)
        outs.append(_dot(hi[:, sl], ones_blk) + _dot(lo[:, sl], ones_blk))
    return outs[0] if len(outs) == 1 else jnp.concatenate(outs, axis=1)


def _rms_mod(x, g, sh, sc):
    ms = jnp.mean(x * x, axis=-1, keepdims=True)
    return (x * lax.rsqrt(ms + EPS) * g) * (1.0 + sc) + sh


def _adaln_kernel(c_ref, w_ref, b_ref, o_ref):
    c = c_ref[...]
    o_ref[...] = _dot(_silu(c), w_ref[...], precision=HIGHEST) + b_ref[...]


def _adaln(cond, mod_w, mod_b):
    depth, d, d6 = mod_w.shape
    n = d6 // d
    return pl.pallas_call(
        _adaln_kernel,
        out_shape=jax.ShapeDtypeStruct((depth, cond.shape[0], d6), F32),
        grid=(depth, n),
        in_specs=[pl.BlockSpec(cond.shape, lambda l, j: (0, 0)),
                  pl.BlockSpec((None, d, d), lambda l, j: (l, 0, j)),
                  pl.BlockSpec((None, 1, d), lambda l, j: (l, 0, j))],
        out_specs=pl.BlockSpec((None, cond.shape[0], d), lambda l, j: (l, 0, j)),
        compiler_params=_cparams(("arbitrary", "arbitrary")),
        name="adaln",
    )(cond, mod_w, mod_b.reshape(depth, 1, d6))


def _proj_even_kernel(x_ref, g_ref, sh_ref, sc_ref, w_ref, wg_ref, cw_ref, rate_ref, gb_ref,
                      main_ref, gates_ref):
    t = pl.program_id(1)
    h = _rms_mod(x_ref[...], g_ref[...], sh_ref[...], sc_ref[...])
    hb = h.astype(BF16)
    qk_w = GDN_H * HD
    ones_blk = jnp.ones((HD, HD), BF16)

    row = lax.broadcasted_iota(jnp.int32, (TM, 1), 0)
    pos = jnp.where(t > 0, row & (CH - 1), row)
    last = jnp.where(t > 0, CH - 1, TM - 1)
    left_ok = pos != 0
    right_ok = pos != last
    for seg in range(3):
        sl = slice(seg * qk_w, (seg + 1) * qk_w)
        u = _dot(hb, w_ref[:, sl])
        um = jnp.where(left_ok, pltpu.roll(u, 1, 0), 0.0)
        up = jnp.where(right_ok, pltpu.roll(u, TM - 1, 0), 0.0)
        cv = _silu(um * cw_ref[0:1, sl] + u * cw_ref[1:2, sl] + up * cw_ref[2:3, sl])
        if seg < 2:
            ss = _group_sum(cv * cv, ones_blk)
            cv = cv * lax.rsqrt(ss + EPS)
            if seg == 0:
                cv = cv * (HD ** -0.5)
        main_ref[:, sl] = cv.astype(BF16)
    for seg in range(3, 8):
        sl = slice(seg * qk_w, (seg + 1) * qk_w)
        u = _dot(hb, w_ref[:, sl])
        if seg == 5:
            u = u * (HD ** -0.5)
        main_ref[:, sl] = u.astype(BF16)

    z = _dot(hb, wg_ref[...]) + gb_ref[...]
    tl = jnp.log(1.0 + jnp.exp(-jnp.abs(z)))
    sp_pos = jnp.maximum(z, 0.0) + tl
    sp_neg = jnp.maximum(-z, 0.0) + tl
    lane = lax.broadcasted_iota(jnp.int32, z.shape, 1)
    ng = N_DIR * GDN_H
    res = jnp.where(lane < ng, -rate_ref[...] * sp_pos,
                    jnp.where(lane < 2 * ng, _sigmoid(z),
                              jnp.where(lane < 3 * ng, z, -sp_neg)))
    gates_ref[...] = res[:, :gates_ref.shape[-1]]


def _tile_mod_spec(d):
    return pl.BlockSpec((None, None, 1, d), lambda b, t: (b, jnp.minimum(t, 1), 0, 0))


def _proj_even(x, g, sh, sc, w_main, w_gate, conv_w, rate, gbias):
    b, t, d = x.shape
    n = w_main.shape[1]
    ngl = 4 * N_DIR * GDN_H
    const = lambda shape: pl.BlockSpec(shape, lambda b_, t_: (0,) * len(shape))
    return pl.pallas_call(
        _proj_even_kernel,
        out_shape=(jax.ShapeDtypeStruct((b, t, n), BF16), jax.ShapeDtypeStruct((b, t, ngl), F32)),
        grid=(b, t // TM),
        in_specs=[pl.BlockSpec((None, TM, d), lambda b_, t_: (b_, t_, 0)),
                  const((1, d)), _tile_mod_spec(d), _tile_mod_spec(d),
                  const(w_main.shape), const(w_gate.shape), const(conv_w.shape),
                  const(rate.shape), const(gbias.shape)],
        out_specs=(pl.BlockSpec((None, TM, n), lambda b_, t_: (b_, t_, 0)),
                   pl.BlockSpec((None, TM, ngl), lambda b_, t_: (b_, t_, 0))),
        compiler_params=_cparams(("arbitrary", "arbitrary")),
        name="proj_even",
    )(x, g, sh, sc, w_main, w_gate, conv_w, rate, gbias)


def _rev_tile(i, nt):
    return jnp.where(i == 0, 0, nt - i)


def _tri_masks():
    r = lax.broadcasted_iota(jnp.int32, (CH, CH), 0)
    c = lax.broadcasted_iota(jnp.int32, (CH, CH), 1)
    return r >= c, r > c, r <= c, r < c


SCAN_BG = 2


def _scan_streams(heads):
    groups = [(bi, d) for bi in range(SCAN_BG) for d in range(N_DIR)]
    streams = [(gi, hh) for gi in range(len(groups)) for hh in range(heads)]
    return groups, streams


def _cumsum_both(groups, gcs, grs, lower, upper):
    tri = (lower.astype(F32), upper.astype(F32))
    cs_c = [_dot(tri[d], gc, precision=HIGHEST) for (_, d), gc in zip(groups, gcs)]
    cs_r = [_dot(gr, tri[1 - d], precision=HIGHEST) for (_, d), gr in zip(groups, grs)]
    return cs_c, cs_r


def _store_heads(refs, groups, rows, outs, heads):
    for gi, (bi, d) in enumerate(groups):
        o_ref = refs[d][-1]
        tile = jnp.concatenate(outs[gi * heads:(gi + 1) * heads], axis=1)
        o_ref[bi, rows[d], :] = tile.astype(o_ref.dtype)


def _gdn_kernel(qf, kf, vf, qb, kb, vb, gcf, grf, gcb, grb, of, ob, s_ref):
    i = pl.program_id(1)

    @pl.when(i == 0)
    def _():
        s_ref[...] = jnp.zeros_like(s_ref)

    lower, lstrict, upper, ustrict = _tri_masks()
    refs = ((qf, kf, vf, gcf, grf, of), (qb, kb, vb, gcb, grb, ob))
    ng = N_DIR * GDN_H
    groups, streams = _scan_streams(GDN_H)

    def chunk_body(cc, carry):
        cidx = (cc, CPB - 1 - cc)
        rows = tuple(pl.ds(pl.multiple_of(c * CH, CH), CH) for c in cidx)
        gcs = [refs[d][3][bi, cidx[d]] for bi, d in groups]
        grs = [refs[d][4][bi, cidx[d]] for bi, d in groups]
        cs_c, cs_r = _cumsum_both(groups, gcs, grs, lower, upper)
        st = []
        for gi, hh in streams:
            bi, d = groups[gi]
            ci = d * GDN_H + hh
            cols = slice(hh * HD, (hh + 1) * HD)
            incl, strict = (lower, lstrict) if d == 0 else (upper, ustrict)
            g_col = cs_c[gi][:, ci:ci + 1]
            g_row = cs_r[gi][ci:ci + 1, :]
            beta = gcs[gi][:, ng + ci:ng + ci + 1]
            tot = g_col[CH - 1:CH, :] if d == 0 else g_col[0:1, :]
            decay = jnp.where(incl, jnp.exp(jnp.where(incl, g_col - g_row, 0.0)), 0.0)
            q = refs[d][0][bi, rows[d], cols].astype(F32)
            k = refs[d][1][bi, rows[d], cols].astype(F32)
            v = refs[d][2][bi, rows[d], cols].astype(F32)
            kbeta = k * beta
            eg = jnp.exp(g_col)
            st.append(dict(
                strict=strict, decay=decay, kb=k.astype(BF16), kbetab=kbeta.astype(BF16), qb=q.astype(BF16),
                x=jnp.concatenate([v * beta, kbeta * eg], axis=1),
                qe=(q * eg).astype(BF16), kdec=(k * jnp.exp(tot - g_col)).astype(BF16), cd=jnp.exp(tot)))
        kk = [_dot_nt(s["kbetab"], s["kb"]) for s in st]
        qk = [(_dot_nt(s["qb"], s["kb"]) * s["decay"]).astype(BF16) for s in st]
        p = [-jnp.where(s["strict"], m * s["decay"], 0.0) for s, m in zip(st, kk)]
        x = [s["x"] for s in st]
        for j in range(6):
            pb = [m.astype(BF16) for m in p]
            x = [xx + _dot(m, xx.astype(BF16)) for m, xx in zip(pb, x)]
            if j < 5:
                p = [_dot(m, m) for m in pb]
        s_old = [s_ref[si] for si in range(len(streams))]
        sb = [s.astype(BF16) for s in s_old]
        vnb = [(xx[:, :HD] - _dot(xx[:, HD:].astype(BF16), s)).astype(BF16) for xx, s in zip(x, sb)]
        o = [_dot(s["qe"], sbi) + _dot(m, vn) for s, sbi, m, vn in zip(st, sb, qk, vnb)]
        s_new = [so * s["cd"] + _dot_tn(s["kdec"], vn) for s, so, vn in zip(st, s_old, vnb)]
        for si, s in enumerate(s_new):
            s_ref[si] = s
        _store_heads(refs, groups, rows, o, GDN_H)
        return carry

    lax.fori_loop(0, CPB, chunk_body, 0)


def _scan_specs(nt, width, col):
    fwd = pl.BlockSpec((SCAN_BG, TM, width), lambda b, i: (b, i, col))
    bwd = pl.BlockSpec((SCAN_BG, TM, width), lambda b, i: (b, _rev_tile(i, nt), col))
    return fwd, bwd


def _gate_specs(nt, shape):
    fwd = pl.BlockSpec((SCAN_BG, CPB) + shape, lambda b, i: (b, i, 0, 0))
    bwd = pl.BlockSpec((SCAN_BG, CPB) + shape, lambda b, i: (b, _rev_tile(i, nt), 0, 0))
    return fwd, bwd


def _gdn_scan(main, gates_c, gates_r):
    b, t, _ = main.shape
    nt = t // TM
    w = GDN_H * HD
    ngl = gates_c.shape[-1]
    qf, qb = _scan_specs(nt, w, 0)
    kf, kb = _scan_specs(nt, w, 1)
    vf, vb = _scan_specs(nt, w, 2)
    gcf, gcb = _gate_specs(nt, (CH, ngl))
    grf, grb = _gate_specs(nt, (ngl, CH))
    of, ob = _scan_specs(nt, w, 0)
    return pl.pallas_call(
        _gdn_kernel,
        out_shape=(jax.ShapeDtypeStruct((b, t, w), BF16), jax.ShapeDtypeStruct((b, t, w), BF16)),
        grid=(b // SCAN_BG, nt),
        in_specs=[qf, kf, vf, qb, kb, vb, gcf, grf, gcb, grb],
        out_specs=(of, ob),
        scratch_shapes=[pltpu.VMEM((SCAN_BG * N_DIR * GDN_H, HD, HD), F32)],
        compiler_params=_cparams(("arbitrary", "arbitrary")),
        name="gdn_scan",
    )(main, main, main, main, main, main, gates_c, gates_r, gates_c, gates_r)


def _mlstm_kernel(qf, kf, vf, qb, kb, vb, gcf, grf, gcb, grb, of, ob, c_ref, m_ref):
    i = pl.program_id(1)

    @pl.when(i == 0)
    def _():
        c_ref[...] = jnp.zeros_like(c_ref)
        m_ref[...] = jnp.zeros_like(m_ref)

    lower, _, upper, _ = _tri_masks()
    refs = ((qf, kf, vf, gcf, grf, of), (qb, kb, vb, gcb, grb, ob))
    ng = N_DIR * GDN_H
    i_off = 2 * ng
    f_off = 2 * ng + N_DIR * ML_H
    lane = lax.broadcasted_iota(jnp.int32, (CH, HD), 1)
    ones_col = jnp.where(lane == 0, 1.0, 0.0).astype(BF16)

    groups, streams = _scan_streams(ML_H)

    def chunk_body(cc, carry):
        cidx = (cc, CPB - 1 - cc)
        rows = tuple(pl.ds(pl.multiple_of(c * CH, CH), CH) for c in cidx)
        gcs = [refs[d][3][bi, cidx[d]] for bi, d in groups]
        grs = [refs[d][4][bi, cidx[d]] for bi, d in groups]
        cs_c, cs_r = _cumsum_both(groups, gcs, grs, lower, upper)
        ns = len(streams)
        dirs = [groups[gi][1] for gi, _ in streams]
        chan = [groups[gi][1] * ML_H + hh for gi, hh in streams]
        b_col = [cs_c[gi][:, f_off + c:f_off + c + 1] for (gi, _), c in zip(streams, chan)]
        b_row = [cs_r[gi][f_off + c:f_off + c + 1, :] for (gi, _), c in zip(streams, chan)]
        i_col = [gcs[gi][:, i_off + c:i_off + c + 1] for (gi, _), c in zip(streams, chan)]
        i_row = [grs[gi][i_off + c:i_off + c + 1, :] for (gi, _), c in zip(streams, chan)]
        b_tot = [bc[CH - 1:CH, :] if d == 0 else bc[0:1, :] for bc, d in zip(b_col, dirs)]
        m_rows = [m_ref[si] for si in range(ns)]
        m_old = [mr[:, 0:1] for mr in m_rows]
        d_in = [jnp.where(lower if d == 0 else upper, bc - br + ir, NEG)
                for d, bc, br, ir in zip(dirs, b_col, b_row, i_row)]
        d_end = [bt - br + ir for bt, br, ir in zip(b_tot, b_row, i_row)]
        mx_in = [jnp.max(a, axis=-1, keepdims=True) for a in d_in]
        mx_end = [jnp.max(a, axis=-1, keepdims=True) for a in d_end]
        d_carry = [bc + m for bc, m in zip(b_col, m_old)]
        m_t = [jnp.maximum(a, b_) for a, b_ in zip(d_carry, mx_in)]
        carry_end = [bt + m for bt, m in zip(b_tot, m_old)]
        m_new = [jnp.maximum(a, b_) for a, b_ in zip(carry_end, mx_end)]
        p_in = [jnp.exp(a - b_) for a, b_ in zip(d_in, m_t)]
        w_end = [jnp.exp(bt - bc + ic - mn) for bt, bc, ic, mn in zip(b_tot, b_col, i_col, m_new)]
        st = []
        for si, (gi, hh) in enumerate(streams):
            bi, d = groups[gi]
            cols = slice(hh * HD, (hh + 1) * HD)
            k = refs[d][1][bi, rows[d], cols]
            v = refs[d][2][bi, rows[d], cols]
            st.append(dict(
                q=refs[d][0][bi, rows[d], cols], k=k,
                v_aug=jnp.concatenate([v, ones_col], axis=1),
                p_in=p_in[si], w_carry=jnp.exp(d_carry[si] - m_t[si]), floor=jnp.exp(-m_t[si]),
                kw=(k.astype(F32) * w_end[si]).astype(BF16),
                f_end=jnp.exp(carry_end[si] - m_new[si]),
                m_new=jnp.broadcast_to(m_new[si], m_rows[si].shape)))
        sc = [(_dot_nt(s["q"], s["k"]) * s["p_in"]).astype(BF16) for s in st]
        c_old = [c_ref[si] for si in range(len(streams))]
        qc = [_dot(s["q"], c.astype(BF16)) for s, c in zip(st, c_old)]
        nd = [s["w_carry"] * a + _dot(m, s["v_aug"]) for s, a, m in zip(st, qc, sc)]
        hout = [a[:, :HD] / jnp.maximum(jnp.abs(a[:, HD:HD + 1]), s["floor"]) for s, a in zip(st, nd)]
        c_new = [s["f_end"] * c + _dot_tn(s["kw"], s["v_aug"]) for s, c in zip(st, c_old)]
        for si, (s, c) in enumerate(zip(st, c_new)):
            c_ref[si] = c
            m_ref[si] = s["m_new"]
        _store_heads(refs, groups, rows, hout, ML_H)
        return carry

    lax.fori_loop(0, CPB, chunk_body, 0)


def _mlstm_scan(main, gates_c, gates_r):
    b, t, _ = main.shape
    nt = t // TM
    w = ML_H * HD
    ngl = gates_c.shape[-1]
    qf, qb = _scan_specs(nt, w, 4)
    kf, kb = _scan_specs(nt, w, 5)
    vf, vb = _scan_specs(nt, w, 6)
    gcf, gcb = _gate_specs(nt, (CH, ngl))
    grf, grb = _gate_specs(nt, (ngl, CH))
    of, ob = _scan_specs(nt, w, 0)
    return pl.pallas_call(
        _mlstm_kernel,
        out_shape=(jax.ShapeDtypeStruct((b, t, w), BF16), jax.ShapeDtypeStruct((b, t, w), BF16)),
        grid=(b // SCAN_BG, nt),
        in_specs=[qf, kf, vf, qb, kb, vb, gcf, grf, gcb, grb],
        out_specs=(of, ob),
        scratch_shapes=[pltpu.VMEM((SCAN_BG * N_DIR * ML_H, HD, 2 * HD), F32),
                        pltpu.VMEM((SCAN_BG * N_DIR * ML_H, 1, HD), F32)],
        compiler_params=_cparams(("arbitrary", "arbitrary")),
        name="mlstm_scan",
    )(main, main, main, main, main, main, gates_c, gates_r, gates_c, gates_r)


def _ret_kernel(qf, kf, vf, qb, kb, vb, intra_ref, cross_ref, tail_ref, cd_ref, of, ob, s_ref):
    i = pl.program_id(1)

    @pl.when(i == 0)
    def _():
        s_ref[...] = jnp.zeros_like(s_ref)

    refs = ((qf, kf, vf, of), (qb, kb, vb, ob))
    groups, streams = _scan_streams(RET_H)

    def chunk_body(cc, carry):
        cidx = (cc, CPB - 1 - cc)
        rows = tuple(pl.ds(pl.multiple_of(c * CH, CH), CH) for c in cidx)
        st = []
        for gi, hh in streams:
            bi, d = groups[gi]
            ti = d * RET_H + hh
            kcols = slice(hh * HD, (hh + 1) * HD)
            q = refs[d][0][bi, rows[d], kcols]
            k = refs[d][1][bi, rows[d], kcols]
            st.append(dict(
                ti=ti, q=q, k=k, v=refs[d][2][bi, rows[d], slice(hh * RET_DV, (hh + 1) * RET_DV)],
                qc=(q.astype(F32) * cross_ref[ti]).astype(BF16),
                kt=(k.astype(F32) * tail_ref[ti]).astype(BF16)))
        sc = [(_dot_nt(s["q"], s["k"]) * intra_ref[s["ti"]]).astype(BF16) for s in st]
        s_old = [s_ref[si] for si in range(len(streams))]
        o = [_dot(m, s["v"]) + _dot(s["qc"], so.astype(BF16)) for s, m, so in zip(st, sc, s_old)]
        s_new = [cd_ref[s["ti"]] * so + _dot_tn(s["kt"], s["v"]) for s, so in zip(st, s_old)]
        for si, s in enumerate(s_new):
            s_ref[si] = s
        _store_heads(refs, groups, rows, o, RET_H)
        return carry

    lax.fori_loop(0, CPB, chunk_body, 0)


def _ret_tables():
    pos = np.arange(CH, dtype=np.float64)
    intra, cross, tail, cd = [], [], [], []
    for d in range(N_DIR):
        expo = 5.0 + np.arange(RET_H, dtype=np.float64)
        if d == 1:
            expo = expo[::-1]
        lg = np.log1p(-np.exp2(-expo))
        p = pos if d == 0 else (CH - 1.0 - pos)
        diff = p[:, None] - p[None, :]
        for hh in range(RET_H):
            intra.append(np.where(diff >= 0, np.exp(np.where(diff >= 0, diff, 0.0) * lg[hh]), 0.0))
            cross.append(np.broadcast_to(np.exp((p + 1.0) * lg[hh])[:, None], (CH, HD)))
            tail.append(np.broadcast_to(np.exp((CH - 1.0 - p) * lg[hh])[:, None], (CH, HD)))
            cd.append(np.full((1, RET_DV), np.exp(CH * lg[hh])))
    f = lambda a: jnp.asarray(np.stack(a), F32)
    return f(intra), f(cross), f(tail), f(cd)


def _ret_scan(main):
    b, t, _ = main.shape
    nt = t // TM
    qw = RET_H * HD
    vw = RET_H * RET_DV
    qf, qb = _scan_specs(nt, qw, 0)
    kf, kb = _scan_specs(nt, qw, 1)
    vf, vb = _scan_specs(nt, vw, 1)
    of, ob = _scan_specs(nt, vw, 0)
    tabs = _ret_tables()
    const = lambda a: pl.BlockSpec(a.shape, lambda b_, i_: (0,) * a.ndim)
    return pl.pallas_call(
        _ret_kernel,
        out_shape=(jax.ShapeDtypeStruct((b, t, vw), BF16), jax.ShapeDtypeStruct((b, t, vw), BF16)),
        grid=(b // SCAN_BG, nt),
        in_specs=[qf, kf, vf, qb, kb, vb] + [const(a) for a in tabs],
        out_specs=(of, ob),
        scratch_shapes=[pltpu.VMEM((SCAN_BG * N_DIR * RET_H, HD, RET_DV), F32)],
        compiler_params=_cparams(("arbitrary", "arbitrary")),
        name="ret_scan",
    )(main, main, main, main, main, main, *tabs)


CODE_SHIFT = 17


def _route_and_pack(h2, rw_ref, rb_ref, h2p_ref, code_ref, cnt_ref, base_ref):
    h2b = h2.astype(BF16)
    logits = _dot(h2b, rw_ref[...]) + rb_ref[...]
    lane = lax.broadcasted_iota(jnp.int32, logits.shape, 1)
    lane_f = lane.astype(F32)
    vals, idxs = [], []
    cur = logits
    for _ in range(TOP_K):
        m = jnp.max(cur, axis=-1, keepdims=True)
        ix = jnp.min(jnp.where(cur == m, lane_f, float(logits.shape[1])), axis=-1, keepdims=True)
        vals.append(m)
        idxs.append(ix)
        cur = jnp.where(lane_f == ix, NEG, cur)
    es = [jnp.exp(v - vals[0]) for v in vals]
    tot = es[0] + es[1] + es[2] + es[3]

    @pl.when((pl.program_id(0) == 0) & (pl.program_id(1) == 0))
    def _():
        base_ref[...] = jnp.zeros_like(base_ref)

    tm = logits.shape[0]
    onehot = jnp.zeros(logits.shape, F32)
    for j in range(TOP_K):
        onehot = jnp.where(lane_f == idxs[j], 1.0, onehot)
    r_i = lax.broadcasted_iota(jnp.int32, (tm, tm), 0)
    c_i = lax.broadcasted_iota(jnp.int32, (tm, tm), 1)
    before = _dot(jnp.where(r_i > c_i, 1.0, 0.0).astype(BF16), onehot.astype(BF16))
    base = base_ref[...]
    pos = before + base
    total = base + before[tm - 1:tm, :] + onehot[tm - 1:tm, :]
    base_ref[...] = total
    cnt_ref[...] = total

    code = jnp.zeros(logits.shape, F32)
    ext_w = jnp.zeros(logits.shape, F32)
    ext_i = jnp.zeros(logits.shape, F32)
    for j in range(TOP_K):
        rank = jnp.sum(jnp.where(lane_f == idxs[j], pos, 0.0), axis=-1, keepdims=True)
        code = jnp.where(lane == j, idxs[j] * float(1 << CODE_SHIFT) + rank, code)
        ext_w = jnp.where(lane == j, es[j] / tot, ext_w)
        ext_i = jnp.where(lane == TOP_K + j, idxs[j], ext_i)
    code_ref[...] = code.astype(jnp.int32)
    half = h2.shape[1] // 2
    r = h2b.astype(F32)
    lo = lax.shift_right_logical(pltpu.bitcast(r[:, :half], U32), jnp.uint32(16))
    hi = pltpu.bitcast(r[:, half:], U32) & jnp.uint32(0xFFFF0000)
    h2p_ref[:, :half] = hi | lo
    h2p_ref[:, half:] = pltpu.bitcast(ext_w, U32) | pltpu.bitcast(ext_i.astype(jnp.int32), U32)


def _merge_even_kernel(ogf, ogb, omf, omb, z_ref, mo_ref, gg_ref, mg_ref, wo_ref, x_ref, g1_ref,
                       n2_ref, sh_ref, sc_ref, rw_ref, rb_ref, x1_ref, h2p_ref, code_ref, cnt_ref, base_ref):
    ones_blk = jnp.ones((HD, HD), BF16)
    og = ogf[...].astype(F32) + ogb[...].astype(F32)
    ms = _group_sum(og * og, ones_blk) * (1.0 / HD)
    a = og * lax.rsqrt(ms + EPS) * gg_ref[...] * _silu(z_ref[...].astype(F32))
    om = omf[...].astype(F32) + omb[...].astype(F32)
    ms = _group_sum(om * om, ones_blk) * (1.0 / HD)
    m = om * lax.rsqrt(ms + EPS) * mg_ref[...] * _sigmoid(mo_ref[...].astype(F32))
    cat = jnp.concatenate([a, m], axis=1).astype(BF16)
    y = _dot(cat, wo_ref[...])
    x1 = x_ref[...] + g1_ref[...] * y
    x1_ref[...] = x1
    h2 = _rms_mod(x1, n2_ref[...], sh_ref[...], sc_ref[...])
    _route_and_pack(h2, rw_ref, rb_ref, h2p_ref, code_ref, cnt_ref, base_ref)


def _merge_odd_kernel(of, ob, gate_ref, ng_ref, wo_ref, x_ref, g1_ref,
                      n2_ref, sh_ref, sc_ref, rw_ref, rb_ref, x1_ref, h2p_ref, code_ref, cnt_ref, base_ref):
    ones_blk = jnp.ones((RET_DV, RET_DV), BF16)
    o = of[...].astype(F32) + ob[...].astype(F32)
    o = o - _group_sum(o, ones_blk) * (1.0 / RET_DV)
    ms = _group_sum(o * o, ones_blk) * (1.0 / RET_DV)
    y = o * lax.rsqrt(ms + EPS) * ng_ref[...] * _silu(gate_ref[...].astype(F32))
    y = _dot(y.astype(BF16), wo_ref[...])
    x1 = x_ref[...] + g1_ref[...] * y
    x1_ref[...] = x1
    h2 = _rms_mod(x1, n2_ref[...], sh_ref[...], sc_ref[...])
    _route_and_pack(h2, rw_ref, rb_ref, h2p_ref, code_ref, cnt_ref, base_ref)


def _merge_out(b, t, d):
    shapes = (jax.ShapeDtypeStruct((b, t, d), F32), jax.ShapeDtypeStruct((b, t, d // 2 + 128), U32),
              jax.ShapeDtypeStruct((b, t, 128), jnp.int32))
    specs = tuple(pl.BlockSpec((None, TM, s.shape[-1]), lambda b_, t_: (b_, t_, 0)) for s in shapes)
    cnt = jax.ShapeDtypeStruct((1, 128), F32)
    return shapes + (cnt,), specs + (pl.BlockSpec((1, 128), lambda b_, t_: (0, 0)),)


def _merge_even(og_f, og_b, om_f, om_b, main, gdn_g, ml_g, w_out, x, g1, n2, sh2, sc2, rw, rb):
    b, t, d = x.shape
    w = GDN_H * HD
    tile = lambda width, col: pl.BlockSpec((None, TM, width), lambda b_, t_: (b_, t_, col))
    const = lambda a: pl.BlockSpec(a.shape, lambda b_, t_: (0,) * a.ndim)
    shapes, specs = _merge_out(b, t, d)
    return pl.pallas_call(
        _merge_even_kernel,
        out_shape=shapes,
        grid=(b, t // TM),
        in_specs=[tile(w, 0), tile(w, 0), tile(w, 0), tile(w, 0), tile(w, 3), tile(w, 7),
                  const(gdn_g), const(ml_g), const(w_out), tile(d, 0), _tile_mod_spec(d),
                  const(n2), _tile_mod_spec(d), _tile_mod_spec(d), const(rw), const(rb)],
        out_specs=specs,
        scratch_shapes=[pltpu.VMEM((1, 128), F32)],
        compiler_params=_cparams(("arbitrary", "arbitrary")),
        name="merge_even",
    )(og_f, og_b, om_f, om_b, main, main, gdn_g, ml_g, w_out, x, g1, n2, sh2, sc2, rw, rb)


def _merge_odd(o_f, o_b, main, ret_g, w_out, x, g1, n2, sh2, sc2, rw, rb):
    b, t, d = x.shape
    vw = RET_H * RET_DV
    tile = lambda width, col: pl.BlockSpec((None, TM, width), lambda b_, t_: (b_, t_, col))
    const = lambda a: pl.BlockSpec(a.shape, lambda b_, t_: (0,) * a.ndim)
    shapes, specs = _merge_out(b, t, d)
    return pl.pallas_call(
        _merge_odd_kernel,
        out_shape=shapes,
        grid=(b, t // TM),
        in_specs=[tile(vw, 0), tile(vw, 0), tile(vw, 2), const(ret_g), const(w_out), tile(d, 0),
                  _tile_mod_spec(d), const(n2), _tile_mod_spec(d), _tile_mod_spec(d), const(rw), const(rb)],
        out_specs=specs,
        scratch_shapes=[pltpu.VMEM((1, 128), F32)],
        compiler_params=_cparams(("arbitrary", "arbitrary")),
        name="merge_odd",
    )(o_f, o_b, main, ret_g, w_out, x, g1, n2, sh2, sc2, rw, rb)


def _proj_odd_kernel(x_ref, acc_ref, g2_ref, g_ref, sh_ref, sc_ref, w_ref, cos_ref, sin_ref,
                     x2_ref, main_ref):
    x2 = x_ref[...] + g2_ref[...] * acc_ref[...]
    x2_ref[...] = x2
    hb = _rms_mod(x2, g_ref[...], sh_ref[...], sc_ref[...]).astype(BF16)
    qk_w = RET_H * HD
    cos = cos_ref[...]
    sin = sin_ref[...]
    n_seg = w_ref.shape[1] // qk_w
    for seg in range(n_seg):
        sl = slice(seg * qk_w, (seg + 1) * qk_w)
        u = _dot(hb, w_ref[:, sl])
        if seg < 2:
            parts = []
            for hh in range(RET_H):
                uh = u[:, hh * HD:(hh + 1) * HD]
                parts.append(uh * cos + pltpu.roll(uh, HD // 2, 1) * sin)
            u = jnp.concatenate(parts, axis=1)
            if seg == 1:
                u = u * (HD ** -0.5)
        main_ref[:, sl] = u.astype(BF16)


def _rope_tables(t):
    half = HD // 2
    freqs = ROPE_BASE ** (-jnp.arange(half, dtype=F32) / half)
    ang = jnp.arange(t, dtype=F32)[:, None] * freqs[None, :]
    cos, sin = jnp.cos(ang), jnp.sin(ang)
    return jnp.concatenate([cos, cos], axis=1), jnp.concatenate([-sin, sin], axis=1)


def _proj_odd(x, acc, g2, g, sh, sc, w):
    b, t, d = x.shape
    n = w.shape[1]
    cos, sin = _rope_tables(t)
    const = lambda a: pl.BlockSpec(a.shape, lambda b_, t_: (0,) * a.ndim)
    tile = lambda width: pl.BlockSpec((None, TM, width), lambda b_, t_: (b_, t_, 0))
    rope = pl.BlockSpec((TM, HD), lambda b_, t_: (t_, 0))
    nt = t // TM
    acc_tile = pl.BlockSpec((TM, d), lambda b_, t_: (b_ * nt + t_, 0))
    return pl.pallas_call(
        _proj_odd_kernel,
        out_shape=(jax.ShapeDtypeStruct((b, t, d), F32), jax.ShapeDtypeStruct((b, t, n), BF16)),
        grid=(b, nt),
        in_specs=[tile(d), acc_tile, _tile_mod_spec(d), const(g), _tile_mod_spec(d), _tile_mod_spec(d),
                  const(w), rope, rope],
        out_specs=(tile(d), tile(n)),
        compiler_params=_cparams(("arbitrary", "arbitrary")),
        name="proj_odd",
    )(x, acc, g2, g, sh, sc, w, cos, sin)


ROWS_PER_STEP = 8


def _moe_gather_kernel(src_ref, h_ref, xs_ref):
    def body(r8, carry):
        base = pl.multiple_of(r8 * ROWS_PER_STEP, ROWS_PER_STEP)
        rows = [h_ref[pl.ds(src_ref[0, base + j], 1), :] for j in range(ROWS_PER_STEP)]
        xs_ref[pl.ds(base, ROWS_PER_STEP), :] = jnp.concatenate(rows, axis=0)
        return carry

    lax.fori_loop(0, MOE_BLOCK // ROWS_PER_STEP, body, 0)


def _moe_gather(row_src, h2p, n_blocks):
    n, half = h2p.shape
    return pl.pallas_call(
        _moe_gather_kernel,
        out_shape=jax.ShapeDtypeStruct((n_blocks * MOE_BLOCK, half), U32),
        grid=(n_blocks,),
        in_specs=[pl.BlockSpec((None, 1, MOE_BLOCK), lambda i: (i, 0, 0), memory_space=pltpu.SMEM),
                  pl.BlockSpec((n, half), lambda i: (0, 0))],
        out_specs=pl.BlockSpec((MOE_BLOCK, half), lambda i: (i, 0)),
        compiler_params=_cparams(("arbitrary",)),
        name="moe_gather",
    )(row_src.reshape(n_blocks, 1, MOE_BLOCK), h2p)


def _moe_mm_kernel(be_ref, xs_ref, wgu_ref, bgu_ref, wdn_ref, bdn_ref, ys_ref, wgu_s, wdn_s):
    i = pl.program_id(0)
    e = be_ref[i]
    e_prev = be_ref[jnp.maximum(i - 1, 0)]

    @pl.when((i == 0) | (e != e_prev))
    def _():
        wgu_s[...] = wgu_ref[...].astype(BF16)
        bits = pltpu.bitcast(wdn_ref[...].astype(BF16).astype(F32), U32)
        both = (bits & jnp.uint32(0xFFFF0000)) | lax.shift_right_logical(bits, jnp.uint32(16))
        wdn_s[...] = pltpu.bitcast(both, BF16)

    half = wgu_ref.shape[0] // 2
    xu = xs_ref[:, :half]
    ext = xs_ref[:, half:]
    ext_lane = lax.broadcasted_iota(jnp.int32, ext.shape, 1)
    ids = pltpu.roll(pltpu.bitcast(ext, jnp.int32), ext.shape[1] - TOP_K, 1)
    hit = jnp.where(ext_lane < TOP_K, ids, -1) == e
    row_w = jnp.sum(jnp.where(hit, pltpu.bitcast(ext, F32), 0.0), axis=-1, keepdims=True)
    lo = pltpu.bitcast(lax.shift_left(xu, jnp.uint32(16)), F32).astype(BF16)
    hi = pltpu.bitcast(xu & jnp.uint32(0xFFFF0000), F32).astype(BF16)
    gu = _dot(lo, wgu_s[:half, :]) + _dot(hi, wgu_s[half:, :]) + bgu_ref[...]
    nxt = jnp.concatenate([pltpu.roll(gu[:, c * 128:(c + 1) * 128], 127, 1)
                           for c in range(gu.shape[1] // 128)], axis=1)
    gate = jnp.minimum(gu, SWIGLU_LIMIT)
    up = jnp.clip(nxt, -SWIGLU_LIMIT, SWIGLU_LIMIT)
    act = (up + 1.0) * gate * _sigmoid(SWIGLU_ALPHA * gate)
    lane = lax.broadcasted_iota(jnp.int32, act.shape, 1)
    act = jnp.where((lane & 1) == 0, act, 0.0)
    y = _dot(act.astype(BF16), wdn_s[...]) + bdn_ref[...]
    ys_ref[...] = y * row_w


def _moe_mm(block_e, xs, w_gu, b_gu, w_dn, b_dn):
    rows, width = xs.shape
    n_blocks = rows // MOE_BLOCK
    n_e, d, f2 = w_gu.shape
    ew = lambda shape: pl.BlockSpec((None,) + shape, lambda i, be: (be[i], 0, 0))
    grid_spec = pltpu.PrefetchScalarGridSpec(
        num_scalar_prefetch=1,
        grid=(n_blocks,),
        in_specs=[pl.BlockSpec((MOE_BLOCK, width), lambda i, be: (i, 0)),
                  ew((d, f2)), ew((1, f2)), ew((f2 // 2, d)), ew((1, d))],
        out_specs=pl.BlockSpec((MOE_BLOCK, d), lambda i, be: (i, 0)),
        scratch_shapes=[pltpu.VMEM((d, f2), BF16), pltpu.VMEM((f2, d), BF16)],
    )
    return pl.pallas_call(
        _moe_mm_kernel,
        out_shape=jax.ShapeDtypeStruct((rows, d), F32),
        grid_spec=grid_spec,
        compiler_params=_cparams(("arbitrary",)),
        name="moe_mm",
    )(block_e, xs, w_gu, b_gu.reshape(n_e, 1, f2), w_dn, b_dn.reshape(n_e, 1, d))


COMBINE_COLS = 512


def _moe_combine_kernel(dst_ref, ys_ref, acc_ref):
    i = pl.program_id(1)

    @pl.when(i == 0)
    def _():
        acc_ref[...] = jnp.zeros_like(acc_ref)

    def body(r8, carry):
        base = pl.multiple_of(r8 * ROWS_PER_STEP, ROWS_PER_STEP)
        y8 = ys_ref[pl.ds(base, ROWS_PER_STEP), :]
        toks = [dst_ref[0, base + j] for j in range(ROWS_PER_STEP)]
        cur = [acc_ref[pl.ds(tk, 1), :] for tk in toks]
        for j in range(ROWS_PER_STEP):
            acc_ref[pl.ds(toks[j], 1), :] = cur[j] + y8[j:j + 1, :]
        return carry

    lax.fori_loop(0, MOE_BLOCK // ROWS_PER_STEP, body, 0)


def _moe_combine(row_dst, ys, n_rows_out):
    rows, d = ys.shape
    n_blocks = rows // MOE_BLOCK
    return pl.pallas_call(
        _moe_combine_kernel,
        out_shape=jax.ShapeDtypeStruct((n_rows_out, d), F32),
        grid=(d // COMBINE_COLS, n_blocks),
        in_specs=[pl.BlockSpec((None, 1, MOE_BLOCK), lambda j, i: (i, 0, 0), memory_space=pltpu.SMEM),
                  pl.BlockSpec((MOE_BLOCK, COMBINE_COLS), lambda j, i: (i, j))],
        out_specs=pl.BlockSpec((n_rows_out, COMBINE_COLS), lambda j, i: (0, j), pipeline_mode=pl.Buffered(1)),
        compiler_params=_cparams(("arbitrary", "arbitrary")),
        name="moe_combine",
    )(row_dst.reshape(n_blocks, 1, MOE_BLOCK), ys)


def _moe_inverse_kernel(off_ref, code_ref, inv_ref):
    def init(r, carry):
        inv_ref[r] = -1
        return carry

    lax.fori_loop(0, inv_ref.shape[0], init, 0, unroll=8)

    def body(a, carry):
        code = code_ref[a]
        e = lax.shift_right_logical(code, CODE_SHIFT)
        inv_ref[off_ref[e] + (code & ((1 << CODE_SHIFT) - 1))] = a
        return carry

    lax.fori_loop(0, code_ref.shape[0], body, 0, unroll=8)


def _moe_inverse(offsets, codes, rows):
    smem = pl.BlockSpec(memory_space=pltpu.SMEM)
    return pl.pallas_call(
        _moe_inverse_kernel,
        out_shape=jax.ShapeDtypeStruct((rows,), jnp.int32),
        in_specs=[smem, smem],
        out_specs=smem,
        name="moe_inverse",
    )(offsets, codes)


def _moe(h2p, codes, counts, w_gu, b_gu, w_dn, b_dn):
    n = h2p.shape[0]
    n_assign = n * TOP_K
    n_blocks = -(-n_assign // MOE_BLOCK) + N_EXPERTS
    rows = n_blocks * MOE_BLOCK
    padded = -(-counts // MOE_BLOCK) * MOE_BLOCK
    ends = jnp.cumsum(padded)
    block_e = jnp.minimum(jnp.searchsorted(ends, jnp.arange(n_blocks) * MOE_BLOCK, side='right'),
                          N_EXPERTS - 1).astype(jnp.int32)
    inv = _moe_inverse((ends - padded).astype(jnp.int32), codes, rows)
    tok = lax.shift_right_logical(inv, TOP_K.bit_length() - 1)
    row_src = jnp.where(inv >= 0, tok, 0)
    row_dst = jnp.where(inv >= 0, tok, n)
    xs = _moe_gather(row_src, h2p, n_blocks)
    ys = _moe_mm(block_e, xs, w_gu, b_gu, w_dn, b_dn)
    return _moe_combine(row_dst, ys, n + 8)


def _final_kernel(x_ref, acc_ref, g2_ref, g_ref, o_ref):
    x = x_ref[...] + g2_ref[...] * acc_ref[...]
    ms = jnp.mean(x * x, axis=-1, keepdims=True)
    o_ref[...] = x * lax.rsqrt(ms + EPS) * g_ref[...]


def _final(x, acc, g2, g, n_ctx_tiles):
    b, t, d = x.shape
    nt = t // TM - n_ctx_tiles
    tile_in = pl.BlockSpec((None, TM, d), lambda b_, t_: (b_, t_ + n_ctx_tiles, 0))
    acc_tile = pl.BlockSpec((TM, d), lambda b_, t_: (b_ * (t // TM) + t_ + n_ctx_tiles, 0))
    return pl.pallas_call(
        _final_kernel,
        out_shape=jax.ShapeDtypeStruct((b, nt * TM, d), F32),
        grid=(b, nt),
        in_specs=[tile_in, acc_tile,
                  pl.BlockSpec((None, None, 1, d), lambda b_, t_: (b_, 1, 0, 0)),
                  pl.BlockSpec((1, d), lambda b_, t_: (0, 0))],
        out_specs=pl.BlockSpec((None, TM, d), lambda b_, t_: (b_, t_, 0)),
        compiler_params=_cparams(("arbitrary", "arbitrary")),
        name="final_norm",
    )(x, acc, g2, g)


def _mod_tables(mod, b, d):
    outs = []
    for j in range(6):
        m = mod[:, j * d:(j + 1) * d]
        lat = m[:b]
        ctx = jnp.broadcast_to(m[b:b + 1], (b, d))
        outs.append(jnp.stack([ctx, lat], axis=1)[:, :, None, :])
    return outs


def _router_params(router_w, router_b):
    d, e = router_w.shape
    rw = jnp.zeros((d, 128), F32).at[:, :e].set(router_w).astype(BF16)
    rb = jnp.full((1, 128), NEG, F32).at[0, :e].set(router_b)
    return rw, rb


def kernel(x, c, ctx, c_ctx, mod_w, mod_b, norm1_g, norm2_g, ev_w_in, ev_conv_w, gdn_a_log, gdn_dt_bias,
           gdn_norm_g, ml_i_bias, ml_f_bias, ml_norm_g, ev_w_out, od_w_in, ret_norm_g, od_w_out,
           router_w, router_b, moe_w_gu, moe_b_gu, moe_w_dn, moe_b_dn, final_g):
    b, s, d = x.shape
    n_ctx = ctx.shape[1]
    depth = mod_w.shape[0]
    assert n_ctx == TM and s % TM == 0 and depth == 2 and b % SCAN_BG == 0
    t = n_ctx + s
    n_tok = b * t

    cond = jnp.concatenate([c, c_ctx[None, :], jnp.zeros((8 - b - 1, d), F32)], axis=0)
    mod = _adaln(cond, mod_w, mod_b)
    xa = jnp.concatenate([ctx, x], axis=1)

    sh1, sc1, g1, sh2, sc2, g2 = _mod_tables(mod[0], b, d)
    qk_w = GDN_H * HD
    conv_ch = 3 * qk_w
    ng = N_DIR * GDN_H
    w_in = ev_w_in[0]
    o_z = conv_ch
    o_a = o_z + qk_w
    o_mq = o_a + 2 * ng
    o_i = o_mq + 4 * qk_w
    w_main = jnp.concatenate([w_in[:, :o_a], w_in[:, o_mq:o_i]], axis=1).astype(BF16)
    w_gate = jnp.concatenate([w_in[:, o_a:o_mq], w_in[:, o_i:o_i + 2 * ng],
                              jnp.zeros((d, 128 - 4 * ng), F32)], axis=1).astype(BF16)
    zeros_g = jnp.zeros((ng,), F32)
    rate = jnp.concatenate([jnp.exp(gdn_a_log[0].astype(F32)).reshape(-1), jnp.zeros((128 - ng,), F32)])[None, :]
    gbias = jnp.concatenate([gdn_dt_bias[0].reshape(-1), zeros_g, ml_i_bias[0].reshape(-1),
                             ml_f_bias[0].reshape(-1), jnp.zeros((128 - 4 * ng,), F32)])[None, :].astype(F32)
    main, gates = _proj_even(xa, norm1_g[0][None, :], sh1, sc1, w_main, w_gate, ev_conv_w[0], rate, gbias)
    gates_c = gates.reshape(b, t // CH, CH, 4 * ng)
    gates_r = jnp.swapaxes(gates_c, 2, 3)
    og_f, og_b = _gdn_scan(main, gates_c, gates_r)
    om_f, om_b = _mlstm_scan(main, gates_c, gates_r)
    rw, rb = _router_params(router_w[0], router_b[0])
    gdn_g = jnp.tile(gdn_norm_g[0], GDN_H)[None, :]
    x1, h2p, code, cnt = _merge_even(og_f, og_b, om_f, om_b, main, gdn_g, ml_norm_g[0][None, :],
                                     ev_w_out[0].astype(BF16), xa, g1, norm2_g[0][None, :], sh2, sc2, rw, rb)
    acc = _moe(h2p.reshape(n_tok, -1), code[:, :, :TOP_K].reshape(-1), cnt[0, :N_EXPERTS].astype(jnp.int32),
               moe_w_gu[0], moe_b_gu[0], moe_w_dn[0], moe_b_dn[0])
    g2_prev = g2

    sh1, sc1, g1, sh2, sc2, g2 = _mod_tables(mod[1], b, d)
    x2, main_o = _proj_odd(x1, acc, g2_prev, norm1_g[1][None, :], sh1, sc1, od_w_in[0].astype(BF16))
    o_f, o_b = _ret_scan(main_o)
    rw, rb = _router_params(router_w[1], router_b[1])
    x3, h2p, code, cnt = _merge_odd(o_f, o_b, main_o, ret_norm_g[0][None, :], od_w_out[0].astype(BF16),
                                    x2, g1, norm2_g[1][None, :], sh2, sc2, rw, rb)
    acc = _moe(h2p.reshape(n_tok, -1), code[:, :, :TOP_K].reshape(-1), cnt[0, :N_EXPERTS].astype(jnp.int32),
               moe_w_gu[1], moe_b_gu[1], moe_w_dn[1], moe_b_dn[1])
    return _final(x3, acc, g2, final_g[None, :], n_ctx // TM)
```

```python
import functools
import math

import jax
import jax.numpy as jnp
import numpy as np
from jax import lax
from jax.experimental import pallas as pl
from jax.experimental.pallas import tpu as pltpu

F32 = jnp.float32
BF16 = jnp.bfloat16
U32 = jnp.uint32
HIGHEST = lax.Precision.HIGHEST

EPS = 1e-6
CH = 64
TM = 256
CPB = TM // CH
HD = 128
N_DIR = 2
GDN_H = 4
ML_H = 4
RET_H = 8
RET_DV = 256
CONV_W = 3
N_EXPERTS = 32
TOP_K = 4
SWIGLU_ALPHA = 1.702
SWIGLU_LIMIT = 7.0
MOE_BLOCK = 256
ROPE_BASE = 10000.0
NEG = -1e30
VMEM_LIMIT = 56 * 1024 * 1024


def _cparams(sem):
    return pltpu.CompilerParams(dimension_semantics=sem, vmem_limit_bytes=VMEM_LIMIT)


def _dot(a, b, precision=None):
    return jnp.dot(a, b, preferred_element_type=F32, precision=precision)


def _dot_nt(a, b):
    return lax.dot_general(a, b, (((1,), (1,)), ((), ())), preferred_element_type=F32)


def _dot_tn(a, b):
    return lax.dot_general(a, b, (((0,), (0,)), ((), ())), preferred_element_type=F32)


def _sigmoid(x):
    return 1.0 / (1.0 + jnp.exp(-x))


def _silu(x):
    return x * _sigmoid(x)


def _group_sum(x, ones_blk):
    w = ones_blk.shape[0]
    hi = x.astype(BF16)
    lo = (x - hi.astype(F32)).astype(BF16)
    outs = []
    for j in range(x.shape[1] // w):
        sl = slice(j * w, (j + 1) * w)
        outs.append(_dot(hi[:, sl], ones_blk) + _dot(lo[:, sl], ones_blk))
    return outs[0] if len(outs) == 1 else jnp.concatenate(outs, axis=1)


def _rms_mod(x, g, sh, sc):
    ms = jnp.mean(x * x, axis=-1, keepdims=True)
    return (x * lax.rsqrt(ms + EPS) * g) * (1.0 + sc) + sh


def _adaln_kernel(c_ref, w_ref, b_ref, o_ref):
    c = c_ref[...]
    o_ref[...] = _dot(_silu(c), w_ref[...], precision=HIGHEST) + b_ref[...]


def _adaln(cond, mod_w, mod_b):
    depth, d, d6 = mod_w.shape
    n = d6 // d
    return pl.pallas_call(
        _adaln_kernel,
        out_shape=jax.ShapeDtypeStruct((depth, cond.shape[0], d6), F32),
        grid=(depth, n),
        in_specs=[pl.BlockSpec(cond.shape, lambda l, j: (0, 0)),
                  pl.BlockSpec((None, d, d), lambda l, j: (l, 0, j)),
                  pl.BlockSpec((None, 1, d), lambda l, j: (l, 0, j))],
        out_specs=pl.BlockSpec((None, cond.shape[0], d), lambda l, j: (l, 0, j)),
        compiler_params=_cparams(("arbitrary", "arbitrary")),
        name="adaln",
    )(cond, mod_w, mod_b.reshape(depth, 1, d6))


def _proj_even_kernel(x_ref, g_ref, sh_ref, sc_ref, w_ref, wg_ref, cw_ref, rate_ref, gb_ref,
                      main_ref, gates_ref):
    t = pl.program_id(1)
    h = _rms_mod(x_ref[...], g_ref[...], sh_ref[...], sc_ref[...])
    hb = h.astype(BF16)
    qk_w = GDN_H * HD
    ones_blk = jnp.ones((HD, HD), BF16)

    row = lax.broadcasted_iota(jnp.int32, (TM, 1), 0)
    pos = jnp.where(t > 0, row & (CH - 1), row)
    last = jnp.where(t > 0, CH - 1, TM - 1)
    left_ok = pos != 0
    right_ok = pos != last
    for seg in range(3):
        sl = slice(seg * qk_w, (seg + 1) * qk_w)
        u = _dot(hb, w_ref[:, sl])
        um = jnp.where(left_ok, pltpu.roll(u, 1, 0), 0.0)
        up = jnp.where(right_ok, pltpu.roll(u, TM - 1, 0), 0.0)
        cv = _silu(um * cw_ref[0:1, sl] + u * cw_ref[1:2, sl] + up * cw_ref[2:3, sl])
        if seg < 2:
            ss = _group_sum(cv * cv, ones_blk)
            cv = cv * lax.rsqrt(ss + EPS)
            if seg == 0:
                cv = cv * (HD ** -0.5)
        main_ref[:, sl] = cv.astype(BF16)
    for seg in range(3, 8):
        sl = slice(seg * qk_w, (seg + 1) * qk_w)
        u = _dot(hb, w_ref[:, sl])
        if seg == 5:
            u = u * (HD ** -0.5)
        main_ref[:, sl] = u.astype(BF16)

    z = _dot(hb, wg_ref[...]) + gb_ref[...]
    tl = jnp.log(1.0 + jnp.exp(-jnp.abs(z)))
    sp_pos = jnp.maximum(z, 0.0) + tl
    sp_neg = jnp.maximum(-z, 0.0) + tl
    lane = lax.broadcasted_iota(jnp.int32, z.shape, 1)
    ng = N_DIR * GDN_H
    res = jnp.where(lane < ng, -rate_ref[...] * sp_pos,
                    jnp.where(lane < 2 * ng, _sigmoid(z),
                              jnp.where(lane < 3 * ng, z, -sp_neg)))
    gates_ref[...] = res[:, :gates_ref.shape[-1]]


def _tile_mod_spec(d):
    return pl.BlockSpec((None, None, 1, d), lambda b, t: (b, jnp.minimum(t, 1), 0, 0))


def _proj_even(x, g, sh, sc, w_main, w_gate, conv_w, rate, gbias):
    b, t, d = x.shape
    n = w_main.shape[1]
    ngl = 4 * N_DIR * GDN_H
    const = lambda shape: pl.BlockSpec(shape, lambda b_, t_: (0,) * len(shape))
    return pl.pallas_call(
        _proj_even_kernel,
        out_shape=(jax.ShapeDtypeStruct((b, t, n), BF16), jax.ShapeDtypeStruct((b, t, ngl), F32)),
        grid=(b, t // TM),
        in_specs=[pl.BlockSpec((None, TM, d), lambda b_, t_: (b_, t_, 0)),
                  const((1, d)), _tile_mod_spec(d), _tile_mod_spec(d),
                  const(w_main.shape), const(w_gate.shape), const(conv_w.shape),
                  const(rate.shape), const(gbias.shape)],
        out_specs=(pl.BlockSpec((None, TM, n), lambda b_, t_: (b_, t_, 0)),
                   pl.BlockSpec((None, TM, ngl), lambda b_, t_: (b_, t_, 0))),
        compiler_params=_cparams(("arbitrary", "arbitrary")),
        name="proj_even",
    )(x, g, sh, sc, w_main, w_gate, conv_w, rate, gbias)


def _rev_tile(i, nt):
    return jnp.where(i == 0, 0, nt - i)


def _tri_masks():
    r = lax.broadcasted_iota(jnp.int32, (CH, CH), 0)
    c = lax.broadcasted_iota(jnp.int32, (CH, CH), 1)
    return r >= c, r > c, r <= c, r < c


SCAN_BG = 2


def _scan_streams(heads):
    groups = [(bi, d) for bi in range(SCAN_BG) for d in range(N_DIR)]
    streams = [(gi, hh) for gi in range(len(groups)) for hh in range(heads)]
    return groups, streams


def _cumsum_both(groups, gcs, grs, lower, upper):
    tri = (lower.astype(F32), upper.astype(F32))
    cs_c = [_dot(tri[d], gc, precision=HIGHEST) for (_, d), gc in zip(groups, gcs)]
    cs_r = [_dot(gr, tri[1 - d], precision=HIGHEST) for (_, d), gr in zip(groups, grs)]
    return cs_c, cs_r


def _store_heads(refs, groups, rows, outs, heads):
    for gi, (bi, d) in enumerate(groups):
        o_ref = refs[d][-1]
        tile = jnp.concatenate(outs[gi * heads:(gi + 1) * heads], axis=1)
        o_ref[bi, rows[d], :] = tile.astype(o_ref.dtype)


def _gdn_kernel(qf, kf, vf, qb, kb, vb, gcf, grf, gcb, grb, of, ob, s_ref):
    i = pl.program_id(1)

    @pl.when(i == 0)
    def _():
        s_ref[...] = jnp.zeros_like(s_ref)

    lower, lstrict, upper, ustrict = _tri_masks()
    refs = ((qf, kf, vf, gcf, grf, of), (qb, kb, vb, gcb, grb, ob))
    ng = N_DIR * GDN_H
    groups, streams = _scan_streams(GDN_H)

    def chunk_body(cc, carry):
        cidx = (cc, CPB - 1 - cc)
        rows = tuple(pl.ds(pl.multiple_of(c * CH, CH), CH) for c in cidx)
        gcs = [refs[d][3][bi, cidx[d]] for bi, d in groups]
        grs = [refs[d][4][bi, cidx[d]] for bi, d in groups]
        cs_c, cs_r = _cumsum_both(groups, gcs, grs, lower, upper)
        st = []
        for gi, hh in streams:
            bi, d = groups[gi]
            ci = d * GDN_H + hh
            cols = slice(hh * HD, (hh + 1) * HD)
            incl, strict = (lower, lstrict) if d == 0 else (upper, ustrict)
            g_col = cs_c[gi][:, ci:ci + 1]
            g_row = cs_r[gi][ci:ci + 1, :]
            beta = gcs[gi][:, ng + ci:ng + ci + 1]
            tot = g_col[CH - 1:CH, :] if d == 0 else g_col[0:1, :]
            decay = jnp.where(incl, jnp.exp(jnp.where(incl, g_col - g_row, 0.0)), 0.0)
            q = refs[d][0][bi, rows[d], cols].astype(F32)
            k = refs[d][1][bi, rows[d], cols].astype(F32)
            v = refs[d][2][bi, rows[d], cols].astype(F32)
            kbeta = k * beta
            eg = jnp.exp(g_col)
            st.append(dict(
                strict=strict, decay=decay, kb=k.astype(BF16), kbetab=kbeta.astype(BF16), qb=q.astype(BF16),
                x=jnp.concatenate([v * beta, kbeta * eg], axis=1),
                qe=(q * eg).astype(BF16), kdec=(k * jnp.exp(tot - g_col)).astype(BF16), cd=jnp.exp(tot)))
        kk = [_dot_nt(s["kbetab"], s["kb"]) for s in st]
        qk = [(_dot_nt(s["qb"], s["kb"]) * s["decay"]).astype(BF16) for s in st]
        p = [-jnp.where(s["strict"], m * s["decay"], 0.0) for s, m in zip(st, kk)]
        x = [s["x"] for s in st]
        for j in range(6):
            pb = [m.astype(BF16) for m in p]
            x = [xx + _dot(m, xx.astype(BF16)) for m, xx in zip(pb, x)]
            if j < 5:
                p = [_dot(m, m) for m in pb]
        s_old = [s_ref[si] for si in range(len(streams))]
        sb = [s.astype(BF16) for s in s_old]
        vnb = [(xx[:, :HD] - _dot(xx[:, HD:].astype(BF16), s)).astype(BF16) for xx, s in zip(x, sb)]
        o = [_dot(s["qe"], sbi) + _dot(m, vn) for s, sbi, m, vn in zip(st, sb, qk, vnb)]
        s_new = [so * s["cd"] + _dot_tn(s["kdec"], vn) for s, so, vn in zip(st, s_old, vnb)]
        for si, s in enumerate(s_new):
            s_ref[si] = s
        _store_heads(refs, groups, rows, o, GDN_H)
        return carry

    lax.fori_loop(0, CPB, chunk_body, 0)


def _scan_specs(nt, width, col):
    fwd = pl.BlockSpec((SCAN_BG, TM, width), lambda b, i: (b, i, col))
    bwd = pl.BlockSpec((SCAN_BG, TM, width), lambda b, i: (b, _rev_tile(i, nt), col))
    return fwd, bwd


def _gate_specs(nt, shape):
    fwd = pl.BlockSpec((SCAN_BG, CPB) + shape, lambda b, i: (b, i, 0, 0))
    bwd = pl.BlockSpec((SCAN_BG, CPB) + shape, lambda b, i: (b, _rev_tile(i, nt), 0, 0))
    return fwd, bwd


def _gdn_scan(main, gates_c, gates_r):
    b, t, _ = main.shape
    nt = t // TM
    w = GDN_H * HD
    ngl = gates_c.shape[-1]
    qf, qb = _scan_specs(nt, w, 0)
    kf, kb = _scan_specs(nt, w, 1)
    vf, vb = _scan_specs(nt, w, 2)
    gcf, gcb = _gate_specs(nt, (CH, ngl))
    grf, grb = _gate_specs(nt, (ngl, CH))
    of, ob = _scan_specs(nt, w, 0)
    return pl.pallas_call(
        _gdn_kernel,
        out_shape=(jax.ShapeDtypeStruct((b, t, w), BF16), jax.ShapeDtypeStruct((b, t, w), BF16)),
        grid=(b // SCAN_BG, nt),
        in_specs=[qf, kf, vf, qb, kb, vb, gcf, grf, gcb, grb],
        out_specs=(of, ob),
        scratch_shapes=[pltpu.VMEM((SCAN_BG * N_DIR * GDN_H, HD, HD), F32)],
        compiler_params=_cparams(("arbitrary", "arbitrary")),
        name="gdn_scan",
    )(main, main, main, main, main, main, gates_c, gates_r, gates_c, gates_r)


def _mlstm_kernel(qf, kf, vf, qb, kb, vb, gcf, grf, gcb, grb, of, ob, c_ref, m_ref):
    i = pl.program_id(1)

    @pl.when(i == 0)
    def _():
        c_ref[...] = jnp.zeros_like(c_ref)
        m_ref[...] = jnp.zeros_like(m_ref)

    lower, _, upper, _ = _tri_masks()
    refs = ((qf, kf, vf, gcf, grf, of), (qb, kb, vb, gcb, grb, ob))
    ng = N_DIR * GDN_H
    i_off = 2 * ng
    f_off = 2 * ng + N_DIR * ML_H
    lane = lax.broadcasted_iota(jnp.int32, (CH, HD), 1)
    ones_col = jnp.where(lane == 0, 1.0, 0.0).astype(BF16)

    groups, streams = _scan_streams(ML_H)

    def chunk_body(cc, carry):
        cidx = (cc, CPB - 1 - cc)
        rows = tuple(pl.ds(pl.multiple_of(c * CH, CH), CH) for c in cidx)
        gcs = [refs[d][3][bi, cidx[d]] for bi, d in groups]
        grs = [refs[d][4][bi, cidx[d]] for bi, d in groups]
        cs_c, cs_r = _cumsum_both(groups, gcs, grs, lower, upper)
        ns = len(streams)
        dirs = [groups[gi][1] for gi, _ in streams]
        chan = [groups[gi][1] * ML_H + hh for gi, hh in streams]
        b_col = [cs_c[gi][:, f_off + c:f_off + c + 1] for (gi, _), c in zip(streams, chan)]
        b_row = [cs_r[gi][f_off + c:f_off + c + 1, :] for (gi, _), c in zip(streams, chan)]
        i_col = [gcs[gi][:, i_off + c:i_off + c + 1] for (gi, _), c in zip(streams, chan)]
        i_row = [grs[gi][i_off + c:i_off + c + 1, :] for (gi, _), c in zip(streams, chan)]
        b_tot = [bc[CH - 1:CH, :] if d == 0 else bc[0:1, :] for bc, d in zip(b_col, dirs)]
        m_rows = [m_ref[si] for si in range(ns)]
        m_old = [mr[:, 0:1] for mr in m_rows]
        d_in = [jnp.where(lower if d == 0 else upper, bc - br + ir, NEG)
                for d, bc, br, ir in zip(dirs, b_col, b_row, i_row)]
        d_end = [bt - br + ir for bt, br, ir in zip(b_tot, b_row, i_row)]
        mx_in = [jnp.max(a, axis=-1, keepdims=True) for a in d_in]
        mx_end = [jnp.max(a, axis=-1, keepdims=True) for a in d_end]
        d_carry = [bc + m for bc, m in zip(b_col, m_old)]
        m_t = [jnp.maximum(a, b_) for a, b_ in zip(d_carry, mx_in)]
        carry_end = [bt + m for bt, m in zip(b_tot, m_old)]
        m_new = [jnp.maximum(a, b_) for a, b_ in zip(carry_end, mx_end)]
        p_in = [jnp.exp(a - b_) for a, b_ in zip(d_in, m_t)]
        w_end = [jnp.exp(bt - bc + ic - mn) for bt, bc, ic, mn in zip(b_tot, b_col, i_col, m_new)]
        st = []
        for si, (gi, hh) in enumerate(streams):
            bi, d = groups[gi]
            cols = slice(hh * HD, (hh + 1) * HD)
            k = refs[d][1][bi, rows[d], cols]
            v = refs[d][2][bi, rows[d], cols]
            st.append(dict(
                q=refs[d][0][bi, rows[d], cols], k=k,
                v_aug=jnp.concatenate([v, ones_col], axis=1),
                p_in=p_in[si], w_carry=jnp.exp(d_carry[si] - m_t[si]), floor=jnp.exp(-m_t[si]),
                kw=(k.astype(F32) * w_end[si]).astype(BF16),
                f_end=jnp.exp(carry_end[si] - m_new[si]),
                m_new=jnp.broadcast_to(m_new[si], m_rows[si].shape)))
        sc = [(_dot_nt(s["q"], s["k"]) * s["p_in"]).astype(BF16) for s in st]
        c_old = [c_ref[si] for si in range(len(streams))]
        qc = [_dot(s["q"], c.astype(BF16)) for s, c in zip(st, c_old)]
        nd = [s["w_carry"] * a + _dot(m, s["v_aug"]) for s, a, m in zip(st, qc, sc)]
        hout = [a[:, :HD] / jnp.maximum(jnp.abs(a[:, HD:HD + 1]), s["floor"]) for s, a in zip(st, nd)]
        c_new = [s["f_end"] * c + _dot_tn(s["kw"], s["v_aug"]) for s, c in zip(st, c_old)]
        for si, (s, c) in enumerate(zip(st, c_new)):
            c_ref[si] = c
            m_ref[si] = s["m_new"]
        _store_heads(refs, groups, rows, hout, ML_H)
        return carry

    lax.fori_loop(0, CPB, chunk_body, 0)


def _mlstm_scan(main, gates_c, gates_r):
    b, t, _ = main.shape
    nt = t // TM
    w = ML_H * HD
    ngl = gates_c.shape[-1]
    qf, qb = _scan_specs(nt, w, 4)
    kf, kb = _scan_specs(nt, w, 5)
    vf, vb = _scan_specs(nt, w, 6)
    gcf, gcb = _gate_specs(nt, (CH, ngl))
    grf, grb = _gate_specs(nt, (ngl, CH))
    of, ob = _scan_specs(nt, w, 0)
    return pl.pallas_call(
        _mlstm_kernel,
        out_shape=(jax.ShapeDtypeStruct((b, t, w), BF16), jax.ShapeDtypeStruct((b, t, w), BF16)),
        grid=(b // SCAN_BG, nt),
        in_specs=[qf, kf, vf, qb, kb, vb, gcf, grf, gcb, grb],
        out_specs=(of, ob),
        scratch_shapes=[pltpu.VMEM((SCAN_BG * N_DIR * ML_H, HD, 2 * HD), F32),
                        pltpu.VMEM((SCAN_BG * N_DIR * ML_H, 1, HD), F32)],
        compiler_params=_cparams(("arbitrary", "arbitrary")),
        name="mlstm_scan",
    )(main, main, main, main, main, main, gates_c, gates_r, gates_c, gates_r)


def _ret_kernel(qf, kf, vf, qb, kb, vb, intra_ref, cross_ref, tail_ref, cd_ref, of, ob, s_ref):
    i = pl.program_id(1)

    @pl.when(i == 0)
    def _():
        s_ref[...] = jnp.zeros_like(s_ref)

    refs = ((qf, kf, vf, of), (qb, kb, vb, ob))
    groups, streams = _scan_streams(RET_H)

    def chunk_body(cc, carry):
        cidx = (cc, CPB - 1 - cc)
        rows = tuple(pl.ds(pl.multiple_of(c * CH, CH), CH) for c in cidx)
        st = []
        for gi, hh in streams:
            bi, d = groups[gi]
            ti = d * RET_H + hh
            kcols = slice(hh * HD, (hh + 1) * HD)
            q = refs[d][0][bi, rows[d], kcols]
            k = refs[d][1][bi, rows[d], kcols]
            st.append(dict(
                ti=ti, q=q, k=k, v=refs[d][2][bi, rows[d], slice(hh * RET_DV, (hh + 1) * RET_DV)],
                qc=(q.astype(F32) * cross_ref[ti]).astype(BF16),
                kt=(k.astype(F32) * tail_ref[ti]).astype(BF16)))
        sc = [(_dot_nt(s["q"], s["k"]) * intra_ref[s["ti"]]).astype(BF16) for s in st]
        s_old = [s_ref[si] for si in range(len(streams))]
        o = [_dot(m, s["v"]) + _dot(s["qc"], so.astype(BF16)) for s, m, so in zip(st, sc, s_old)]
        s_new = [cd_ref[s["ti"]] * so + _dot_tn(s["kt"], s["v"]) for s, so in zip(st, s_old)]
        for si, s in enumerate(s_new):
            s_ref[si] = s
        _store_heads(refs, groups, rows, o, RET_H)
        return carry

    lax.fori_loop(0, CPB, chunk_body, 0)


def _ret_tables():
    pos = np.arange(CH, dtype=np.float64)
    intra, cross, tail, cd = [], [], [], []
    for d in range(N_DIR):
        expo = 5.0 + np.arange(RET_H, dtype=np.float64)
        if d == 1:
            expo = expo[::-1]
        lg = np.log1p(-np.exp2(-expo))
        p = pos if d == 0 else (CH - 1.0 - pos)
        diff = p[:, None] - p[None, :]
        for hh in range(RET_H):
            intra.append(np.where(diff >= 0, np.exp(np.where(diff >= 0, diff, 0.0) * lg[hh]), 0.0))
            cross.append(np.broadcast_to(np.exp((p + 1.0) * lg[hh])[:, None], (CH, HD)))
            tail.append(np.broadcast_to(np.exp((CH - 1.0 - p) * lg[hh])[:, None], (CH, HD)))
            cd.append(np.full((1, RET_DV), np.exp(CH * lg[hh])))
    f = lambda a: jnp.asarray(np.stack(a), F32)
    return f(intra), f(cross), f(tail), f(cd)


def _ret_scan(main):
    b, t, _ = main.shape
    nt = t // TM
    qw = RET_H * HD
    vw = RET_H * RET_DV
    qf, qb = _scan_specs(nt, qw, 0)
    kf, kb = _scan_specs(nt, qw, 1)
    vf, vb = _scan_specs(nt, vw, 1)
    of, ob = _scan_specs(nt, vw, 0)
    tabs = _ret_tables()
    const = lambda a: pl.BlockSpec(a.shape, lambda b_, i_: (0,) * a.ndim)
    return pl.pallas_call(
        _ret_kernel,
        out_shape=(jax.ShapeDtypeStruct((b, t, vw), BF16), jax.ShapeDtypeStruct((b, t, vw), BF16)),
        grid=(b // SCAN_BG, nt),
        in_specs=[qf, kf, vf, qb, kb, vb] + [const(a) for a in tabs],
        out_specs=(of, ob),
        scratch_shapes=[pltpu.VMEM((SCAN_BG * N_DIR * RET_H, HD, RET_DV), F32)],
        compiler_params=_cparams(("arbitrary", "arbitrary")),
        name="ret_scan",
    )(main, main, main, main, main, main, *tabs)


CODE_SHIFT = 17


def _route_and_pack(h2, rw_ref, rb_ref, h2p_ref, code_ref, cnt_ref, base_ref):
    h2b = h2.astype(BF16)
    logits = _dot(h2b, rw_ref[...]) + rb_ref[...]
    lane = lax.broadcasted_iota(jnp.int32, logits.shape, 1)
    lane_f = lane.astype(F32)
    vals, idxs = [], []
    cur = logits
    for _ in range(TOP_K):
        m = jnp.max(cur, axis=-1, keepdims=True)
        ix = jnp.min(jnp.where(cur == m, lane_f, float(logits.shape[1])), axis=-1, keepdims=True)
        vals.append(m)
        idxs.append(ix)
        cur = jnp.where(lane_f == ix, NEG, cur)
    es = [jnp.exp(v - vals[0]) for v in vals]
    tot = es[0] + es[1] + es[2] + es[3]

    @pl.when((pl.program_id(0) == 0) & (pl.program_id(1) == 0))
    def _():
        base_ref[...] = jnp.zeros_like(base_ref)

    tm = logits.shape[0]
    onehot = jnp.zeros(logits.shape, F32)
    for j in range(TOP_K):
        onehot = jnp.where(lane_f == idxs[j], 1.0, onehot)
    r_i = lax.broadcasted_iota(jnp.int32, (tm, tm), 0)
    c_i = lax.broadcasted_iota(jnp.int32, (tm, tm), 1)
    before = _dot(jnp.where(r_i > c_i, 1.0, 0.0).astype(BF16), onehot.astype(BF16))
    base = base_ref[...]
    pos = before + base
    total = base + before[tm - 1:tm, :] + onehot[tm - 1:tm, :]
    base_ref[...] = total
    cnt_ref[...] = total

    code = jnp.zeros(logits.shape, F32)
    ext_w = jnp.zeros(logits.shape, F32)
    ext_i = jnp.zeros(logits.shape, F32)
    for j in range(TOP_K):
        rank = jnp.sum(jnp.where(lane_f == idxs[j], pos, 0.0), axis=-1, keepdims=True)
        code = jnp.where(lane == j, idxs[j] * float(1 << CODE_SHIFT) + rank, code)
        ext_w = jnp.where(lane == j, es[j] / tot, ext_w)
        ext_i = jnp.where(lane == TOP_K + j, idxs[j], ext_i)
    code_ref[...] = code.astype(jnp.int32)
    half = h2.shape[1] // 2
    r = h2b.astype(F32)
    lo = lax.shift_right_logical(pltpu.bitcast(r[:, :half], U32), jnp.uint32(16))
    hi = pltpu.bitcast(r[:, half:], U32) & jnp.uint32(0xFFFF0000)
    h2p_ref[:, :half] = hi | lo
    h2p_ref[:, half:] = pltpu.bitcast(ext_w, U32) | pltpu.bitcast(ext_i.astype(jnp.int32), U32)


def _merge_even_kernel(ogf, ogb, omf, omb, z_ref, mo_ref, gg_ref, mg_ref, wo_ref, x_ref, g1_ref,
                       n2_ref, sh_ref, sc_ref, rw_ref, rb_ref, x1_ref, h2p_ref, code_ref, cnt_ref, base_ref):
    ones_blk = jnp.ones((HD, HD), BF16)
    og = ogf[...].astype(F32) + ogb[...].astype(F32)
    ms = _group_sum(og * og, ones_blk) * (1.0 / HD)
    a = og * lax.rsqrt(ms + EPS) * gg_ref[...] * _silu(z_ref[...].astype(F32))
    om = omf[...].astype(F32) + omb[...].astype(F32)
    ms = _group_sum(om * om, ones_blk) * (1.0 / HD)
    m = om * lax.rsqrt(ms + EPS) * mg_ref[...] * _sigmoid(mo_ref[...].astype(F32))
    cat = jnp.concatenate([a, m], axis=1).astype(BF16)
    y = _dot(cat, wo_ref[...])
    x1 = x_ref[...] + g1_ref[...] * y
    x1_ref[...] = x1
    h2 = _rms_mod(x1, n2_ref[...], sh_ref[...], sc_ref[...])
    _route_and_pack(h2, rw_ref, rb_ref, h2p_ref, code_ref, cnt_ref, base_ref)


def _merge_odd_kernel(of, ob, gate_ref, ng_ref, wo_ref, x_ref, g1_ref,
                      n2_ref, sh_ref, sc_ref, rw_ref, rb_ref, x1_ref, h2p_ref, code_ref, cnt_ref, base_ref):
    ones_blk = jnp.ones((RET_DV, RET_DV), BF16)
    o = of[...].astype(F32) + ob[...].astype(F32)
    o = o - _group_sum(o, ones_blk) * (1.0 / RET_DV)
    ms = _group_sum(o * o, ones_blk) * (1.0 / RET_DV)
    y = o * lax.rsqrt(ms + EPS) * ng_ref[...] * _silu(gate_ref[...].astype(F32))
    y = _dot(y.astype(BF16), wo_ref[...])
    x1 = x_ref[...] + g1_ref[...] * y
    x1_ref[...] = x1
    h2 = _rms_mod(x1, n2_ref[...], sh_ref[...], sc_ref[...])
    _route_and_pack(h2, rw_ref, rb_ref, h2p_ref, code_ref, cnt_ref, base_ref)


def _merge_out(b, t, d):
    shapes = (jax.ShapeDtypeStruct((b, t, d), F32), jax.ShapeDtypeStruct((b, t, d // 2 + 128), U32),
              jax.ShapeDtypeStruct((b, t, 128), jnp.int32))
    specs = tuple(pl.BlockSpec((None, TM, s.shape[-1]), lambda b_, t_: (b_, t_, 0)) for s in shapes)
    cnt = jax.ShapeDtypeStruct((1, 128), F32)
    return shapes + (cnt,), specs + (pl.BlockSpec((1, 128), lambda b_, t_: (0, 0)),)


def _merge_even(og_f, og_b, om_f, om_b, main, gdn_g, ml_g, w_out, x, g1, n2, sh2, sc2, rw, rb):
    b, t, d = x.shape
    w = GDN_H * HD
    tile = lambda width, col: pl.BlockSpec((None, TM, width), lambda b_, t_: (b_, t_, col))
    const = lambda a: pl.BlockSpec(a.shape, lambda b_, t_: (0,) * a.ndim)
    shapes, specs = _merge_out(b, t, d)
    return pl.pallas_call(
        _merge_even_kernel,
        out_shape=shapes,
        grid=(b, t // TM),
        in_specs=[tile(w, 0), tile(w, 0), tile(w, 0), tile(w, 0), tile(w, 3), tile(w, 7),
                  const(gdn_g), const(ml_g), const(w_out), tile(d, 0), _tile_mod_spec(d),
                  const(n2), _tile_mod_spec(d), _tile_mod_spec(d), const(rw), const(rb)],
        out_specs=specs,
        scratch_shapes=[pltpu.VMEM((1, 128), F32)],
        compiler_params=_cparams(("arbitrary", "arbitrary")),
        name="merge_even",
    )(og_f, og_b, om_f, om_b, main, main, gdn_g, ml_g, w_out, x, g1, n2, sh2, sc2, rw, rb)


def _merge_odd(o_f, o_b, main, ret_g, w_out, x, g1, n2, sh2, sc2, rw, rb):
    b, t, d = x.shape
    vw = RET_H * RET_DV
    tile = lambda width, col: pl.BlockSpec((None, TM, width), lambda b_, t_: (b_, t_, col))
    const = lambda a: pl.BlockSpec(a.shape, lambda b_, t_: (0,) * a.ndim)
    shapes, specs = _merge_out(b, t, d)
    return pl.pallas_call(
        _merge_odd_kernel,
        out_shape=shapes,
        grid=(b, t // TM),
        in_specs=[tile(vw, 0), tile(vw, 0), tile(vw, 2), const(ret_g), const(w_out), tile(d, 0),
                  _tile_mod_spec(d), const(n2), _tile_mod_spec(d), _tile_mod_spec(d), const(rw), const(rb)],
        out_specs=specs,
        scratch_shapes=[pltpu.VMEM((1, 128), F32)],
        compiler_params=_cparams(("arbitrary", "arbitrary")),
        name="merge_odd",
    )(o_f, o_b, main, ret_g, w_out, x, g1, n2, sh2, sc2, rw, rb)


def _proj_odd_kernel(x_ref, acc_ref, g2_ref, g_ref, sh_ref, sc_ref, w_ref, cos_ref, sin_ref,
                     x2_ref, main_ref):
    x2 = x_ref[...] + g2_ref[...] * acc_ref[...]
    x2_ref[...] = x2
    hb = _rms_mod(x2, g_ref[...], sh_ref[...], sc_ref[...]).astype(BF16)
    qk_w = RET_H * HD
    cos = cos_ref[...]
    sin = sin_ref[...]
    n_seg = w_ref.shape[1] // qk_w
    for seg in range(n_seg):
        sl = slice(seg * qk_w, (seg + 1) * qk_w)
        u = _dot(hb, w_ref[:, sl])
        if seg < 2:
            parts = []
            for hh in range(RET_H):
                uh = u[:, hh * HD:(hh + 1) * HD]
                parts.append(uh * cos + pltpu.roll(uh, HD // 2, 1) * sin)
            u = jnp.concatenate(parts, axis=1)
            if seg == 1:
                u = u * (HD ** -0.5)
        main_ref[:, sl] = u.astype(BF16)


def _rope_tables(t):
    half = HD // 2
    freqs = ROPE_BASE ** (-jnp.arange(half, dtype=F32) / half)
    ang = jnp.arange(t, dtype=F32)[:, None] * freqs[None, :]
    cos, sin = jnp.cos(ang), jnp.sin(ang)
    return jnp.concatenate([cos, cos], axis=1), jnp.concatenate([-sin, sin], axis=1)


def _proj_odd(x, acc, g2, g, sh, sc, w):
    b, t, d = x.shape
    n = w.shape[1]
    cos, sin = _rope_tables(t)
    const = lambda a: pl.BlockSpec(a.shape, lambda b_, t_: (0,) * a.ndim)
    tile = lambda width: pl.BlockSpec((None, TM, width), lambda b_, t_: (b_, t_, 0))
    rope = pl.BlockSpec((TM, HD), lambda b_, t_: (t_, 0))
    nt = t // TM
    acc_tile = pl.BlockSpec((TM, d), lambda b_, t_: (b_ * nt + t_, 0))
    return pl.pallas_call(
        _proj_odd_kernel,
        out_shape=(jax.ShapeDtypeStruct((b, t, d), F32), jax.ShapeDtypeStruct((b, t, n), BF16)),
        grid=(b, nt),
        in_specs=[tile(d), acc_tile, _tile_mod_spec(d), const(g), _tile_mod_spec(d), _tile_mod_spec(d),
                  const(w), rope, rope],
        out_specs=(tile(d), tile(n)),
        compiler_params=_cparams(("arbitrary", "arbitrary")),
        name="proj_odd",
    )(x, acc, g2, g, sh, sc, w, cos, sin)


ROWS_PER_STEP = 8


def _moe_gather_kernel(src_ref, h_ref, xs_ref):
    def body(r8, carry):
        base = pl.multiple_of(r8 * ROWS_PER_STEP, ROWS_PER_STEP)
        rows = [h_ref[pl.ds(src_ref[0, base + j], 1), :] for j in range(ROWS_PER_STEP)]
        xs_ref[pl.ds(base, ROWS_PER_STEP), :] = jnp.concatenate(rows, axis=0)
        return carry

    lax.fori_loop(0, MOE_BLOCK // ROWS_PER_STEP, body, 0)


def _moe_gather(row_src, h2p, n_blocks):
    n, half = h2p.shape
    return pl.pallas_call(
        _moe_gather_kernel,
        out_shape=jax.ShapeDtypeStruct((n_blocks * MOE_BLOCK, half), U32),
        grid=(n_blocks,),
        in_specs=[pl.BlockSpec((None, 1, MOE_BLOCK), lambda i: (i, 0, 0), memory_space=pltpu.SMEM),
                  pl.BlockSpec((n, half), lambda i: (0, 0))],
        out_specs=pl.BlockSpec((MOE_BLOCK, half), lambda i: (i, 0)),
        compiler_params=_cparams(("arbitrary",)),
        name="moe_gather",
    )(row_src.reshape(n_blocks, 1, MOE_BLOCK), h2p)


LANES = 128


def _moe_mm_kernel(be_ref, nu_ref, xs_ref, wgu_ref, bgu_ref, wdn_ref, bdn_ref, ys_ref, wgu_s, wdn_s):
    i = pl.program_id(0)
    e = be_ref[i]
    half = wgu_ref.shape[0] // 2
    hl = LANES // 2

    @pl.when(i >= nu_ref[0])
    def _():
        ys_ref[...] = jnp.zeros_like(ys_ref)

    @pl.when(i < nu_ref[0])
    def _():
        @pl.when((i == 0) | (e != be_ref[jnp.maximum(i - 1, 0)]))
        def _():
            wgu_s[...] = wgu_ref[...].astype(BF16)
            for p in range(wdn_ref.shape[0] // LANES):
                first = wdn_ref[p * LANES:p * LANES + hl, :].astype(BF16).astype(F32)
                second = wdn_ref[p * LANES + hl:(p + 1) * LANES, :].astype(BF16).astype(F32)
                word = (pltpu.bitcast(second, U32) & jnp.uint32(0xFFFF0000)) | \
                    lax.shift_right_logical(pltpu.bitcast(first, U32), jnp.uint32(16))
                wdn_s[p * LANES:(p + 1) * LANES, :] = pltpu.bitcast(word, BF16)

        xu = xs_ref[:, :half]
        ext = xs_ref[:, half:]
        ext_lane = lax.broadcasted_iota(jnp.int32, ext.shape, 1)
        ids = pltpu.roll(pltpu.bitcast(ext, jnp.int32), ext.shape[1] - TOP_K, 1)
        hit = jnp.where(ext_lane < TOP_K, ids, -1) == e
        row_w = jnp.sum(jnp.where(hit, pltpu.bitcast(ext, F32), 0.0), axis=-1, keepdims=True)
        lo = pltpu.bitcast(lax.shift_left(xu, jnp.uint32(16)), F32).astype(BF16)
        hi = pltpu.bitcast(xu & jnp.uint32(0xFFFF0000), F32).astype(BF16)
        gu = _dot(lo, wgu_s[:half, :]) + _dot(hi, wgu_s[half:, :]) + bgu_ref[...]
        even = (lax.broadcasted_iota(jnp.int32, (gu.shape[0], LANES), 1) & 1) == 0
        acts = []
        for p in range(gu.shape[1] // (2 * LANES)):
            a = gu[:, 2 * p * LANES:(2 * p + 1) * LANES]
            b = gu[:, (2 * p + 1) * LANES:(2 * p + 2) * LANES]
            gate = jnp.minimum(jnp.where(even, a, pltpu.roll(b, 1, 1)), SWIGLU_LIMIT)
            up = jnp.clip(jnp.where(even, pltpu.roll(a, LANES - 1, 1), b), -SWIGLU_LIMIT, SWIGLU_LIMIT)
            acts.append(((up + 1.0) * gate * _sigmoid(SWIGLU_ALPHA * gate)).astype(BF16))
        y = _dot(jnp.concatenate(acts, axis=1), wdn_s[...]) + bdn_ref[...]
        ys_ref[...] = y * row_w


def _moe_mm(block_e, n_used, xs, layer, w_gu, b_gu, w_dn, b_dn):
    rows, width = xs.shape
    n_blocks = rows // MOE_BLOCK
    depth, n_e, d, f2 = w_gu.shape
    ew = lambda shape: pl.BlockSpec((None, None) + shape, lambda i, be, nu: (layer, be[i], 0, 0))
    grid_spec = pltpu.PrefetchScalarGridSpec(
        num_scalar_prefetch=2,
        grid=(n_blocks,),
        in_specs=[pl.BlockSpec((MOE_BLOCK, width), lambda i, be, nu: (i, 0)),
                  ew((d, f2)), ew((1, f2)), ew((f2 // 2, d)), ew((1, d))],
        out_specs=pl.BlockSpec((MOE_BLOCK, d), lambda i, be, nu: (i, 0)),
        scratch_shapes=[pltpu.VMEM((d, f2), BF16), pltpu.VMEM((f2 // 2, d), BF16)],
    )
    return pl.pallas_call(
        _moe_mm_kernel,
        out_shape=jax.ShapeDtypeStruct((rows, d), F32),
        grid_spec=grid_spec,
        compiler_params=_cparams(("arbitrary",)),
        name="moe_mm",
    )(block_e, n_used, xs, w_gu, b_gu.reshape(depth, n_e, 1, f2), w_dn, b_dn.reshape(depth, n_e, 1, d))


COMBINE_COLS = 512


def _moe_combine_kernel(nu_ref, dst_ref, ys_ref, acc_ref):
    i = pl.program_id(1)

    @pl.when(i == 0)
    def _():
        acc_ref[...] = jnp.zeros_like(acc_ref)

    def body(r8, carry):
        base = pl.multiple_of(r8 * ROWS_PER_STEP, ROWS_PER_STEP)
        y8 = ys_ref[pl.ds(base, ROWS_PER_STEP), :]
        toks = [dst_ref[0, base + j] for j in range(ROWS_PER_STEP)]
        cur = [acc_ref[pl.ds(tk, 1), :] for tk in toks]
        for j in range(ROWS_PER_STEP):
            acc_ref[pl.ds(toks[j], 1), :] = cur[j] + y8[j:j + 1, :]
        return carry

    @pl.when(i < nu_ref[0])
    def _():
        lax.fori_loop(0, MOE_BLOCK // ROWS_PER_STEP, body, 0)


def _moe_combine(n_used, row_dst, ys, n_rows_out):
    rows, d = ys.shape
    n_blocks = rows // MOE_BLOCK
    grid_spec = pltpu.PrefetchScalarGridSpec(
        num_scalar_prefetch=1,
        grid=(d // COMBINE_COLS, n_blocks),
        in_specs=[pl.BlockSpec((None, 1, MOE_BLOCK), lambda j, i, nu: (i, 0, 0), memory_space=pltpu.SMEM),
                  pl.BlockSpec((MOE_BLOCK, COMBINE_COLS), lambda j, i, nu: (i, j))],
        out_specs=pl.BlockSpec((n_rows_out, COMBINE_COLS), lambda j, i, nu: (0, j),
                               pipeline_mode=pl.Buffered(1)),
    )
    return pl.pallas_call(
        _moe_combine_kernel,
        out_shape=jax.ShapeDtypeStruct((n_rows_out, d), F32),
        grid_spec=grid_spec,
        compiler_params=_cparams(("arbitrary", "arbitrary")),
        name="moe_combine",
    )(n_used, row_dst.reshape(n_blocks, 1, MOE_BLOCK), ys)


def _moe_inverse_kernel(dest_ref, inv_ref):
    def init(r, carry):
        inv_ref[r] = -1
        return carry

    lax.fori_loop(0, inv_ref.shape[0], init, 0, unroll=8)

    def body(a, carry):
        inv_ref[dest_ref[a]] = a
        return carry

    lax.fori_loop(0, dest_ref.shape[0], body, 0, unroll=8)


def _moe_inverse(dest, rows):
    smem = pl.BlockSpec(memory_space=pltpu.SMEM)
    return pl.pallas_call(
        _moe_inverse_kernel,
        out_shape=jax.ShapeDtypeStruct((rows,), jnp.int32),
        in_specs=[smem],
        out_specs=smem,
        name="moe_inverse",
    )(dest)


def _moe(h2p, codes, counts, layer, w_gu, b_gu, w_dn, b_dn):
    n = h2p.shape[0]
    n_assign = n * TOP_K
    n_blocks = -(-n_assign // MOE_BLOCK) + N_EXPERTS
    rows = n_blocks * MOE_BLOCK
    padded = -(-counts // MOE_BLOCK) * MOE_BLOCK
    ends = jnp.cumsum(padded)
    offsets = ends - padded
    starts = jnp.arange(n_blocks, dtype=jnp.int32) * MOE_BLOCK
    block_e = jnp.minimum(jnp.sum(ends[None, :] <= starts[:, None], axis=1), N_EXPERTS - 1).astype(jnp.int32)
    n_used = (ends[-1:] // MOE_BLOCK).astype(jnp.int32)
    expert = lax.shift_right_logical(codes, CODE_SHIFT)
    which = expert[:, None] == jnp.arange(N_EXPERTS, dtype=jnp.int32)[None, :]
    dest = (codes & ((1 << CODE_SHIFT) - 1)) + jnp.sum(jnp.where(which, offsets[None, :], 0), axis=1)
    inv = _moe_inverse(dest.astype(jnp.int32), rows)
    tok = lax.shift_right_logical(inv, TOP_K.bit_length() - 1)
    row_src = jnp.where(inv >= 0, tok, 0)
    row_dst = jnp.where(inv >= 0, tok, n)
    xs = _moe_gather(row_src, h2p, n_blocks)
    ys = _moe_mm(block_e, n_used, xs, layer, w_gu, b_gu, w_dn, b_dn)
    return _moe_combine(n_used, row_dst, ys, n + 8)


def _final_kernel(x_ref, acc_ref, g2_ref, g_ref, o_ref):
    x = x_ref[...] + g2_ref[...] * acc_ref[...]
    ms = jnp.mean(x * x, axis=-1, keepdims=True)
    o_ref[...] = x * lax.rsqrt(ms + EPS) * g_ref[...]


def _final(x, acc, g2, g, n_ctx_tiles):
    b, t, d = x.shape
    nt = t // TM - n_ctx_tiles
    tile_in = pl.BlockSpec((None, TM, d), lambda b_, t_: (b_, t_ + n_ctx_tiles, 0))
    acc_tile = pl.BlockSpec((TM, d), lambda b_, t_: (b_ * (t // TM) + t_ + n_ctx_tiles, 0))
    return pl.pallas_call(
        _final_kernel,
        out_shape=jax.ShapeDtypeStruct((b, nt * TM, d), F32),
        grid=(b, nt),
        in_specs=[tile_in, acc_tile,
                  pl.BlockSpec((None, None, 1, d), lambda b_, t_: (b_, 1, 0, 0)),
                  pl.BlockSpec((1, d), lambda b_, t_: (0, 0))],
        out_specs=pl.BlockSpec((None, TM, d), lambda b_, t_: (b_, t_, 0)),
        compiler_params=_cparams(("arbitrary", "arbitrary")),
        name="final_norm",
    )(x, acc, g2, g)


def _mod_tables(mod, b, d):
    outs = []
    for j in range(6):
        m = mod[:, j * d:(j + 1) * d]
        lat = m[:b]
        ctx = jnp.broadcast_to(m[b:b + 1], (b, d))
        outs.append(jnp.stack([ctx, lat], axis=1)[:, :, None, :])
    return outs


def _router_params(router_w, router_b):
    d, e = router_w.shape
    rw = jnp.zeros((d, 128), F32).at[:, :e].set(router_w).astype(BF16)
    rb = jnp.full((1, 128), NEG, F32).at[0, :e].set(router_b)
    return rw, rb


def kernel(x, c, ctx, c_ctx, mod_w, mod_b, norm1_g, norm2_g, ev_w_in, ev_conv_w, gdn_a_log, gdn_dt_bias,
           gdn_norm_g, ml_i_bias, ml_f_bias, ml_norm_g, ev_w_out, od_w_in, ret_norm_g, od_w_out,
           router_w, router_b, moe_w_gu, moe_b_gu, moe_w_dn, moe_b_dn, final_g):
    b, s, d = x.shape
    n_ctx = ctx.shape[1]
    depth = mod_w.shape[0]
    assert n_ctx == TM and s % TM == 0 and depth == 2 and b % SCAN_BG == 0
    t = n_ctx + s
    n_tok = b * t

    cond = jnp.concatenate([c, c_ctx[None, :], jnp.zeros((8 - b - 1, d), F32)], axis=0)
    mod = _adaln(cond, mod_w, mod_b)
    xa = jnp.concatenate([ctx, x], axis=1)

    sh1, sc1, g1, sh2, sc2, g2 = _mod_tables(mod[0], b, d)
    qk_w = GDN_H * HD
    conv_ch = 3 * qk_w
    ng = N_DIR * GDN_H
    w_in = ev_w_in[0]
    o_z = conv_ch
    o_a = o_z + qk_w
    o_mq = o_a + 2 * ng
    o_i = o_mq + 4 * qk_w
    w_main = jnp.concatenate([w_in[:, :o_a], w_in[:, o_mq:o_i]], axis=1).astype(BF16)
    w_gate = jnp.concatenate([w_in[:, o_a:o_mq], w_in[:, o_i:o_i + 2 * ng],
                              jnp.zeros((d, 128 - 4 * ng), F32)], axis=1).astype(BF16)
    zeros_g = jnp.zeros((ng,), F32)
    rate = jnp.concatenate([jnp.exp(gdn_a_log[0].astype(F32)).reshape(-1), jnp.zeros((128 - ng,), F32)])[None, :]
    gbias = jnp.concatenate([gdn_dt_bias[0].reshape(-1), zeros_g, ml_i_bias[0].reshape(-1),
                             ml_f_bias[0].reshape(-1), jnp.zeros((128 - 4 * ng,), F32)])[None, :].astype(F32)
    main, gates = _proj_even(xa, norm1_g[0][None, :], sh1, sc1, w_main, w_gate, ev_conv_w[0], rate, gbias)
    gates_c = gates.reshape(b, t // CH, CH, 4 * ng)
    gates_r = jnp.swapaxes(gates_c, 2, 3)
    og_f, og_b = _gdn_scan(main, gates_c, gates_r)
    om_f, om_b = _mlstm_scan(main, gates_c, gates_r)
    rw, rb = _router_params(router_w[0], router_b[0])
    gdn_g = jnp.tile(gdn_norm_g[0], GDN_H)[None, :]
    x1, h2p, code, cnt = _merge_even(og_f, og_b, om_f, om_b, main, gdn_g, ml_norm_g[0][None, :],
                                     ev_w_out[0].astype(BF16), xa, g1, norm2_g[0][None, :], sh2, sc2, rw, rb)
    acc = _moe(h2p.reshape(n_tok, -1), code[:, :, :TOP_K].reshape(-1), cnt[0, :N_EXPERTS].astype(jnp.int32),
               0, moe_w_gu, moe_b_gu, moe_w_dn, moe_b_dn)
    g2_prev = g2

    sh1, sc1, g1, sh2, sc2, g2 = _mod_tables(mod[1], b, d)
    x2, main_o = _proj_odd(x1, acc, g2_prev, norm1_g[1][None, :], sh1, sc1, od_w_in[0].astype(BF16))
    o_f, o_b = _ret_scan(main_o)
    rw, rb = _router_params(router_w[1], router_b[1])
    x3, h2p, code, cnt = _merge_odd(o_f, o_b, main_o, ret_norm_g[0][None, :], od_w_out[0].astype(BF16),
                                    x2, g1, norm2_g[1][None, :], sh2, sc2, rw, rb)
    acc = _moe(h2p.reshape(n_tok, -1), code[:, :, :TOP_K].reshape(-1), cnt[0, :N_EXPERTS].astype(jnp.int32),
               1, moe_w_gu, moe_b_gu, moe_w_dn, moe_b_dn)
    return _final(x3, acc, g2, final_g[None, :], n_ctx // TM)
```

```python
import functools
import math

import jax
import jax.numpy as jnp
import numpy as np
from jax import lax
from jax.experimental import pallas as pl
from jax.experimental.pallas import tpu as pltpu

F32 = jnp.float32
BF16 = jnp.bfloat16
U32 = jnp.uint32
HIGHEST = lax.Precision.HIGHEST

EPS = 1e-6
CH = 64
TM = 256
CPB = TM // CH
HD = 128
N_DIR = 2
GDN_H = 4
ML_H = 4
RET_H = 8
RET_DV = 256
CONV_W = 3
N_EXPERTS = 32
TOP_K = 4
SWIGLU_ALPHA = 1.702
SWIGLU_LIMIT = 7.0
MOE_BLOCK = 256
ROPE_BASE = 10000.0
NEG = -1e30
VMEM_LIMIT = 56 * 1024 * 1024


def _cparams(sem):
    return pltpu.CompilerParams(dimension_semantics=sem, vmem_limit_bytes=VMEM_LIMIT)


def _dot(a, b, precision=None):
    return jnp.dot(a, b, preferred_element_type=F32, precision=precision)


def _dot_nt(a, b):
    return lax.dot_general(a, b, (((1,), (1,)), ((), ())), preferred_element_type=F32)


def _dot_tn(a, b):
    return lax.dot_general(a, b, (((0,), (0,)), ((), ())), preferred_element_type=F32)


def _sigmoid(x):
    return 1.0 / (1.0 + jnp.exp(-x))


def _silu(x):
    return x * _sigmoid(x)


def _group_sum(x, ones_blk):
    w = ones_blk.shape[0]
    hi = x.astype(BF16)
    lo = (x - hi.astype(F32)).astype(BF16)
    outs = []
    for j in range(x.shape[1] // w):
        sl = slice(j * w, (j + 1) * w)
        outs.append(_dot(hi[:, sl], ones_blk) + _dot(lo[:, sl], ones_blk))
    return outs[0] if len(outs) == 1 else jnp.concatenate(outs, axis=1)


def _rms_mod(x, g, sh, sc):
    ms = jnp.mean(x * x, axis=-1, keepdims=True)
    return (x * lax.rsqrt(ms + EPS) * g) * (1.0 + sc) + sh


LANES = 128
PACK = 4


def _store_row_packed(ref, x, lead=()):
    rows = x.shape[0]
    for g in range(x.shape[1] // LANES):
        ref[lead + (pl.ds(g, rows, stride=PACK), slice(None))] = x[:, g * LANES:(g + 1) * LANES]


def _load_row_packed(ref, rows, lead=()):
    return jnp.concatenate([ref[lead + (pl.ds(g, rows, stride=PACK), slice(None))] for g in range(PACK)], axis=1)


def _load_acc(acc_ref):
    rows = acc_ref.shape[1] // PACK
    return jnp.concatenate([_load_row_packed(acc_ref, rows, lead=(h,)) for h in range(acc_ref.shape[0])], axis=1)


def _acc_tile_spec(acc, tile_of):
    return pl.BlockSpec((acc.shape[0], TM * PACK, LANES), lambda b_, t_: (0, tile_of(b_, t_), 0))


def _adaln_kernel(c_ref, w_ref, b_ref, o_ref):
    c = c_ref[...]
    o_ref[...] = _dot(_silu(c), w_ref[...], precision=HIGHEST) + b_ref[...]


def _adaln(cond, mod_w, mod_b):
    depth, d, d6 = mod_w.shape
    n = d6 // d
    return pl.pallas_call(
        _adaln_kernel,
        out_shape=jax.ShapeDtypeStruct((depth, cond.shape[0], d6), F32),
        grid=(depth, n),
        in_specs=[pl.BlockSpec(cond.shape, lambda l, j: (0, 0)),
                  pl.BlockSpec((None, d, d), lambda l, j: (l, 0, j)),
                  pl.BlockSpec((None, 1, d), lambda l, j: (l, 0, j))],
        out_specs=pl.BlockSpec((None, cond.shape[0], d), lambda l, j: (l, 0, j)),
        compiler_params=_cparams(("arbitrary", "arbitrary")),
        name="adaln",
    )(cond, mod_w, mod_b.reshape(depth, 1, d6))


def _proj_even_kernel(x_ref, g_ref, sh_ref, sc_ref, w_ref, wg_ref, cw_ref, rate_ref, gb_ref,
                      main_ref, gates_ref):
    t = pl.program_id(1)
    h = _rms_mod(x_ref[...], g_ref[...], sh_ref[...], sc_ref[...])
    hb = h.astype(BF16)
    qk_w = GDN_H * HD
    ones_blk = jnp.ones((HD, HD), BF16)

    row = lax.broadcasted_iota(jnp.int32, (TM, 1), 0)
    pos = jnp.where(t > 0, row & (CH - 1), row)
    last = jnp.where(t > 0, CH - 1, TM - 1)
    left_ok = pos != 0
    right_ok = pos != last
    for seg in range(3):
        sl = slice(seg * qk_w, (seg + 1) * qk_w)
        u = _dot(hb, w_ref[:, sl])
        um = jnp.where(left_ok, pltpu.roll(u, 1, 0), 0.0)
        up = jnp.where(right_ok, pltpu.roll(u, TM - 1, 0), 0.0)
        cv = _silu(um * cw_ref[0:1, sl] + u * cw_ref[1:2, sl] + up * cw_ref[2:3, sl])
        if seg < 2:
            ss = _group_sum(cv * cv, ones_blk)
            cv = cv * lax.rsqrt(ss + EPS)
            if seg == 0:
                cv = cv * (HD ** -0.5)
        main_ref[:, sl] = cv.astype(BF16)
    for seg in range(3, 8):
        sl = slice(seg * qk_w, (seg + 1) * qk_w)
        u = _dot(hb, w_ref[:, sl])
        if seg == 5:
            u = u * (HD ** -0.5)
        main_ref[:, sl] = u.astype(BF16)

    z = _dot(hb, wg_ref[...]) + gb_ref[...]
    tl = jnp.log(1.0 + jnp.exp(-jnp.abs(z)))
    sp_pos = jnp.maximum(z, 0.0) + tl
    sp_neg = jnp.maximum(-z, 0.0) + tl
    lane = lax.broadcasted_iota(jnp.int32, z.shape, 1)
    ng = N_DIR * GDN_H
    res = jnp.where(lane < ng, -rate_ref[...] * sp_pos,
                    jnp.where(lane < 2 * ng, _sigmoid(z),
                              jnp.where(lane < 3 * ng, z, -sp_neg)))
    gates_ref[...] = res[:, :gates_ref.shape[-1]]


def _tile_mod_spec(d):
    return pl.BlockSpec((None, None, 1, d), lambda b, t: (b, jnp.minimum(t, 1), 0, 0))


def _proj_even(x, g, sh, sc, w_main, w_gate, conv_w, rate, gbias):
    b, t, d = x.shape
    n = w_main.shape[1]
    ngl = 4 * N_DIR * GDN_H
    const = lambda shape: pl.BlockSpec(shape, lambda b_, t_: (0,) * len(shape))
    return pl.pallas_call(
        _proj_even_kernel,
        out_shape=(jax.ShapeDtypeStruct((b, t, n), BF16), jax.ShapeDtypeStruct((b, t, ngl), F32)),
        grid=(b, t // TM),
        in_specs=[pl.BlockSpec((None, TM, d), lambda b_, t_: (b_, t_, 0)),
                  const((1, d)), _tile_mod_spec(d), _tile_mod_spec(d),
                  const(w_main.shape), const(w_gate.shape), const(conv_w.shape),
                  const(rate.shape), const(gbias.shape)],
        out_specs=(pl.BlockSpec((None, TM, n), lambda b_, t_: (b_, t_, 0)),
                   pl.BlockSpec((None, TM, ngl), lambda b_, t_: (b_, t_, 0))),
        compiler_params=_cparams(("arbitrary", "arbitrary")),
        name="proj_even",
    )(x, g, sh, sc, w_main, w_gate, conv_w, rate, gbias)


def _rev_tile(i, nt):
    return jnp.where(i == 0, 0, nt - i)


def _tri_masks():
    r = lax.broadcasted_iota(jnp.int32, (CH, CH), 0)
    c = lax.broadcasted_iota(jnp.int32, (CH, CH), 1)
    return r >= c, r > c, r <= c, r < c


SCAN_BG = 2


def _scan_streams(heads):
    groups = [(bi, d) for bi in range(SCAN_BG) for d in range(N_DIR)]
    streams = [(gi, hh) for gi in range(len(groups)) for hh in range(heads)]
    return groups, streams


def _cumsum_both(groups, gcs, grs, lower, upper):
    tri = (lower.astype(F32), upper.astype(F32))
    cs_c = [_dot(tri[d], gc, precision=HIGHEST) for (_, d), gc in zip(groups, gcs)]
    cs_r = [_dot(gr, tri[1 - d], precision=HIGHEST) for (_, d), gr in zip(groups, grs)]
    return cs_c, cs_r


def _store_heads(refs, groups, rows, outs, heads):
    for gi, (bi, d) in enumerate(groups):
        o_ref = refs[d][-1]
        tile = jnp.concatenate(outs[gi * heads:(gi + 1) * heads], axis=1)
        o_ref[bi, rows[d], :] = tile.astype(o_ref.dtype)


def _gdn_kernel(qf, kf, vf, qb, kb, vb, gcf, grf, gcb, grb, of, ob, s_ref):
    i = pl.program_id(1)

    @pl.when(i == 0)
    def _():
        s_ref[...] = jnp.zeros_like(s_ref)

    lower, lstrict, upper, ustrict = _tri_masks()
    refs = ((qf, kf, vf, gcf, grf, of), (qb, kb, vb, gcb, grb, ob))
    ng = N_DIR * GDN_H
    groups, streams = _scan_streams(GDN_H)

    def chunk_body(cc, carry):
        cidx = (cc, CPB - 1 - cc)
        rows = tuple(pl.ds(pl.multiple_of(c * CH, CH), CH) for c in cidx)
        gcs = [refs[d][3][bi, cidx[d]] for bi, d in groups]
        grs = [refs[d][4][bi, cidx[d]] for bi, d in groups]
        cs_c, cs_r = _cumsum_both(groups, gcs, grs, lower, upper)
        st = []
        for gi, hh in streams:
            bi, d = groups[gi]
            ci = d * GDN_H + hh
            cols = slice(hh * HD, (hh + 1) * HD)
            incl, strict = (lower, lstrict) if d == 0 else (upper, ustrict)
            g_col = cs_c[gi][:, ci:ci + 1]
            g_row = cs_r[gi][ci:ci + 1, :]
            beta = gcs[gi][:, ng + ci:ng + ci + 1]
            tot = g_col[CH - 1:CH, :] if d == 0 else g_col[0:1, :]
            decay = jnp.where(incl, jnp.exp(jnp.where(incl, g_col - g_row, 0.0)), 0.0)
            q = refs[d][0][bi, rows[d], cols].astype(F32)
            k = refs[d][1][bi, rows[d], cols].astype(F32)
            v = refs[d][2][bi, rows[d], cols].astype(F32)
            kbeta = k * beta
            eg = jnp.exp(g_col)
            st.append(dict(
                strict=strict, decay=decay, kb=k.astype(BF16), kbetab=kbeta.astype(BF16), qb=q.astype(BF16),
                x=jnp.concatenate([v * beta, kbeta * eg], axis=1),
                qe=(q * eg).astype(BF16), kdec=(k * jnp.exp(tot - g_col)).astype(BF16), cd=jnp.exp(tot)))
        kk = [_dot_nt(s["kbetab"], s["kb"]) for s in st]
        qk = [(_dot_nt(s["qb"], s["kb"]) * s["decay"]).astype(BF16) for s in st]
        p = [-jnp.where(s["strict"], m * s["decay"], 0.0) for s, m in zip(st, kk)]
        x = [s["x"] for s in st]
        for j in range(6):
            pb = [m.astype(BF16) for m in p]
            x = [xx + _dot(m, xx.astype(BF16)) for m, xx in zip(pb, x)]
            if j < 5:
                p = [_dot(m, m) for m in pb]
        s_old = [s_ref[si] for si in range(len(streams))]
        sb = [s.astype(BF16) for s in s_old]
        vnb = [(xx[:, :HD] - _dot(xx[:, HD:].astype(BF16), s)).astype(BF16) for xx, s in zip(x, sb)]
        o = [_dot(s["qe"], sbi) + _dot(m, vn) for s, sbi, m, vn in zip(st, sb, qk, vnb)]
        s_new = [so * s["cd"] + _dot_tn(s["kdec"], vn) for s, so, vn in zip(st, s_old, vnb)]
        for si, s in enumerate(s_new):
            s_ref[si] = s
        _store_heads(refs, groups, rows, o, GDN_H)
        return carry

    lax.fori_loop(0, CPB, chunk_body, 0)


def _scan_specs(nt, width, col):
    fwd = pl.BlockSpec((SCAN_BG, TM, width), lambda b, i: (b, i, col))
    bwd = pl.BlockSpec((SCAN_BG, TM, width), lambda b, i: (b, _rev_tile(i, nt), col))
    return fwd, bwd


def _gate_specs(nt, shape):
    fwd = pl.BlockSpec((SCAN_BG, CPB) + shape, lambda b, i: (b, i, 0, 0))
    bwd = pl.BlockSpec((SCAN_BG, CPB) + shape, lambda b, i: (b, _rev_tile(i, nt), 0, 0))
    return fwd, bwd


def _gdn_scan(main, gates_c, gates_r):
    b, t, _ = main.shape
    nt = t // TM
    w = GDN_H * HD
    ngl = gates_c.shape[-1]
    qf, qb = _scan_specs(nt, w, 0)
    kf, kb = _scan_specs(nt, w, 1)
    vf, vb = _scan_specs(nt, w, 2)
    gcf, gcb = _gate_specs(nt, (CH, ngl))
    grf, grb = _gate_specs(nt, (ngl, CH))
    of, ob = _scan_specs(nt, w, 0)
    return pl.pallas_call(
        _gdn_kernel,
        out_shape=(jax.ShapeDtypeStruct((b, t, w), BF16), jax.ShapeDtypeStruct((b, t, w), BF16)),
        grid=(b // SCAN_BG, nt),
        in_specs=[qf, kf, vf, qb, kb, vb, gcf, grf, gcb, grb],
        out_specs=(of, ob),
        scratch_shapes=[pltpu.VMEM((SCAN_BG * N_DIR * GDN_H, HD, HD), F32)],
        compiler_params=_cparams(("arbitrary", "arbitrary")),
        name="gdn_scan",
    )(main, main, main, main, main, main, gates_c, gates_r, gates_c, gates_r)


def _mlstm_kernel(qf, kf, vf, qb, kb, vb, gcf, grf, gcb, grb, of, ob, c_ref, m_ref):
    i = pl.program_id(1)

    @pl.when(i == 0)
    def _():
        c_ref[...] = jnp.zeros_like(c_ref)
        m_ref[...] = jnp.zeros_like(m_ref)

    lower, _, upper, _ = _tri_masks()
    refs = ((qf, kf, vf, gcf, grf, of), (qb, kb, vb, gcb, grb, ob))
    ng = N_DIR * GDN_H
    i_off = 2 * ng
    f_off = 2 * ng + N_DIR * ML_H
    lane = lax.broadcasted_iota(jnp.int32, (CH, HD), 1)
    ones_col = jnp.where(lane == 0, 1.0, 0.0).astype(BF16)

    groups, streams = _scan_streams(ML_H)

    def chunk_body(cc, carry):
        cidx = (cc, CPB - 1 - cc)
        rows = tuple(pl.ds(pl.multiple_of(c * CH, CH), CH) for c in cidx)
        gcs = [refs[d][3][bi, cidx[d]] for bi, d in groups]
        grs = [refs[d][4][bi, cidx[d]] for bi, d in groups]
        cs_c, cs_r = _cumsum_both(groups, gcs, grs, lower, upper)
        ns = len(streams)
        dirs = [groups[gi][1] for gi, _ in streams]
        chan = [groups[gi][1] * ML_H + hh for gi, hh in streams]
        b_col = [cs_c[gi][:, f_off + c:f_off + c + 1] for (gi, _), c in zip(streams, chan)]
        b_row = [cs_r[gi][f_off + c:f_off + c + 1, :] for (gi, _), c in zip(streams, chan)]
        i_col = [gcs[gi][:, i_off + c:i_off + c + 1] for (gi, _), c in zip(streams, chan)]
        i_row = [grs[gi][i_off + c:i_off + c + 1, :] for (gi, _), c in zip(streams, chan)]
        b_tot = [bc[CH - 1:CH, :] if d == 0 else bc[0:1, :] for bc, d in zip(b_col, dirs)]
        m_rows = [m_ref[si] for si in range(ns)]
        m_old = [mr[:, 0:1] for mr in m_rows]
        d_in = [jnp.where(lower if d == 0 else upper, bc - br + ir, NEG)
                for d, bc, br, ir in zip(dirs, b_col, b_row, i_row)]
        d_end = [bt - br + ir for bt, br, ir in zip(b_tot, b_row, i_row)]
        mx_in = [jnp.max(a, axis=-1, keepdims=True) for a in d_in]
        mx_end = [jnp.max(a, axis=-1, keepdims=True) for a in d_end]
        d_carry = [bc + m for bc, m in zip(b_col, m_old)]
        m_t = [jnp.maximum(a, b_) for a, b_ in zip(d_carry, mx_in)]
        carry_end = [bt + m for bt, m in zip(b_tot, m_old)]
        m_new = [jnp.maximum(a, b_) for a, b_ in zip(carry_end, mx_end)]
        p_in = [jnp.exp(a - b_) for a, b_ in zip(d_in, m_t)]
        w_end = [jnp.exp(bt - bc + ic - mn) for bt, bc, ic, mn in zip(b_tot, b_col, i_col, m_new)]
        st = []
        for si, (gi, hh) in enumerate(streams):
            bi, d = groups[gi]
            cols = slice(hh * HD, (hh + 1) * HD)
            k = refs[d][1][bi, rows[d], cols]
            v = refs[d][2][bi, rows[d], cols]
            st.append(dict(
                q=refs[d][0][bi, rows[d], cols], k=k,
                v_aug=jnp.concatenate([v, ones_col], axis=1),
                p_in=p_in[si], w_carry=jnp.exp(d_carry[si] - m_t[si]), floor=jnp.exp(-m_t[si]),
                kw=(k.astype(F32) * w_end[si]).astype(BF16),
                f_end=jnp.exp(carry_end[si] - m_new[si]),
                m_new=jnp.broadcast_to(m_new[si], m_rows[si].shape)))
        sc = [(_dot_nt(s["q"], s["k"]) * s["p_in"]).astype(BF16) for s in st]
        c_old = [c_ref[si] for si in range(len(streams))]
        qc = [_dot(s["q"], c.astype(BF16)) for s, c in zip(st, c_old)]
        nd = [s["w_carry"] * a + _dot(m, s["v_aug"]) for s, a, m in zip(st, qc, sc)]
        hout = [a[:, :HD] / jnp.maximum(jnp.abs(a[:, HD:HD + 1]), s["floor"]) for s, a in zip(st, nd)]
        c_new = [s["f_end"] * c + _dot_tn(s["kw"], s["v_aug"]) for s, c in zip(st, c_old)]
        for si, (s, c) in enumerate(zip(st, c_new)):
            c_ref[si] = c
            m_ref[si] = s["m_new"]
        _store_heads(refs, groups, rows, hout, ML_H)
        return carry

    lax.fori_loop(0, CPB, chunk_body, 0)


def _mlstm_scan(main, gates_c, gates_r):
    b, t, _ = main.shape
    nt = t // TM
    w = ML_H * HD
    ngl = gates_c.shape[-1]
    qf, qb = _scan_specs(nt, w, 4)
    kf, kb = _scan_specs(nt, w, 5)
    vf, vb = _scan_specs(nt, w, 6)
    gcf, gcb = _gate_specs(nt, (CH, ngl))
    grf, grb = _gate_specs(nt, (ngl, CH))
    of, ob = _scan_specs(nt, w, 0)
    return pl.pallas_call(
        _mlstm_kernel,
        out_shape=(jax.ShapeDtypeStruct((b, t, w), BF16), jax.ShapeDtypeStruct((b, t, w), BF16)),
        grid=(b // SCAN_BG, nt),
        in_specs=[qf, kf, vf, qb, kb, vb, gcf, grf, gcb, grb],
        out_specs=(of, ob),
        scratch_shapes=[pltpu.VMEM((SCAN_BG * N_DIR * ML_H, HD, 2 * HD), F32),
                        pltpu.VMEM((SCAN_BG * N_DIR * ML_H, 1, HD), F32)],
        compiler_params=_cparams(("arbitrary", "arbitrary")),
        name="mlstm_scan",
    )(main, main, main, main, main, main, gates_c, gates_r, gates_c, gates_r)


def _ret_kernel(qf, kf, vf, qb, kb, vb, intra_ref, cross_ref, tail_ref, cd_ref, of, ob, s_ref):
    i = pl.program_id(1)

    @pl.when(i == 0)
    def _():
        s_ref[...] = jnp.zeros_like(s_ref)

    refs = ((qf, kf, vf, of), (qb, kb, vb, ob))
    groups, streams = _scan_streams(RET_H)

    def chunk_body(cc, carry):
        cidx = (cc, CPB - 1 - cc)
        rows = tuple(pl.ds(pl.multiple_of(c * CH, CH), CH) for c in cidx)
        st = []
        for gi, hh in streams:
            bi, d = groups[gi]
            ti = d * RET_H + hh
            kcols = slice(hh * HD, (hh + 1) * HD)
            q = refs[d][0][bi, rows[d], kcols]
            k = refs[d][1][bi, rows[d], kcols]
            st.append(dict(
                ti=ti, q=q, k=k, v=refs[d][2][bi, rows[d], slice(hh * RET_DV, (hh + 1) * RET_DV)],
                qc=(q.astype(F32) * cross_ref[ti]).astype(BF16),
                kt=(k.astype(F32) * tail_ref[ti]).astype(BF16)))
        sc = [(_dot_nt(s["q"], s["k"]) * intra_ref[s["ti"]]).astype(BF16) for s in st]
        s_old = [s_ref[si] for si in range(len(streams))]
        o = [_dot(m, s["v"]) + _dot(s["qc"], so.astype(BF16)) for s, m, so in zip(st, sc, s_old)]
        s_new = [cd_ref[s["ti"]] * so + _dot_tn(s["kt"], s["v"]) for s, so in zip(st, s_old)]
        for si, s in enumerate(s_new):
            s_ref[si] = s
        _store_heads(refs, groups, rows, o, RET_H)
        return carry

    lax.fori_loop(0, CPB, chunk_body, 0)


def _ret_tables():
    pos = np.arange(CH, dtype=np.float64)
    intra, cross, tail, cd = [], [], [], []
    for d in range(N_DIR):
        expo = 5.0 + np.arange(RET_H, dtype=np.float64)
        if d == 1:
            expo = expo[::-1]
        lg = np.log1p(-np.exp2(-expo))
        p = pos if d == 0 else (CH - 1.0 - pos)
        diff = p[:, None] - p[None, :]
        for hh in range(RET_H):
            intra.append(np.where(diff >= 0, np.exp(np.where(diff >= 0, diff, 0.0) * lg[hh]), 0.0))
            cross.append(np.broadcast_to(np.exp((p + 1.0) * lg[hh])[:, None], (CH, HD)))
            tail.append(np.broadcast_to(np.exp((CH - 1.0 - p) * lg[hh])[:, None], (CH, HD)))
            cd.append(np.full((1, RET_DV), np.exp(CH * lg[hh])))
    f = lambda a: jnp.asarray(np.stack(a), F32)
    return f(intra), f(cross), f(tail), f(cd)


def _ret_scan(main):
    b, t, _ = main.shape
    nt = t // TM
    qw = RET_H * HD
    vw = RET_H * RET_DV
    qf, qb = _scan_specs(nt, qw, 0)
    kf, kb = _scan_specs(nt, qw, 1)
    vf, vb = _scan_specs(nt, vw, 1)
    of, ob = _scan_specs(nt, vw, 0)
    tabs = _ret_tables()
    const = lambda a: pl.BlockSpec(a.shape, lambda b_, i_: (0,) * a.ndim)
    return pl.pallas_call(
        _ret_kernel,
        out_shape=(jax.ShapeDtypeStruct((b, t, vw), BF16), jax.ShapeDtypeStruct((b, t, vw), BF16)),
        grid=(b // SCAN_BG, nt),
        in_specs=[qf, kf, vf, qb, kb, vb] + [const(a) for a in tabs],
        out_specs=(of, ob),
        scratch_shapes=[pltpu.VMEM((SCAN_BG * N_DIR * RET_H, HD, RET_DV), F32)],
        compiler_params=_cparams(("arbitrary", "arbitrary")),
        name="ret_scan",
    )(main, main, main, main, main, main, *tabs)


CODE_SHIFT = 17


def _route_and_pack(h2, rw_ref, rb_ref, h2p_ref, code_ref, cnt_ref, base_ref):
    h2b = h2.astype(BF16)
    logits = _dot(h2b, rw_ref[...]) + rb_ref[...]
    lane = lax.broadcasted_iota(jnp.int32, logits.shape, 1)
    lane_f = lane.astype(F32)
    vals, idxs = [], []
    cur = logits
    for _ in range(TOP_K):
        m = jnp.max(cur, axis=-1, keepdims=True)
        ix = jnp.min(jnp.where(cur == m, lane_f, float(logits.shape[1])), axis=-1, keepdims=True)
        vals.append(m)
        idxs.append(ix)
        cur = jnp.where(lane_f == ix, NEG, cur)
    es = [jnp.exp(v - vals[0]) for v in vals]
    tot = es[0] + es[1] + es[2] + es[3]

    @pl.when((pl.program_id(0) == 0) & (pl.program_id(1) == 0))
    def _():
        base_ref[...] = jnp.zeros_like(base_ref)

    tm = logits.shape[0]
    onehot = jnp.zeros(logits.shape, F32)
    for j in range(TOP_K):
        onehot = jnp.where(lane_f == idxs[j], 1.0, onehot)
    r_i = lax.broadcasted_iota(jnp.int32, (tm, tm), 0)
    c_i = lax.broadcasted_iota(jnp.int32, (tm, tm), 1)
    before = _dot(jnp.where(r_i > c_i, 1.0, 0.0).astype(BF16), onehot.astype(BF16))
    base = base_ref[...]
    pos = before + base
    total = base + before[tm - 1:tm, :] + onehot[tm - 1:tm, :]
    base_ref[...] = total
    cnt_ref[...] = total

    code = jnp.zeros(logits.shape, F32)
    wgt = jnp.zeros(logits.shape, F32)
    for j in range(TOP_K):
        rank = jnp.sum(jnp.where(lane_f == idxs[j], pos, 0.0), axis=-1, keepdims=True)
        code = jnp.where(lane == j, idxs[j] * float(1 << CODE_SHIFT) + rank, code)
        wgt = jnp.where(lane == TOP_K + j, es[j] / tot, wgt)
    code_ref[...] = jnp.where(lane < TOP_K, code.astype(jnp.int32), pltpu.bitcast(wgt, jnp.int32))
    half = h2.shape[1] // 2
    r = h2b.astype(F32)
    lo = lax.shift_right_logical(pltpu.bitcast(r[:, :half], U32), jnp.uint32(16))
    hi = pltpu.bitcast(r[:, half:], U32) & jnp.uint32(0xFFFF0000)
    _store_row_packed(h2p_ref, hi | lo)


def _merge_even_kernel(ogf, ogb, omf, omb, z_ref, mo_ref, gg_ref, mg_ref, wo_ref, x_ref, g1_ref,
                       n2_ref, sh_ref, sc_ref, rw_ref, rb_ref, x1_ref, h2p_ref, code_ref, cnt_ref, base_ref):
    ones_blk = jnp.ones((HD, HD), BF16)
    og = ogf[...].astype(F32) + ogb[...].astype(F32)
    ms = _group_sum(og * og, ones_blk) * (1.0 / HD)
    a = og * lax.rsqrt(ms + EPS) * gg_ref[...] * _silu(z_ref[...].astype(F32))
    om = omf[...].astype(F32) + omb[...].astype(F32)
    ms = _group_sum(om * om, ones_blk) * (1.0 / HD)
    m = om * lax.rsqrt(ms + EPS) * mg_ref[...] * _sigmoid(mo_ref[...].astype(F32))
    cat = jnp.concatenate([a, m], axis=1).astype(BF16)
    y = _dot(cat, wo_ref[...])
    x1 = x_ref[...] + g1_ref[...] * y
    x1_ref[...] = x1
    h2 = _rms_mod(x1, n2_ref[...], sh_ref[...], sc_ref[...])
    _route_and_pack(h2, rw_ref, rb_ref, h2p_ref, code_ref, cnt_ref, base_ref)


def _merge_odd_kernel(of, ob, gate_ref, ng_ref, wo_ref, x_ref, g1_ref,
                      n2_ref, sh_ref, sc_ref, rw_ref, rb_ref, x1_ref, h2p_ref, code_ref, cnt_ref, base_ref):
    ones_blk = jnp.ones((RET_DV, RET_DV), BF16)
    o = of[...].astype(F32) + ob[...].astype(F32)
    o = o - _group_sum(o, ones_blk) * (1.0 / RET_DV)
    ms = _group_sum(o * o, ones_blk) * (1.0 / RET_DV)
    y = o * lax.rsqrt(ms + EPS) * ng_ref[...] * _silu(gate_ref[...].astype(F32))
    y = _dot(y.astype(BF16), wo_ref[...])
    x1 = x_ref[...] + g1_ref[...] * y
    x1_ref[...] = x1
    h2 = _rms_mod(x1, n2_ref[...], sh_ref[...], sc_ref[...])
    _route_and_pack(h2, rw_ref, rb_ref, h2p_ref, code_ref, cnt_ref, base_ref)


def _merge_out(b, t, d):
    assert d // 2 == PACK * LANES
    shapes = (jax.ShapeDtypeStruct((b, t, d), F32), jax.ShapeDtypeStruct((b, t * PACK, LANES), U32),
              jax.ShapeDtypeStruct((b, t, LANES), jnp.int32), jax.ShapeDtypeStruct((1, LANES), F32))
    tile = lambda rows, width: pl.BlockSpec((None, rows, width), lambda b_, t_: (b_, t_, 0))
    specs = (tile(TM, d), tile(TM * PACK, LANES), tile(TM, LANES), pl.BlockSpec((1, LANES), lambda b_, t_: (0, 0)))
    return shapes, specs


def _merge_even(og_f, og_b, om_f, om_b, main, gdn_g, ml_g, w_out, x, g1, n2, sh2, sc2, rw, rb):
    b, t, d = x.shape
    w = GDN_H * HD
    tile = lambda width, col: pl.BlockSpec((None, TM, width), lambda b_, t_: (b_, t_, col))
    const = lambda a: pl.BlockSpec(a.shape, lambda b_, t_: (0,) * a.ndim)
    shapes, specs = _merge_out(b, t, d)
    return pl.pallas_call(
        _merge_even_kernel,
        out_shape=shapes,
        grid=(b, t // TM),
        in_specs=[tile(w, 0), tile(w, 0), tile(w, 0), tile(w, 0), tile(w, 3), tile(w, 7),
                  const(gdn_g), const(ml_g), const(w_out), tile(d, 0), _tile_mod_spec(d),
                  const(n2), _tile_mod_spec(d), _tile_mod_spec(d), const(rw), const(rb)],
        out_specs=specs,
        scratch_shapes=[pltpu.VMEM((1, 128), F32)],
        compiler_params=_cparams(("arbitrary", "arbitrary")),
        name="merge_even",
    )(og_f, og_b, om_f, om_b, main, main, gdn_g, ml_g, w_out, x, g1, n2, sh2, sc2, rw, rb)


def _merge_odd(o_f, o_b, main, ret_g, w_out, x, g1, n2, sh2, sc2, rw, rb):
    b, t, d = x.shape
    vw = RET_H * RET_DV
    tile = lambda width, col: pl.BlockSpec((None, TM, width), lambda b_, t_: (b_, t_, col))
    const = lambda a: pl.BlockSpec(a.shape, lambda b_, t_: (0,) * a.ndim)
    shapes, specs = _merge_out(b, t, d)
    return pl.pallas_call(
        _merge_odd_kernel,
        out_shape=shapes,
        grid=(b, t // TM),
        in_specs=[tile(vw, 0), tile(vw, 0), tile(vw, 2), const(ret_g), const(w_out), tile(d, 0),
                  _tile_mod_spec(d), const(n2), _tile_mod_spec(d), _tile_mod_spec(d), const(rw), const(rb)],
        out_specs=specs,
        scratch_shapes=[pltpu.VMEM((1, 128), F32)],
        compiler_params=_cparams(("arbitrary", "arbitrary")),
        name="merge_odd",
    )(o_f, o_b, main, ret_g, w_out, x, g1, n2, sh2, sc2, rw, rb)


def _proj_odd_kernel(x_ref, acc_ref, g2_ref, g_ref, sh_ref, sc_ref, w_ref, cos_ref, sin_ref,
                     x2_ref, main_ref):
    x2 = x_ref[...] + g2_ref[...] * _load_acc(acc_ref)
    x2_ref[...] = x2
    hb = _rms_mod(x2, g_ref[...], sh_ref[...], sc_ref[...]).astype(BF16)
    qk_w = RET_H * HD
    cos = cos_ref[...]
    sin = sin_ref[...]
    n_seg = w_ref.shape[1] // qk_w
    for seg in range(n_seg):
        sl = slice(seg * qk_w, (seg + 1) * qk_w)
        u = _dot(hb, w_ref[:, sl])
        if seg < 2:
            parts = []
            for hh in range(RET_H):
                uh = u[:, hh * HD:(hh + 1) * HD]
                parts.append(uh * cos + pltpu.roll(uh, HD // 2, 1) * sin)
            u = jnp.concatenate(parts, axis=1)
            if seg == 1:
                u = u * (HD ** -0.5)
        main_ref[:, sl] = u.astype(BF16)


def _rope_tables(t):
    half = HD // 2
    freqs = ROPE_BASE ** (-jnp.arange(half, dtype=F32) / half)
    ang = jnp.arange(t, dtype=F32)[:, None] * freqs[None, :]
    cos, sin = jnp.cos(ang), jnp.sin(ang)
    return jnp.concatenate([cos, cos], axis=1), jnp.concatenate([-sin, sin], axis=1)


def _proj_odd(x, acc, g2, g, sh, sc, w):
    b, t, d = x.shape
    n = w.shape[1]
    cos, sin = _rope_tables(t)
    const = lambda a: pl.BlockSpec(a.shape, lambda b_, t_: (0,) * a.ndim)
    tile = lambda width: pl.BlockSpec((None, TM, width), lambda b_, t_: (b_, t_, 0))
    rope = pl.BlockSpec((TM, HD), lambda b_, t_: (t_, 0))
    nt = t // TM
    acc_tile = _acc_tile_spec(acc, lambda b_, t_: b_ * nt + t_)
    return pl.pallas_call(
        _proj_odd_kernel,
        out_shape=(jax.ShapeDtypeStruct((b, t, d), F32), jax.ShapeDtypeStruct((b, t, n), BF16)),
        grid=(b, nt),
        in_specs=[tile(d), acc_tile, _tile_mod_spec(d), const(g), _tile_mod_spec(d), _tile_mod_spec(d),
                  const(w), rope, rope],
        out_specs=(tile(d), tile(n)),
        compiler_params=_cparams(("arbitrary", "arbitrary")),
        name="proj_odd",
    )(x, acc, g2, g, sh, sc, w, cos, sin)


ROWS_PER_STEP = 8


def _packed_row(r):
    return pl.ds(pl.multiple_of(r * PACK, PACK), PACK)


def _moe_gather_kernel(src_ref, h_ref, xs_ref):
    def body(r8, carry):
        base = r8 * ROWS_PER_STEP
        rows = [h_ref[_packed_row(src_ref[0, base + j]), :] for j in range(ROWS_PER_STEP)]
        for j in range(ROWS_PER_STEP):
            xs_ref[_packed_row(base + j), :] = rows[j]
        return carry

    lax.fori_loop(0, MOE_BLOCK // ROWS_PER_STEP, body, 0)


def _moe_gather(row_src, h2p, n_blocks):
    return pl.pallas_call(
        _moe_gather_kernel,
        out_shape=jax.ShapeDtypeStruct((n_blocks * MOE_BLOCK * PACK, LANES), U32),
        grid=(n_blocks,),
        in_specs=[pl.BlockSpec((None, 1, MOE_BLOCK), lambda i: (i, 0, 0), memory_space=pltpu.SMEM),
                  pl.BlockSpec(h2p.shape, lambda i: (0, 0))],
        out_specs=pl.BlockSpec((MOE_BLOCK * PACK, LANES), lambda i: (i, 0)),
        compiler_params=_cparams(("arbitrary",)),
        name="moe_gather",
    )(row_src.reshape(n_blocks, 1, MOE_BLOCK), h2p)


def _moe_mm_kernel(be_ref, nu_ref, xs_ref, wgu_ref, bgu_ref, wdn_ref, bdn_ref, ys_ref, wgu_s, wdn_s):
    i = pl.program_id(0)
    e = be_ref[i]
    half = wgu_ref.shape[0] // 2
    hl = LANES // 2

    @pl.when(i >= nu_ref[0])
    def _():
        ys_ref[...] = jnp.zeros_like(ys_ref)

    @pl.when(i < nu_ref[0])
    def _():
        @pl.when((i == 0) | (e != be_ref[jnp.maximum(i - 1, 0)]))
        def _():
            wgu_s[...] = wgu_ref[...].astype(BF16)
            for p in range(wdn_ref.shape[0] // LANES):
                first = wdn_ref[p * LANES:p * LANES + hl, :].astype(BF16).astype(F32)
                second = wdn_ref[p * LANES + hl:(p + 1) * LANES, :].astype(BF16).astype(F32)
                word = (pltpu.bitcast(second, U32) & jnp.uint32(0xFFFF0000)) | \
                    lax.shift_right_logical(pltpu.bitcast(first, U32), jnp.uint32(16))
                wdn_s[p * LANES:(p + 1) * LANES, :] = pltpu.bitcast(word, BF16)

        xu = _load_row_packed(xs_ref, MOE_BLOCK)
        lo = pltpu.bitcast(lax.shift_left(xu, jnp.uint32(16)), F32).astype(BF16)
        hi = pltpu.bitcast(xu & jnp.uint32(0xFFFF0000), F32).astype(BF16)
        gu = _dot(lo, wgu_s[:half, :]) + _dot(hi, wgu_s[half:, :]) + bgu_ref[...]
        even = (lax.broadcasted_iota(jnp.int32, (gu.shape[0], LANES), 1) & 1) == 0
        acts = []
        for p in range(gu.shape[1] // (2 * LANES)):
            a = gu[:, 2 * p * LANES:(2 * p + 1) * LANES]
            b = gu[:, (2 * p + 1) * LANES:(2 * p + 2) * LANES]
            gate = jnp.minimum(jnp.where(even, a, pltpu.roll(b, 1, 1)), SWIGLU_LIMIT)
            up = jnp.clip(jnp.where(even, pltpu.roll(a, LANES - 1, 1), b), -SWIGLU_LIMIT, SWIGLU_LIMIT)
            acts.append(((up + 1.0) * gate * _sigmoid(SWIGLU_ALPHA * gate)).astype(BF16))
        y = _dot(jnp.concatenate(acts, axis=1), wdn_s[...]) + bdn_ref[...]
        cw = PACK * LANES
        for h in range(y.shape[1] // cw):
            _store_row_packed(ys_ref, y[:, h * cw:(h + 1) * cw], lead=(h,))


def _moe_mm(block_e, n_used, xs, layer, w_gu, b_gu, w_dn, b_dn):
    n_blocks = xs.shape[0] // (MOE_BLOCK * PACK)
    depth, n_e, d, f2 = w_gu.shape
    n_half = d // (PACK * LANES)
    ew = lambda shape: pl.BlockSpec((None, None) + shape, lambda i, be, nu: (layer, be[i], 0, 0))
    grid_spec = pltpu.PrefetchScalarGridSpec(
        num_scalar_prefetch=2,
        grid=(n_blocks,),
        in_specs=[pl.BlockSpec((MOE_BLOCK * PACK, LANES), lambda i, be, nu: (i, 0)),
                  ew((d, f2)), ew((1, f2)), ew((f2 // 2, d)), ew((1, d))],
        out_specs=pl.BlockSpec((n_half, MOE_BLOCK * PACK, LANES), lambda i, be, nu: (0, i, 0)),
        scratch_shapes=[pltpu.VMEM((d, f2), BF16), pltpu.VMEM((f2 // 2, d), BF16)],
    )
    return pl.pallas_call(
        _moe_mm_kernel,
        out_shape=jax.ShapeDtypeStruct((n_half, xs.shape[0], LANES), F32),
        grid_spec=grid_spec,
        compiler_params=_cparams(("arbitrary",)),
        name="moe_mm",
    )(block_e, n_used, xs, w_gu, b_gu.reshape(depth, n_e, 1, f2), w_dn, b_dn.reshape(depth, n_e, 1, d))


def _moe_combine_kernel(nu_ref, dst_ref, w_ref, ys_ref, acc_ref):
    i = pl.program_id(1)

    @pl.when(i == 0)
    def _():
        acc_ref[...] = jnp.zeros_like(acc_ref)

    def body(r8, carry):
        base = r8 * ROWS_PER_STEP
        dst = [_packed_row(dst_ref[0, base + j]) for j in range(ROWS_PER_STEP)]
        group = ROWS_PER_STEP * PACK
        ys = ys_ref[pl.ds(pl.multiple_of(r8 * group, group), group), :]
        new = [acc_ref[dst[j], :] + w_ref[0, base + j] * ys[j * PACK:(j + 1) * PACK, :]
               for j in range(ROWS_PER_STEP)]
        for j in range(ROWS_PER_STEP):
            acc_ref[dst[j], :] = new[j]
        return carry

    @pl.when(i < nu_ref[0])
    def _():
        lax.fori_loop(0, MOE_BLOCK // ROWS_PER_STEP, body, 0)


def _moe_combine(n_used, row_dst, row_w, ys, n_rows_out):
    n_half, packed_rows, _ = ys.shape
    n_blocks = packed_rows // (MOE_BLOCK * PACK)
    idx_spec = pl.BlockSpec((None, 1, MOE_BLOCK), lambda j, i, nu: (i, 0, 0), memory_space=pltpu.SMEM)
    grid_spec = pltpu.PrefetchScalarGridSpec(
        num_scalar_prefetch=1,
        grid=(n_half, n_blocks),
        in_specs=[idx_spec, idx_spec,
                  pl.BlockSpec((None, MOE_BLOCK * PACK, LANES), lambda j, i, nu: (j, i, 0))],
        out_specs=pl.BlockSpec((None, n_rows_out * PACK, LANES), lambda j, i, nu: (j, 0, 0),
                               pipeline_mode=pl.Buffered(1)),
    )
    return pl.pallas_call(
        _moe_combine_kernel,
        out_shape=jax.ShapeDtypeStruct((n_half, n_rows_out * PACK, LANES), F32),
        grid_spec=grid_spec,
        compiler_params=_cparams(("arbitrary", "arbitrary")),
        name="moe_combine",
    )(n_used, row_dst.reshape(n_blocks, 1, MOE_BLOCK), row_w.reshape(n_blocks, 1, MOE_BLOCK), ys)


def _moe_inverse_kernel(dest_ref, inv_ref):
    def init(r, carry):
        inv_ref[r] = -1
        return carry

    lax.fori_loop(0, inv_ref.shape[0], init, 0, unroll=8)

    def body(a, carry):
        inv_ref[dest_ref[a]] = a
        return carry

    lax.fori_loop(0, dest_ref.shape[0], body, 0, unroll=8)


def _moe_inverse(dest, rows):
    smem = pl.BlockSpec(memory_space=pltpu.SMEM)
    return pl.pallas_call(
        _moe_inverse_kernel,
        out_shape=jax.ShapeDtypeStruct((rows,), jnp.int32),
        in_specs=[smem],
        out_specs=smem,
        name="moe_inverse",
    )(dest)


def _moe(h2p, codes, weights, counts, layer, w_gu, b_gu, w_dn, b_dn):
    n = h2p.shape[0] // PACK
    n_assign = n * TOP_K
    n_blocks = -(-n_assign // MOE_BLOCK) + N_EXPERTS
    rows = n_blocks * MOE_BLOCK
    padded = -(-counts // MOE_BLOCK) * MOE_BLOCK
    ends = jnp.cumsum(padded)
    offsets = ends - padded
    starts = jnp.arange(n_blocks, dtype=jnp.int32) * MOE_BLOCK
    block_e = jnp.minimum(jnp.sum(ends[None, :] <= starts[:, None], axis=1), N_EXPERTS - 1).astype(jnp.int32)
    n_used = (ends[-1:] // MOE_BLOCK).astype(jnp.int32)
    expert = lax.shift_right_logical(codes, CODE_SHIFT)
    which = expert[:, None] == jnp.arange(N_EXPERTS, dtype=jnp.int32)[None, :]
    dest = (codes & ((1 << CODE_SHIFT) - 1)) + jnp.sum(jnp.where(which, offsets[None, :], 0), axis=1)
    inv = _moe_inverse(dest.astype(jnp.int32), rows)
    tok = lax.shift_right_logical(inv, TOP_K.bit_length() - 1)
    row_src = jnp.where(inv >= 0, tok, 0)
    row_dst = jnp.where(inv >= 0, tok, n)
    row_w = jnp.where(inv >= 0, jnp.take(weights, jnp.maximum(inv, 0)), 0.0)
    xs = _moe_gather(row_src, h2p, n_blocks)
    ys = _moe_mm(block_e, n_used, xs, layer, w_gu, b_gu, w_dn, b_dn)
    return _moe_combine(n_used, row_dst, row_w, ys, n + 8)


def _final_kernel(x_ref, acc_ref, g2_ref, g_ref, o_ref):
    x = x_ref[...] + g2_ref[...] * _load_acc(acc_ref)
    ms = jnp.mean(x * x, axis=-1, keepdims=True)
    o_ref[...] = x * lax.rsqrt(ms + EPS) * g_ref[...]


def _final(x, acc, g2, g, n_ctx_tiles):
    b, t, d = x.shape
    nt = t // TM - n_ctx_tiles
    tile_in = pl.BlockSpec((None, TM, d), lambda b_, t_: (b_, t_ + n_ctx_tiles, 0))
    acc_tile = _acc_tile_spec(acc, lambda b_, t_: b_ * (t // TM) + t_ + n_ctx_tiles)
    return pl.pallas_call(
        _final_kernel,
        out_shape=jax.ShapeDtypeStruct((b, nt * TM, d), F32),
        grid=(b, nt),
        in_specs=[tile_in, acc_tile,
                  pl.BlockSpec((None, None, 1, d), lambda b_, t_: (b_, 1, 0, 0)),
                  pl.BlockSpec((1, d), lambda b_, t_: (0, 0))],
        out_specs=pl.BlockSpec((None, TM, d), lambda b_, t_: (b_, t_, 0)),
        compiler_params=_cparams(("arbitrary", "arbitrary")),
        name="final_norm",
    )(x, acc, g2, g)


def _mod_tables(mod, b, d):
    outs = []
    for j in range(6):
        m = mod[:, j * d:(j + 1) * d]
        lat = m[:b]
        ctx = jnp.broadcast_to(m[b:b + 1], (b, d))
        outs.append(jnp.stack([ctx, lat], axis=1)[:, :, None, :])
    return outs


def _routing(code):
    codes = code[:, :, :TOP_K].reshape(-1)
    weights = lax.bitcast_convert_type(code[:, :, TOP_K:2 * TOP_K], F32).reshape(-1)
    return codes, weights


def _router_params(router_w, router_b):
    d, e = router_w.shape
    rw = jnp.zeros((d, 128), F32).at[:, :e].set(router_w).astype(BF16)
    rb = jnp.full((1, 128), NEG, F32).at[0, :e].set(router_b)
    return rw, rb


def kernel(x, c, ctx, c_ctx, mod_w, mod_b, norm1_g, norm2_g, ev_w_in, ev_conv_w, gdn_a_log, gdn_dt_bias,
           gdn_norm_g, ml_i_bias, ml_f_bias, ml_norm_g, ev_w_out, od_w_in, ret_norm_g, od_w_out,
           router_w, router_b, moe_w_gu, moe_b_gu, moe_w_dn, moe_b_dn, final_g):
    b, s, d = x.shape
    n_ctx = ctx.shape[1]
    depth = mod_w.shape[0]
    assert n_ctx == TM and s % TM == 0 and depth == 2 and b % SCAN_BG == 0
    t = n_ctx + s
    n_tok = b * t

    cond = jnp.concatenate([c, c_ctx[None, :], jnp.zeros((8 - b - 1, d), F32)], axis=0)
    mod = _adaln(cond, mod_w, mod_b)
    xa = jnp.concatenate([ctx, x], axis=1)

    sh1, sc1, g1, sh2, sc2, g2 = _mod_tables(mod[0], b, d)
    qk_w = GDN_H * HD
    conv_ch = 3 * qk_w
    ng = N_DIR * GDN_H
    w_in = ev_w_in[0]
    o_z = conv_ch
    o_a = o_z + qk_w
    o_mq = o_a + 2 * ng
    o_i = o_mq + 4 * qk_w
    w_main = jnp.concatenate([w_in[:, :o_a], w_in[:, o_mq:o_i]], axis=1).astype(BF16)
    w_gate = jnp.concatenate([w_in[:, o_a:o_mq], w_in[:, o_i:o_i + 2 * ng],
                              jnp.zeros((d, 128 - 4 * ng), F32)], axis=1).astype(BF16)
    zeros_g = jnp.zeros((ng,), F32)
    rate = jnp.concatenate([jnp.exp(gdn_a_log[0].astype(F32)).reshape(-1), jnp.zeros((128 - ng,), F32)])[None, :]
    gbias = jnp.concatenate([gdn_dt_bias[0].reshape(-1), zeros_g, ml_i_bias[0].reshape(-1),
                             ml_f_bias[0].reshape(-1), jnp.zeros((128 - 4 * ng,), F32)])[None, :].astype(F32)
    main, gates = _proj_even(xa, norm1_g[0][None, :], sh1, sc1, w_main, w_gate, ev_conv_w[0], rate, gbias)
    gates_c = gates.reshape(b, t // CH, CH, 4 * ng)
    gates_r = jnp.swapaxes(gates_c, 2, 3)
    og_f, og_b = _gdn_scan(main, gates_c, gates_r)
    om_f, om_b = _mlstm_scan(main, gates_c, gates_r)
    rw, rb = _router_params(router_w[0], router_b[0])
    gdn_g = jnp.tile(gdn_norm_g[0], GDN_H)[None, :]
    x1, h2p, code, cnt = _merge_even(og_f, og_b, om_f, om_b, main, gdn_g, ml_norm_g[0][None, :],
                                     ev_w_out[0].astype(BF16), xa, g1, norm2_g[0][None, :], sh2, sc2, rw, rb)
    acc = _moe(h2p.reshape(n_tok * PACK, LANES), *_routing(code), cnt[0, :N_EXPERTS].astype(jnp.int32),
               0, moe_w_gu, moe_b_gu, moe_w_dn, moe_b_dn)
    g2_prev = g2

    sh1, sc1, g1, sh2, sc2, g2 = _mod_tables(mod[1], b, d)
    x2, main_o = _proj_odd(x1, acc, g2_prev, norm1_g[1][None, :], sh1, sc1, od_w_in[0].astype(BF16))
    o_f, o_b = _ret_scan(main_o)
    rw, rb = _router_params(router_w[1], router_b[1])
    x3, h2p, code, cnt = _merge_odd(o_f, o_b, main_o, ret_norm_g[0][None, :], od_w_out[0].astype(BF16),
                                    x2, g1, norm2_g[1][None, :], sh2, sc2, rw, rb)
    acc = _moe(h2p.reshape(n_tok * PACK, LANES), *_routing(code), cnt[0, :N_EXPERTS].astype(jnp.int32),
               1, moe_w_gu, moe_b_gu, moe_w_dn, moe_b_dn)
    return _final(x3, acc, g2, final_g[None, :], n_ctx // TM)
```

```python
import functools
import math

import jax
import jax.numpy as jnp
import numpy as np
from jax import lax
from jax.experimental import pallas as pl
from jax.experimental.pallas import tpu as pltpu

F32 = jnp.float32
BF16 = jnp.bfloat16
U32 = jnp.uint32
HIGHEST = lax.Precision.HIGHEST

EPS = 1e-6
CH = 64
TM = 256
CPB = TM // CH
HD = 128
N_DIR = 2
GDN_H = 4
ML_H = 4
RET_H = 8
RET_DV = 256
CONV_W = 3
N_EXPERTS = 32
TOP_K = 4
SWIGLU_ALPHA = 1.702
SWIGLU_LIMIT = 7.0
MOE_BLOCK = 256
ROPE_BASE = 10000.0
NEG = -1e30
VMEM_LIMIT = 56 * 1024 * 1024


def _cparams(sem):
    return pltpu.CompilerParams(dimension_semantics=sem, vmem_limit_bytes=VMEM_LIMIT)


def _dot(a, b, precision=None):
    return jnp.dot(a, b, preferred_element_type=F32, precision=precision)


def _dot_nt(a, b):
    return lax.dot_general(a, b, (((1,), (1,)), ((), ())), preferred_element_type=F32)


def _dot_tn(a, b):
    return lax.dot_general(a, b, (((0,), (0,)), ((), ())), preferred_element_type=F32)


def _sigmoid(x):
    return 1.0 / (1.0 + jnp.exp(-x))


def _silu(x):
    return x * _sigmoid(x)


def _group_sum(x, w):
    outs = []
    for j in range(x.shape[1] // w):
        s = jnp.sum(x[:, j * w:(j + 1) * w], axis=-1, keepdims=True)
        outs.append(jnp.broadcast_to(s, (x.shape[0], w)))
    return outs[0] if len(outs) == 1 else jnp.concatenate(outs, axis=1)


def _rms_mod(x, g, sh, sc):
    ms = jnp.mean(x * x, axis=-1, keepdims=True)
    return (x * lax.rsqrt(ms + EPS) * g) * (1.0 + sc) + sh


LANES = 128
PACK = 4


def _store_row_packed(ref, x, lead=()):
    rows = x.shape[0]
    for g in range(x.shape[1] // LANES):
        ref[lead + (pl.ds(g, rows, stride=PACK), slice(None))] = x[:, g * LANES:(g + 1) * LANES]


def _load_row_packed(ref, rows, lead=()):
    return jnp.concatenate([ref[lead + (pl.ds(g, rows, stride=PACK), slice(None))] for g in range(PACK)], axis=1)


def _load_acc(acc_ref):
    rows = acc_ref.shape[1] // PACK
    return jnp.concatenate([_load_row_packed(acc_ref, rows, lead=(h,)) for h in range(acc_ref.shape[0])], axis=1)


def _acc_tile_spec(acc, tile_of):
    return pl.BlockSpec((acc.shape[0], TM * PACK, LANES), lambda b_, t_: (0, tile_of(b_, t_), 0))


def _adaln_kernel(c_ref, w_ref, b_ref, o_ref):
    c = c_ref[...]
    o_ref[...] = _dot(_silu(c), w_ref[...], precision=HIGHEST) + b_ref[...]


def _adaln(cond, mod_w, mod_b):
    depth, d, d6 = mod_w.shape
    n = d6 // d
    return pl.pallas_call(
        _adaln_kernel,
        out_shape=jax.ShapeDtypeStruct((depth, cond.shape[0], d6), F32),
        grid=(depth, n),
        in_specs=[pl.BlockSpec(cond.shape, lambda l, j: (0, 0)),
                  pl.BlockSpec((None, d, d), lambda l, j: (l, 0, j)),
                  pl.BlockSpec((None, 1, d), lambda l, j: (l, 0, j))],
        out_specs=pl.BlockSpec((None, cond.shape[0], d), lambda l, j: (l, 0, j)),
        compiler_params=_cparams(("arbitrary", "arbitrary")),
        name="adaln",
    )(cond, mod_w, mod_b.reshape(depth, 1, d6))


def _proj_even_kernel(x_ref, g_ref, sh_ref, sc_ref, w_ref, wg_ref, cw_ref, rate_ref, gb_ref,
                      main_ref, gates_ref):
    t = pl.program_id(1)
    h = _rms_mod(x_ref[...], g_ref[...], sh_ref[...], sc_ref[...])
    hb = h.astype(BF16)
    qk_w = GDN_H * HD

    row = lax.broadcasted_iota(jnp.int32, (TM, 1), 0)
    pos = jnp.where(t > 0, row & (CH - 1), row)
    last = jnp.where(t > 0, CH - 1, TM - 1)
    left_ok = pos != 0
    right_ok = pos != last
    for seg in range(3):
        sl = slice(seg * qk_w, (seg + 1) * qk_w)
        u = _dot(hb, w_ref[:, sl])
        um = jnp.where(left_ok, pltpu.roll(u, 1, 0), 0.0)
        up = jnp.where(right_ok, pltpu.roll(u, TM - 1, 0), 0.0)
        cv = _silu(um * cw_ref[0:1, sl] + u * cw_ref[1:2, sl] + up * cw_ref[2:3, sl])
        if seg < 2:
            ss = _group_sum(cv * cv, HD)
            cv = cv * lax.rsqrt(ss + EPS)
            if seg == 0:
                cv = cv * (HD ** -0.5)
        main_ref[:, sl] = cv.astype(BF16)
    for seg in range(3, 8):
        sl = slice(seg * qk_w, (seg + 1) * qk_w)
        u = _dot(hb, w_ref[:, sl])
        if seg == 5:
            u = u * (HD ** -0.5)
        main_ref[:, sl] = u.astype(BF16)

    z = _dot(hb, wg_ref[...]) + gb_ref[...]
    tl = jnp.log(1.0 + jnp.exp(-jnp.abs(z)))
    sp_pos = jnp.maximum(z, 0.0) + tl
    sp_neg = jnp.maximum(-z, 0.0) + tl
    lane = lax.broadcasted_iota(jnp.int32, z.shape, 1)
    ng = N_DIR * GDN_H
    res = jnp.where(lane < ng, -rate_ref[...] * sp_pos,
                    jnp.where(lane < 2 * ng, _sigmoid(z),
                              jnp.where(lane < 3 * ng, z, -sp_neg)))
    gates_ref[...] = res[:, :gates_ref.shape[-1]]


def _tile_mod_spec(d):
    return pl.BlockSpec((None, None, 1, d), lambda b, t: (b, jnp.minimum(t, 1), 0, 0))


def _proj_even(x, g, sh, sc, w_main, w_gate, conv_w, rate, gbias):
    b, t, d = x.shape
    n = w_main.shape[1]
    ngl = 4 * N_DIR * GDN_H
    const = lambda shape: pl.BlockSpec(shape, lambda b_, t_: (0,) * len(shape))
    return pl.pallas_call(
        _proj_even_kernel,
        out_shape=(jax.ShapeDtypeStruct((b, t, n), BF16), jax.ShapeDtypeStruct((b, t, ngl), F32)),
        grid=(b, t // TM),
        in_specs=[pl.BlockSpec((None, TM, d), lambda b_, t_: (b_, t_, 0)),
                  const((1, d)), _tile_mod_spec(d), _tile_mod_spec(d),
                  const(w_main.shape), const(w_gate.shape), const(conv_w.shape),
                  const(rate.shape), const(gbias.shape)],
        out_specs=(pl.BlockSpec((None, TM, n), lambda b_, t_: (b_, t_, 0)),
                   pl.BlockSpec((None, TM, ngl), lambda b_, t_: (b_, t_, 0))),
        compiler_params=_cparams(("arbitrary", "arbitrary")),
        name="proj_even",
    )(x, g, sh, sc, w_main, w_gate, conv_w, rate, gbias)


def _rev_tile(i, nt):
    return jnp.where(i == 0, 0, nt - i)


def _tri_masks():
    r = lax.broadcasted_iota(jnp.int32, (CH, CH), 0)
    c = lax.broadcasted_iota(jnp.int32, (CH, CH), 1)
    return r >= c, r > c, r <= c, r < c


SCAN_BG = 2


def _scan_streams(heads):
    groups = [(bi, d) for bi in range(SCAN_BG) for d in range(N_DIR)]
    streams = [(gi, hh) for gi in range(len(groups)) for hh in range(heads)]
    return groups, streams


def _cumsum_both(groups, gcs, grs, lower, upper):
    tri = (lower.astype(F32), upper.astype(F32))
    cs_c = [_dot(tri[d], gc, precision=HIGHEST) for (_, d), gc in zip(groups, gcs)]
    cs_r = [_dot(gr, tri[1 - d], precision=HIGHEST) for (_, d), gr in zip(groups, grs)]
    return cs_c, cs_r


def _store_heads(refs, groups, rows, outs, heads):
    for gi, (bi, d) in enumerate(groups):
        o_ref = refs[d][-1]
        tile = jnp.concatenate(outs[gi * heads:(gi + 1) * heads], axis=1)
        o_ref[bi, rows[d], :] = tile.astype(o_ref.dtype)


def _gdn_kernel(qf, kf, vf, qb, kb, vb, gcf, grf, gcb, grb, of, ob, s_ref):
    i = pl.program_id(1)

    @pl.when(i == 0)
    def _():
        s_ref[...] = jnp.zeros_like(s_ref)

    lower, lstrict, upper, ustrict = _tri_masks()
    refs = ((qf, kf, vf, gcf, grf, of), (qb, kb, vb, gcb, grb, ob))
    ng = N_DIR * GDN_H
    groups, streams = _scan_streams(GDN_H)

    def chunk_body(cc, carry):
        cidx = (cc, CPB - 1 - cc)
        rows = tuple(pl.ds(pl.multiple_of(c * CH, CH), CH) for c in cidx)
        gcs = [refs[d][3][bi, cidx[d]] for bi, d in groups]
        grs = [refs[d][4][bi, cidx[d]] for bi, d in groups]
        cs_c, cs_r = _cumsum_both(groups, gcs, grs, lower, upper)
        st = []
        for gi, hh in streams:
            bi, d = groups[gi]
            ci = d * GDN_H + hh
            cols = slice(hh * HD, (hh + 1) * HD)
            incl, strict = (lower, lstrict) if d == 0 else (upper, ustrict)
            g_col = cs_c[gi][:, ci:ci + 1]
            g_row = cs_r[gi][ci:ci + 1, :]
            beta = gcs[gi][:, ng + ci:ng + ci + 1]
            tot = g_col[CH - 1:CH, :] if d == 0 else g_col[0:1, :]
            decay = jnp.where(incl, jnp.exp(jnp.where(incl, g_col - g_row, 0.0)), 0.0)
            q = refs[d][0][bi, rows[d], cols].astype(F32)
            k = refs[d][1][bi, rows[d], cols].astype(F32)
            v = refs[d][2][bi, rows[d], cols].astype(F32)
            kbeta = k * beta
            eg = jnp.exp(g_col)
            st.append(dict(
                strict=strict, decay=decay, kb=k.astype(BF16), kbetab=kbeta.astype(BF16), qb=q.astype(BF16),
                x=jnp.concatenate([v * beta, kbeta * eg], axis=1),
                qe=(q * eg).astype(BF16), kdec=(k * jnp.exp(tot - g_col)).astype(BF16), cd=jnp.exp(tot)))
        kk = [_dot_nt(s["kbetab"], s["kb"]) for s in st]
        qk = [(_dot_nt(s["qb"], s["kb"]) * s["decay"]).astype(BF16) for s in st]
        p = [-jnp.where(s["strict"], m * s["decay"], 0.0) for s, m in zip(st, kk)]
        x = [s["x"] for s in st]
        for j in range(6):
            pb = [m.astype(BF16) for m in p]
            x = [xx + _dot(m, xx.astype(BF16)) for m, xx in zip(pb, x)]
            if j < 5:
                p = [_dot(m, m) for m in pb]
        s_old = [s_ref[si] for si in range(len(streams))]
        sb = [s.astype(BF16) for s in s_old]
        vnb = [(xx[:, :HD] - _dot(xx[:, HD:].astype(BF16), s)).astype(BF16) for xx, s in zip(x, sb)]
        o = [_dot(s["qe"], sbi) + _dot(m, vn) for s, sbi, m, vn in zip(st, sb, qk, vnb)]
        s_new = [so * s["cd"] + _dot_tn(s["kdec"], vn) for s, so, vn in zip(st, s_old, vnb)]
        for si, s in enumerate(s_new):
            s_ref[si] = s
        _store_heads(refs, groups, rows, o, GDN_H)
        return carry

    lax.fori_loop(0, CPB, chunk_body, 0)


def _scan_specs(nt, width, col):
    fwd = pl.BlockSpec((SCAN_BG, TM, width), lambda b, i: (b, i, col))
    bwd = pl.BlockSpec((SCAN_BG, TM, width), lambda b, i: (b, _rev_tile(i, nt), col))
    return fwd, bwd


def _gate_specs(nt, shape):
    fwd = pl.BlockSpec((SCAN_BG, CPB) + shape, lambda b, i: (b, i, 0, 0))
    bwd = pl.BlockSpec((SCAN_BG, CPB) + shape, lambda b, i: (b, _rev_tile(i, nt), 0, 0))
    return fwd, bwd


def _gdn_scan(main, gates_c, gates_r):
    b, t, _ = main.shape
    nt = t // TM
    w = GDN_H * HD
    ngl = gates_c.shape[-1]
    qf, qb = _scan_specs(nt, w, 0)
    kf, kb = _scan_specs(nt, w, 1)
    vf, vb = _scan_specs(nt, w, 2)
    gcf, gcb = _gate_specs(nt, (CH, ngl))
    grf, grb = _gate_specs(nt, (ngl, CH))
    of, ob = _scan_specs(nt, w, 0)
    return pl.pallas_call(
        _gdn_kernel,
        out_shape=(jax.ShapeDtypeStruct((b, t, w), BF16), jax.ShapeDtypeStruct((b, t, w), BF16)),
        grid=(b // SCAN_BG, nt),
        in_specs=[qf, kf, vf, qb, kb, vb, gcf, grf, gcb, grb],
        out_specs=(of, ob),
        scratch_shapes=[pltpu.VMEM((SCAN_BG * N_DIR * GDN_H, HD, HD), F32)],
        compiler_params=_cparams(("arbitrary", "arbitrary")),
        name="gdn_scan",
    )(main, main, main, main, main, main, gates_c, gates_r, gates_c, gates_r)


def _mlstm_kernel(qf, kf, vf, qb, kb, vb, gcf, grf, gcb, grb, of, ob, c_ref, m_ref):
    i = pl.program_id(1)

    @pl.when(i == 0)
    def _():
        c_ref[...] = jnp.zeros_like(c_ref)
        m_ref[...] = jnp.zeros_like(m_ref)

    lower, _, upper, _ = _tri_masks()
    refs = ((qf, kf, vf, gcf, grf, of), (qb, kb, vb, gcb, grb, ob))
    ng = N_DIR * GDN_H
    i_off = 2 * ng
    f_off = 2 * ng + N_DIR * ML_H
    lane = lax.broadcasted_iota(jnp.int32, (CH, HD), 1)
    ones_col = jnp.where(lane == 0, 1.0, 0.0).astype(BF16)

    groups, streams = _scan_streams(ML_H)

    def chunk_body(cc, carry):
        cidx = (cc, CPB - 1 - cc)
        rows = tuple(pl.ds(pl.multiple_of(c * CH, CH), CH) for c in cidx)
        gcs = [refs[d][3][bi, cidx[d]] for bi, d in groups]
        grs = [refs[d][4][bi, cidx[d]] for bi, d in groups]
        cs_c, cs_r = _cumsum_both(groups, gcs, grs, lower, upper)
        ns = len(streams)
        dirs = [groups[gi][1] for gi, _ in streams]
        chan = [groups[gi][1] * ML_H + hh for gi, hh in streams]
        b_col = [cs_c[gi][:, f_off + c:f_off + c + 1] for (gi, _), c in zip(streams, chan)]
        b_row = [cs_r[gi][f_off + c:f_off + c + 1, :] for (gi, _), c in zip(streams, chan)]
        i_col = [gcs[gi][:, i_off + c:i_off + c + 1] for (gi, _), c in zip(streams, chan)]
        i_row = [grs[gi][i_off + c:i_off + c + 1, :] for (gi, _), c in zip(streams, chan)]
        b_tot = [bc[CH - 1:CH, :] if d == 0 else bc[0:1, :] for bc, d in zip(b_col, dirs)]
        m_rows = [m_ref[si] for si in range(ns)]
        m_old = [mr[:, 0:1] for mr in m_rows]
        d_in = [jnp.where(lower if d == 0 else upper, bc - br + ir, NEG)
                for d, bc, br, ir in zip(dirs, b_col, b_row, i_row)]
        d_end = [bt - br + ir for bt, br, ir in zip(b_tot, b_row, i_row)]
        mx_in = [jnp.max(a, axis=-1, keepdims=True) for a in d_in]
        mx_end = [jnp.max(a, axis=-1, keepdims=True) for a in d_end]
        d_carry = [bc + m for bc, m in zip(b_col, m_old)]
        m_t = [jnp.maximum(a, b_) for a, b_ in zip(d_carry, mx_in)]
        carry_end = [bt + m for bt, m in zip(b_tot, m_old)]
        m_new = [jnp.maximum(a, b_) for a, b_ in zip(carry_end, mx_end)]
        p_in = [jnp.exp(a - b_) for a, b_ in zip(d_in, m_t)]
        w_end = [jnp.exp(bt - bc + ic - mn) for bt, bc, ic, mn in zip(b_tot, b_col, i_col, m_new)]
        st = []
        for si, (gi, hh) in enumerate(streams):
            bi, d = groups[gi]
            cols = slice(hh * HD, (hh + 1) * HD)
            k = refs[d][1][bi, rows[d], cols]
            v = refs[d][2][bi, rows[d], cols]
            st.append(dict(
                q=refs[d][0][bi, rows[d], cols], k=k,
                v_aug=jnp.concatenate([v, ones_col], axis=1),
                p_in=p_in[si], w_carry=jnp.exp(d_carry[si] - m_t[si]), floor=jnp.exp(-m_t[si]),
                kw=(k.astype(F32) * w_end[si]).astype(BF16),
                f_end=jnp.exp(carry_end[si] - m_new[si]),
                m_new=jnp.broadcast_to(m_new[si], m_rows[si].shape)))
        sc = [(_dot_nt(s["q"], s["k"]) * s["p_in"]).astype(BF16) for s in st]
        c_old = [c_ref[si] for si in range(len(streams))]
        qc = [_dot(s["q"], c.astype(BF16)) for s, c in zip(st, c_old)]
        nd = [s["w_carry"] * a + _dot(m, s["v_aug"]) for s, a, m in zip(st, qc, sc)]
        hout = [a[:, :HD] / jnp.maximum(jnp.abs(a[:, HD:HD + 1]), s["floor"]) for s, a in zip(st, nd)]
        c_new = [s["f_end"] * c + _dot_tn(s["kw"], s["v_aug"]) for s, c in zip(st, c_old)]
        for si, (s, c) in enumerate(zip(st, c_new)):
            c_ref[si] = c
            m_ref[si] = s["m_new"]
        _store_heads(refs, groups, rows, hout, ML_H)
        return carry

    lax.fori_loop(0, CPB, chunk_body, 0)


def _mlstm_scan(main, gates_c, gates_r):
    b, t, _ = main.shape
    nt = t // TM
    w = ML_H * HD
    ngl = gates_c.shape[-1]
    qf, qb = _scan_specs(nt, w, 4)
    kf, kb = _scan_specs(nt, w, 5)
    vf, vb = _scan_specs(nt, w, 6)
    gcf, gcb = _gate_specs(nt, (CH, ngl))
    grf, grb = _gate_specs(nt, (ngl, CH))
    of, ob = _scan_specs(nt, w, 0)
    return pl.pallas_call(
        _mlstm_kernel,
        out_shape=(jax.ShapeDtypeStruct((b, t, w), BF16), jax.ShapeDtypeStruct((b, t, w), BF16)),
        grid=(b // SCAN_BG, nt),
        in_specs=[qf, kf, vf, qb, kb, vb, gcf, grf, gcb, grb],
        out_specs=(of, ob),
        scratch_shapes=[pltpu.VMEM((SCAN_BG * N_DIR * ML_H, HD, 2 * HD), F32),
                        pltpu.VMEM((SCAN_BG * N_DIR * ML_H, 1, HD), F32)],
        compiler_params=_cparams(("arbitrary", "arbitrary")),
        name="mlstm_scan",
    )(main, main, main, main, main, main, gates_c, gates_r, gates_c, gates_r)


def _ret_kernel(qf, kf, vf, qb, kb, vb, intra_ref, cross_ref, tail_ref, cd_ref, of, ob, s_ref):
    i = pl.program_id(1)

    @pl.when(i == 0)
    def _():
        s_ref[...] = jnp.zeros_like(s_ref)

    refs = ((qf, kf, vf, of), (qb, kb, vb, ob))
    groups, streams = _scan_streams(RET_H)

    def chunk_body(cc, carry):
        cidx = (cc, CPB - 1 - cc)
        rows = tuple(pl.ds(pl.multiple_of(c * CH, CH), CH) for c in cidx)
        st = []
        for gi, hh in streams:
            bi, d = groups[gi]
            ti = d * RET_H + hh
            kcols = slice(hh * HD, (hh + 1) * HD)
            q = refs[d][0][bi, rows[d], kcols]
            k = refs[d][1][bi, rows[d], kcols]
            st.append(dict(
                ti=ti, q=q, k=k, v=refs[d][2][bi, rows[d], slice(hh * RET_DV, (hh + 1) * RET_DV)],
                qc=(q.astype(F32) * cross_ref[ti]).astype(BF16),
                kt=(k.astype(F32) * tail_ref[ti]).astype(BF16)))
        sc = [(_dot_nt(s["q"], s["k"]) * intra_ref[s["ti"]]).astype(BF16) for s in st]
        s_old = [s_ref[si] for si in range(len(streams))]
        o = [_dot(m, s["v"]) + _dot(s["qc"], so.astype(BF16)) for s, m, so in zip(st, sc, s_old)]
        s_new = [cd_ref[s["ti"]] * so + _dot_tn(s["kt"], s["v"]) for s, so in zip(st, s_old)]
        for si, s in enumerate(s_new):
            s_ref[si] = s
        _store_heads(refs, groups, rows, o, RET_H)
        return carry

    lax.fori_loop(0, CPB, chunk_body, 0)


def _ret_tables():
    pos = np.arange(CH, dtype=np.float64)
    intra, cross, tail, cd = [], [], [], []
    for d in range(N_DIR):
        expo = 5.0 + np.arange(RET_H, dtype=np.float64)
        if d == 1:
            expo = expo[::-1]
        lg = np.log1p(-np.exp2(-expo))
        p = pos if d == 0 else (CH - 1.0 - pos)
        diff = p[:, None] - p[None, :]
        for hh in range(RET_H):
            intra.append(np.where(diff >= 0, np.exp(np.where(diff >= 0, diff, 0.0) * lg[hh]), 0.0))
            cross.append(np.broadcast_to(np.exp((p + 1.0) * lg[hh])[:, None], (CH, HD)))
            tail.append(np.broadcast_to(np.exp((CH - 1.0 - p) * lg[hh])[:, None], (CH, HD)))
            cd.append(np.full((1, RET_DV), np.exp(CH * lg[hh])))
    f = lambda a: jnp.asarray(np.stack(a), F32)
    return f(intra), f(cross), f(tail), f(cd)


def _ret_scan(main):
    b, t, _ = main.shape
    nt = t // TM
    qw = RET_H * HD
    vw = RET_H * RET_DV
    qf, qb = _scan_specs(nt, qw, 0)
    kf, kb = _scan_specs(nt, qw, 1)
    vf, vb = _scan_specs(nt, vw, 1)
    of, ob = _scan_specs(nt, vw, 0)
    tabs = _ret_tables()
    const = lambda a: pl.BlockSpec(a.shape, lambda b_, i_: (0,) * a.ndim)
    return pl.pallas_call(
        _ret_kernel,
        out_shape=(jax.ShapeDtypeStruct((b, t, vw), BF16), jax.ShapeDtypeStruct((b, t, vw), BF16)),
        grid=(b // SCAN_BG, nt),
        in_specs=[qf, kf, vf, qb, kb, vb] + [const(a) for a in tabs],
        out_specs=(of, ob),
        scratch_shapes=[pltpu.VMEM((SCAN_BG * N_DIR * RET_H, HD, RET_DV), F32)],
        compiler_params=_cparams(("arbitrary", "arbitrary")),
        name="ret_scan",
    )(main, main, main, main, main, main, *tabs)


CODE_SHIFT = 17


def _route_and_pack(h2, rw_ref, rb_ref, h2p_ref, code_ref, cnt_ref, base_ref):
    h2b = h2.astype(BF16)
    logits = _dot(h2b, rw_ref[...]) + rb_ref[...]
    lane = lax.broadcasted_iota(jnp.int32, logits.shape, 1)
    lane_f = lane.astype(F32)
    vals, idxs = [], []
    cur = logits
    for _ in range(TOP_K):
        m = jnp.max(cur, axis=-1, keepdims=True)
        ix = jnp.min(jnp.where(cur == m, lane_f, float(logits.shape[1])), axis=-1, keepdims=True)
        vals.append(m)
        idxs.append(ix)
        cur = jnp.where(lane_f == ix, NEG, cur)
    es = [jnp.exp(v - vals[0]) for v in vals]
    tot = es[0] + es[1] + es[2] + es[3]

    @pl.when((pl.program_id(0) == 0) & (pl.program_id(1) == 0))
    def _():
        base_ref[...] = jnp.zeros_like(base_ref)

    tm = logits.shape[0]
    onehot = jnp.zeros(logits.shape, F32)
    for j in range(TOP_K):
        onehot = jnp.where(lane_f == idxs[j], 1.0, onehot)
    r_i = lax.broadcasted_iota(jnp.int32, (tm, tm), 0)
    c_i = lax.broadcasted_iota(jnp.int32, (tm, tm), 1)
    before = _dot(jnp.where(r_i > c_i, 1.0, 0.0).astype(BF16), onehot.astype(BF16))
    base = base_ref[...]
    pos = before + base
    total = base + before[tm - 1:tm, :] + onehot[tm - 1:tm, :]
    base_ref[...] = total
    cnt_ref[...] = total

    code = jnp.zeros(logits.shape, F32)
    wgt = jnp.zeros(logits.shape, F32)
    for j in range(TOP_K):
        rank = jnp.sum(jnp.where(lane_f == idxs[j], pos, 0.0), axis=-1, keepdims=True)
        code = jnp.where(lane == j, idxs[j] * float(1 << CODE_SHIFT) + rank, code)
        wgt = jnp.where(lane == TOP_K + j, es[j] / tot, wgt)
    code_ref[...] = jnp.where(lane < TOP_K, code.astype(jnp.int32), pltpu.bitcast(wgt, jnp.int32))
    half = h2.shape[1] // 2
    r = h2b.astype(F32)
    lo = lax.shift_right_logical(pltpu.bitcast(r[:, :half], U32), jnp.uint32(16))
    hi = pltpu.bitcast(r[:, half:], U32) & jnp.uint32(0xFFFF0000)
    _store_row_packed(h2p_ref, hi | lo)


def _merge_even_kernel(ogf, ogb, omf, omb, z_ref, mo_ref, gg_ref, mg_ref, wo_ref, x_ref, g1_ref,
                       n2_ref, sh_ref, sc_ref, rw_ref, rb_ref, x1_ref, h2p_ref, code_ref, cnt_ref, base_ref):
    og = ogf[...].astype(F32) + ogb[...].astype(F32)
    ms = _group_sum(og * og, HD) * (1.0 / HD)
    a = og * lax.rsqrt(ms + EPS) * gg_ref[...] * _silu(z_ref[...].astype(F32))
    om = omf[...].astype(F32) + omb[...].astype(F32)
    ms = _group_sum(om * om, HD) * (1.0 / HD)
    m = om * lax.rsqrt(ms + EPS) * mg_ref[...] * _sigmoid(mo_ref[...].astype(F32))
    cat = jnp.concatenate([a, m], axis=1).astype(BF16)
    y = _dot(cat, wo_ref[...])
    x1 = x_ref[...] + g1_ref[...] * y
    x1_ref[...] = x1
    h2 = _rms_mod(x1, n2_ref[...], sh_ref[...], sc_ref[...])
    _route_and_pack(h2, rw_ref, rb_ref, h2p_ref, code_ref, cnt_ref, base_ref)


def _merge_odd_kernel(of, ob, gate_ref, ng_ref, wo_ref, x_ref, g1_ref,
                      n2_ref, sh_ref, sc_ref, rw_ref, rb_ref, x1_ref, h2p_ref, code_ref, cnt_ref, base_ref):
    o = of[...].astype(F32) + ob[...].astype(F32)
    o = o - _group_sum(o, RET_DV) * (1.0 / RET_DV)
    ms = _group_sum(o * o, RET_DV) * (1.0 / RET_DV)
    y = o * lax.rsqrt(ms + EPS) * ng_ref[...] * _silu(gate_ref[...].astype(F32))
    y = _dot(y.astype(BF16), wo_ref[...])
    x1 = x_ref[...] + g1_ref[...] * y
    x1_ref[...] = x1
    h2 = _rms_mod(x1, n2_ref[...], sh_ref[...], sc_ref[...])
    _route_and_pack(h2, rw_ref, rb_ref, h2p_ref, code_ref, cnt_ref, base_ref)


def _merge_out(b, t, d):
    assert d // 2 == PACK * LANES
    shapes = (jax.ShapeDtypeStruct((b, t, d), F32), jax.ShapeDtypeStruct((b, t * PACK, LANES), U32),
              jax.ShapeDtypeStruct((b, t, LANES), jnp.int32), jax.ShapeDtypeStruct((1, LANES), F32))
    tile = lambda rows, width: pl.BlockSpec((None, rows, width), lambda b_, t_: (b_, t_, 0))
    specs = (tile(TM, d), tile(TM * PACK, LANES), tile(TM, LANES), pl.BlockSpec((1, LANES), lambda b_, t_: (0, 0)))
    return shapes, specs


def _merge_even(og_f, og_b, om_f, om_b, main, gdn_g, ml_g, w_out, x, g1, n2, sh2, sc2, rw, rb):
    b, t, d = x.shape
    w = GDN_H * HD
    tile = lambda width, col: pl.BlockSpec((None, TM, width), lambda b_, t_: (b_, t_, col))
    const = lambda a: pl.BlockSpec(a.shape, lambda b_, t_: (0,) * a.ndim)
    shapes, specs = _merge_out(b, t, d)
    return pl.pallas_call(
        _merge_even_kernel,
        out_shape=shapes,
        grid=(b, t // TM),
        in_specs=[tile(w, 0), tile(w, 0), tile(w, 0), tile(w, 0), tile(w, 3), tile(w, 7),
                  const(gdn_g), const(ml_g), const(w_out), tile(d, 0), _tile_mod_spec(d),
                  const(n2), _tile_mod_spec(d), _tile_mod_spec(d), const(rw), const(rb)],
        out_specs=specs,
        scratch_shapes=[pltpu.VMEM((1, 128), F32)],
        compiler_params=_cparams(("arbitrary", "arbitrary")),
        name="merge_even",
    )(og_f, og_b, om_f, om_b, main, main, gdn_g, ml_g, w_out, x, g1, n2, sh2, sc2, rw, rb)


def _merge_odd(o_f, o_b, main, ret_g, w_out, x, g1, n2, sh2, sc2, rw, rb):
    b, t, d = x.shape
    vw = RET_H * RET_DV
    tile = lambda width, col: pl.BlockSpec((None, TM, width), lambda b_, t_: (b_, t_, col))
    const = lambda a: pl.BlockSpec(a.shape, lambda b_, t_: (0,) * a.ndim)
    shapes, specs = _merge_out(b, t, d)
    return pl.pallas_call(
        _merge_odd_kernel,
        out_shape=shapes,
        grid=(b, t // TM),
        in_specs=[tile(vw, 0), tile(vw, 0), tile(vw, 2), const(ret_g), const(w_out), tile(d, 0),
                  _tile_mod_spec(d), const(n2), _tile_mod_spec(d), _tile_mod_spec(d), const(rw), const(rb)],
        out_specs=specs,
        scratch_shapes=[pltpu.VMEM((1, 128), F32)],
        compiler_params=_cparams(("arbitrary", "arbitrary")),
        name="merge_odd",
    )(o_f, o_b, main, ret_g, w_out, x, g1, n2, sh2, sc2, rw, rb)


def _proj_odd_kernel(x_ref, acc_ref, g2_ref, g_ref, sh_ref, sc_ref, w_ref, cos_ref, sin_ref,
                     x2_ref, main_ref):
    x2 = x_ref[...] + g2_ref[...] * _load_acc(acc_ref)
    x2_ref[...] = x2
    hb = _rms_mod(x2, g_ref[...], sh_ref[...], sc_ref[...]).astype(BF16)
    qk_w = RET_H * HD
    cos = cos_ref[...]
    sin = sin_ref[...]
    n_seg = w_ref.shape[1] // qk_w
    for seg in range(n_seg):
        sl = slice(seg * qk_w, (seg + 1) * qk_w)
        u = _dot(hb, w_ref[:, sl])
        if seg < 2:
            parts = []
            for hh in range(RET_H):
                uh = u[:, hh * HD:(hh + 1) * HD]
                parts.append(uh * cos + pltpu.roll(uh, HD // 2, 1) * sin)
            u = jnp.concatenate(parts, axis=1)
            if seg == 1:
                u = u * (HD ** -0.5)
        main_ref[:, sl] = u.astype(BF16)


def _rope_tables(t):
    half = HD // 2
    freqs = ROPE_BASE ** (-jnp.arange(half, dtype=F32) / half)
    ang = jnp.arange(t, dtype=F32)[:, None] * freqs[None, :]
    cos, sin = jnp.cos(ang), jnp.sin(ang)
    return jnp.concatenate([cos, cos], axis=1), jnp.concatenate([-sin, sin], axis=1)


def _proj_odd(x, acc, g2, g, sh, sc, w):
    b, t, d = x.shape
    n = w.shape[1]
    cos, sin = _rope_tables(t)
    const = lambda a: pl.BlockSpec(a.shape, lambda b_, t_: (0,) * a.ndim)
    tile = lambda width: pl.BlockSpec((None, TM, width), lambda b_, t_: (b_, t_, 0))
    rope = pl.BlockSpec((TM, HD), lambda b_, t_: (t_, 0))
    nt = t // TM
    acc_tile = _acc_tile_spec(acc, lambda b_, t_: b_ * nt + t_)
    return pl.pallas_call(
        _proj_odd_kernel,
        out_shape=(jax.ShapeDtypeStruct((b, t, d), F32), jax.ShapeDtypeStruct((b, t, n), BF16)),
        grid=(b, nt),
        in_specs=[tile(d), acc_tile, _tile_mod_spec(d), const(g), _tile_mod_spec(d), _tile_mod_spec(d),
                  const(w), rope, rope],
        out_specs=(tile(d), tile(n)),
        compiler_params=_cparams(("arbitrary", "arbitrary")),
        name="proj_odd",
    )(x, acc, g2, g, sh, sc, w, cos, sin)


ROWS_PER_STEP = 8


def _packed_row(r):
    return pl.ds(pl.multiple_of(r * PACK, PACK), PACK)


def _move_rows(n_blocks):
    return MOE_BLOCK * next(k for k in (4, 2, 1) if n_blocks % k == 0)


def _moe_gather_kernel(src_ref, h_ref, xs_ref):
    def body(r8, carry):
        base = r8 * ROWS_PER_STEP
        rows = [h_ref[_packed_row(src_ref[0, base + j]), :] for j in range(ROWS_PER_STEP)]
        for j in range(ROWS_PER_STEP):
            xs_ref[_packed_row(base + j), :] = rows[j]
        return carry

    lax.fori_loop(0, src_ref.shape[1] // ROWS_PER_STEP, body, 0)


def _moe_gather(row_src, h2p, n_blocks):
    step = _move_rows(n_blocks)
    n_steps = n_blocks * MOE_BLOCK // step
    return pl.pallas_call(
        _moe_gather_kernel,
        out_shape=jax.ShapeDtypeStruct((n_blocks * MOE_BLOCK * PACK, LANES), U32),
        grid=(n_steps,),
        in_specs=[pl.BlockSpec((None, 1, step), lambda i: (i, 0, 0), memory_space=pltpu.SMEM),
                  pl.BlockSpec(h2p.shape, lambda i: (0, 0))],
        out_specs=pl.BlockSpec((step * PACK, LANES), lambda i: (i, 0)),
        compiler_params=_cparams(("arbitrary",)),
        name="moe_gather",
    )(row_src.reshape(n_steps, 1, step), h2p)


def _moe_mm_kernel(be_ref, nu_ref, xs_ref, wgu_ref, bgu_ref, wdn_ref, bdn_ref, ys_ref, wgu_s, wdn_s):
    i = pl.program_id(0)
    e = be_ref[i]
    half = wgu_ref.shape[0] // 2
    hl = LANES // 2

    @pl.when(i >= nu_ref[0])
    def _():
        ys_ref[...] = jnp.zeros_like(ys_ref)

    @pl.when(i < nu_ref[0])
    def _():
        @pl.when((i == 0) | (e != be_ref[jnp.maximum(i - 1, 0)]))
        def _():
            wgu_s[...] = wgu_ref[...].astype(BF16)
            for p in range(wdn_ref.shape[0] // LANES):
                first = wdn_ref[p * LANES:p * LANES + hl, :].astype(BF16).astype(F32)
                second = wdn_ref[p * LANES + hl:(p + 1) * LANES, :].astype(BF16).astype(F32)
                word = (pltpu.bitcast(second, U32) & jnp.uint32(0xFFFF0000)) | \
                    lax.shift_right_logical(pltpu.bitcast(first, U32), jnp.uint32(16))
                wdn_s[p * LANES:(p + 1) * LANES, :] = pltpu.bitcast(word, BF16)

        xu = _load_row_packed(xs_ref, MOE_BLOCK)
        lo = pltpu.bitcast(lax.shift_left(xu, jnp.uint32(16)), F32).astype(BF16)
        hi = pltpu.bitcast(xu & jnp.uint32(0xFFFF0000), F32).astype(BF16)
        gu = _dot(lo, wgu_s[:half, :]) + _dot(hi, wgu_s[half:, :]) + bgu_ref[...]
        even = (lax.broadcasted_iota(jnp.int32, (gu.shape[0], LANES), 1) & 1) == 0
        acts = []
        for p in range(gu.shape[1] // (2 * LANES)):
            a = gu[:, 2 * p * LANES:(2 * p + 1) * LANES]
            b = gu[:, (2 * p + 1) * LANES:(2 * p + 2) * LANES]
            gate = jnp.minimum(jnp.where(even, a, pltpu.roll(b, 1, 1)), SWIGLU_LIMIT)
            up = jnp.clip(jnp.where(even, pltpu.roll(a, LANES - 1, 1), b), -SWIGLU_LIMIT, SWIGLU_LIMIT)
            acts.append(((up + 1.0) * gate * _sigmoid(SWIGLU_ALPHA * gate)).astype(BF16))
        y = _dot(jnp.concatenate(acts, axis=1), wdn_s[...]) + bdn_ref[...]
        cw = PACK * LANES
        for h in range(y.shape[1] // cw):
            _store_row_packed(ys_ref, y[:, h * cw:(h + 1) * cw], lead=(h,))


def _moe_mm(block_e, n_used, xs, layer, w_gu, b_gu, w_dn, b_dn):
    n_blocks = xs.shape[0] // (MOE_BLOCK * PACK)
    depth, n_e, d, f2 = w_gu.shape
    n_half = d // (PACK * LANES)
    ew = lambda shape: pl.BlockSpec((None, None) + shape, lambda i, be, nu: (layer, be[i], 0, 0))
    grid_spec = pltpu.PrefetchScalarGridSpec(
        num_scalar_prefetch=2,
        grid=(n_blocks,),
        in_specs=[pl.BlockSpec((MOE_BLOCK * PACK, LANES), lambda i, be, nu: (i, 0)),
                  ew((d, f2)), ew((1, f2)), ew((f2 // 2, d)), ew((1, d))],
        out_specs=pl.BlockSpec((n_half, MOE_BLOCK * PACK, LANES), lambda i, be, nu: (0, i, 0)),
        scratch_shapes=[pltpu.VMEM((d, f2), BF16), pltpu.VMEM((f2 // 2, d), BF16)],
    )
    return pl.pallas_call(
        _moe_mm_kernel,
        out_shape=jax.ShapeDtypeStruct((n_half, xs.shape[0], LANES), F32),
        grid_spec=grid_spec,
        compiler_params=_cparams(("arbitrary",)),
        name="moe_mm",
    )(block_e, n_used, xs, w_gu, b_gu.reshape(depth, n_e, 1, f2), w_dn, b_dn.reshape(depth, n_e, 1, d))


def _moe_combine_kernel(nu_ref, dst_ref, w_ref, ys_ref, acc_ref):
    i = pl.program_id(1)

    @pl.when(i == 0)
    def _():
        acc_ref[...] = jnp.zeros_like(acc_ref)

    def body(r8, carry):
        base = r8 * ROWS_PER_STEP
        dst = [_packed_row(dst_ref[0, base + j]) for j in range(ROWS_PER_STEP)]
        group = ROWS_PER_STEP * PACK
        ys = ys_ref[pl.ds(pl.multiple_of(r8 * group, group), group), :]
        new = [acc_ref[dst[j], :] + w_ref[0, base + j] * ys[j * PACK:(j + 1) * PACK, :]
               for j in range(ROWS_PER_STEP)]
        for j in range(ROWS_PER_STEP):
            acc_ref[dst[j], :] = new[j]
        return carry

    step = dst_ref.shape[1]

    @pl.when(i * (step // MOE_BLOCK) < nu_ref[0])
    def _():
        lax.fori_loop(0, step // ROWS_PER_STEP, body, 0)


def _moe_combine(n_used, row_dst, row_w, ys, n_rows_out):
    n_half, packed_rows, _ = ys.shape
    step = _move_rows(packed_rows // (MOE_BLOCK * PACK))
    n_blocks = packed_rows // (step * PACK)
    idx_spec = pl.BlockSpec((None, 1, step), lambda j, i, nu: (i, 0, 0), memory_space=pltpu.SMEM)
    grid_spec = pltpu.PrefetchScalarGridSpec(
        num_scalar_prefetch=1,
        grid=(n_half, n_blocks),
        in_specs=[idx_spec, idx_spec,
                  pl.BlockSpec((None, step * PACK, LANES), lambda j, i, nu: (j, i, 0))],
        out_specs=pl.BlockSpec((None, n_rows_out * PACK, LANES), lambda j, i, nu: (j, 0, 0),
                               pipeline_mode=pl.Buffered(1)),
    )
    return pl.pallas_call(
        _moe_combine_kernel,
        out_shape=jax.ShapeDtypeStruct((n_half, n_rows_out * PACK, LANES), F32),
        grid_spec=grid_spec,
        compiler_params=_cparams(("arbitrary", "arbitrary")),
        name="moe_combine",
    )(n_used, row_dst.reshape(n_blocks, 1, step), row_w.reshape(n_blocks, 1, step), ys)


def _moe_inverse_kernel(pad_lo_ref, pad_hi_ref, dest_ref, inv_ref):
    def pad_range(k, carry):
        def init(r, c):
            inv_ref[r] = -1
            return c

        lax.fori_loop(pad_lo_ref[k], pad_hi_ref[k], init, 0)
        return carry

    lax.fori_loop(0, pad_lo_ref.shape[0], pad_range, 0)

    def body(a, carry):
        inv_ref[dest_ref[a]] = a
        return carry

    lax.fori_loop(0, dest_ref.shape[0], body, 0, unroll=8)


def _moe_inverse(pad_lo, pad_hi, dest, rows):
    smem = pl.BlockSpec(memory_space=pltpu.SMEM)
    return pl.pallas_call(
        _moe_inverse_kernel,
        out_shape=jax.ShapeDtypeStruct((rows,), jnp.int32),
        in_specs=[smem, smem, smem],
        out_specs=smem,
        name="moe_inverse",
    )(pad_lo, pad_hi, dest)


def _moe(h2p, codes, weights, counts, layer, w_gu, b_gu, w_dn, b_dn):
    n = h2p.shape[0] // PACK
    n_assign = n * TOP_K
    n_blocks = -(-n_assign // MOE_BLOCK) + N_EXPERTS
    rows = n_blocks * MOE_BLOCK
    padded = -(-counts // MOE_BLOCK) * MOE_BLOCK
    ends = jnp.cumsum(padded)
    offsets = ends - padded
    starts = jnp.arange(n_blocks, dtype=jnp.int32) * MOE_BLOCK
    block_e = jnp.minimum(jnp.sum(ends[None, :] <= starts[:, None], axis=1), N_EXPERTS - 1).astype(jnp.int32)
    n_used = (ends[-1:] // MOE_BLOCK).astype(jnp.int32)
    expert = lax.shift_right_logical(codes, CODE_SHIFT)
    which = expert[:, None] == jnp.arange(N_EXPERTS, dtype=jnp.int32)[None, :]
    dest = (codes & ((1 << CODE_SHIFT) - 1)) + jnp.sum(jnp.where(which, offsets[None, :], 0), axis=1)
    pad_lo = jnp.concatenate([offsets + counts, ends[-1:]]).astype(jnp.int32)
    pad_hi = jnp.concatenate([ends, jnp.full((1,), rows, ends.dtype)]).astype(jnp.int32)
    inv = _moe_inverse(pad_lo, pad_hi, dest.astype(jnp.int32), rows)
    tok = lax.shift_right_logical(inv, TOP_K.bit_length() - 1)
    row_src = jnp.where(inv >= 0, tok, 0)
    row_dst = jnp.where(inv >= 0, tok, n)
    row_w = jnp.where(inv >= 0, jnp.take(weights, jnp.maximum(inv, 0)), 0.0)
    xs = _moe_gather(row_src, h2p, n_blocks)
    ys = _moe_mm(block_e, n_used, xs, layer, w_gu, b_gu, w_dn, b_dn)
    return _moe_combine(n_used, row_dst, row_w, ys, n + 8)


def _final_kernel(x_ref, acc_ref, g2_ref, g_ref, o_ref):
    x = x_ref[...] + g2_ref[...] * _load_acc(acc_ref)
    ms = jnp.mean(x * x, axis=-1, keepdims=True)
    o_ref[...] = x * lax.rsqrt(ms + EPS) * g_ref[...]


def _final(x, acc, g2, g, n_ctx_tiles):
    b, t, d = x.shape
    nt = t // TM - n_ctx_tiles
    tile_in = pl.BlockSpec((None, TM, d), lambda b_, t_: (b_, t_ + n_ctx_tiles, 0))
    acc_tile = _acc_tile_spec(acc, lambda b_, t_: b_ * (t // TM) + t_ + n_ctx_tiles)
    return pl.pallas_call(
        _final_kernel,
        out_shape=jax.ShapeDtypeStruct((b, nt * TM, d), F32),
        grid=(b, nt),
        in_specs=[tile_in, acc_tile,
                  pl.BlockSpec((None, None, 1, d), lambda b_, t_: (b_, 1, 0, 0)),
                  pl.BlockSpec((1, d), lambda b_, t_: (0, 0))],
        out_specs=pl.BlockSpec((None, TM, d), lambda b_, t_: (b_, t_, 0)),
        compiler_params=_cparams(("arbitrary", "arbitrary")),
        name="final_norm",
    )(x, acc, g2, g)


def _mod_tables(mod, b, d):
    outs = []
    for j in range(6):
        m = mod[:, j * d:(j + 1) * d]
        lat = m[:b]
        ctx = jnp.broadcast_to(m[b:b + 1], (b, d))
        outs.append(jnp.stack([ctx, lat], axis=1)[:, :, None, :])
    return outs


def _routing(code):
    codes = code[:, :, :TOP_K].reshape(-1)
    weights = lax.bitcast_convert_type(code[:, :, TOP_K:2 * TOP_K], F32).reshape(-1)
    return codes, weights


def _router_params(router_w, router_b):
    d, e = router_w.shape
    rw = jnp.zeros((d, 128), F32).at[:, :e].set(router_w).astype(BF16)
    rb = jnp.full((1, 128), NEG, F32).at[0, :e].set(router_b)
    return rw, rb


def kernel(x, c, ctx, c_ctx, mod_w, mod_b, norm1_g, norm2_g, ev_w_in, ev_conv_w, gdn_a_log, gdn_dt_bias,
           gdn_norm_g, ml_i_bias, ml_f_bias, ml_norm_g, ev_w_out, od_w_in, ret_norm_g, od_w_out,
           router_w, router_b, moe_w_gu, moe_b_gu, moe_w_dn, moe_b_dn, final_g):
    b, s, d = x.shape
    n_ctx = ctx.shape[1]
    depth = mod_w.shape[0]
    assert n_ctx == TM and s % TM == 0 and depth == 2 and b % SCAN_BG == 0
    t = n_ctx + s
    n_tok = b * t

    cond = jnp.concatenate([c, c_ctx[None, :], jnp.zeros((8 - b - 1, d), F32)], axis=0)
    mod = _adaln(cond, mod_w, mod_b)
    xa = jnp.concatenate([ctx, x], axis=1)

    sh1, sc1, g1, sh2, sc2, g2 = _mod_tables(mod[0], b, d)
    qk_w = GDN_H * HD
    conv_ch = 3 * qk_w
    ng = N_DIR * GDN_H
    w_in = ev_w_in[0]
    o_z = conv_ch
    o_a = o_z + qk_w
    o_mq = o_a + 2 * ng
    o_i = o_mq + 4 * qk_w
    w_main = jnp.concatenate([w_in[:, :o_a], w_in[:, o_mq:o_i]], axis=1).astype(BF16)
    w_gate = jnp.concatenate([w_in[:, o_a:o_mq], w_in[:, o_i:o_i + 2 * ng],
                              jnp.zeros((d, 128 - 4 * ng), F32)], axis=1).astype(BF16)
    zeros_g = jnp.zeros((ng,), F32)
    rate = jnp.concatenate([jnp.exp(gdn_a_log[0].astype(F32)).reshape(-1), jnp.zeros((128 - ng,), F32)])[None, :]
    gbias = jnp.concatenate([gdn_dt_bias[0].reshape(-1), zeros_g, ml_i_bias[0].reshape(-1),
                             ml_f_bias[0].reshape(-1), jnp.zeros((128 - 4 * ng,), F32)])[None, :].astype(F32)
    main, gates = _proj_even(xa, norm1_g[0][None, :], sh1, sc1, w_main, w_gate, ev_conv_w[0], rate, gbias)
    gates_c = gates.reshape(b, t // CH, CH, 4 * ng)
    gates_r = jnp.swapaxes(gates_c, 2, 3)
    og_f, og_b = _gdn_scan(main, gates_c, gates_r)
    om_f, om_b = _mlstm_scan(main, gates_c, gates_r)
    rw, rb = _router_params(router_w[0], router_b[0])
    gdn_g = jnp.tile(gdn_norm_g[0], GDN_H)[None, :]
    x1, h2p, code, cnt = _merge_even(og_f, og_b, om_f, om_b, main, gdn_g, ml_norm_g[0][None, :],
                                     ev_w_out[0].astype(BF16), xa, g1, norm2_g[0][None, :], sh2, sc2, rw, rb)
    acc = _moe(h2p.reshape(n_tok * PACK, LANES), *_routing(code), cnt[0, :N_EXPERTS].astype(jnp.int32),
               0, moe_w_gu, moe_b_gu, moe_w_dn, moe_b_dn)
    g2_prev = g2

    sh1, sc1, g1, sh2, sc2, g2 = _mod_tables(mod[1], b, d)
    x2, main_o = _proj_odd(x1, acc, g2_prev, norm1_g[1][None, :], sh1, sc1, od_w_in[0].astype(BF16))
    o_f, o_b = _ret_scan(main_o)
    rw, rb = _router_params(router_w[1], router_b[1])
    x3, h2p, code, cnt = _merge_odd(o_f, o_b, main_o, ret_norm_g[0][None, :], od_w_out[0].astype(BF16),
                                    x2, g1, norm2_g[1][None, :], sh2, sc2, rw, rb)
    acc = _moe(h2p.reshape(n_tok * PACK, LANES), *_routing(code), cnt[0, :N_EXPERTS].astype(jnp.int32),
               1, moe_w_gu, moe_b_gu, moe_w_dn, moe_b_dn)
    return _final(x3, acc, g2, final_g[None, :], n_ctx // TM)
```

```python
import functools
import math

import jax
import jax.numpy as jnp
import numpy as np
from jax import lax
from jax.experimental import pallas as pl
from jax.experimental.pallas import tpu as pltpu

F32 = jnp.float32
BF16 = jnp.bfloat16
U32 = jnp.uint32
HIGHEST = lax.Precision.HIGHEST

EPS = 1e-6
CH = 64
TM = 256
CPB = TM // CH
HD = 128
N_DIR = 2
GDN_H = 4
ML_H = 4
RET_H = 8
RET_DV = 256
CONV_W = 3
N_EXPERTS = 32
TOP_K = 4
SWIGLU_ALPHA = 1.702
SWIGLU_LIMIT = 7.0
MOE_BLOCK = 256
ROPE_BASE = 10000.0
NEG = -1e30
VMEM_LIMIT = 56 * 1024 * 1024


def _cparams(sem):
    return pltpu.CompilerParams(dimension_semantics=sem, vmem_limit_bytes=VMEM_LIMIT)


def _dot(a, b, precision=None):
    return jnp.dot(a, b, preferred_element_type=F32, precision=precision)


def _dot_nt(a, b):
    return lax.dot_general(a, b, (((1,), (1,)), ((), ())), preferred_element_type=F32)


def _dot_tn(a, b):
    return lax.dot_general(a, b, (((0,), (0,)), ((), ())), preferred_element_type=F32)


def _sigmoid(x):
    return 1.0 / (1.0 + jnp.exp(-x))


def _silu(x):
    return x * _sigmoid(x)


def _group_sum(x, w):
    outs = []
    for j in range(x.shape[1] // w):
        s = jnp.sum(x[:, j * w:(j + 1) * w], axis=-1, keepdims=True)
        outs.append(jnp.broadcast_to(s, (x.shape[0], w)))
    return outs[0] if len(outs) == 1 else jnp.concatenate(outs, axis=1)


def _rms_mod(x, g, sh, sc):
    ms = jnp.mean(x * x, axis=-1, keepdims=True)
    return (x * lax.rsqrt(ms + EPS) * g) * (1.0 + sc) + sh


LANES = 128
PACK = 4


def _store_row_packed(ref, x, lead=()):
    rows = x.shape[0]
    for g in range(x.shape[1] // LANES):
        ref[lead + (pl.ds(g, rows, stride=PACK), slice(None))] = x[:, g * LANES:(g + 1) * LANES]


def _load_row_packed(ref, rows, lead=()):
    return jnp.concatenate([ref[lead + (pl.ds(g, rows, stride=PACK), slice(None))] for g in range(PACK)], axis=1)


def _load_acc(acc_ref):
    rows = acc_ref.shape[1] // PACK
    return jnp.concatenate([_load_row_packed(acc_ref, rows, lead=(h,)) for h in range(acc_ref.shape[0])], axis=1)


def _acc_tile_spec(acc, tile_of):
    return pl.BlockSpec((acc.shape[0], TM * PACK, LANES), lambda b_, t_: (0, tile_of(b_, t_), 0))


def _adaln_kernel(c_ref, w_ref, b_ref, o_ref):
    c = c_ref[...]
    o_ref[...] = _dot(_silu(c), w_ref[...], precision=HIGHEST) + b_ref[...]


def _adaln(cond, mod_w, mod_b):
    depth, d, d6 = mod_w.shape
    n = d6 // d
    return pl.pallas_call(
        _adaln_kernel,
        out_shape=jax.ShapeDtypeStruct((depth, cond.shape[0], d6), F32),
        grid=(depth, n),
        in_specs=[pl.BlockSpec(cond.shape, lambda l, j: (0, 0)),
                  pl.BlockSpec((None, d, d), lambda l, j: (l, 0, j)),
                  pl.BlockSpec((None, 1, d), lambda l, j: (l, 0, j))],
        out_specs=pl.BlockSpec((None, cond.shape[0], d), lambda l, j: (l, 0, j)),
        compiler_params=_cparams(("arbitrary", "arbitrary")),
        name="adaln",
    )(cond, mod_w, mod_b.reshape(depth, 1, d6))


def _proj_even_kernel(x_ref, g_ref, sh_ref, sc_ref, w_ref, wg_ref, cw_ref, rate_ref, gb_ref,
                      main_ref, gates_ref):
    t = pl.program_id(1)
    h = _rms_mod(x_ref[...], g_ref[...], sh_ref[...], sc_ref[...])
    hb = h.astype(BF16)
    qk_w = GDN_H * HD

    row = lax.broadcasted_iota(jnp.int32, (TM, 1), 0)
    pos = jnp.where(t > 0, row & (CH - 1), row)
    last = jnp.where(t > 0, CH - 1, TM - 1)
    left_ok = pos != 0
    right_ok = pos != last
    for seg in range(3):
        sl = slice(seg * qk_w, (seg + 1) * qk_w)
        u = _dot(hb, w_ref[:, sl])
        um = jnp.where(left_ok, pltpu.roll(u, 1, 0), 0.0)
        up = jnp.where(right_ok, pltpu.roll(u, TM - 1, 0), 0.0)
        cv = _silu(um * cw_ref[0:1, sl] + u * cw_ref[1:2, sl] + up * cw_ref[2:3, sl])
        if seg < 2:
            ss = _group_sum(cv * cv, HD)
            cv = cv * lax.rsqrt(ss + EPS)
            if seg == 0:
                cv = cv * (HD ** -0.5)
        main_ref[:, sl] = cv.astype(BF16)
    for seg in range(3, 8):
        sl = slice(seg * qk_w, (seg + 1) * qk_w)
        u = _dot(hb, w_ref[:, sl])
        if seg == 5:
            u = u * (HD ** -0.5)
        main_ref[:, sl] = u.astype(BF16)

    z = _dot(hb, wg_ref[...]) + gb_ref[...]
    tl = jnp.log(1.0 + jnp.exp(-jnp.abs(z)))
    sp_pos = jnp.maximum(z, 0.0) + tl
    sp_neg = jnp.maximum(-z, 0.0) + tl
    lane = lax.broadcasted_iota(jnp.int32, z.shape, 1)
    ng = N_DIR * GDN_H
    res = jnp.where(lane < ng, -rate_ref[...] * sp_pos,
                    jnp.where(lane < 2 * ng, _sigmoid(z),
                              jnp.where(lane < 3 * ng, z, -sp_neg)))
    gates_ref[...] = res[:, :gates_ref.shape[-1]]


def _tile_mod_spec(d):
    return pl.BlockSpec((None, None, 1, d), lambda b, t: (b, jnp.minimum(t, 1), 0, 0))


def _proj_even(x, g, sh, sc, w_main, w_gate, conv_w, rate, gbias):
    b, t, d = x.shape
    n = w_main.shape[1]
    ngl = 4 * N_DIR * GDN_H
    const = lambda shape: pl.BlockSpec(shape, lambda b_, t_: (0,) * len(shape))
    return pl.pallas_call(
        _proj_even_kernel,
        out_shape=(jax.ShapeDtypeStruct((b, t, n), BF16), jax.ShapeDtypeStruct((b, t, ngl), F32)),
        grid=(b, t // TM),
        in_specs=[pl.BlockSpec((None, TM, d), lambda b_, t_: (b_, t_, 0)),
                  const((1, d)), _tile_mod_spec(d), _tile_mod_spec(d),
                  const(w_main.shape), const(w_gate.shape), const(conv_w.shape),
                  const(rate.shape), const(gbias.shape)],
        out_specs=(pl.BlockSpec((None, TM, n), lambda b_, t_: (b_, t_, 0)),
                   pl.BlockSpec((None, TM, ngl), lambda b_, t_: (b_, t_, 0))),
        compiler_params=_cparams(("arbitrary", "arbitrary")),
        name="proj_even",
    )(x, g, sh, sc, w_main, w_gate, conv_w, rate, gbias)


def _rev_tile(i, nt):
    return jnp.where(i == 0, 0, nt - i)


def _tri_masks():
    r = lax.broadcasted_iota(jnp.int32, (CH, CH), 0)
    c = lax.broadcasted_iota(jnp.int32, (CH, CH), 1)
    return r >= c, r > c, r <= c, r < c


SCAN_BG = 2


def _scan_streams(heads):
    groups = [(bi, d) for bi in range(SCAN_BG) for d in range(N_DIR)]
    streams = [(gi, hh) for gi in range(len(groups)) for hh in range(heads)]
    return groups, streams


def _cumsum_both(groups, gcs, grs, lower, upper):
    tri = (lower.astype(F32), upper.astype(F32))
    cs_c = [_dot(tri[d], gc, precision=HIGHEST) for (_, d), gc in zip(groups, gcs)]
    cs_r = [_dot(gr, tri[1 - d], precision=HIGHEST) for (_, d), gr in zip(groups, grs)]
    return cs_c, cs_r


def _store_heads(refs, groups, rows, outs, heads):
    for gi, (bi, d) in enumerate(groups):
        o_ref = refs[d][-1]
        tile = jnp.concatenate(outs[gi * heads:(gi + 1) * heads], axis=1)
        o_ref[bi, rows[d], :] = tile.astype(o_ref.dtype)


class _GdnChunk:
    def __init__(self, refs, s_ref):
        self.refs, self.s_ref = refs, s_ref
        self.groups, self.streams = _scan_streams(GDN_H)

    def prelude(self, masks, rows, gcs, cs_c, cs_r):
        lower, lstrict, upper, ustrict = masks
        refs, groups = self.refs, self.groups
        ng = N_DIR * GDN_H
        st = []
        for gi, hh in self.streams:
            bi, d = groups[gi]
            ci = d * GDN_H + hh
            cols = slice(hh * HD, (hh + 1) * HD)
            incl, strict = (lower, lstrict) if d == 0 else (upper, ustrict)
            g_col = cs_c[gi][:, ci:ci + 1]
            g_row = cs_r[gi][ci:ci + 1, :]
            beta = gcs[gi][:, ng + ci:ng + ci + 1]
            tot = g_col[CH - 1:CH, :] if d == 0 else g_col[0:1, :]
            decay = jnp.where(incl, jnp.exp(jnp.where(incl, g_col - g_row, 0.0)), 0.0)
            q = refs[d][0][bi, rows[d], cols].astype(F32)
            k = refs[d][1][bi, rows[d], cols].astype(F32)
            v = refs[d][2][bi, rows[d], cols].astype(F32)
            kbeta = k * beta
            eg = jnp.exp(g_col)
            st.append(dict(
                strict=strict, decay=decay, kb=k.astype(BF16), kbetab=kbeta.astype(BF16), qb=q.astype(BF16),
                x=jnp.concatenate([v * beta, kbeta * eg], axis=1),
                qe=(q * eg).astype(BF16), kdec=(k * jnp.exp(tot - g_col)).astype(BF16), cd=jnp.exp(tot)))
        self.st = st

    def matmuls(self, rows):
        st, s_ref, streams = self.st, self.s_ref, self.streams
        kk = [_dot_nt(s["kbetab"], s["kb"]) for s in st]
        qk = [(_dot_nt(s["qb"], s["kb"]) * s["decay"]).astype(BF16) for s in st]
        p = [-jnp.where(s["strict"], m * s["decay"], 0.0) for s, m in zip(st, kk)]
        x = [s["x"] for s in st]
        for j in range(6):
            pb = [m.astype(BF16) for m in p]
            x = [xx + _dot(m, xx.astype(BF16)) for m, xx in zip(pb, x)]
            if j < 5:
                p = [_dot(m, m) for m in pb]
        s_old = [s_ref[si] for si in range(len(streams))]
        sb = [s.astype(BF16) for s in s_old]
        vnb = [(xx[:, :HD] - _dot(xx[:, HD:].astype(BF16), s)).astype(BF16) for xx, s in zip(x, sb)]
        o = [_dot(s["qe"], sbi) + _dot(m, vn) for s, sbi, m, vn in zip(st, sb, qk, vnb)]
        s_new = [so * s["cd"] + _dot_tn(s["kdec"], vn) for s, so, vn in zip(st, s_old, vnb)]
        for si, s in enumerate(s_new):
            s_ref[si] = s
        _store_heads(self.refs, self.groups, rows, o, GDN_H)


def _scan_specs(nt, width, col):
    fwd = pl.BlockSpec((SCAN_BG, TM, width), lambda b, i: (b, i, col))
    bwd = pl.BlockSpec((SCAN_BG, TM, width), lambda b, i: (b, _rev_tile(i, nt), col))
    return fwd, bwd


def _gate_specs(nt, shape):
    fwd = pl.BlockSpec((SCAN_BG, CPB) + shape, lambda b, i: (b, i, 0, 0))
    bwd = pl.BlockSpec((SCAN_BG, CPB) + shape, lambda b, i: (b, _rev_tile(i, nt), 0, 0))
    return fwd, bwd


class _MlstmChunk:
    def __init__(self, refs, c_ref, m_ref):
        self.refs, self.c_ref, self.m_ref = refs, c_ref, m_ref
        self.groups, self.streams = _scan_streams(ML_H)

    def prelude(self, masks, rows, gcs, grs, cs_c, cs_r):
        lower, _, upper, _ = masks
        refs, groups, streams, m_ref = self.refs, self.groups, self.streams, self.m_ref
        ng = N_DIR * GDN_H
        i_off = 2 * ng
        f_off = 2 * ng + N_DIR * ML_H
        lane = lax.broadcasted_iota(jnp.int32, (CH, HD), 1)
        ones_col = jnp.where(lane == 0, 1.0, 0.0).astype(BF16)
        ns = len(streams)
        dirs = [groups[gi][1] for gi, _ in streams]
        chan = [groups[gi][1] * ML_H + hh for gi, hh in streams]
        b_col = [cs_c[gi][:, f_off + c:f_off + c + 1] for (gi, _), c in zip(streams, chan)]
        b_row = [cs_r[gi][f_off + c:f_off + c + 1, :] for (gi, _), c in zip(streams, chan)]
        i_col = [gcs[gi][:, i_off + c:i_off + c + 1] for (gi, _), c in zip(streams, chan)]
        i_row = [grs[gi][i_off + c:i_off + c + 1, :] for (gi, _), c in zip(streams, chan)]
        b_tot = [bc[CH - 1:CH, :] if d == 0 else bc[0:1, :] for bc, d in zip(b_col, dirs)]
        m_rows = [m_ref[si] for si in range(ns)]
        m_old = [mr[:, 0:1] for mr in m_rows]
        d_in = [jnp.where(lower if d == 0 else upper, bc - br + ir, NEG)
                for d, bc, br, ir in zip(dirs, b_col, b_row, i_row)]
        d_end = [bt - br + ir for bt, br, ir in zip(b_tot, b_row, i_row)]
        mx_in = [jnp.max(a, axis=-1, keepdims=True) for a in d_in]
        mx_end = [jnp.max(a, axis=-1, keepdims=True) for a in d_end]
        d_carry = [bc + m for bc, m in zip(b_col, m_old)]
        m_t = [jnp.maximum(a, b_) for a, b_ in zip(d_carry, mx_in)]
        carry_end = [bt + m for bt, m in zip(b_tot, m_old)]
        m_new = [jnp.maximum(a, b_) for a, b_ in zip(carry_end, mx_end)]
        p_in = [jnp.exp(a - b_) for a, b_ in zip(d_in, m_t)]
        w_end = [jnp.exp(bt - bc + ic - mn) for bt, bc, ic, mn in zip(b_tot, b_col, i_col, m_new)]
        st = []
        for si, (gi, hh) in enumerate(streams):
            bi, d = groups[gi]
            cols = slice(hh * HD, (hh + 1) * HD)
            k = refs[d][1][bi, rows[d], cols]
            v = refs[d][2][bi, rows[d], cols]
            st.append(dict(
                q=refs[d][0][bi, rows[d], cols], k=k,
                v_aug=jnp.concatenate([v, ones_col], axis=1),
                p_in=p_in[si], w_carry=jnp.exp(d_carry[si] - m_t[si]), floor=jnp.exp(-m_t[si]),
                kw=(k.astype(F32) * w_end[si]).astype(BF16),
                f_end=jnp.exp(carry_end[si] - m_new[si]),
                m_new=jnp.broadcast_to(m_new[si], m_rows[si].shape)))
        self.st = st

    def matmuls(self, rows):
        st, c_ref, m_ref, streams = self.st, self.c_ref, self.m_ref, self.streams
        sc = [(_dot_nt(s["q"], s["k"]) * s["p_in"]).astype(BF16) for s in st]
        c_old = [c_ref[si] for si in range(len(streams))]
        qc = [_dot(s["q"], c.astype(BF16)) for s, c in zip(st, c_old)]
        nd = [s["w_carry"] * a + _dot(m, s["v_aug"]) for s, a, m in zip(st, qc, sc)]
        hout = [a[:, :HD] / jnp.maximum(jnp.abs(a[:, HD:HD + 1]), s["floor"]) for s, a in zip(st, nd)]
        c_new = [s["f_end"] * c + _dot_tn(s["kw"], s["v_aug"]) for s, c in zip(st, c_old)]
        for si, (s, c) in enumerate(zip(st, c_new)):
            c_ref[si] = c
            m_ref[si] = s["m_new"]
        _store_heads(self.refs, self.groups, rows, hout, ML_H)


def _even_scan_kernel(gqf, gkf, gvf, gqb, gkb, gvb, mqf, mkf, mvf, mqb, mkb, mvb, gcf, grf, gcb, grb,
                      ogf, ogb, omf, omb, s_ref, c_ref, m_ref):
    i = pl.program_id(1)

    @pl.when(i == 0)
    def _():
        s_ref[...] = jnp.zeros_like(s_ref)
        c_ref[...] = jnp.zeros_like(c_ref)
        m_ref[...] = jnp.zeros_like(m_ref)

    masks = _tri_masks()
    gdn = _GdnChunk(((gqf, gkf, gvf, ogf), (gqb, gkb, gvb, ogb)), s_ref)
    mls = _MlstmChunk(((mqf, mkf, mvf, omf), (mqb, mkb, mvb, omb)), c_ref, m_ref)
    gate_refs = ((gcf, grf), (gcb, grb))

    def chunk_body(cc, carry):
        cidx = (cc, CPB - 1 - cc)
        rows = tuple(pl.ds(pl.multiple_of(c * CH, CH), CH) for c in cidx)
        gcs = [gate_refs[d][0][bi, cidx[d]] for bi, d in gdn.groups]
        grs = [gate_refs[d][1][bi, cidx[d]] for bi, d in gdn.groups]
        cs_c, cs_r = _cumsum_both(gdn.groups, gcs, grs, masks[0], masks[2])
        gdn.prelude(masks, rows, gcs, cs_c, cs_r)
        mls.prelude(masks, rows, gcs, grs, cs_c, cs_r)
        gdn.matmuls(rows)
        mls.matmuls(rows)
        return carry

    lax.fori_loop(0, CPB, chunk_body, 0)


def _even_scan(main, gates_c, gates_r):
    b, t, _ = main.shape
    nt = t // TM
    w = GDN_H * HD
    ngl = gates_c.shape[-1]
    qkv = lambda cols: [_scan_specs(nt, w, c)[d] for d in range(N_DIR) for c in cols]
    gcf, gcb = _gate_specs(nt, (CH, ngl))
    grf, grb = _gate_specs(nt, (ngl, CH))
    of, ob = _scan_specs(nt, w, 0)
    out = jax.ShapeDtypeStruct((b, t, w), BF16)
    n_streams = SCAN_BG * N_DIR * GDN_H
    return pl.pallas_call(
        _even_scan_kernel,
        out_shape=(out, out, out, out),
        grid=(b // SCAN_BG, nt),
        in_specs=qkv((0, 1, 2)) + qkv((4, 5, 6)) + [gcf, grf, gcb, grb],
        out_specs=(of, ob, of, ob),
        scratch_shapes=[pltpu.VMEM((n_streams, HD, HD), F32), pltpu.VMEM((n_streams, HD, 2 * HD), F32),
                        pltpu.VMEM((n_streams, 1, HD), F32)],
        compiler_params=_cparams(("arbitrary", "arbitrary")),
        name="even_scan",
    )(*([main] * 12), gates_c, gates_r, gates_c, gates_r)


def _ret_kernel(qf, kf, vf, qb, kb, vb, intra_ref, cross_ref, tail_ref, cd_ref, of, ob, s_ref):
    i = pl.program_id(1)

    @pl.when(i == 0)
    def _():
        s_ref[...] = jnp.zeros_like(s_ref)

    refs = ((qf, kf, vf, of), (qb, kb, vb, ob))
    groups, streams = _scan_streams(RET_H)

    def chunk_body(cc, carry):
        cidx = (cc, CPB - 1 - cc)
        rows = tuple(pl.ds(pl.multiple_of(c * CH, CH), CH) for c in cidx)
        st = []
        for gi, hh in streams:
            bi, d = groups[gi]
            ti = d * RET_H + hh
            kcols = slice(hh * HD, (hh + 1) * HD)
            q = refs[d][0][bi, rows[d], kcols]
            k = refs[d][1][bi, rows[d], kcols]
            st.append(dict(
                ti=ti, q=q, k=k, v=refs[d][2][bi, rows[d], slice(hh * RET_DV, (hh + 1) * RET_DV)],
                qc=(q.astype(F32) * cross_ref[ti]).astype(BF16),
                kt=(k.astype(F32) * tail_ref[ti]).astype(BF16)))
        sc = [(_dot_nt(s["q"], s["k"]) * intra_ref[s["ti"]]).astype(BF16) for s in st]
        s_old = [s_ref[si] for si in range(len(streams))]
        o = [_dot(m, s["v"]) + _dot(s["qc"], so.astype(BF16)) for s, m, so in zip(st, sc, s_old)]
        s_new = [cd_ref[s["ti"]] * so + _dot_tn(s["kt"], s["v"]) for s, so in zip(st, s_old)]
        for si, s in enumerate(s_new):
            s_ref[si] = s
        _store_heads(refs, groups, rows, o, RET_H)
        return carry

    lax.fori_loop(0, CPB, chunk_body, 0)


def _ret_tables():
    pos = np.arange(CH, dtype=np.float64)
    intra, cross, tail, cd = [], [], [], []
    for d in range(N_DIR):
        expo = 5.0 + np.arange(RET_H, dtype=np.float64)
        if d == 1:
            expo = expo[::-1]
        lg = np.log1p(-np.exp2(-expo))
        p = pos if d == 0 else (CH - 1.0 - pos)
        diff = p[:, None] - p[None, :]
        for hh in range(RET_H):
            intra.append(np.where(diff >= 0, np.exp(np.where(diff >= 0, diff, 0.0) * lg[hh]), 0.0))
            cross.append(np.broadcast_to(np.exp((p + 1.0) * lg[hh])[:, None], (CH, HD)))
            tail.append(np.broadcast_to(np.exp((CH - 1.0 - p) * lg[hh])[:, None], (CH, HD)))
            cd.append(np.full((1, RET_DV), np.exp(CH * lg[hh])))
    f = lambda a: jnp.asarray(np.stack(a), F32)
    return f(intra), f(cross), f(tail), f(cd)


def _ret_scan(main):
    b, t, _ = main.shape
    nt = t // TM
    qw = RET_H * HD
    vw = RET_H * RET_DV
    qf, qb = _scan_specs(nt, qw, 0)
    kf, kb = _scan_specs(nt, qw, 1)
    vf, vb = _scan_specs(nt, vw, 1)
    of, ob = _scan_specs(nt, vw, 0)
    tabs = _ret_tables()
    const = lambda a: pl.BlockSpec(a.shape, lambda b_, i_: (0,) * a.ndim)
    return pl.pallas_call(
        _ret_kernel,
        out_shape=(jax.ShapeDtypeStruct((b, t, vw), BF16), jax.ShapeDtypeStruct((b, t, vw), BF16)),
        grid=(b // SCAN_BG, nt),
        in_specs=[qf, kf, vf, qb, kb, vb] + [const(a) for a in tabs],
        out_specs=(of, ob),
        scratch_shapes=[pltpu.VMEM((SCAN_BG * N_DIR * RET_H, HD, RET_DV), F32)],
        compiler_params=_cparams(("arbitrary", "arbitrary")),
        name="ret_scan",
    )(main, main, main, main, main, main, *tabs)


CODE_SHIFT = 17


def _route_and_pack(h2, rw_ref, rb_ref, h2p_ref, code_ref, cnt_ref, base_ref, skip_context):
    h2b = h2.astype(BF16)
    logits = _dot(h2b, rw_ref[...]) + rb_ref[...]
    lane = lax.broadcasted_iota(jnp.int32, logits.shape, 1)
    lane_f = lane.astype(F32)
    vals, idxs = [], []
    cur = logits
    for _ in range(TOP_K):
        m = jnp.max(cur, axis=-1, keepdims=True)
        ix = jnp.min(jnp.where(cur == m, lane_f, float(logits.shape[1])), axis=-1, keepdims=True)
        vals.append(m)
        idxs.append(ix)
        cur = jnp.where(lane_f == ix, NEG, cur)
    es = [jnp.exp(v - vals[0]) for v in vals]
    tot = es[0] + es[1] + es[2] + es[3]

    @pl.when((pl.program_id(0) == 0) & (pl.program_id(1) == 0))
    def _():
        base_ref[...] = jnp.zeros_like(base_ref)

    tm = logits.shape[0]
    routed = pl.program_id(1) > 0
    onehot = jnp.zeros(logits.shape, F32)
    for j in range(TOP_K):
        onehot = jnp.where(lane_f == idxs[j], 1.0, onehot)
    if skip_context:
        onehot = onehot * jnp.where(routed, 1.0, 0.0)
    r_i = lax.broadcasted_iota(jnp.int32, (tm, tm), 0)
    c_i = lax.broadcasted_iota(jnp.int32, (tm, tm), 1)
    before = _dot(jnp.where(r_i > c_i, 1.0, 0.0).astype(BF16), onehot.astype(BF16))
    base = base_ref[...]
    pos = before + base
    total = base + before[tm - 1:tm, :] + onehot[tm - 1:tm, :]
    base_ref[...] = total
    cnt_ref[...] = total

    code = jnp.zeros(logits.shape, F32)
    wgt = jnp.zeros(logits.shape, F32)
    for j in range(TOP_K):
        rank = jnp.sum(jnp.where(lane_f == idxs[j], pos, 0.0), axis=-1, keepdims=True)
        code = jnp.where(lane == j, idxs[j] * float(1 << CODE_SHIFT) + rank, code)
        wgt = jnp.where(lane == TOP_K + j, es[j] / tot, wgt)
    code_i = code.astype(jnp.int32)
    if skip_context:
        code_i = jnp.where(routed, code_i, -1)
    code_ref[...] = jnp.where(lane < TOP_K, code_i, pltpu.bitcast(wgt, jnp.int32))
    half = h2.shape[1] // 2
    r = h2b.astype(F32)
    lo = lax.shift_right_logical(pltpu.bitcast(r[:, :half], U32), jnp.uint32(16))
    hi = pltpu.bitcast(r[:, half:], U32) & jnp.uint32(0xFFFF0000)
    _store_row_packed(h2p_ref, hi | lo)


def _merge_even_kernel(ogf, ogb, omf, omb, z_ref, mo_ref, gg_ref, mg_ref, wo_ref, x_ref, g1_ref,
                       n2_ref, sh_ref, sc_ref, rw_ref, rb_ref, x1_ref, h2p_ref, code_ref, cnt_ref, base_ref):
    og = ogf[...].astype(F32) + ogb[...].astype(F32)
    ms = _group_sum(og * og, HD) * (1.0 / HD)
    a = og * lax.rsqrt(ms + EPS) * gg_ref[...] * _silu(z_ref[...].astype(F32))
    om = omf[...].astype(F32) + omb[...].astype(F32)
    ms = _group_sum(om * om, HD) * (1.0 / HD)
    m = om * lax.rsqrt(ms + EPS) * mg_ref[...] * _sigmoid(mo_ref[...].astype(F32))
    cat = jnp.concatenate([a, m], axis=1).astype(BF16)
    y = _dot(cat, wo_ref[...])
    x1 = x_ref[...] + g1_ref[...] * y
    x1_ref[...] = x1
    h2 = _rms_mod(x1, n2_ref[...], sh_ref[...], sc_ref[...])
    _route_and_pack(h2, rw_ref, rb_ref, h2p_ref, code_ref, cnt_ref, base_ref, skip_context=False)


def _merge_odd_kernel(of, ob, gate_ref, ng_ref, wo_ref, x_ref, g1_ref,
                      n2_ref, sh_ref, sc_ref, rw_ref, rb_ref, x1_ref, h2p_ref, code_ref, cnt_ref, base_ref):
    o = of[...].astype(F32) + ob[...].astype(F32)
    o = o - _group_sum(o, RET_DV) * (1.0 / RET_DV)
    ms = _group_sum(o * o, RET_DV) * (1.0 / RET_DV)
    y = o * lax.rsqrt(ms + EPS) * ng_ref[...] * _silu(gate_ref[...].astype(F32))
    y = _dot(y.astype(BF16), wo_ref[...])
    x1 = x_ref[...] + g1_ref[...] * y
    x1_ref[...] = x1
    h2 = _rms_mod(x1, n2_ref[...], sh_ref[...], sc_ref[...])
    _route_and_pack(h2, rw_ref, rb_ref, h2p_ref, code_ref, cnt_ref, base_ref, skip_context=True)


def _merge_out(b, t, d):
    assert d // 2 == PACK * LANES
    shapes = (jax.ShapeDtypeStruct((b, t, d), F32), jax.ShapeDtypeStruct((b, t * PACK, LANES), U32),
              jax.ShapeDtypeStruct((b, t, LANES), jnp.int32), jax.ShapeDtypeStruct((1, LANES), F32))
    tile = lambda rows, width: pl.BlockSpec((None, rows, width), lambda b_, t_: (b_, t_, 0))
    specs = (tile(TM, d), tile(TM * PACK, LANES), tile(TM, LANES), pl.BlockSpec((1, LANES), lambda b_, t_: (0, 0)))
    return shapes, specs


def _merge_even(og_f, og_b, om_f, om_b, main, gdn_g, ml_g, w_out, x, g1, n2, sh2, sc2, rw, rb):
    b, t, d = x.shape
    w = GDN_H * HD
    tile = lambda width, col: pl.BlockSpec((None, TM, width), lambda b_, t_: (b_, t_, col))
    const = lambda a: pl.BlockSpec(a.shape, lambda b_, t_: (0,) * a.ndim)
    shapes, specs = _merge_out(b, t, d)
    return pl.pallas_call(
        _merge_even_kernel,
        out_shape=shapes,
        grid=(b, t // TM),
        in_specs=[tile(w, 0), tile(w, 0), tile(w, 0), tile(w, 0), tile(w, 3), tile(w, 7),
                  const(gdn_g), const(ml_g), const(w_out), tile(d, 0), _tile_mod_spec(d),
                  const(n2), _tile_mod_spec(d), _tile_mod_spec(d), const(rw), const(rb)],
        out_specs=specs,
        scratch_shapes=[pltpu.VMEM((1, 128), F32)],
        compiler_params=_cparams(("arbitrary", "arbitrary")),
        name="merge_even",
    )(og_f, og_b, om_f, om_b, main, main, gdn_g, ml_g, w_out, x, g1, n2, sh2, sc2, rw, rb)


def _merge_odd(o_f, o_b, main, ret_g, w_out, x, g1, n2, sh2, sc2, rw, rb):
    b, t, d = x.shape
    vw = RET_H * RET_DV
    tile = lambda width, col: pl.BlockSpec((None, TM, width), lambda b_, t_: (b_, t_, col))
    const = lambda a: pl.BlockSpec(a.shape, lambda b_, t_: (0,) * a.ndim)
    shapes, specs = _merge_out(b, t, d)
    return pl.pallas_call(
        _merge_odd_kernel,
        out_shape=shapes,
        grid=(b, t // TM),
        in_specs=[tile(vw, 0), tile(vw, 0), tile(vw, 2), const(ret_g), const(w_out), tile(d, 0),
                  _tile_mod_spec(d), const(n2), _tile_mod_spec(d), _tile_mod_spec(d), const(rw), const(rb)],
        out_specs=specs,
        scratch_shapes=[pltpu.VMEM((1, 128), F32)],
        compiler_params=_cparams(("arbitrary", "arbitrary")),
        name="merge_odd",
    )(o_f, o_b, main, ret_g, w_out, x, g1, n2, sh2, sc2, rw, rb)


def _proj_odd_kernel(x_ref, acc_ref, g2_ref, g_ref, sh_ref, sc_ref, w_ref, cos_ref, sin_ref,
                     x2_ref, main_ref):
    x2 = x_ref[...] + g2_ref[...] * _load_acc(acc_ref)
    x2_ref[...] = x2
    hb = _rms_mod(x2, g_ref[...], sh_ref[...], sc_ref[...]).astype(BF16)
    qk_w = RET_H * HD
    cos = cos_ref[...]
    sin = sin_ref[...]
    n_seg = w_ref.shape[1] // qk_w
    for seg in range(n_seg):
        sl = slice(seg * qk_w, (seg + 1) * qk_w)
        u = _dot(hb, w_ref[:, sl])
        if seg < 2:
            parts = []
            for hh in range(RET_H):
                uh = u[:, hh * HD:(hh + 1) * HD]
                parts.append(uh * cos + pltpu.roll(uh, HD // 2, 1) * sin)
            u = jnp.concatenate(parts, axis=1)
            if seg == 1:
                u = u * (HD ** -0.5)
        main_ref[:, sl] = u.astype(BF16)


def _rope_tables(t):
    half = HD // 2
    freqs = ROPE_BASE ** (-jnp.arange(half, dtype=F32) / half)
    ang = jnp.arange(t, dtype=F32)[:, None] * freqs[None, :]
    cos, sin = jnp.cos(ang), jnp.sin(ang)
    return jnp.concatenate([cos, cos], axis=1), jnp.concatenate([-sin, sin], axis=1)


def _proj_odd(x, acc, g2, g, sh, sc, w):
    b, t, d = x.shape
    n = w.shape[1]
    cos, sin = _rope_tables(t)
    const = lambda a: pl.BlockSpec(a.shape, lambda b_, t_: (0,) * a.ndim)
    tile = lambda width: pl.BlockSpec((None, TM, width), lambda b_, t_: (b_, t_, 0))
    rope = pl.BlockSpec((TM, HD), lambda b_, t_: (t_, 0))
    nt = t // TM
    acc_tile = _acc_tile_spec(acc, lambda b_, t_: b_ * nt + t_)
    return pl.pallas_call(
        _proj_odd_kernel,
        out_shape=(jax.ShapeDtypeStruct((b, t, d), F32), jax.ShapeDtypeStruct((b, t, n), BF16)),
        grid=(b, nt),
        in_specs=[tile(d), acc_tile, _tile_mod_spec(d), const(g), _tile_mod_spec(d), _tile_mod_spec(d),
                  const(w), rope, rope],
        out_specs=(tile(d), tile(n)),
        compiler_params=_cparams(("arbitrary", "arbitrary")),
        name="proj_odd",
    )(x, acc, g2, g, sh, sc, w, cos, sin)


ROWS_PER_STEP = 8


def _packed_row(r):
    return pl.ds(pl.multiple_of(r * PACK, PACK), PACK)


def _move_rows(n_blocks):
    return MOE_BLOCK * next(k for k in (4, 2, 1) if n_blocks % k == 0)


def _moe_gather_kernel(src_ref, h_ref, xs_ref):
    def body(r8, carry):
        base = r8 * ROWS_PER_STEP
        rows = [h_ref[_packed_row(src_ref[0, base + j]), :] for j in range(ROWS_PER_STEP)]
        for j in range(ROWS_PER_STEP):
            xs_ref[_packed_row(base + j), :] = rows[j]
        return carry

    lax.fori_loop(0, src_ref.shape[1] // ROWS_PER_STEP, body, 0)


def _moe_gather(row_src, h2p, n_blocks):
    step = _move_rows(n_blocks)
    n_steps = n_blocks * MOE_BLOCK // step
    return pl.pallas_call(
        _moe_gather_kernel,
        out_shape=jax.ShapeDtypeStruct((n_blocks * MOE_BLOCK * PACK, LANES), U32),
        grid=(n_steps,),
        in_specs=[pl.BlockSpec((None, 1, step), lambda i: (i, 0, 0), memory_space=pltpu.SMEM),
                  pl.BlockSpec(h2p.shape, lambda i: (0, 0))],
        out_specs=pl.BlockSpec((step * PACK, LANES), lambda i: (i, 0)),
        compiler_params=_cparams(("arbitrary",)),
        name="moe_gather",
    )(row_src.reshape(n_steps, 1, step), h2p)


def _moe_mm_kernel(be_ref, nu_ref, xs_ref, rw_ref, wgu_ref, bgu_ref, wdn_ref, bdn_ref, ys_ref, wgu_s, wdn_s):
    i = pl.program_id(0)
    e = be_ref[i]
    half = wgu_ref.shape[0] // 2
    hl = LANES // 2

    @pl.when(i >= nu_ref[0])
    def _():
        ys_ref[...] = jnp.zeros_like(ys_ref)

    @pl.when(i < nu_ref[0])
    def _():
        @pl.when((i == 0) | (e != be_ref[jnp.maximum(i - 1, 0)]))
        def _():
            wgu_s[...] = wgu_ref[...].astype(BF16)
            for p in range(wdn_ref.shape[0] // LANES):
                first = wdn_ref[p * LANES:p * LANES + hl, :].astype(BF16).astype(F32)
                second = wdn_ref[p * LANES + hl:(p + 1) * LANES, :].astype(BF16).astype(F32)
                word = (pltpu.bitcast(second, U32) & jnp.uint32(0xFFFF0000)) | \
                    lax.shift_right_logical(pltpu.bitcast(first, U32), jnp.uint32(16))
                wdn_s[p * LANES:(p + 1) * LANES, :] = pltpu.bitcast(word, BF16)

        xu = _load_row_packed(xs_ref, MOE_BLOCK)
        lo = pltpu.bitcast(lax.shift_left(xu, jnp.uint32(16)), F32).astype(BF16)
        hi = pltpu.bitcast(xu & jnp.uint32(0xFFFF0000), F32).astype(BF16)
        gu = _dot(lo, wgu_s[:half, :]) + _dot(hi, wgu_s[half:, :]) + bgu_ref[...]
        even = (lax.broadcasted_iota(jnp.int32, (gu.shape[0], LANES), 1) & 1) == 0
        acts = []
        for p in range(gu.shape[1] // (2 * LANES)):
            a = gu[:, 2 * p * LANES:(2 * p + 1) * LANES]
            b = gu[:, (2 * p + 1) * LANES:(2 * p + 2) * LANES]
            gate = jnp.minimum(jnp.where(even, a, pltpu.roll(b, 1, 1)), SWIGLU_LIMIT)
            up = jnp.clip(jnp.where(even, pltpu.roll(a, LANES - 1, 1), b), -SWIGLU_LIMIT, SWIGLU_LIMIT)
            acts.append(((up + 1.0) * gate * _sigmoid(SWIGLU_ALPHA * gate)).astype(BF16))
        y = _dot(jnp.concatenate(acts, axis=1), wdn_s[...]) + bdn_ref[...]
        y = y * jnp.broadcast_to(rw_ref[...], (8, rw_ref.shape[1])).T[:, 0:1]
        cw = PACK * LANES
        for h in range(y.shape[1] // cw):
            _store_row_packed(ys_ref, y[:, h * cw:(h + 1) * cw], lead=(h,))


def _moe_mm(block_e, n_used, xs, row_w, layer, w_gu, b_gu, w_dn, b_dn):
    n_blocks = xs.shape[0] // (MOE_BLOCK * PACK)
    depth, n_e, d, f2 = w_gu.shape
    n_half = d // (PACK * LANES)
    ew = lambda shape: pl.BlockSpec((None, None) + shape, lambda i, be, nu: (layer, be[i], 0, 0))
    grid_spec = pltpu.PrefetchScalarGridSpec(
        num_scalar_prefetch=2,
        grid=(n_blocks,),
        in_specs=[pl.BlockSpec((MOE_BLOCK * PACK, LANES), lambda i, be, nu: (i, 0)),
                  pl.BlockSpec((None, 1, MOE_BLOCK), lambda i, be, nu: (i, 0, 0)),
                  ew((d, f2)), ew((1, f2)), ew((f2 // 2, d)), ew((1, d))],
        out_specs=pl.BlockSpec((n_half, MOE_BLOCK * PACK, LANES), lambda i, be, nu: (0, i, 0)),
        scratch_shapes=[pltpu.VMEM((d, f2), BF16), pltpu.VMEM((f2 // 2, d), BF16)],
    )
    return pl.pallas_call(
        _moe_mm_kernel,
        out_shape=jax.ShapeDtypeStruct((n_half, xs.shape[0], LANES), F32),
        grid_spec=grid_spec,
        compiler_params=_cparams(("arbitrary",)),
        name="moe_mm",
    )(block_e, n_used, xs, row_w.reshape(n_blocks, 1, MOE_BLOCK), w_gu, b_gu.reshape(depth, n_e, 1, f2),
      w_dn, b_dn.reshape(depth, n_e, 1, d))


def _moe_combine_kernel(nu_ref, dst_ref, ys_ref, acc_ref):
    i = pl.program_id(1)

    @pl.when(i == 0)
    def _():
        acc_ref[...] = jnp.zeros_like(acc_ref)

    def body(r8, carry):
        base = r8 * ROWS_PER_STEP
        dst = [_packed_row(dst_ref[0, base + j]) for j in range(ROWS_PER_STEP)]
        group = ROWS_PER_STEP * PACK
        ys = ys_ref[pl.ds(pl.multiple_of(r8 * group, group), group), :]
        new = [acc_ref[dst[j], :] + ys[j * PACK:(j + 1) * PACK, :] for j in range(ROWS_PER_STEP)]
        for j in range(ROWS_PER_STEP):
            acc_ref[dst[j], :] = new[j]
        return carry

    step = dst_ref.shape[1]

    @pl.when(i * (step // MOE_BLOCK) < nu_ref[0])
    def _():
        lax.fori_loop(0, step // ROWS_PER_STEP, body, 0)


def _moe_combine(n_used, row_dst, ys, n_rows_out):
    n_half, packed_rows, _ = ys.shape
    step = _move_rows(packed_rows // (MOE_BLOCK * PACK))
    n_blocks = packed_rows // (step * PACK)
    idx_spec = pl.BlockSpec((None, 1, step), lambda j, i, nu: (i, 0, 0), memory_space=pltpu.SMEM)
    grid_spec = pltpu.PrefetchScalarGridSpec(
        num_scalar_prefetch=1,
        grid=(n_half, n_blocks),
        in_specs=[idx_spec, pl.BlockSpec((None, step * PACK, LANES), lambda j, i, nu: (j, i, 0))],
        out_specs=pl.BlockSpec((None, n_rows_out * PACK, LANES), lambda j, i, nu: (j, 0, 0),
                               pipeline_mode=pl.Buffered(1)),
    )
    return pl.pallas_call(
        _moe_combine_kernel,
        out_shape=jax.ShapeDtypeStruct((n_half, n_rows_out * PACK, LANES), F32),
        grid_spec=grid_spec,
        compiler_params=_cparams(("arbitrary", "arbitrary")),
        name="moe_combine",
    )(n_used, row_dst.reshape(n_blocks, 1, step), ys)


def _moe_inverse_kernel(pad_lo_ref, pad_hi_ref, dest_ref, inv_ref):
    def pad_range(k, carry):
        def init(r, c):
            inv_ref[r] = -1
            return c

        lax.fori_loop(pad_lo_ref[k], pad_hi_ref[k], init, 0)
        return carry

    lax.fori_loop(0, pad_lo_ref.shape[0], pad_range, 0)

    def body(a, carry):
        inv_ref[dest_ref[a]] = a
        return carry

    lax.fori_loop(0, dest_ref.shape[0], body, 0, unroll=8)


def _moe_inverse(pad_lo, pad_hi, dest, rows):
    smem = pl.BlockSpec(memory_space=pltpu.SMEM)
    return pl.pallas_call(
        _moe_inverse_kernel,
        out_shape=jax.ShapeDtypeStruct((rows,), jnp.int32),
        in_specs=[smem, smem, smem],
        out_specs=smem,
        name="moe_inverse",
    )(pad_lo, pad_hi, dest)


def _moe(h2p, codes, weights, counts, layer, w_gu, b_gu, w_dn, b_dn):
    n = h2p.shape[0] // PACK
    n_assign = n * TOP_K
    n_blocks = -(-n_assign // MOE_BLOCK) + N_EXPERTS
    rows = n_blocks * MOE_BLOCK
    padded = -(-counts // MOE_BLOCK) * MOE_BLOCK
    ends = jnp.cumsum(padded)
    offsets = ends - padded
    starts = jnp.arange(n_blocks, dtype=jnp.int32) * MOE_BLOCK
    block_e = jnp.minimum(jnp.sum(ends[None, :] <= starts[:, None], axis=1), N_EXPERTS - 1).astype(jnp.int32)
    n_used = (ends[-1:] // MOE_BLOCK).astype(jnp.int32)
    expert = lax.shift_right_logical(codes, CODE_SHIFT)
    which = expert[:, None] == jnp.arange(N_EXPERTS, dtype=jnp.int32)[None, :]
    dest = (codes & ((1 << CODE_SHIFT) - 1)) + jnp.sum(jnp.where(which, offsets[None, :], 0), axis=1)
    spare = 8
    dest = jnp.where(codes >= 0, dest, rows)
    pad_lo = jnp.concatenate([offsets + counts, ends[-1:]]).astype(jnp.int32)
    pad_hi = jnp.concatenate([ends, jnp.full((1,), rows + spare, ends.dtype)]).astype(jnp.int32)
    inv = _moe_inverse(pad_lo, pad_hi, dest.astype(jnp.int32), rows + spare)[:rows]
    tok = lax.shift_right_logical(inv, TOP_K.bit_length() - 1)
    row_src = jnp.where(inv >= 0, tok, 0)
    row_dst = jnp.where(inv >= 0, tok, n)
    row_w = jnp.where(inv >= 0, jnp.take(weights, jnp.maximum(inv, 0)), 0.0)
    xs = _moe_gather(row_src, h2p, n_blocks)
    ys = _moe_mm(block_e, n_used, xs, row_w, layer, w_gu, b_gu, w_dn, b_dn)
    return _moe_combine(n_used, row_dst, ys, n + 8)


def _final_kernel(x_ref, acc_ref, g2_ref, g_ref, o_ref):
    x = x_ref[...] + g2_ref[...] * _load_acc(acc_ref)
    ms = jnp.mean(x * x, axis=-1, keepdims=True)
    o_ref[...] = x * lax.rsqrt(ms + EPS) * g_ref[...]


def _final(x, acc, g2, g, n_ctx_tiles):
    b, t, d = x.shape
    nt = t // TM - n_ctx_tiles
    tile_in = pl.BlockSpec((None, TM, d), lambda b_, t_: (b_, t_ + n_ctx_tiles, 0))
    acc_tile = _acc_tile_spec(acc, lambda b_, t_: b_ * (t // TM) + t_ + n_ctx_tiles)
    return pl.pallas_call(
        _final_kernel,
        out_shape=jax.ShapeDtypeStruct((b, nt * TM, d), F32),
        grid=(b, nt),
        in_specs=[tile_in, acc_tile,
                  pl.BlockSpec((None, None, 1, d), lambda b_, t_: (b_, 1, 0, 0)),
                  pl.BlockSpec((1, d), lambda b_, t_: (0, 0))],
        out_specs=pl.BlockSpec((None, TM, d), lambda b_, t_: (b_, t_, 0)),
        compiler_params=_cparams(("arbitrary", "arbitrary")),
        name="final_norm",
    )(x, acc, g2, g)


def _mod_tables(mod, b, d):
    outs = []
    for j in range(6):
        m = mod[:, j * d:(j + 1) * d]
        lat = m[:b]
        ctx = jnp.broadcast_to(m[b:b + 1], (b, d))
        outs.append(jnp.stack([ctx, lat], axis=1)[:, :, None, :])
    return outs


def _routing(code):
    codes = code[:, :, :TOP_K].reshape(-1)
    weights = lax.bitcast_convert_type(code[:, :, TOP_K:2 * TOP_K], F32).reshape(-1)
    return codes, weights


def _router_params(router_w, router_b):
    d, e = router_w.shape
    rw = jnp.zeros((d, 128), F32).at[:, :e].set(router_w).astype(BF16)
    rb = jnp.full((1, 128), NEG, F32).at[0, :e].set(router_b)
    return rw, rb


def kernel(x, c, ctx, c_ctx, mod_w, mod_b, norm1_g, norm2_g, ev_w_in, ev_conv_w, gdn_a_log, gdn_dt_bias,
           gdn_norm_g, ml_i_bias, ml_f_bias, ml_norm_g, ev_w_out, od_w_in, ret_norm_g, od_w_out,
           router_w, router_b, moe_w_gu, moe_b_gu, moe_w_dn, moe_b_dn, final_g):
    b, s, d = x.shape
    n_ctx = ctx.shape[1]
    depth = mod_w.shape[0]
    assert n_ctx == TM and s % TM == 0 and depth == 2 and b % SCAN_BG == 0
    t = n_ctx + s
    n_tok = b * t

    cond = jnp.concatenate([c, c_ctx[None, :], jnp.zeros((8 - b - 1, d), F32)], axis=0)
    mod = _adaln(cond, mod_w, mod_b)
    xa = jnp.concatenate([ctx, x], axis=1)

    sh1, sc1, g1, sh2, sc2, g2 = _mod_tables(mod[0], b, d)
    qk_w = GDN_H * HD
    conv_ch = 3 * qk_w
    ng = N_DIR * GDN_H
    w_in = ev_w_in[0]
    o_z = conv_ch
    o_a = o_z + qk_w
    o_mq = o_a + 2 * ng
    o_i = o_mq + 4 * qk_w
    w_main = jnp.concatenate([w_in[:, :o_a], w_in[:, o_mq:o_i]], axis=1).astype(BF16)
    w_gate = jnp.concatenate([w_in[:, o_a:o_mq], w_in[:, o_i:o_i + 2 * ng],
                              jnp.zeros((d, 128 - 4 * ng), F32)], axis=1).astype(BF16)
    zeros_g = jnp.zeros((ng,), F32)
    rate = jnp.concatenate([jnp.exp(gdn_a_log[0].astype(F32)).reshape(-1), jnp.zeros((128 - ng,), F32)])[None, :]
    gbias = jnp.concatenate([gdn_dt_bias[0].reshape(-1), zeros_g, ml_i_bias[0].reshape(-1),
                             ml_f_bias[0].reshape(-1), jnp.zeros((128 - 4 * ng,), F32)])[None, :].astype(F32)
    main, gates = _proj_even(xa, norm1_g[0][None, :], sh1, sc1, w_main, w_gate, ev_conv_w[0], rate, gbias)
    gates_c = gates.reshape(b, t // CH, CH, 4 * ng)
    gates_r = jnp.swapaxes(gates_c, 2, 3)
    og_f, og_b, om_f, om_b = _even_scan(main, gates_c, gates_r)
    rw, rb = _router_params(router_w[0], router_b[0])
    gdn_g = jnp.tile(gdn_norm_g[0], GDN_H)[None, :]
    x1, h2p, code, cnt = _merge_even(og_f, og_b, om_f, om_b, main, gdn_g, ml_norm_g[0][None, :],
                                     ev_w_out[0].astype(BF16), xa, g1, norm2_g[0][None, :], sh2, sc2, rw, rb)
    acc = _moe(h2p.reshape(n_tok * PACK, LANES), *_routing(code), cnt[0, :N_EXPERTS].astype(jnp.int32),
               0, moe_w_gu, moe_b_gu, moe_w_dn, moe_b_dn)
    g2_prev = g2

    sh1, sc1, g1, sh2, sc2, g2 = _mod_tables(mod[1], b, d)
    x2, main_o = _proj_odd(x1, acc, g2_prev, norm1_g[1][None, :], sh1, sc1, od_w_in[0].astype(BF16))
    o_f, o_b = _ret_scan(main_o)
    rw, rb = _router_params(router_w[1], router_b[1])
    x3, h2p, code, cnt = _merge_odd(o_f, o_b, main_o, ret_norm_g[0][None, :], od_w_out[0].astype(BF16),
                                    x2, g1, norm2_g[1][None, :], sh2, sc2, rw, rb)
    acc = _moe(h2p.reshape(n_tok * PACK, LANES), *_routing(code), cnt[0, :N_EXPERTS].astype(jnp.int32),
               1, moe_w_gu, moe_b_gu, moe_w_dn, moe_b_dn)
    return _final(x3, acc, g2, final_g[None, :], n_ctx // TM)
```

```python
import functools
import math

import jax
import jax.numpy as jnp
import numpy as np
from jax import lax
from jax.experimental import pallas as pl
from jax.experimental.pallas import tpu as pltpu

F32 = jnp.float32
BF16 = jnp.bfloat16
U32 = jnp.uint32
HIGHEST = lax.Precision.HIGHEST

EPS = 1e-6
CH = 64
TM = 256
CPB = TM // CH
HD = 128
N_DIR = 2
GDN_H = 4
ML_H = 4
RET_H = 8
RET_DV = 256
CONV_W = 3
N_EXPERTS = 32
TOP_K = 4
SWIGLU_ALPHA = 1.702
SWIGLU_LIMIT = 7.0
MOE_BLOCK = 512
ROPE_BASE = 10000.0
NEG = -1e30
VMEM_LIMIT = 56 * 1024 * 1024


def _cparams(sem):
    return pltpu.CompilerParams(dimension_semantics=sem, vmem_limit_bytes=VMEM_LIMIT)


def _dot(a, b, precision=None):
    return jnp.dot(a, b, preferred_element_type=F32, precision=precision)


def _dot_nt(a, b):
    return lax.dot_general(a, b, (((1,), (1,)), ((), ())), preferred_element_type=F32)


def _dot_tn(a, b):
    return lax.dot_general(a, b, (((0,), (0,)), ((), ())), preferred_element_type=F32)


def _sigmoid(x):
    return 1.0 / (1.0 + jnp.exp(-x))


def _silu(x):
    return x * _sigmoid(x)


def _group_sum(x, w):
    outs = []
    for j in range(x.shape[1] // w):
        s = jnp.sum(x[:, j * w:(j + 1) * w], axis=-1, keepdims=True)
        outs.append(jnp.broadcast_to(s, (x.shape[0], w)))
    return outs[0] if len(outs) == 1 else jnp.concatenate(outs, axis=1)


def _rms_mod(x, g, sh, sc):
    ms = jnp.mean(x * x, axis=-1, keepdims=True)
    return (x * lax.rsqrt(ms + EPS) * g) * (1.0 + sc) + sh


LANES = 128
PACK = 4


def _store_row_packed(ref, x, lead=()):
    rows = x.shape[0]
    for g in range(x.shape[1] // LANES):
        ref[lead + (pl.ds(g, rows, stride=PACK), slice(None))] = x[:, g * LANES:(g + 1) * LANES]


def _load_row_packed(ref, rows, lead=()):
    return jnp.concatenate([ref[lead + (pl.ds(g, rows, stride=PACK), slice(None))] for g in range(PACK)], axis=1)


def _load_acc(acc_ref):
    rows = acc_ref.shape[1] // PACK
    return jnp.concatenate([_load_row_packed(acc_ref, rows, lead=(h,)) for h in range(acc_ref.shape[0])], axis=1)


def _acc_tile_spec(acc, tile_of):
    return pl.BlockSpec((acc.shape[0], TM * PACK, LANES), lambda b_, t_: (0, tile_of(b_, t_), 0))


def _adaln_kernel(c_ref, w_ref, b_ref, o_ref):
    c = c_ref[...]
    o_ref[...] = _dot(_silu(c), w_ref[...], precision=HIGHEST) + b_ref[...]


def _adaln(cond, mod_w, mod_b):
    depth, d, d6 = mod_w.shape
    n = d6 // d
    return pl.pallas_call(
        _adaln_kernel,
        out_shape=jax.ShapeDtypeStruct((depth, cond.shape[0], d6), F32),
        grid=(depth, n),
        in_specs=[pl.BlockSpec(cond.shape, lambda l, j: (0, 0)),
                  pl.BlockSpec((None, d, d), lambda l, j: (l, 0, j)),
                  pl.BlockSpec((None, 1, d), lambda l, j: (l, 0, j))],
        out_specs=pl.BlockSpec((None, cond.shape[0], d), lambda l, j: (l, 0, j)),
        compiler_params=_cparams(("arbitrary", "arbitrary")),
        name="adaln",
    )(cond, mod_w, mod_b.reshape(depth, 1, d6))


def _proj_even_kernel(x_ref, g_ref, sh_ref, sc_ref, w_ref, wg_ref, cw_ref, rate_ref, gb_ref,
                      main_ref, gates_ref):
    t = pl.program_id(1)
    h = _rms_mod(x_ref[...], g_ref[...], sh_ref[...], sc_ref[...])
    hb = h.astype(BF16)
    qk_w = GDN_H * HD

    row = lax.broadcasted_iota(jnp.int32, (TM, 1), 0)
    pos = jnp.where(t > 0, row & (CH - 1), row)
    last = jnp.where(t > 0, CH - 1, TM - 1)
    left_ok = pos != 0
    right_ok = pos != last
    for seg in range(3):
        sl = slice(seg * qk_w, (seg + 1) * qk_w)
        u = _dot(hb, w_ref[:, sl])
        um = jnp.where(left_ok, pltpu.roll(u, 1, 0), 0.0)
        up = jnp.where(right_ok, pltpu.roll(u, TM - 1, 0), 0.0)
        cv = _silu(um * cw_ref[0:1, sl] + u * cw_ref[1:2, sl] + up * cw_ref[2:3, sl])
        if seg < 2:
            ss = _group_sum(cv * cv, HD)
            cv = cv * lax.rsqrt(ss + EPS)
            if seg == 0:
                cv = cv * (HD ** -0.5)
        main_ref[:, sl] = cv.astype(BF16)
    for seg in range(3, 8):
        sl = slice(seg * qk_w, (seg + 1) * qk_w)
        u = _dot(hb, w_ref[:, sl])
        if seg == 5:
            u = u * (HD ** -0.5)
        main_ref[:, sl] = u.astype(BF16)

    z = _dot(hb, wg_ref[...]) + gb_ref[...]
    tl = jnp.log(1.0 + jnp.exp(-jnp.abs(z)))
    sp_pos = jnp.maximum(z, 0.0) + tl
    sp_neg = jnp.maximum(-z, 0.0) + tl
    lane = lax.broadcasted_iota(jnp.int32, z.shape, 1)
    ng = N_DIR * GDN_H
    res = jnp.where(lane < ng, -rate_ref[...] * sp_pos,
                    jnp.where(lane < 2 * ng, _sigmoid(z),
                              jnp.where(lane < 3 * ng, z, -sp_neg)))
    gates_ref[...] = res[:, :gates_ref.shape[-1]]


def _tile_mod_spec(d):
    return pl.BlockSpec((None, None, 1, d), lambda b, t: (b, jnp.minimum(t, 1), 0, 0))


def _proj_even(x, g, sh, sc, w_main, w_gate, conv_w, rate, gbias):
    b, t, d = x.shape
    n = w_main.shape[1]
    ngl = 4 * N_DIR * GDN_H
    const = lambda shape: pl.BlockSpec(shape, lambda b_, t_: (0,) * len(shape))
    return pl.pallas_call(
        _proj_even_kernel,
        out_shape=(jax.ShapeDtypeStruct((b, t, n), BF16), jax.ShapeDtypeStruct((b, t, ngl), F32)),
        grid=(b, t // TM),
        in_specs=[pl.BlockSpec((None, TM, d), lambda b_, t_: (b_, t_, 0)),
                  const((1, d)), _tile_mod_spec(d), _tile_mod_spec(d),
                  const(w_main.shape), const(w_gate.shape), const(conv_w.shape),
                  const(rate.shape), const(gbias.shape)],
        out_specs=(pl.BlockSpec((None, TM, n), lambda b_, t_: (b_, t_, 0)),
                   pl.BlockSpec((None, TM, ngl), lambda b_, t_: (b_, t_, 0))),
        compiler_params=_cparams(("arbitrary", "arbitrary")),
        name="proj_even",
    )(x, g, sh, sc, w_main, w_gate, conv_w, rate, gbias)


def _rev_tile(i, nt):
    return jnp.where(i == 0, 0, nt - i)


def _tri_masks():
    r = lax.broadcasted_iota(jnp.int32, (CH, CH), 0)
    c = lax.broadcasted_iota(jnp.int32, (CH, CH), 1)
    return r >= c, r > c, r <= c, r < c


SCAN_BG = 2


def _scan_streams(heads):
    groups = [(bi, d) for bi in range(SCAN_BG) for d in range(N_DIR)]
    streams = [(gi, hh) for gi in range(len(groups)) for hh in range(heads)]
    return groups, streams


def _cumsum_both(groups, gcs, grs, lower, upper):
    tri = (lower.astype(F32), upper.astype(F32))
    cs_c = [_dot(tri[d], gc, precision=HIGHEST) for (_, d), gc in zip(groups, gcs)]
    cs_r = [_dot(gr, tri[1 - d], precision=HIGHEST) for (_, d), gr in zip(groups, grs)]
    return cs_c, cs_r


def _store_heads(refs, groups, rows, outs, heads):
    for gi, (bi, d) in enumerate(groups):
        o_ref = refs[d][-1]
        tile = jnp.concatenate(outs[gi * heads:(gi + 1) * heads], axis=1)
        o_ref[bi, rows[d], :] = tile.astype(o_ref.dtype)


class _GdnChunk:
    def __init__(self, refs, s_ref):
        self.refs, self.s_ref = refs, s_ref
        self.groups, self.streams = _scan_streams(GDN_H)

    def prelude(self, masks, rows, gcs, cs_c, cs_r):
        lower, lstrict, upper, ustrict = masks
        refs, groups = self.refs, self.groups
        ng = N_DIR * GDN_H
        st = []
        for gi, hh in self.streams:
            bi, d = groups[gi]
            ci = d * GDN_H + hh
            cols = slice(hh * HD, (hh + 1) * HD)
            incl, strict = (lower, lstrict) if d == 0 else (upper, ustrict)
            g_col = cs_c[gi][:, ci:ci + 1]
            g_row = cs_r[gi][ci:ci + 1, :]
            beta = gcs[gi][:, ng + ci:ng + ci + 1]
            tot = g_col[CH - 1:CH, :] if d == 0 else g_col[0:1, :]
            decay = jnp.where(incl, jnp.exp(jnp.where(incl, g_col - g_row, 0.0)), 0.0)
            q = refs[d][0][bi, rows[d], cols].astype(F32)
            k = refs[d][1][bi, rows[d], cols].astype(F32)
            v = refs[d][2][bi, rows[d], cols].astype(F32)
            kbeta = k * beta
            eg = jnp.exp(g_col)
            st.append(dict(
                strict=strict, decay=decay, kb=k.astype(BF16), kbetab=kbeta.astype(BF16), qb=q.astype(BF16),
                x=jnp.concatenate([v * beta, kbeta * eg], axis=1),
                qe=(q * eg).astype(BF16), kdec=(k * jnp.exp(tot - g_col)).astype(BF16), cd=jnp.exp(tot)))
        self.st = st

    def matmuls(self, rows):
        st, s_ref, streams = self.st, self.s_ref, self.streams
        kk = [_dot_nt(s["kbetab"], s["kb"]) for s in st]
        qk = [(_dot_nt(s["qb"], s["kb"]) * s["decay"]).astype(BF16) for s in st]
        p = [-jnp.where(s["strict"], m * s["decay"], 0.0) for s, m in zip(st, kk)]
        x = [s["x"] for s in st]
        for j in range(6):
            pb = [m.astype(BF16) for m in p]
            x = [xx + _dot(m, xx.astype(BF16)) for m, xx in zip(pb, x)]
            if j < 5:
                p = [_dot(m, m) for m in pb]
        s_old = [s_ref[si] for si in range(len(streams))]
        sb = [s.astype(BF16) for s in s_old]
        vnb = [(xx[:, :HD] - _dot(xx[:, HD:].astype(BF16), s)).astype(BF16) for xx, s in zip(x, sb)]
        o = [_dot(s["qe"], sbi) + _dot(m, vn) for s, sbi, m, vn in zip(st, sb, qk, vnb)]
        s_new = [so * s["cd"] + _dot_tn(s["kdec"], vn) for s, so, vn in zip(st, s_old, vnb)]
        for si, s in enumerate(s_new):
            s_ref[si] = s
        _store_heads(self.refs, self.groups, rows, o, GDN_H)


def _scan_specs(nt, width, col):
    fwd = pl.BlockSpec((SCAN_BG, TM, width), lambda b, i: (b, i, col))
    bwd = pl.BlockSpec((SCAN_BG, TM, width), lambda b, i: (b, _rev_tile(i, nt), col))
    return fwd, bwd


def _gate_specs(nt, shape):
    fwd = pl.BlockSpec((SCAN_BG, CPB) + shape, lambda b, i: (b, i, 0, 0))
    bwd = pl.BlockSpec((SCAN_BG, CPB) + shape, lambda b, i: (b, _rev_tile(i, nt), 0, 0))
    return fwd, bwd


class _MlstmChunk:
    def __init__(self, refs, c_ref, m_ref):
        self.refs, self.c_ref, self.m_ref = refs, c_ref, m_ref
        self.groups, self.streams = _scan_streams(ML_H)

    def prelude(self, masks, rows, gcs, grs, cs_c, cs_r):
        lower, _, upper, _ = masks
        refs, groups, streams, m_ref = self.refs, self.groups, self.streams, self.m_ref
        ng = N_DIR * GDN_H
        i_off = 2 * ng
        f_off = 2 * ng + N_DIR * ML_H
        lane = lax.broadcasted_iota(jnp.int32, (CH, HD), 1)
        ones_col = jnp.where(lane == 0, 1.0, 0.0).astype(BF16)
        ns = len(streams)
        dirs = [groups[gi][1] for gi, _ in streams]
        chan = [groups[gi][1] * ML_H + hh for gi, hh in streams]
        b_col = [cs_c[gi][:, f_off + c:f_off + c + 1] for (gi, _), c in zip(streams, chan)]
        b_row = [cs_r[gi][f_off + c:f_off + c + 1, :] for (gi, _), c in zip(streams, chan)]
        i_col = [gcs[gi][:, i_off + c:i_off + c + 1] for (gi, _), c in zip(streams, chan)]
        i_row = [grs[gi][i_off + c:i_off + c + 1, :] for (gi, _), c in zip(streams, chan)]
        b_tot = [bc[CH - 1:CH, :] if d == 0 else bc[0:1, :] for bc, d in zip(b_col, dirs)]
        m_rows = [m_ref[si] for si in range(ns)]
        m_old = [mr[:, 0:1] for mr in m_rows]
        d_in = [jnp.where(lower if d == 0 else upper, bc - br + ir, NEG)
                for d, bc, br, ir in zip(dirs, b_col, b_row, i_row)]
        d_end = [bt - br + ir for bt, br, ir in zip(b_tot, b_row, i_row)]
        mx_in = [jnp.max(a, axis=-1, keepdims=True) for a in d_in]
        mx_end = [jnp.max(a, axis=-1, keepdims=True) for a in d_end]
        d_carry = [bc + m for bc, m in zip(b_col, m_old)]
        m_t = [jnp.maximum(a, b_) for a, b_ in zip(d_carry, mx_in)]
        carry_end = [bt + m for bt, m in zip(b_tot, m_old)]
        m_new = [jnp.maximum(a, b_) for a, b_ in zip(carry_end, mx_end)]
        p_in = [jnp.exp(a - b_) for a, b_ in zip(d_in, m_t)]
        w_end = [jnp.exp(bt - bc + ic - mn) for bt, bc, ic, mn in zip(b_tot, b_col, i_col, m_new)]
        st = []
        for si, (gi, hh) in enumerate(streams):
            bi, d = groups[gi]
            cols = slice(hh * HD, (hh + 1) * HD)
            k = refs[d][1][bi, rows[d], cols]
            v = refs[d][2][bi, rows[d], cols]
            st.append(dict(
                q=refs[d][0][bi, rows[d], cols], k=k,
                v_aug=jnp.concatenate([v, ones_col], axis=1),
                p_in=p_in[si], w_carry=jnp.exp(d_carry[si] - m_t[si]), floor=jnp.exp(-m_t[si]),
                kw=(k.astype(F32) * w_end[si]).astype(BF16),
                f_end=jnp.exp(carry_end[si] - m_new[si]),
                m_new=jnp.broadcast_to(m_new[si], m_rows[si].shape)))
        self.st = st

    def matmuls(self, rows):
        st, c_ref, m_ref, streams = self.st, self.c_ref, self.m_ref, self.streams
        sc = [(_dot_nt(s["q"], s["k"]) * s["p_in"]).astype(BF16) for s in st]
        c_old = [c_ref[si] for si in range(len(streams))]
        qc = [_dot(s["q"], c.astype(BF16)) for s, c in zip(st, c_old)]
        nd = [s["w_carry"] * a + _dot(m, s["v_aug"]) for s, a, m in zip(st, qc, sc)]
        hout = [a[:, :HD] / jnp.maximum(jnp.abs(a[:, HD:HD + 1]), s["floor"]) for s, a in zip(st, nd)]
        c_new = [s["f_end"] * c + _dot_tn(s["kw"], s["v_aug"]) for s, c in zip(st, c_old)]
        for si, (s, c) in enumerate(zip(st, c_new)):
            c_ref[si] = c
            m_ref[si] = s["m_new"]
        _store_heads(self.refs, self.groups, rows, hout, ML_H)


def _even_scan_kernel(gqf, gkf, gvf, gqb, gkb, gvb, mqf, mkf, mvf, mqb, mkb, mvb, gcf, grf, gcb, grb,
                      ogf, ogb, omf, omb, s_ref, c_ref, m_ref):
    i = pl.program_id(1)

    @pl.when(i == 0)
    def _():
        s_ref[...] = jnp.zeros_like(s_ref)
        c_ref[...] = jnp.zeros_like(c_ref)
        m_ref[...] = jnp.zeros_like(m_ref)

    masks = _tri_masks()
    gdn = _GdnChunk(((gqf, gkf, gvf, ogf), (gqb, gkb, gvb, ogb)), s_ref)
    mls = _MlstmChunk(((mqf, mkf, mvf, omf), (mqb, mkb, mvb, omb)), c_ref, m_ref)
    gate_refs = ((gcf, grf), (gcb, grb))

    def chunk_body(cc, carry):
        cidx = (cc, CPB - 1 - cc)
        rows = tuple(pl.ds(pl.multiple_of(c * CH, CH), CH) for c in cidx)
        gcs = [gate_refs[d][0][bi, cidx[d]] for bi, d in gdn.groups]
        grs = [gate_refs[d][1][bi, cidx[d]] for bi, d in gdn.groups]
        cs_c, cs_r = _cumsum_both(gdn.groups, gcs, grs, masks[0], masks[2])
        gdn.prelude(masks, rows, gcs, cs_c, cs_r)
        mls.prelude(masks, rows, gcs, grs, cs_c, cs_r)
        gdn.matmuls(rows)
        mls.matmuls(rows)
        return carry

    lax.fori_loop(0, CPB, chunk_body, 0)


def _even_scan(main, gates_c, gates_r):
    b, t, _ = main.shape
    nt = t // TM
    w = GDN_H * HD
    ngl = gates_c.shape[-1]
    qkv = lambda cols: [_scan_specs(nt, w, c)[d] for d in range(N_DIR) for c in cols]
    gcf, gcb = _gate_specs(nt, (CH, ngl))
    grf, grb = _gate_specs(nt, (ngl, CH))
    of, ob = _scan_specs(nt, w, 0)
    out = jax.ShapeDtypeStruct((b, t, w), BF16)
    n_streams = SCAN_BG * N_DIR * GDN_H
    return pl.pallas_call(
        _even_scan_kernel,
        out_shape=(out, out, out, out),
        grid=(b // SCAN_BG, nt),
        in_specs=qkv((0, 1, 2)) + qkv((4, 5, 6)) + [gcf, grf, gcb, grb],
        out_specs=(of, ob, of, ob),
        scratch_shapes=[pltpu.VMEM((n_streams, HD, HD), F32), pltpu.VMEM((n_streams, HD, 2 * HD), F32),
                        pltpu.VMEM((n_streams, 1, HD), F32)],
        compiler_params=_cparams(("arbitrary", "arbitrary")),
        name="even_scan",
    )(*([main] * 12), gates_c, gates_r, gates_c, gates_r)


def _ret_kernel(qf, kf, vf, qb, kb, vb, intra_ref, cross_ref, tail_ref, cd_ref, of, ob, s_ref):
    i = pl.program_id(1)

    @pl.when(i == 0)
    def _():
        s_ref[...] = jnp.zeros_like(s_ref)

    refs = ((qf, kf, vf, of), (qb, kb, vb, ob))
    groups, streams = _scan_streams(RET_H)

    def chunk_body(cc, carry):
        cidx = (cc, CPB - 1 - cc)
        rows = tuple(pl.ds(pl.multiple_of(c * CH, CH), CH) for c in cidx)
        st = []
        for gi, hh in streams:
            bi, d = groups[gi]
            ti = d * RET_H + hh
            kcols = slice(hh * HD, (hh + 1) * HD)
            q = refs[d][0][bi, rows[d], kcols]
            k = refs[d][1][bi, rows[d], kcols]
            st.append(dict(
                ti=ti, q=q, k=k, v=refs[d][2][bi, rows[d], slice(hh * RET_DV, (hh + 1) * RET_DV)],
                qc=(q.astype(F32) * cross_ref[ti]).astype(BF16),
                kt=(k.astype(F32) * tail_ref[ti]).astype(BF16)))
        sc = [(_dot_nt(s["q"], s["k"]) * intra_ref[s["ti"]]).astype(BF16) for s in st]
        s_old = [s_ref[si] for si in range(len(streams))]
        o = [_dot(m, s["v"]) + _dot(s["qc"], so.astype(BF16)) for s, m, so in zip(st, sc, s_old)]
        s_new = [cd_ref[s["ti"]] * so + _dot_tn(s["kt"], s["v"]) for s, so in zip(st, s_old)]
        for si, s in enumerate(s_new):
            s_ref[si] = s
        _store_heads(refs, groups, rows, o, RET_H)
        return carry

    lax.fori_loop(0, CPB, chunk_body, 0)


def _ret_tables():
    pos = np.arange(CH, dtype=np.float64)
    intra, cross, tail, cd = [], [], [], []
    for d in range(N_DIR):
        expo = 5.0 + np.arange(RET_H, dtype=np.float64)
        if d == 1:
            expo = expo[::-1]
        lg = np.log1p(-np.exp2(-expo))
        p = pos if d == 0 else (CH - 1.0 - pos)
        diff = p[:, None] - p[None, :]
        for hh in range(RET_H):
            intra.append(np.where(diff >= 0, np.exp(np.where(diff >= 0, diff, 0.0) * lg[hh]), 0.0))
            cross.append(np.broadcast_to(np.exp((p + 1.0) * lg[hh])[:, None], (CH, HD)))
            tail.append(np.broadcast_to(np.exp((CH - 1.0 - p) * lg[hh])[:, None], (CH, HD)))
            cd.append(np.full((1, RET_DV), np.exp(CH * lg[hh])))
    f = lambda a: jnp.asarray(np.stack(a), F32)
    return f(intra), f(cross), f(tail), f(cd)


def _ret_scan(main):
    b, t, _ = main.shape
    nt = t // TM
    qw = RET_H * HD
    vw = RET_H * RET_DV
    qf, qb = _scan_specs(nt, qw, 0)
    kf, kb = _scan_specs(nt, qw, 1)
    vf, vb = _scan_specs(nt, vw, 1)
    of, ob = _scan_specs(nt, vw, 0)
    tabs = _ret_tables()
    const = lambda a: pl.BlockSpec(a.shape, lambda b_, i_: (0,) * a.ndim)
    return pl.pallas_call(
        _ret_kernel,
        out_shape=(jax.ShapeDtypeStruct((b, t, vw), BF16), jax.ShapeDtypeStruct((b, t, vw), BF16)),
        grid=(b // SCAN_BG, nt),
        in_specs=[qf, kf, vf, qb, kb, vb] + [const(a) for a in tabs],
        out_specs=(of, ob),
        scratch_shapes=[pltpu.VMEM((SCAN_BG * N_DIR * RET_H, HD, RET_DV), F32)],
        compiler_params=_cparams(("arbitrary", "arbitrary")),
        name="ret_scan",
    )(main, main, main, main, main, main, *tabs)


CODE_SHIFT = 17


def _route_and_pack(h2, rw_ref, rb_ref, h2p_ref, code_ref, cnt_ref, base_ref, skip_context):
    h2b = h2.astype(BF16)
    logits = _dot(h2b, rw_ref[...]) + rb_ref[...]
    lane = lax.broadcasted_iota(jnp.int32, logits.shape, 1)
    lane_f = lane.astype(F32)
    vals, idxs = [], []
    cur = logits
    for _ in range(TOP_K):
        m = jnp.max(cur, axis=-1, keepdims=True)
        ix = jnp.min(jnp.where(cur == m, lane_f, float(logits.shape[1])), axis=-1, keepdims=True)
        vals.append(m)
        idxs.append(ix)
        cur = jnp.where(lane_f == ix, NEG, cur)
    es = [jnp.exp(v - vals[0]) for v in vals]
    tot = es[0] + es[1] + es[2] + es[3]

    @pl.when((pl.program_id(0) == 0) & (pl.program_id(1) == 0))
    def _():
        base_ref[...] = jnp.zeros_like(base_ref)

    tm = logits.shape[0]
    routed = pl.program_id(1) > 0
    onehot = jnp.zeros(logits.shape, F32)
    for j in range(TOP_K):
        onehot = jnp.where(lane_f == idxs[j], 1.0, onehot)
    if skip_context:
        onehot = onehot * jnp.where(routed, 1.0, 0.0)
    r_i = lax.broadcasted_iota(jnp.int32, (tm, tm), 0)
    c_i = lax.broadcasted_iota(jnp.int32, (tm, tm), 1)
    before = _dot(jnp.where(r_i > c_i, 1.0, 0.0).astype(BF16), onehot.astype(BF16))
    base = base_ref[...]
    pos = before + base
    total = base + before[tm - 1:tm, :] + onehot[tm - 1:tm, :]
    base_ref[...] = total
    cnt_ref[...] = total

    code = jnp.zeros(logits.shape, F32)
    wgt = jnp.zeros(logits.shape, F32)
    for j in range(TOP_K):
        rank = jnp.sum(jnp.where(lane_f == idxs[j], pos, 0.0), axis=-1, keepdims=True)
        code = jnp.where(lane == j, idxs[j] * float(1 << CODE_SHIFT) + rank, code)
        wgt = jnp.where(lane == TOP_K + j, es[j] / tot, wgt)
    code_i = code.astype(jnp.int32)
    if skip_context:
        code_i = jnp.where(routed, code_i, -1)
    code_ref[...] = jnp.where(lane < TOP_K, code_i, pltpu.bitcast(wgt, jnp.int32))
    half = h2.shape[1] // 2
    r = h2b.astype(F32)
    lo = lax.shift_right_logical(pltpu.bitcast(r[:, :half], U32), jnp.uint32(16))
    hi = pltpu.bitcast(r[:, half:], U32) & jnp.uint32(0xFFFF0000)
    _store_row_packed(h2p_ref, hi | lo)


def _merge_even_kernel(ogf, ogb, omf, omb, z_ref, mo_ref, gg_ref, mg_ref, wo_ref, x_ref, g1_ref,
                       n2_ref, sh_ref, sc_ref, rw_ref, rb_ref, x1_ref, h2p_ref, code_ref, cnt_ref, base_ref):
    og = ogf[...].astype(F32) + ogb[...].astype(F32)
    ms = _group_sum(og * og, HD) * (1.0 / HD)
    a = og * lax.rsqrt(ms + EPS) * gg_ref[...] * _silu(z_ref[...].astype(F32))
    om = omf[...].astype(F32) + omb[...].astype(F32)
    ms = _group_sum(om * om, HD) * (1.0 / HD)
    m = om * lax.rsqrt(ms + EPS) * mg_ref[...] * _sigmoid(mo_ref[...].astype(F32))
    cat = jnp.concatenate([a, m], axis=1).astype(BF16)
    y = _dot(cat, wo_ref[...])
    x1 = x_ref[...] + g1_ref[...] * y
    x1_ref[...] = x1
    h2 = _rms_mod(x1, n2_ref[...], sh_ref[...], sc_ref[...])
    _route_and_pack(h2, rw_ref, rb_ref, h2p_ref, code_ref, cnt_ref, base_ref, skip_context=False)


def _merge_odd_kernel(of, ob, gate_ref, ng_ref, wo_ref, x_ref, g1_ref,
                      n2_ref, sh_ref, sc_ref, rw_ref, rb_ref, x1_ref, h2p_ref, code_ref, cnt_ref, base_ref):
    o = of[...].astype(F32) + ob[...].astype(F32)
    o = o - _group_sum(o, RET_DV) * (1.0 / RET_DV)
    ms = _group_sum(o * o, RET_DV) * (1.0 / RET_DV)
    y = o * lax.rsqrt(ms + EPS) * ng_ref[...] * _silu(gate_ref[...].astype(F32))
    y = _dot(y.astype(BF16), wo_ref[...])
    x1 = x_ref[...] + g1_ref[...] * y
    x1_ref[...] = x1
    h2 = _rms_mod(x1, n2_ref[...], sh_ref[...], sc_ref[...])
    _route_and_pack(h2, rw_ref, rb_ref, h2p_ref, code_ref, cnt_ref, base_ref, skip_context=True)


def _merge_out(b, t, d):
    assert d // 2 == PACK * LANES
    shapes = (jax.ShapeDtypeStruct((b, t, d), F32), jax.ShapeDtypeStruct((b, t * PACK, LANES), U32),
              jax.ShapeDtypeStruct((b, t, LANES), jnp.int32), jax.ShapeDtypeStruct((1, LANES), F32))
    tile = lambda rows, width: pl.BlockSpec((None, rows, width), lambda b_, t_: (b_, t_, 0))
    specs = (tile(TM, d), tile(TM * PACK, LANES), tile(TM, LANES), pl.BlockSpec((1, LANES), lambda b_, t_: (0, 0)))
    return shapes, specs


def _merge_even(og_f, og_b, om_f, om_b, main, gdn_g, ml_g, w_out, x, g1, n2, sh2, sc2, rw, rb):
    b, t, d = x.shape
    w = GDN_H * HD
    tile = lambda width, col: pl.BlockSpec((None, TM, width), lambda b_, t_: (b_, t_, col))
    const = lambda a: pl.BlockSpec(a.shape, lambda b_, t_: (0,) * a.ndim)
    shapes, specs = _merge_out(b, t, d)
    return pl.pallas_call(
        _merge_even_kernel,
        out_shape=shapes,
        grid=(b, t // TM),
        in_specs=[tile(w, 0), tile(w, 0), tile(w, 0), tile(w, 0), tile(w, 3), tile(w, 7),
                  const(gdn_g), const(ml_g), const(w_out), tile(d, 0), _tile_mod_spec(d),
                  const(n2), _tile_mod_spec(d), _tile_mod_spec(d), const(rw), const(rb)],
        out_specs=specs,
        scratch_shapes=[pltpu.VMEM((1, 128), F32)],
        compiler_params=_cparams(("arbitrary", "arbitrary")),
        name="merge_even",
    )(og_f, og_b, om_f, om_b, main, main, gdn_g, ml_g, w_out, x, g1, n2, sh2, sc2, rw, rb)


def _merge_odd(o_f, o_b, main, ret_g, w_out, x, g1, n2, sh2, sc2, rw, rb):
    b, t, d = x.shape
    vw = RET_H * RET_DV
    tile = lambda width, col: pl.BlockSpec((None, TM, width), lambda b_, t_: (b_, t_, col))
    const = lambda a: pl.BlockSpec(a.shape, lambda b_, t_: (0,) * a.ndim)
    shapes, specs = _merge_out(b, t, d)
    return pl.pallas_call(
        _merge_odd_kernel,
        out_shape=shapes,
        grid=(b, t // TM),
        in_specs=[tile(vw, 0), tile(vw, 0), tile(vw, 2), const(ret_g), const(w_out), tile(d, 0),
                  _tile_mod_spec(d), const(n2), _tile_mod_spec(d), _tile_mod_spec(d), const(rw), const(rb)],
        out_specs=specs,
        scratch_shapes=[pltpu.VMEM((1, 128), F32)],
        compiler_params=_cparams(("arbitrary", "arbitrary")),
        name="merge_odd",
    )(o_f, o_b, main, ret_g, w_out, x, g1, n2, sh2, sc2, rw, rb)


def _proj_odd_kernel(x_ref, acc_ref, g2_ref, g_ref, sh_ref, sc_ref, w_ref, cos_ref, sin_ref,
                     x2_ref, main_ref):
    x2 = x_ref[...] + g2_ref[...] * _load_acc(acc_ref)
    x2_ref[...] = x2
    hb = _rms_mod(x2, g_ref[...], sh_ref[...], sc_ref[...]).astype(BF16)
    qk_w = RET_H * HD
    cos = cos_ref[...]
    sin = sin_ref[...]
    n_seg = w_ref.shape[1] // qk_w
    for seg in range(n_seg):
        sl = slice(seg * qk_w, (seg + 1) * qk_w)
        u = _dot(hb, w_ref[:, sl])
        if seg < 2:
            parts = []
            for hh in range(RET_H):
                uh = u[:, hh * HD:(hh + 1) * HD]
                parts.append(uh * cos + pltpu.roll(uh, HD // 2, 1) * sin)
            u = jnp.concatenate(parts, axis=1)
            if seg == 1:
                u = u * (HD ** -0.5)
        main_ref[:, sl] = u.astype(BF16)


def _rope_tables(t):
    half = HD // 2
    freqs = ROPE_BASE ** (-jnp.arange(half, dtype=F32) / half)
    ang = jnp.arange(t, dtype=F32)[:, None] * freqs[None, :]
    cos, sin = jnp.cos(ang), jnp.sin(ang)
    return jnp.concatenate([cos, cos], axis=1), jnp.concatenate([-sin, sin], axis=1)


def _proj_odd(x, acc, g2, g, sh, sc, w):
    b, t, d = x.shape
    n = w.shape[1]
    cos, sin = _rope_tables(t)
    const = lambda a: pl.BlockSpec(a.shape, lambda b_, t_: (0,) * a.ndim)
    tile = lambda width: pl.BlockSpec((None, TM, width), lambda b_, t_: (b_, t_, 0))
    rope = pl.BlockSpec((TM, HD), lambda b_, t_: (t_, 0))
    nt = t // TM
    acc_tile = _acc_tile_spec(acc, lambda b_, t_: b_ * nt + t_)
    return pl.pallas_call(
        _proj_odd_kernel,
        out_shape=(jax.ShapeDtypeStruct((b, t, d), F32), jax.ShapeDtypeStruct((b, t, n), BF16)),
        grid=(b, nt),
        in_specs=[tile(d), acc_tile, _tile_mod_spec(d), const(g), _tile_mod_spec(d), _tile_mod_spec(d),
                  const(w), rope, rope],
        out_specs=(tile(d), tile(n)),
        compiler_params=_cparams(("arbitrary", "arbitrary")),
        name="proj_odd",
    )(x, acc, g2, g, sh, sc, w, cos, sin)


ROWS_PER_STEP = 8


def _packed_row(r):
    return pl.ds(pl.multiple_of(r * PACK, PACK), PACK)


def _move_rows(n_blocks):
    return MOE_BLOCK * next(k for k in (4, 2, 1) if n_blocks % k == 0)


def _moe_gather_kernel(src_ref, h_ref, xs_ref):
    def body(r8, carry):
        base = r8 * ROWS_PER_STEP
        rows = [h_ref[_packed_row(src_ref[0, base + j]), :] for j in range(ROWS_PER_STEP)]
        for j in range(ROWS_PER_STEP):
            xs_ref[_packed_row(base + j), :] = rows[j]
        return carry

    lax.fori_loop(0, src_ref.shape[1] // ROWS_PER_STEP, body, 0)


def _moe_gather(row_src, h2p, n_blocks):
    step = _move_rows(n_blocks)
    n_steps = n_blocks * MOE_BLOCK // step
    return pl.pallas_call(
        _moe_gather_kernel,
        out_shape=jax.ShapeDtypeStruct((n_blocks * MOE_BLOCK * PACK, LANES), U32),
        grid=(n_steps,),
        in_specs=[pl.BlockSpec((None, 1, step), lambda i: (i, 0, 0), memory_space=pltpu.SMEM),
                  pl.BlockSpec(h2p.shape, lambda i: (0, 0))],
        out_specs=pl.BlockSpec((step * PACK, LANES), lambda i: (i, 0)),
        compiler_params=_cparams(("arbitrary",)),
        name="moe_gather",
    )(row_src.reshape(n_steps, 1, step), h2p)


def _moe_mm_kernel(be_ref, nu_ref, xs_ref, rw_ref, wgu_ref, bgu_ref, wdn_ref, bdn_ref, ys_ref, wgu_s, wdn_s):
    i = pl.program_id(0)
    e = be_ref[i]
    half = wgu_ref.shape[0] // 2
    hl = LANES // 2

    @pl.when(i >= nu_ref[0])
    def _():
        ys_ref[...] = jnp.zeros_like(ys_ref)

    @pl.when(i < nu_ref[0])
    def _():
        @pl.when((i == 0) | (e != be_ref[jnp.maximum(i - 1, 0)]))
        def _():
            wgu_s[...] = wgu_ref[...].astype(BF16)
            for p in range(wdn_ref.shape[0] // LANES):
                first = wdn_ref[p * LANES:p * LANES + hl, :].astype(BF16).astype(F32)
                second = wdn_ref[p * LANES + hl:(p + 1) * LANES, :].astype(BF16).astype(F32)
                word = (pltpu.bitcast(second, U32) & jnp.uint32(0xFFFF0000)) | \
                    lax.shift_right_logical(pltpu.bitcast(first, U32), jnp.uint32(16))
                wdn_s[p * LANES:(p + 1) * LANES, :] = pltpu.bitcast(word, BF16)

        xu = _load_row_packed(xs_ref, MOE_BLOCK)
        lo = pltpu.bitcast(lax.shift_left(xu, jnp.uint32(16)), F32).astype(BF16)
        hi = pltpu.bitcast(xu & jnp.uint32(0xFFFF0000), F32).astype(BF16)
        gu = _dot(lo, wgu_s[:half, :]) + _dot(hi, wgu_s[half:, :]) + bgu_ref[...]
        even = (lax.broadcasted_iota(jnp.int32, (gu.shape[0], LANES), 1) & 1) == 0
        acts = []
        for p in range(gu.shape[1] // (2 * LANES)):
            a = gu[:, 2 * p * LANES:(2 * p + 1) * LANES]
            b = gu[:, (2 * p + 1) * LANES:(2 * p + 2) * LANES]
            gate = jnp.minimum(jnp.where(even, a, pltpu.roll(b, 1, 1)), SWIGLU_LIMIT)
            up = jnp.clip(jnp.where(even, pltpu.roll(a, LANES - 1, 1), b), -SWIGLU_LIMIT, SWIGLU_LIMIT)
            acts.append(((up + 1.0) * gate * _sigmoid(SWIGLU_ALPHA * gate)).astype(BF16))
        y = _dot(jnp.concatenate(acts, axis=1), wdn_s[...]) + bdn_ref[...]
        y = y * jnp.broadcast_to(rw_ref[...], (8, rw_ref.shape[1])).T[:, 0:1]
        cw = PACK * LANES
        for h in range(y.shape[1] // cw):
            _store_row_packed(ys_ref, y[:, h * cw:(h + 1) * cw], lead=(h,))


def _moe_mm(block_e, n_used, xs, row_w, layer, w_gu, b_gu, w_dn, b_dn):
    n_blocks = xs.shape[0] // (MOE_BLOCK * PACK)
    depth, n_e, d, f2 = w_gu.shape
    n_half = d // (PACK * LANES)
    ew = lambda shape: pl.BlockSpec((None, None) + shape, lambda i, be, nu: (layer, be[i], 0, 0))
    grid_spec = pltpu.PrefetchScalarGridSpec(
        num_scalar_prefetch=2,
        grid=(n_blocks,),
        in_specs=[pl.BlockSpec((MOE_BLOCK * PACK, LANES), lambda i, be, nu: (i, 0)),
                  pl.BlockSpec((None, 1, MOE_BLOCK), lambda i, be, nu: (i, 0, 0)),
                  ew((d, f2)), ew((1, f2)), ew((f2 // 2, d)), ew((1, d))],
        out_specs=pl.BlockSpec((n_half, MOE_BLOCK * PACK, LANES), lambda i, be, nu: (0, i, 0)),
        scratch_shapes=[pltpu.VMEM((d, f2), BF16), pltpu.VMEM((f2 // 2, d), BF16)],
    )
    return pl.pallas_call(
        _moe_mm_kernel,
        out_shape=jax.ShapeDtypeStruct((n_half, xs.shape[0], LANES), F32),
        grid_spec=grid_spec,
        compiler_params=_cparams(("arbitrary",)),
        name="moe_mm",
    )(block_e, n_used, xs, row_w.reshape(n_blocks, 1, MOE_BLOCK), w_gu, b_gu.reshape(depth, n_e, 1, f2),
      w_dn, b_dn.reshape(depth, n_e, 1, d))


def _moe_combine_kernel(nu_ref, dst_ref, ys_ref, acc_ref):
    i = pl.program_id(1)

    @pl.when(i == 0)
    def _():
        acc_ref[...] = jnp.zeros_like(acc_ref)

    def body(r8, carry):
        base = r8 * ROWS_PER_STEP
        dst = [_packed_row(dst_ref[0, base + j]) for j in range(ROWS_PER_STEP)]
        group = ROWS_PER_STEP * PACK
        ys = ys_ref[pl.ds(pl.multiple_of(r8 * group, group), group), :]
        new = [acc_ref[dst[j], :] + ys[j * PACK:(j + 1) * PACK, :] for j in range(ROWS_PER_STEP)]
        for j in range(ROWS_PER_STEP):
            acc_ref[dst[j], :] = new[j]
        return carry

    step = dst_ref.shape[1]

    @pl.when(i * (step // MOE_BLOCK) < nu_ref[0])
    def _():
        lax.fori_loop(0, step // ROWS_PER_STEP, body, 0)


def _moe_combine(n_used, row_dst, ys, n_rows_out):
    n_half, packed_rows, _ = ys.shape
    step = _move_rows(packed_rows // (MOE_BLOCK * PACK))
    n_blocks = packed_rows // (step * PACK)
    idx_spec = pl.BlockSpec((None, 1, step), lambda j, i, nu: (i, 0, 0), memory_space=pltpu.SMEM)
    grid_spec = pltpu.PrefetchScalarGridSpec(
        num_scalar_prefetch=1,
        grid=(n_half, n_blocks),
        in_specs=[idx_spec, pl.BlockSpec((None, step * PACK, LANES), lambda j, i, nu: (j, i, 0))],
        out_specs=pl.BlockSpec((None, n_rows_out * PACK, LANES), lambda j, i, nu: (j, 0, 0),
                               pipeline_mode=pl.Buffered(1)),
    )
    return pl.pallas_call(
        _moe_combine_kernel,
        out_shape=jax.ShapeDtypeStruct((n_half, n_rows_out * PACK, LANES), F32),
        grid_spec=grid_spec,
        compiler_params=_cparams(("arbitrary", "arbitrary")),
        name="moe_combine",
    )(n_used, row_dst.reshape(n_blocks, 1, step), ys)


def _moe_inverse_kernel(pad_lo_ref, pad_hi_ref, dest_ref, inv_ref):
    def pad_range(k, carry):
        def init(r, c):
            inv_ref[r] = -1
            return c

        lax.fori_loop(pad_lo_ref[k], pad_hi_ref[k], init, 0)
        return carry

    lax.fori_loop(0, pad_lo_ref.shape[0], pad_range, 0)

    def body(a, carry):
        inv_ref[dest_ref[a]] = a
        return carry

    lax.fori_loop(0, dest_ref.shape[0], body, 0, unroll=8)


def _moe_inverse(pad_lo, pad_hi, dest, rows):
    smem = pl.BlockSpec(memory_space=pltpu.SMEM)
    return pl.pallas_call(
        _moe_inverse_kernel,
        out_shape=jax.ShapeDtypeStruct((rows,), jnp.int32),
        in_specs=[smem, smem, smem],
        out_specs=smem,
        name="moe_inverse",
    )(pad_lo, pad_hi, dest)


def _moe(h2p, codes, weights, counts, layer, w_gu, b_gu, w_dn, b_dn):
    n = h2p.shape[0] // PACK
    n_assign = n * TOP_K
    n_blocks = -(-n_assign // MOE_BLOCK) + N_EXPERTS
    rows = n_blocks * MOE_BLOCK
    padded = -(-counts // MOE_BLOCK) * MOE_BLOCK
    ends = jnp.cumsum(padded)
    offsets = ends - padded
    starts = jnp.arange(n_blocks, dtype=jnp.int32) * MOE_BLOCK
    block_e = jnp.minimum(jnp.sum(ends[None, :] <= starts[:, None], axis=1), N_EXPERTS - 1).astype(jnp.int32)
    n_used = (ends[-1:] // MOE_BLOCK).astype(jnp.int32)
    expert = lax.shift_right_logical(codes, CODE_SHIFT)
    which = expert[:, None] == jnp.arange(N_EXPERTS, dtype=jnp.int32)[None, :]
    dest = (codes & ((1 << CODE_SHIFT) - 1)) + jnp.sum(jnp.where(which, offsets[None, :], 0), axis=1)
    spare = 8
    dest = jnp.where(codes >= 0, dest, rows)
    pad_lo = jnp.concatenate([offsets + counts, ends[-1:]]).astype(jnp.int32)
    pad_hi = jnp.concatenate([ends, jnp.full((1,), rows + spare, ends.dtype)]).astype(jnp.int32)
    inv = _moe_inverse(pad_lo, pad_hi, dest.astype(jnp.int32), rows + spare)[:rows]
    tok = lax.shift_right_logical(inv, TOP_K.bit_length() - 1)
    row_src = jnp.where(inv >= 0, tok, 0)
    row_dst = jnp.where(inv >= 0, tok, n)
    row_w = jnp.where(inv >= 0, jnp.take(weights, jnp.maximum(inv, 0)), 0.0)
    xs = _moe_gather(row_src, h2p, n_blocks)
    ys = _moe_mm(block_e, n_used, xs, row_w, layer, w_gu, b_gu, w_dn, b_dn)
    return _moe_combine(n_used, row_dst, ys, n + 8)


def _final_kernel(x_ref, acc_ref, g2_ref, g_ref, o_ref):
    x = x_ref[...] + g2_ref[...] * _load_acc(acc_ref)
    ms = jnp.mean(x * x, axis=-1, keepdims=True)
    o_ref[...] = x * lax.rsqrt(ms + EPS) * g_ref[...]


def _final(x, acc, g2, g, n_ctx_tiles):
    b, t, d = x.shape
    nt = t // TM - n_ctx_tiles
    tile_in = pl.BlockSpec((None, TM, d), lambda b_, t_: (b_, t_ + n_ctx_tiles, 0))
    acc_tile = _acc_tile_spec(acc, lambda b_, t_: b_ * (t // TM) + t_ + n_ctx_tiles)
    return pl.pallas_call(
        _final_kernel,
        out_shape=jax.ShapeDtypeStruct((b, nt * TM, d), F32),
        grid=(b, nt),
        in_specs=[tile_in, acc_tile,
                  pl.BlockSpec((None, None, 1, d), lambda b_, t_: (b_, 1, 0, 0)),
                  pl.BlockSpec((1, d), lambda b_, t_: (0, 0))],
        out_specs=pl.BlockSpec((None, TM, d), lambda b_, t_: (b_, t_, 0)),
        compiler_params=_cparams(("arbitrary", "arbitrary")),
        name="final_norm",
    )(x, acc, g2, g)


def _mod_tables(mod, b, d):
    outs = []
    for j in range(6):
        m = mod[:, j * d:(j + 1) * d]
        lat = m[:b]
        ctx = jnp.broadcast_to(m[b:b + 1], (b, d))
        outs.append(jnp.stack([ctx, lat], axis=1)[:, :, None, :])
    return outs


def _routing(code):
    codes = code[:, :, :TOP_K].reshape(-1)
    weights = lax.bitcast_convert_type(code[:, :, TOP_K:2 * TOP_K], F32).reshape(-1)
    return codes, weights


def _router_params(router_w, router_b):
    d, e = router_w.shape
    rw = jnp.zeros((d, 128), F32).at[:, :e].set(router_w).astype(BF16)
    rb = jnp.full((1, 128), NEG, F32).at[0, :e].set(router_b)
    return rw, rb


def kernel(x, c, ctx, c_ctx, mod_w, mod_b, norm1_g, norm2_g, ev_w_in, ev_conv_w, gdn_a_log, gdn_dt_bias,
           gdn_norm_g, ml_i_bias, ml_f_bias, ml_norm_g, ev_w_out, od_w_in, ret_norm_g, od_w_out,
           router_w, router_b, moe_w_gu, moe_b_gu, moe_w_dn, moe_b_dn, final_g):
    b, s, d = x.shape
    n_ctx = ctx.shape[1]
    depth = mod_w.shape[0]
    assert n_ctx == TM and s % TM == 0 and depth == 2 and b % SCAN_BG == 0
    t = n_ctx + s
    n_tok = b * t

    cond = jnp.concatenate([c, c_ctx[None, :], jnp.zeros((8 - b - 1, d), F32)], axis=0)
    mod = _adaln(cond, mod_w, mod_b)
    xa = jnp.concatenate([ctx, x], axis=1)

    sh1, sc1, g1, sh2, sc2, g2 = _mod_tables(mod[0], b, d)
    qk_w = GDN_H * HD
    conv_ch = 3 * qk_w
    ng = N_DIR * GDN_H
    w_in = ev_w_in[0]
    o_z = conv_ch
    o_a = o_z + qk_w
    o_mq = o_a + 2 * ng
    o_i = o_mq + 4 * qk_w
    w_main = jnp.concatenate([w_in[:, :o_a], w_in[:, o_mq:o_i]], axis=1).astype(BF16)
    w_gate = jnp.concatenate([w_in[:, o_a:o_mq], w_in[:, o_i:o_i + 2 * ng],
                              jnp.zeros((d, 128 - 4 * ng), F32)], axis=1).astype(BF16)
    zeros_g = jnp.zeros((ng,), F32)
    rate = jnp.concatenate([jnp.exp(gdn_a_log[0].astype(F32)).reshape(-1), jnp.zeros((128 - ng,), F32)])[None, :]
    gbias = jnp.concatenate([gdn_dt_bias[0].reshape(-1), zeros_g, ml_i_bias[0].reshape(-1),
                             ml_f_bias[0].reshape(-1), jnp.zeros((128 - 4 * ng,), F32)])[None, :].astype(F32)
    main, gates = _proj_even(xa, norm1_g[0][None, :], sh1, sc1, w_main, w_gate, ev_conv_w[0], rate, gbias)
    gates_c = gates.reshape(b, t // CH, CH, 4 * ng)
    gates_r = jnp.swapaxes(gates_c, 2, 3)
    og_f, og_b, om_f, om_b = _even_scan(main, gates_c, gates_r)
    rw, rb = _router_params(router_w[0], router_b[0])
    gdn_g = jnp.tile(gdn_norm_g[0], GDN_H)[None, :]
    x1, h2p, code, cnt = _merge_even(og_f, og_b, om_f, om_b, main, gdn_g, ml_norm_g[0][None, :],
                                     ev_w_out[0].astype(BF16), xa, g1, norm2_g[0][None, :], sh2, sc2, rw, rb)
    acc = _moe(h2p.reshape(n_tok * PACK, LANES), *_routing(code), cnt[0, :N_EXPERTS].astype(jnp.int32),
               0, moe_w_gu, moe_b_gu, moe_w_dn, moe_b_dn)
    g2_prev = g2

    sh1, sc1, g1, sh2, sc2, g2 = _mod_tables(mod[1], b, d)
    x2, main_o = _proj_odd(x1, acc, g2_prev, norm1_g[1][None, :], sh1, sc1, od_w_in[0].astype(BF16))
    o_f, o_b = _ret_scan(main_o)
    rw, rb = _router_params(router_w[1], router_b[1])
    x3, h2p, code, cnt = _merge_odd(o_f, o_b, main_o, ret_norm_g[0][None, :], od_w_out[0].astype(BF16),
                                    x2, g1, norm2_g[1][None, :], sh2, sc2, rw, rb)
    acc = _moe(h2p.reshape(n_tok * PACK, LANES), *_routing(code), cnt[0, :N_EXPERTS].astype(jnp.int32),
               1, moe_w_gu, moe_b_gu, moe_w_dn, moe_b_dn)
    return _final(x3, acc, g2, final_g[None, :], n_ctx // TM)
```

```python
import functools
import math

import jax
import jax.numpy as jnp
import numpy as np
from jax import lax
from jax.experimental import pallas as pl
from jax.experimental.pallas import tpu as pltpu

F32 = jnp.float32
BF16 = jnp.bfloat16
U32 = jnp.uint32
HIGHEST = lax.Precision.HIGHEST

EPS = 1e-6
CH = 64
TM = 256
CPB = TM // CH
HD = 128
N_DIR = 2
GDN_H = 4
ML_H = 4
RET_H = 8
RET_DV = 256
CONV_W = 3
N_EXPERTS = 32
TOP_K = 4
SWIGLU_ALPHA = 1.702
SWIGLU_LIMIT = 7.0
MOE_BLOCK = 512
ROPE_BASE = 10000.0
NEG = -1e30
VMEM_LIMIT = 56 * 1024 * 1024


def _cparams(sem):
    return pltpu.CompilerParams(dimension_semantics=sem, vmem_limit_bytes=VMEM_LIMIT)


def _dot(a, b, precision=None):
    return jnp.dot(a, b, preferred_element_type=F32, precision=precision)


def _dot_nt(a, b):
    return lax.dot_general(a, b, (((1,), (1,)), ((), ())), preferred_element_type=F32)


def _dot_tn(a, b):
    return lax.dot_general(a, b, (((0,), (0,)), ((), ())), preferred_element_type=F32)


def _sigmoid(x):
    return 1.0 / (1.0 + jnp.exp(-x))


def _silu(x):
    return x * _sigmoid(x)


def _group_sum(x, w):
    outs = []
    for j in range(x.shape[1] // w):
        s = jnp.sum(x[:, j * w:(j + 1) * w], axis=-1, keepdims=True)
        outs.append(jnp.broadcast_to(s, (x.shape[0], w)))
    return outs[0] if len(outs) == 1 else jnp.concatenate(outs, axis=1)


def _rms_mod(x, g, sh, sc):
    ms = jnp.mean(x * x, axis=-1, keepdims=True)
    return (x * lax.rsqrt(ms + EPS) * g) * (1.0 + sc) + sh


LANES = 128
PACK = 4


def _store_row_packed(ref, x, lead=()):
    rows = x.shape[0]
    for g in range(x.shape[1] // LANES):
        ref[lead + (pl.ds(g, rows, stride=PACK), slice(None))] = x[:, g * LANES:(g + 1) * LANES]


def _load_row_packed(ref, rows, lead=()):
    return jnp.concatenate([ref[lead + (pl.ds(g, rows, stride=PACK), slice(None))] for g in range(PACK)], axis=1)


def _load_acc(acc_ref):
    rows = acc_ref.shape[1] // PACK
    return jnp.concatenate([_load_row_packed(acc_ref, rows, lead=(h,)) for h in range(acc_ref.shape[0])], axis=1)


def _acc_tile_spec(acc, tile_of):
    return pl.BlockSpec((acc.shape[0], TM * PACK, LANES), lambda b_, t_: (0, tile_of(b_, t_), 0))


def _adaln_kernel(c_ref, w_ref, b_ref, o_ref):
    c = c_ref[...]
    o_ref[...] = _dot(_silu(c), w_ref[...], precision=HIGHEST) + b_ref[...]


def _adaln(cond, mod_w, mod_b):
    depth, d, d6 = mod_w.shape
    n = d6 // d
    return pl.pallas_call(
        _adaln_kernel,
        out_shape=jax.ShapeDtypeStruct((depth, cond.shape[0], d6), F32),
        grid=(depth, n),
        in_specs=[pl.BlockSpec(cond.shape, lambda l, j: (0, 0)),
                  pl.BlockSpec((None, d, d), lambda l, j: (l, 0, j)),
                  pl.BlockSpec((None, 1, d), lambda l, j: (l, 0, j))],
        out_specs=pl.BlockSpec((None, cond.shape[0], d), lambda l, j: (l, 0, j)),
        compiler_params=_cparams(("arbitrary", "arbitrary")),
        name="adaln",
    )(cond, mod_w, mod_b.reshape(depth, 1, d6))


def _proj_even_kernel(x_ref, g_ref, sh_ref, sc_ref, w_ref, wg_ref, cw_ref, rate_ref, gb_ref,
                      main_ref, gates_ref):
    t = pl.program_id(1)
    h = _rms_mod(x_ref[...], g_ref[...], sh_ref[...], sc_ref[...])
    hb = h.astype(BF16)
    qk_w = GDN_H * HD

    row = lax.broadcasted_iota(jnp.int32, (TM, 1), 0)
    pos = jnp.where(t > 0, row & (CH - 1), row)
    last = jnp.where(t > 0, CH - 1, TM - 1)
    left_ok = pos != 0
    right_ok = pos != last
    for seg in range(3):
        sl = slice(seg * qk_w, (seg + 1) * qk_w)
        u = _dot(hb, w_ref[:, sl])
        um = jnp.where(left_ok, pltpu.roll(u, 1, 0), 0.0)
        up = jnp.where(right_ok, pltpu.roll(u, TM - 1, 0), 0.0)
        cv = _silu(um * cw_ref[0:1, sl] + u * cw_ref[1:2, sl] + up * cw_ref[2:3, sl])
        if seg < 2:
            ss = _group_sum(cv * cv, HD)
            cv = cv * lax.rsqrt(ss + EPS)
            if seg == 0:
                cv = cv * (HD ** -0.5)
        main_ref[:, sl] = cv.astype(BF16)
    for seg in range(3, 8):
        sl = slice(seg * qk_w, (seg + 1) * qk_w)
        u = _dot(hb, w_ref[:, sl])
        if seg == 5:
            u = u * (HD ** -0.5)
        main_ref[:, sl] = u.astype(BF16)

    z = _dot(hb, wg_ref[...]) + gb_ref[...]
    tl = jnp.log(1.0 + jnp.exp(-jnp.abs(z)))
    sp_pos = jnp.maximum(z, 0.0) + tl
    sp_neg = jnp.maximum(-z, 0.0) + tl
    lane = lax.broadcasted_iota(jnp.int32, z.shape, 1)
    ng = N_DIR * GDN_H
    res = jnp.where(lane < ng, -rate_ref[...] * sp_pos,
                    jnp.where(lane < 2 * ng, _sigmoid(z),
                              jnp.where(lane < 3 * ng, z, -sp_neg)))
    gates_ref[...] = res[:, :gates_ref.shape[-1]]


def _tile_mod_spec(d):
    return pl.BlockSpec((None, None, 1, d), lambda b, t: (b, jnp.minimum(t, 1), 0, 0))


def _proj_even(x, g, sh, sc, w_main, w_gate, conv_w, rate, gbias):
    b, t, d = x.shape
    n = w_main.shape[1]
    ngl = 4 * N_DIR * GDN_H
    const = lambda shape: pl.BlockSpec(shape, lambda b_, t_: (0,) * len(shape))
    return pl.pallas_call(
        _proj_even_kernel,
        out_shape=(jax.ShapeDtypeStruct((b, t, n), BF16), jax.ShapeDtypeStruct((b, t, ngl), F32)),
        grid=(b, t // TM),
        in_specs=[pl.BlockSpec((None, TM, d), lambda b_, t_: (b_, t_, 0)),
                  const((1, d)), _tile_mod_spec(d), _tile_mod_spec(d),
                  const(w_main.shape), const(w_gate.shape), const(conv_w.shape),
                  const(rate.shape), const(gbias.shape)],
        out_specs=(pl.BlockSpec((None, TM, n), lambda b_, t_: (b_, t_, 0)),
                   pl.BlockSpec((None, TM, ngl), lambda b_, t_: (b_, t_, 0))),
        compiler_params=_cparams(("arbitrary", "arbitrary")),
        name="proj_even",
    )(x, g, sh, sc, w_main, w_gate, conv_w, rate, gbias)


def _rev_tile(i, nt):
    return jnp.where(i == 0, 0, nt - i)


def _tri_masks():
    r = lax.broadcasted_iota(jnp.int32, (CH, CH), 0)
    c = lax.broadcasted_iota(jnp.int32, (CH, CH), 1)
    return r >= c, r > c, r <= c, r < c


SCAN_BG = 2
SOLVE_BASE = 8


def _scan_streams(heads):
    groups = [(bi, d) for bi in range(SCAN_BG) for d in range(N_DIR)]
    streams = [(gi, hh) for gi in range(len(groups)) for hh in range(heads)]
    return groups, streams


def _cumsum_both(groups, gcs, grs, lower, upper):
    tri = (lower.astype(F32), upper.astype(F32))
    cs_c = [_dot(tri[d], gc, precision=HIGHEST) for (_, d), gc in zip(groups, gcs)]
    cs_r = [_dot(gr, tri[1 - d], precision=HIGHEST) for (_, d), gr in zip(groups, grs)]
    return cs_c, cs_r


def _store_heads(refs, groups, rows, outs, heads):
    for gi, (bi, d) in enumerate(groups):
        o_ref = refs[d][-1]
        tile = jnp.concatenate(outs[gi * heads:(gi + 1) * heads], axis=1)
        o_ref[bi, rows[d], :] = tile.astype(o_ref.dtype)


class _GdnChunk:
    def __init__(self, refs, s_ref):
        self.refs, self.s_ref = refs, s_ref
        self.groups, self.streams = _scan_streams(GDN_H)

    def prelude(self, masks, rows, gcs, cs_c, cs_r):
        lower, lstrict, upper, ustrict = masks
        refs, groups = self.refs, self.groups
        ng = N_DIR * GDN_H
        st = []
        for gi, hh in self.streams:
            bi, d = groups[gi]
            ci = d * GDN_H + hh
            cols = slice(hh * HD, (hh + 1) * HD)
            incl, strict = (lower, lstrict) if d == 0 else (upper, ustrict)
            g_col = cs_c[gi][:, ci:ci + 1]
            g_row = cs_r[gi][ci:ci + 1, :]
            beta = gcs[gi][:, ng + ci:ng + ci + 1]
            tot = g_col[CH - 1:CH, :] if d == 0 else g_col[0:1, :]
            decay = jnp.where(incl, jnp.exp(jnp.where(incl, g_col - g_row, 0.0)), 0.0)
            q = refs[d][0][bi, rows[d], cols].astype(F32)
            k = refs[d][1][bi, rows[d], cols].astype(F32)
            v = refs[d][2][bi, rows[d], cols].astype(F32)
            kbeta = k * beta
            eg = jnp.exp(g_col)
            st.append(dict(
                strict=strict, decay=decay, kb=k.astype(BF16), kbetab=kbeta.astype(BF16), qb=q.astype(BF16),
                x=jnp.concatenate([v * beta, kbeta * eg], axis=1),
                qe=(q * eg).astype(BF16), kdec=(k * jnp.exp(tot - g_col)).astype(BF16), cd=jnp.exp(tot)))
        self.st = st

    def matmuls(self, rows):
        st, s_ref, streams = self.st, self.s_ref, self.streams
        kk = [_dot_nt(s["kbetab"], s["kb"]) for s in st]
        qk = [(_dot_nt(s["qb"], s["kb"]) * s["decay"]).astype(BF16) for s in st]
        r = lax.broadcasted_iota(jnp.int32, (CH, CH), 0)
        c = lax.broadcasted_iota(jnp.int32, (CH, CH), 1)
        eye = jnp.where(r == c, 1.0, 0.0)
        blocks_differ = lambda w: lax.shift_right_logical(r, w.bit_length() - 1) ^ \
            lax.shift_right_logical(c, w.bit_length() - 1)
        a = [jnp.where(s["strict"], m * s["decay"], 0.0) for s, m in zip(st, kk)]
        p = [-jnp.where(blocks_differ(SOLVE_BASE) == 0, m, 0.0) for m in a]
        t = [eye + m for m in p]
        w = 2
        while w < SOLVE_BASE:
            pb = [m.astype(BF16) for m in p]
            p = [_dot(m, m) for m in pb]
            t = [tt + _dot(tt.astype(BF16), m.astype(BF16)) for tt, m in zip(t, p)]
            w *= 2
        w = SOLVE_BASE
        while w < CH:
            off = blocks_differ(w) == 1
            tb = [tt.astype(BF16) for tt in t]
            ta = [_dot(tt, jnp.where(off, m, 0.0).astype(BF16)) for tt, m in zip(tb, a)]
            t = [tt - _dot(m.astype(BF16), tt2) for tt, m, tt2 in zip(t, ta, tb)]
            w *= 2
        x = [_dot(tt.astype(BF16), s["x"].astype(BF16)) for tt, s in zip(t, st)]
        s_old = [s_ref[si] for si in range(len(streams))]
        sb = [s.astype(BF16) for s in s_old]
        vnb = [(xx[:, :HD] - _dot(xx[:, HD:].astype(BF16), s)).astype(BF16) for xx, s in zip(x, sb)]
        o = [_dot(s["qe"], sbi) + _dot(m, vn) for s, sbi, m, vn in zip(st, sb, qk, vnb)]
        s_new = [so * s["cd"] + _dot_tn(s["kdec"], vn) for s, so, vn in zip(st, s_old, vnb)]
        for si, s in enumerate(s_new):
            s_ref[si] = s
        _store_heads(self.refs, self.groups, rows, o, GDN_H)


def _scan_specs(nt, width, col):
    fwd = pl.BlockSpec((SCAN_BG, TM, width), lambda b, i: (b, i, col))
    bwd = pl.BlockSpec((SCAN_BG, TM, width), lambda b, i: (b, _rev_tile(i, nt), col))
    return fwd, bwd


def _gate_specs(nt, shape):
    fwd = pl.BlockSpec((SCAN_BG, CPB) + shape, lambda b, i: (b, i, 0, 0))
    bwd = pl.BlockSpec((SCAN_BG, CPB) + shape, lambda b, i: (b, _rev_tile(i, nt), 0, 0))
    return fwd, bwd


class _MlstmChunk:
    def __init__(self, refs, c_ref, m_ref):
        self.refs, self.c_ref, self.m_ref = refs, c_ref, m_ref
        self.groups, self.streams = _scan_streams(ML_H)

    def prelude(self, masks, rows, gcs, grs, cs_c, cs_r):
        lower, _, upper, _ = masks
        refs, groups, streams, m_ref = self.refs, self.groups, self.streams, self.m_ref
        ng = N_DIR * GDN_H
        i_off = 2 * ng
        f_off = 2 * ng + N_DIR * ML_H
        lane = lax.broadcasted_iota(jnp.int32, (CH, HD), 1)
        ones_col = jnp.where(lane == 0, 1.0, 0.0).astype(BF16)
        ns = len(streams)
        dirs = [groups[gi][1] for gi, _ in streams]
        chan = [groups[gi][1] * ML_H + hh for gi, hh in streams]
        b_col = [cs_c[gi][:, f_off + c:f_off + c + 1] for (gi, _), c in zip(streams, chan)]
        b_row = [cs_r[gi][f_off + c:f_off + c + 1, :] for (gi, _), c in zip(streams, chan)]
        i_col = [gcs[gi][:, i_off + c:i_off + c + 1] for (gi, _), c in zip(streams, chan)]
        i_row = [grs[gi][i_off + c:i_off + c + 1, :] for (gi, _), c in zip(streams, chan)]
        b_tot = [bc[CH - 1:CH, :] if d == 0 else bc[0:1, :] for bc, d in zip(b_col, dirs)]
        m_rows = [m_ref[si] for si in range(ns)]
        m_old = [mr[:, 0:1] for mr in m_rows]
        d_in = [jnp.where(lower if d == 0 else upper, bc - br + ir, NEG)
                for d, bc, br, ir in zip(dirs, b_col, b_row, i_row)]
        d_end = [bt - br + ir for bt, br, ir in zip(b_tot, b_row, i_row)]
        mx_in = [jnp.max(a, axis=-1, keepdims=True) for a in d_in]
        mx_end = [jnp.max(a, axis=-1, keepdims=True) for a in d_end]
        d_carry = [bc + m for bc, m in zip(b_col, m_old)]
        m_t = [jnp.maximum(a, b_) for a, b_ in zip(d_carry, mx_in)]
        carry_end = [bt + m for bt, m in zip(b_tot, m_old)]
        m_new = [jnp.maximum(a, b_) for a, b_ in zip(carry_end, mx_end)]
        p_in = [jnp.exp(a - b_) for a, b_ in zip(d_in, m_t)]
        w_end = [jnp.exp(bt - bc + ic - mn) for bt, bc, ic, mn in zip(b_tot, b_col, i_col, m_new)]
        st = []
        for si, (gi, hh) in enumerate(streams):
            bi, d = groups[gi]
            cols = slice(hh * HD, (hh + 1) * HD)
            k = refs[d][1][bi, rows[d], cols]
            v = refs[d][2][bi, rows[d], cols]
            st.append(dict(
                q=refs[d][0][bi, rows[d], cols], k=k,
                v_aug=jnp.concatenate([v, ones_col], axis=1),
                p_in=p_in[si], w_carry=jnp.exp(d_carry[si] - m_t[si]), floor=jnp.exp(-m_t[si]),
                kw=(k.astype(F32) * w_end[si]).astype(BF16),
                f_end=jnp.exp(carry_end[si] - m_new[si]),
                m_new=jnp.broadcast_to(m_new[si], m_rows[si].shape)))
        self.st = st

    def matmuls(self, rows):
        st, c_ref, m_ref, streams = self.st, self.c_ref, self.m_ref, self.streams
        sc = [(_dot_nt(s["q"], s["k"]) * s["p_in"]).astype(BF16) for s in st]
        c_old = [c_ref[si] for si in range(len(streams))]
        qc = [_dot(s["q"], c.astype(BF16)) for s, c in zip(st, c_old)]
        nd = [s["w_carry"] * a + _dot(m, s["v_aug"]) for s, a, m in zip(st, qc, sc)]
        hout = [a[:, :HD] / jnp.maximum(jnp.abs(a[:, HD:HD + 1]), s["floor"]) for s, a in zip(st, nd)]
        c_new = [s["f_end"] * c + _dot_tn(s["kw"], s["v_aug"]) for s, c in zip(st, c_old)]
        for si, (s, c) in enumerate(zip(st, c_new)):
            c_ref[si] = c
            m_ref[si] = s["m_new"]
        _store_heads(self.refs, self.groups, rows, hout, ML_H)


def _even_scan_kernel(gqf, gkf, gvf, gqb, gkb, gvb, mqf, mkf, mvf, mqb, mkb, mvb, gcf, grf, gcb, grb,
                      ogf, ogb, omf, omb, s_ref, c_ref, m_ref):
    i = pl.program_id(1)

    @pl.when(i == 0)
    def _():
        s_ref[...] = jnp.zeros_like(s_ref)
        c_ref[...] = jnp.zeros_like(c_ref)
        m_ref[...] = jnp.zeros_like(m_ref)

    masks = _tri_masks()
    gdn = _GdnChunk(((gqf, gkf, gvf, ogf), (gqb, gkb, gvb, ogb)), s_ref)
    mls = _MlstmChunk(((mqf, mkf, mvf, omf), (mqb, mkb, mvb, omb)), c_ref, m_ref)
    gate_refs = ((gcf, grf), (gcb, grb))

    def chunk_body(cc, carry):
        cidx = (cc, CPB - 1 - cc)
        rows = tuple(pl.ds(pl.multiple_of(c * CH, CH), CH) for c in cidx)
        gcs = [gate_refs[d][0][bi, cidx[d]] for bi, d in gdn.groups]
        grs = [gate_refs[d][1][bi, cidx[d]] for bi, d in gdn.groups]
        cs_c, cs_r = _cumsum_both(gdn.groups, gcs, grs, masks[0], masks[2])
        gdn.prelude(masks, rows, gcs, cs_c, cs_r)
        mls.prelude(masks, rows, gcs, grs, cs_c, cs_r)
        gdn.matmuls(rows)
        mls.matmuls(rows)
        return carry

    lax.fori_loop(0, CPB, chunk_body, 0)


def _even_scan(main, gates_c, gates_r):
    b, t, _ = main.shape
    nt = t // TM
    w = GDN_H * HD
    ngl = gates_c.shape[-1]
    qkv = lambda cols: [_scan_specs(nt, w, c)[d] for d in range(N_DIR) for c in cols]
    gcf, gcb = _gate_specs(nt, (CH, ngl))
    grf, grb = _gate_specs(nt, (ngl, CH))
    of, ob = _scan_specs(nt, w, 0)
    out = jax.ShapeDtypeStruct((b, t, w), BF16)
    n_streams = SCAN_BG * N_DIR * GDN_H
    return pl.pallas_call(
        _even_scan_kernel,
        out_shape=(out, out, out, out),
        grid=(b // SCAN_BG, nt),
        in_specs=qkv((0, 1, 2)) + qkv((4, 5, 6)) + [gcf, grf, gcb, grb],
        out_specs=(of, ob, of, ob),
        scratch_shapes=[pltpu.VMEM((n_streams, HD, HD), F32), pltpu.VMEM((n_streams, HD, 2 * HD), F32),
                        pltpu.VMEM((n_streams, 1, HD), F32)],
        compiler_params=_cparams(("arbitrary", "arbitrary")),
        name="even_scan",
    )(*([main] * 12), gates_c, gates_r, gates_c, gates_r)


def _ret_kernel(qf, kf, vf, qb, kb, vb, intra_ref, cross_ref, tail_ref, cd_ref, of, ob, s_ref):
    i = pl.program_id(1)

    @pl.when(i == 0)
    def _():
        s_ref[...] = jnp.zeros_like(s_ref)

    refs = ((qf, kf, vf, of), (qb, kb, vb, ob))
    groups, streams = _scan_streams(RET_H)

    def chunk_body(cc, carry):
        cidx = (cc, CPB - 1 - cc)
        rows = tuple(pl.ds(pl.multiple_of(c * CH, CH), CH) for c in cidx)
        st = []
        for gi, hh in streams:
            bi, d = groups[gi]
            ti = d * RET_H + hh
            kcols = slice(hh * HD, (hh + 1) * HD)
            q = refs[d][0][bi, rows[d], kcols]
            k = refs[d][1][bi, rows[d], kcols]
            st.append(dict(
                ti=ti, q=q, k=k, v=refs[d][2][bi, rows[d], slice(hh * RET_DV, (hh + 1) * RET_DV)],
                qc=(q.astype(F32) * cross_ref[ti]).astype(BF16),
                kt=(k.astype(F32) * tail_ref[ti]).astype(BF16)))
        sc = [(_dot_nt(s["q"], s["k"]) * intra_ref[s["ti"]]).astype(BF16) for s in st]
        s_old = [s_ref[si] for si in range(len(streams))]
        o = [_dot(m, s["v"]) + _dot(s["qc"], so.astype(BF16)) for s, m, so in zip(st, sc, s_old)]
        s_new = [cd_ref[s["ti"]] * so + _dot_tn(s["kt"], s["v"]) for s, so in zip(st, s_old)]
        for si, s in enumerate(s_new):
            s_ref[si] = s
        _store_heads(refs, groups, rows, o, RET_H)
        return carry

    lax.fori_loop(0, CPB, chunk_body, 0)


def _ret_tables():
    pos = np.arange(CH, dtype=np.float64)
    intra, cross, tail, cd = [], [], [], []
    for d in range(N_DIR):
        expo = 5.0 + np.arange(RET_H, dtype=np.float64)
        if d == 1:
            expo = expo[::-1]
        lg = np.log1p(-np.exp2(-expo))
        p = pos if d == 0 else (CH - 1.0 - pos)
        diff = p[:, None] - p[None, :]
        for hh in range(RET_H):
            intra.append(np.where(diff >= 0, np.exp(np.where(diff >= 0, diff, 0.0) * lg[hh]), 0.0))
            cross.append(np.broadcast_to(np.exp((p + 1.0) * lg[hh])[:, None], (CH, HD)))
            tail.append(np.broadcast_to(np.exp((CH - 1.0 - p) * lg[hh])[:, None], (CH, HD)))
            cd.append(np.full((1, RET_DV), np.exp(CH * lg[hh])))
    f = lambda a: jnp.asarray(np.stack(a), F32)
    return f(intra), f(cross), f(tail), f(cd)


def _ret_scan(main):
    b, t, _ = main.shape
    nt = t // TM
    qw = RET_H * HD
    vw = RET_H * RET_DV
    qf, qb = _scan_specs(nt, qw, 0)
    kf, kb = _scan_specs(nt, qw, 1)
    vf, vb = _scan_specs(nt, vw, 1)
    of, ob = _scan_specs(nt, vw, 0)
    tabs = _ret_tables()
    const = lambda a: pl.BlockSpec(a.shape, lambda b_, i_: (0,) * a.ndim)
    return pl.pallas_call(
        _ret_kernel,
        out_shape=(jax.ShapeDtypeStruct((b, t, vw), BF16), jax.ShapeDtypeStruct((b, t, vw), BF16)),
        grid=(b // SCAN_BG, nt),
        in_specs=[qf, kf, vf, qb, kb, vb] + [const(a) for a in tabs],
        out_specs=(of, ob),
        scratch_shapes=[pltpu.VMEM((SCAN_BG * N_DIR * RET_H, HD, RET_DV), F32)],
        compiler_params=_cparams(("arbitrary", "arbitrary")),
        name="ret_scan",
    )(main, main, main, main, main, main, *tabs)


CODE_SHIFT = 17


def _route_and_pack(h2, rw_ref, rb_ref, h2p_ref, code_ref, cnt_ref, base_ref, skip_context):
    h2b = h2.astype(BF16)
    logits = _dot(h2b, rw_ref[...]) + rb_ref[...]
    lane = lax.broadcasted_iota(jnp.int32, logits.shape, 1)
    lane_f = lane.astype(F32)
    vals, idxs = [], []
    cur = logits
    for _ in range(TOP_K):
        m = jnp.max(cur, axis=-1, keepdims=True)
        ix = jnp.min(jnp.where(cur == m, lane_f, float(logits.shape[1])), axis=-1, keepdims=True)
        vals.append(m)
        idxs.append(ix)
        cur = jnp.where(lane_f == ix, NEG, cur)
    es = [jnp.exp(v - vals[0]) for v in vals]
    tot = es[0] + es[1] + es[2] + es[3]

    @pl.when((pl.program_id(0) == 0) & (pl.program_id(1) == 0))
    def _():
        base_ref[...] = jnp.zeros_like(base_ref)

    tm = logits.shape[0]
    routed = pl.program_id(1) > 0
    onehot = jnp.zeros(logits.shape, F32)
    for j in range(TOP_K):
        onehot = jnp.where(lane_f == idxs[j], 1.0, onehot)
    if skip_context:
        onehot = onehot * jnp.where(routed, 1.0, 0.0)
    r_i = lax.broadcasted_iota(jnp.int32, (tm, tm), 0)
    c_i = lax.broadcasted_iota(jnp.int32, (tm, tm), 1)
    before = _dot(jnp.where(r_i > c_i, 1.0, 0.0).astype(BF16), onehot.astype(BF16))
    base = base_ref[...]
    pos = before + base
    total = base + before[tm - 1:tm, :] + onehot[tm - 1:tm, :]
    base_ref[...] = total
    cnt_ref[...] = total

    code = jnp.zeros(logits.shape, F32)
    wgt = jnp.zeros(logits.shape, F32)
    for j in range(TOP_K):
        rank = jnp.sum(jnp.where(lane_f == idxs[j], pos, 0.0), axis=-1, keepdims=True)
        code = jnp.where(lane == j, idxs[j] * float(1 << CODE_SHIFT) + rank, code)
        wgt = jnp.where(lane == TOP_K + j, es[j] / tot, wgt)
    code_i = code.astype(jnp.int32)
    if skip_context:
        code_i = jnp.where(routed, code_i, -1)
    code_ref[...] = jnp.where(lane < TOP_K, code_i, pltpu.bitcast(wgt, jnp.int32))
    half = h2.shape[1] // 2
    r = h2b.astype(F32)
    lo = lax.shift_right_logical(pltpu.bitcast(r[:, :half], U32), jnp.uint32(16))
    hi = pltpu.bitcast(r[:, half:], U32) & jnp.uint32(0xFFFF0000)
    _store_row_packed(h2p_ref, hi | lo)


def _merge_even_kernel(ogf, ogb, omf, omb, z_ref, mo_ref, gg_ref, mg_ref, wo_ref, x_ref, g1_ref,
                       n2_ref, sh_ref, sc_ref, rw_ref, rb_ref, x1_ref, h2p_ref, code_ref, cnt_ref, base_ref):
    og = ogf[...].astype(F32) + ogb[...].astype(F32)
    ms = _group_sum(og * og, HD) * (1.0 / HD)
    a = og * lax.rsqrt(ms + EPS) * gg_ref[...] * _silu(z_ref[...].astype(F32))
    om = omf[...].astype(F32) + omb[...].astype(F32)
    ms = _group_sum(om * om, HD) * (1.0 / HD)
    m = om * lax.rsqrt(ms + EPS) * mg_ref[...] * _sigmoid(mo_ref[...].astype(F32))
    cat = jnp.concatenate([a, m], axis=1).astype(BF16)
    y = _dot(cat, wo_ref[...])
    x1 = x_ref[...] + g1_ref[...] * y
    x1_ref[...] = x1
    h2 = _rms_mod(x1, n2_ref[...], sh_ref[...], sc_ref[...])
    _route_and_pack(h2, rw_ref, rb_ref, h2p_ref, code_ref, cnt_ref, base_ref, skip_context=False)


def _merge_odd_kernel(of, ob, gate_ref, ng_ref, wo_ref, x_ref, g1_ref,
                      n2_ref, sh_ref, sc_ref, rw_ref, rb_ref, x1_ref, h2p_ref, code_ref, cnt_ref, base_ref):
    o = of[...].astype(F32) + ob[...].astype(F32)
    o = o - _group_sum(o, RET_DV) * (1.0 / RET_DV)
    ms = _group_sum(o * o, RET_DV) * (1.0 / RET_DV)
    y = o * lax.rsqrt(ms + EPS) * ng_ref[...] * _silu(gate_ref[...].astype(F32))
    y = _dot(y.astype(BF16), wo_ref[...])
    x1 = x_ref[...] + g1_ref[...] * y
    x1_ref[...] = x1
    h2 = _rms_mod(x1, n2_ref[...], sh_ref[...], sc_ref[...])
    _route_and_pack(h2, rw_ref, rb_ref, h2p_ref, code_ref, cnt_ref, base_ref, skip_context=True)


def _merge_out(b, t, d):
    assert d // 2 == PACK * LANES
    shapes = (jax.ShapeDtypeStruct((b, t, d), F32), jax.ShapeDtypeStruct((b, t * PACK, LANES), U32),
              jax.ShapeDtypeStruct((b, t, LANES), jnp.int32), jax.ShapeDtypeStruct((1, LANES), F32))
    tile = lambda rows, width: pl.BlockSpec((None, rows, width), lambda b_, t_: (b_, t_, 0))
    specs = (tile(TM, d), tile(TM * PACK, LANES), tile(TM, LANES), pl.BlockSpec((1, LANES), lambda b_, t_: (0, 0)))
    return shapes, specs


def _merge_even(og_f, og_b, om_f, om_b, main, gdn_g, ml_g, w_out, x, g1, n2, sh2, sc2, rw, rb):
    b, t, d = x.shape
    w = GDN_H * HD
    tile = lambda width, col: pl.BlockSpec((None, TM, width), lambda b_, t_: (b_, t_, col))
    const = lambda a: pl.BlockSpec(a.shape, lambda b_, t_: (0,) * a.ndim)
    shapes, specs = _merge_out(b, t, d)
    return pl.pallas_call(
        _merge_even_kernel,
        out_shape=shapes,
        grid=(b, t // TM),
        in_specs=[tile(w, 0), tile(w, 0), tile(w, 0), tile(w, 0), tile(w, 3), tile(w, 7),
                  const(gdn_g), const(ml_g), const(w_out), tile(d, 0), _tile_mod_spec(d),
                  const(n2), _tile_mod_spec(d), _tile_mod_spec(d), const(rw), const(rb)],
        out_specs=specs,
        scratch_shapes=[pltpu.VMEM((1, 128), F32)],
        compiler_params=_cparams(("arbitrary", "arbitrary")),
        name="merge_even",
    )(og_f, og_b, om_f, om_b, main, main, gdn_g, ml_g, w_out, x, g1, n2, sh2, sc2, rw, rb)


def _merge_odd(o_f, o_b, main, ret_g, w_out, x, g1, n2, sh2, sc2, rw, rb):
    b, t, d = x.shape
    vw = RET_H * RET_DV
    tile = lambda width, col: pl.BlockSpec((None, TM, width), lambda b_, t_: (b_, t_, col))
    const = lambda a: pl.BlockSpec(a.shape, lambda b_, t_: (0,) * a.ndim)
    shapes, specs = _merge_out(b, t, d)
    return pl.pallas_call(
        _merge_odd_kernel,
        out_shape=shapes,
        grid=(b, t // TM),
        in_specs=[tile(vw, 0), tile(vw, 0), tile(vw, 2), const(ret_g), const(w_out), tile(d, 0),
                  _tile_mod_spec(d), const(n2), _tile_mod_spec(d), _tile_mod_spec(d), const(rw), const(rb)],
        out_specs=specs,
        scratch_shapes=[pltpu.VMEM((1, 128), F32)],
        compiler_params=_cparams(("arbitrary", "arbitrary")),
        name="merge_odd",
    )(o_f, o_b, main, ret_g, w_out, x, g1, n2, sh2, sc2, rw, rb)


def _proj_odd_kernel(x_ref, acc_ref, g2_ref, g_ref, sh_ref, sc_ref, w_ref, cos_ref, sin_ref,
                     x2_ref, main_ref):
    x2 = x_ref[...] + g2_ref[...] * _load_acc(acc_ref)
    x2_ref[...] = x2
    hb = _rms_mod(x2, g_ref[...], sh_ref[...], sc_ref[...]).astype(BF16)
    qk_w = RET_H * HD
    cos = cos_ref[...]
    sin = sin_ref[...]
    n_seg = w_ref.shape[1] // qk_w
    for seg in range(n_seg):
        sl = slice(seg * qk_w, (seg + 1) * qk_w)
        u = _dot(hb, w_ref[:, sl])
        if seg < 2:
            parts = []
            for hh in range(RET_H):
                uh = u[:, hh * HD:(hh + 1) * HD]
                parts.append(uh * cos + pltpu.roll(uh, HD // 2, 1) * sin)
            u = jnp.concatenate(parts, axis=1)
            if seg == 1:
                u = u * (HD ** -0.5)
        main_ref[:, sl] = u.astype(BF16)


def _rope_tables(t):
    half = HD // 2
    freqs = ROPE_BASE ** (-jnp.arange(half, dtype=F32) / half)
    ang = jnp.arange(t, dtype=F32)[:, None] * freqs[None, :]
    cos, sin = jnp.cos(ang), jnp.sin(ang)
    return jnp.concatenate([cos, cos], axis=1), jnp.concatenate([-sin, sin], axis=1)


def _proj_odd(x, acc, g2, g, sh, sc, w):
    b, t, d = x.shape
    n = w.shape[1]
    cos, sin = _rope_tables(t)
    const = lambda a: pl.BlockSpec(a.shape, lambda b_, t_: (0,) * a.ndim)
    tile = lambda width: pl.BlockSpec((None, TM, width), lambda b_, t_: (b_, t_, 0))
    rope = pl.BlockSpec((TM, HD), lambda b_, t_: (t_, 0))
    nt = t // TM
    acc_tile = _acc_tile_spec(acc, lambda b_, t_: b_ * nt + t_)
    return pl.pallas_call(
        _proj_odd_kernel,
        out_shape=(jax.ShapeDtypeStruct((b, t, d), F32), jax.ShapeDtypeStruct((b, t, n), BF16)),
        grid=(b, nt),
        in_specs=[tile(d), acc_tile, _tile_mod_spec(d), const(g), _tile_mod_spec(d), _tile_mod_spec(d),
                  const(w), rope, rope],
        out_specs=(tile(d), tile(n)),
        compiler_params=_cparams(("arbitrary", "arbitrary")),
        name="proj_odd",
    )(x, acc, g2, g, sh, sc, w, cos, sin)


ROWS_PER_STEP = 8


def _packed_row(r):
    return pl.ds(pl.multiple_of(r * PACK, PACK), PACK)


def _move_rows(n_blocks):
    return MOE_BLOCK * next(k for k in (4, 2, 1) if n_blocks % k == 0)


def _moe_gather_kernel(src_ref, h_ref, xs_ref):
    def body(r8, carry):
        base = r8 * ROWS_PER_STEP
        rows = [h_ref[_packed_row(src_ref[0, base + j]), :] for j in range(ROWS_PER_STEP)]
        for j in range(ROWS_PER_STEP):
            xs_ref[_packed_row(base + j), :] = rows[j]
        return carry

    lax.fori_loop(0, src_ref.shape[1] // ROWS_PER_STEP, body, 0)


def _moe_gather(row_src, h2p, n_blocks):
    step = _move_rows(n_blocks)
    n_steps = n_blocks * MOE_BLOCK // step
    return pl.pallas_call(
        _moe_gather_kernel,
        out_shape=jax.ShapeDtypeStruct((n_blocks * MOE_BLOCK * PACK, LANES), U32),
        grid=(n_steps,),
        in_specs=[pl.BlockSpec((None, 1, step), lambda i: (i, 0, 0), memory_space=pltpu.SMEM),
                  pl.BlockSpec(h2p.shape, lambda i: (0, 0))],
        out_specs=pl.BlockSpec((step * PACK, LANES), lambda i: (i, 0)),
        compiler_params=_cparams(("arbitrary",)),
        name="moe_gather",
    )(row_src.reshape(n_steps, 1, step), h2p)


def _moe_mm_kernel(be_ref, nu_ref, xs_ref, rw_ref, wgu_ref, bgu_ref, wdn_ref, bdn_ref, ys_ref, wgu_s, wdn_s):
    i = pl.program_id(0)
    e = be_ref[i]
    half = wgu_ref.shape[0] // 2
    hl = LANES // 2

    @pl.when(i >= nu_ref[0])
    def _():
        ys_ref[...] = jnp.zeros_like(ys_ref)

    @pl.when(i < nu_ref[0])
    def _():
        @pl.when((i == 0) | (e != be_ref[jnp.maximum(i - 1, 0)]))
        def _():
            wgu_s[...] = wgu_ref[...].astype(BF16)
            for p in range(wdn_ref.shape[0] // LANES):
                first = wdn_ref[p * LANES:p * LANES + hl, :].astype(BF16).astype(F32)
                second = wdn_ref[p * LANES + hl:(p + 1) * LANES, :].astype(BF16).astype(F32)
                word = (pltpu.bitcast(second, U32) & jnp.uint32(0xFFFF0000)) | \
                    lax.shift_right_logical(pltpu.bitcast(first, U32), jnp.uint32(16))
                wdn_s[p * LANES:(p + 1) * LANES, :] = pltpu.bitcast(word, BF16)

        xu = _load_row_packed(xs_ref, MOE_BLOCK)
        lo = pltpu.bitcast(lax.shift_left(xu, jnp.uint32(16)), F32).astype(BF16)
        hi = pltpu.bitcast(xu & jnp.uint32(0xFFFF0000), F32).astype(BF16)
        gu = _dot(lo, wgu_s[:half, :]) + _dot(hi, wgu_s[half:, :]) + bgu_ref[...]
        even = (lax.broadcasted_iota(jnp.int32, (gu.shape[0], LANES), 1) & 1) == 0
        acts = []
        for p in range(gu.shape[1] // (2 * LANES)):
            a = gu[:, 2 * p * LANES:(2 * p + 1) * LANES]
            b = gu[:, (2 * p + 1) * LANES:(2 * p + 2) * LANES]
            gate = jnp.minimum(jnp.where(even, a, pltpu.roll(b, 1, 1)), SWIGLU_LIMIT)
            up = jnp.clip(jnp.where(even, pltpu.roll(a, LANES - 1, 1), b), -SWIGLU_LIMIT, SWIGLU_LIMIT)
            acts.append(((up + 1.0) * gate * _sigmoid(SWIGLU_ALPHA * gate)).astype(BF16))
        y = _dot(jnp.concatenate(acts, axis=1), wdn_s[...]) + bdn_ref[...]
        y = y * jnp.broadcast_to(rw_ref[...], (8, rw_ref.shape[1])).T[:, 0:1]
        cw = PACK * LANES
        for h in range(y.shape[1] // cw):
            _store_row_packed(ys_ref, y[:, h * cw:(h + 1) * cw], lead=(h,))


def _moe_mm(block_e, n_used, xs, row_w, layer, w_gu, b_gu, w_dn, b_dn):
    n_blocks = xs.shape[0] // (MOE_BLOCK * PACK)
    depth, n_e, d, f2 = w_gu.shape
    n_half = d // (PACK * LANES)
    ew = lambda shape: pl.BlockSpec((None, None) + shape, lambda i, be, nu: (layer, be[i], 0, 0))
    grid_spec = pltpu.PrefetchScalarGridSpec(
        num_scalar_prefetch=2,
        grid=(n_blocks,),
        in_specs=[pl.BlockSpec((MOE_BLOCK * PACK, LANES), lambda i, be, nu: (i, 0)),
                  pl.BlockSpec((None, 1, MOE_BLOCK), lambda i, be, nu: (i, 0, 0)),
                  ew((d, f2)), ew((1, f2)), ew((f2 // 2, d)), ew((1, d))],
        out_specs=pl.BlockSpec((n_half, MOE_BLOCK * PACK, LANES), lambda i, be, nu: (0, i, 0)),
        scratch_shapes=[pltpu.VMEM((d, f2), BF16), pltpu.VMEM((f2 // 2, d), BF16)],
    )
    return pl.pallas_call(
        _moe_mm_kernel,
        out_shape=jax.ShapeDtypeStruct((n_half, xs.shape[0], LANES), F32),
        grid_spec=grid_spec,
        compiler_params=_cparams(("arbitrary",)),
        name="moe_mm",
    )(block_e, n_used, xs, row_w.reshape(n_blocks, 1, MOE_BLOCK), w_gu, b_gu.reshape(depth, n_e, 1, f2),
      w_dn, b_dn.reshape(depth, n_e, 1, d))


def _moe_combine_kernel(nu_ref, dst_ref, ys_ref, acc_ref):
    i = pl.program_id(1)

    @pl.when(i == 0)
    def _():
        acc_ref[...] = jnp.zeros_like(acc_ref)

    def body(r8, carry):
        base = r8 * ROWS_PER_STEP
        dst = [_packed_row(dst_ref[0, base + j]) for j in range(ROWS_PER_STEP)]
        group = ROWS_PER_STEP * PACK
        ys = ys_ref[pl.ds(pl.multiple_of(r8 * group, group), group), :]
        new = [acc_ref[dst[j], :] + ys[j * PACK:(j + 1) * PACK, :] for j in range(ROWS_PER_STEP)]
        for j in range(ROWS_PER_STEP):
            acc_ref[dst[j], :] = new[j]
        return carry

    step = dst_ref.shape[1]

    @pl.when(i * (step // MOE_BLOCK) < nu_ref[0])
    def _():
        lax.fori_loop(0, step // ROWS_PER_STEP, body, 0)


def _moe_combine(n_used, row_dst, ys, n_rows_out):
    n_half, packed_rows, _ = ys.shape
    step = _move_rows(packed_rows // (MOE_BLOCK * PACK))
    n_blocks = packed_rows // (step * PACK)
    idx_spec = pl.BlockSpec((None, 1, step), lambda j, i, nu: (i, 0, 0), memory_space=pltpu.SMEM)
    grid_spec = pltpu.PrefetchScalarGridSpec(
        num_scalar_prefetch=1,
        grid=(n_half, n_blocks),
        in_specs=[idx_spec, pl.BlockSpec((None, step * PACK, LANES), lambda j, i, nu: (j, i, 0))],
        out_specs=pl.BlockSpec((None, n_rows_out * PACK, LANES), lambda j, i, nu: (j, 0, 0),
                               pipeline_mode=pl.Buffered(1)),
    )
    return pl.pallas_call(
        _moe_combine_kernel,
        out_shape=jax.ShapeDtypeStruct((n_half, n_rows_out * PACK, LANES), F32),
        grid_spec=grid_spec,
        compiler_params=_cparams(("arbitrary", "arbitrary")),
        name="moe_combine",
    )(n_used, row_dst.reshape(n_blocks, 1, step), ys)


def _moe_inverse_kernel(pad_lo_ref, pad_hi_ref, dest_ref, inv_ref):
    def init(r, c):
        inv_ref[r] = -1
        return c

    def pad_range(k, carry):
        lax.fori_loop(pad_lo_ref[k], pad_hi_ref[k], init, 0)
        return carry

    n_ranges = pad_lo_ref.shape[0]
    lax.fori_loop(0, n_ranges - 1, pad_range, 0)

    def pad_block(blk, carry):
        lax.fori_loop(0, MOE_BLOCK, lambda r, c: init(blk * MOE_BLOCK + r, c), 0, unroll=8)
        return carry

    tail_end = pad_hi_ref[n_ranges - 1]
    lax.fori_loop(pad_lo_ref[n_ranges - 1] // MOE_BLOCK, tail_end // MOE_BLOCK, pad_block, 0)
    lax.fori_loop(tail_end // MOE_BLOCK * MOE_BLOCK, tail_end, init, 0)

    def body(a, carry):
        inv_ref[dest_ref[a]] = a
        return carry

    lax.fori_loop(0, dest_ref.shape[0], body, 0, unroll=8)


def _moe_inverse(pad_lo, pad_hi, dest, rows):
    smem = pl.BlockSpec(memory_space=pltpu.SMEM)
    return pl.pallas_call(
        _moe_inverse_kernel,
        out_shape=jax.ShapeDtypeStruct((rows,), jnp.int32),
        in_specs=[smem, smem, smem],
        out_specs=smem,
        name="moe_inverse",
    )(pad_lo, pad_hi, dest)


def _moe(h2p, codes, weights, counts, layer, w_gu, b_gu, w_dn, b_dn):
    n = h2p.shape[0] // PACK
    n_assign = n * TOP_K
    n_blocks = -(-n_assign // MOE_BLOCK) + N_EXPERTS
    rows = n_blocks * MOE_BLOCK
    padded = -(-counts // MOE_BLOCK) * MOE_BLOCK
    ends = jnp.cumsum(padded)
    offsets = ends - padded
    starts = jnp.arange(n_blocks, dtype=jnp.int32) * MOE_BLOCK
    block_e = jnp.minimum(jnp.sum(ends[None, :] <= starts[:, None], axis=1), N_EXPERTS - 1).astype(jnp.int32)
    n_used = (ends[-1:] // MOE_BLOCK).astype(jnp.int32)
    expert = lax.shift_right_logical(codes, CODE_SHIFT)
    which = expert[:, None] == jnp.arange(N_EXPERTS, dtype=jnp.int32)[None, :]
    dest = (codes & ((1 << CODE_SHIFT) - 1)) + jnp.sum(jnp.where(which, offsets[None, :], 0), axis=1)
    spare = 8
    dest = jnp.where(codes >= 0, dest, rows)
    pad_lo = jnp.concatenate([offsets + counts, ends[-1:]]).astype(jnp.int32)
    pad_hi = jnp.concatenate([ends, jnp.full((1,), rows + spare, ends.dtype)]).astype(jnp.int32)
    inv = _moe_inverse(pad_lo, pad_hi, dest.astype(jnp.int32), rows + spare)[:rows]
    tok = lax.shift_right_logical(inv, TOP_K.bit_length() - 1)
    row_src = jnp.where(inv >= 0, tok, 0)
    row_dst = jnp.where(inv >= 0, tok, n)
    row_w = jnp.where(inv >= 0, jnp.take(weights, jnp.maximum(inv, 0)), 0.0)
    xs = _moe_gather(row_src, h2p, n_blocks)
    ys = _moe_mm(block_e, n_used, xs, row_w, layer, w_gu, b_gu, w_dn, b_dn)
    return _moe_combine(n_used, row_dst, ys, n + 8)


def _final_kernel(x_ref, acc_ref, g2_ref, g_ref, o_ref):
    x = x_ref[...] + g2_ref[...] * _load_acc(acc_ref)
    ms = jnp.mean(x * x, axis=-1, keepdims=True)
    o_ref[...] = x * lax.rsqrt(ms + EPS) * g_ref[...]


def _final(x, acc, g2, g, n_ctx_tiles):
    b, t, d = x.shape
    nt = t // TM - n_ctx_tiles
    tile_in = pl.BlockSpec((None, TM, d), lambda b_, t_: (b_, t_ + n_ctx_tiles, 0))
    acc_tile = _acc_tile_spec(acc, lambda b_, t_: b_ * (t // TM) + t_ + n_ctx_tiles)
    return pl.pallas_call(
        _final_kernel,
        out_shape=jax.ShapeDtypeStruct((b, nt * TM, d), F32),
        grid=(b, nt),
        in_specs=[tile_in, acc_tile,
                  pl.BlockSpec((None, None, 1, d), lambda b_, t_: (b_, 1, 0, 0)),
                  pl.BlockSpec((1, d), lambda b_, t_: (0, 0))],
        out_specs=pl.BlockSpec((None, TM, d), lambda b_, t_: (b_, t_, 0)),
        compiler_params=_cparams(("arbitrary", "arbitrary")),
        name="final_norm",
    )(x, acc, g2, g)


def _mod_tables(mod, b, d):
    outs = []
    for j in range(6):
        m = mod[:, j * d:(j + 1) * d]
        lat = m[:b]
        ctx = jnp.broadcast_to(m[b:b + 1], (b, d))
        outs.append(jnp.stack([ctx, lat], axis=1)[:, :, None, :])
    return outs


def _routing(code):
    codes = code[:, :, :TOP_K].reshape(-1)
    weights = lax.bitcast_convert_type(code[:, :, TOP_K:2 * TOP_K], F32).reshape(-1)
    return codes, weights


def _router_params(router_w, router_b):
    d, e = router_w.shape
    rw = jnp.zeros((d, 128), F32).at[:, :e].set(router_w).astype(BF16)
    rb = jnp.full((1, 128), NEG, F32).at[0, :e].set(router_b)
    return rw, rb


def kernel(x, c, ctx, c_ctx, mod_w, mod_b, norm1_g, norm2_g, ev_w_in, ev_conv_w, gdn_a_log, gdn_dt_bias,
           gdn_norm_g, ml_i_bias, ml_f_bias, ml_norm_g, ev_w_out, od_w_in, ret_norm_g, od_w_out,
           router_w, router_b, moe_w_gu, moe_b_gu, moe_w_dn, moe_b_dn, final_g):
    b, s, d = x.shape
    n_ctx = ctx.shape[1]
    depth = mod_w.shape[0]
    assert n_ctx == TM and s % TM == 0 and depth == 2 and b % SCAN_BG == 0
    t = n_ctx + s
    n_tok = b * t

    cond = jnp.concatenate([c, c_ctx[None, :], jnp.zeros((8 - b - 1, d), F32)], axis=0)
    mod = _adaln(cond, mod_w, mod_b)
    xa = jnp.concatenate([ctx, x], axis=1)

    sh1, sc1, g1, sh2, sc2, g2 = _mod_tables(mod[0], b, d)
    qk_w = GDN_H * HD
    conv_ch = 3 * qk_w
    ng = N_DIR * GDN_H
    w_in = ev_w_in[0]
    o_z = conv_ch
    o_a = o_z + qk_w
    o_mq = o_a + 2 * ng
    o_i = o_mq + 4 * qk_w
    w_main = jnp.concatenate([w_in[:, :o_a], w_in[:, o_mq:o_i]], axis=1).astype(BF16)
    w_gate = jnp.concatenate([w_in[:, o_a:o_mq], w_in[:, o_i:o_i + 2 * ng],
                              jnp.zeros((d, 128 - 4 * ng), F32)], axis=1).astype(BF16)
    zeros_g = jnp.zeros((ng,), F32)
    rate = jnp.concatenate([jnp.exp(gdn_a_log[0].astype(F32)).reshape(-1), jnp.zeros((128 - ng,), F32)])[None, :]
    gbias = jnp.concatenate([gdn_dt_bias[0].reshape(-1), zeros_g, ml_i_bias[0].reshape(-1),
                             ml_f_bias[0].reshape(-1), jnp.zeros((128 - 4 * ng,), F32)])[None, :].astype(F32)
    main, gates = _proj_even(xa, norm1_g[0][None, :], sh1, sc1, w_main, w_gate, ev_conv_w[0], rate, gbias)
    gates_c = gates.reshape(b, t // CH, CH, 4 * ng)
    gates_r = jnp.swapaxes(gates_c, 2, 3)
    og_f, og_b, om_f, om_b = _even_scan(main, gates_c, gates_r)
    rw, rb = _router_params(router_w[0], router_b[0])
    gdn_g = jnp.tile(gdn_norm_g[0], GDN_H)[None, :]
    x1, h2p, code, cnt = _merge_even(og_f, og_b, om_f, om_b, main, gdn_g, ml_norm_g[0][None, :],
                                     ev_w_out[0].astype(BF16), xa, g1, norm2_g[0][None, :], sh2, sc2, rw, rb)
    acc = _moe(h2p.reshape(n_tok * PACK, LANES), *_routing(code), cnt[0, :N_EXPERTS].astype(jnp.int32),
               0, moe_w_gu, moe_b_gu, moe_w_dn, moe_b_dn)
    g2_prev = g2

    sh1, sc1, g1, sh2, sc2, g2 = _mod_tables(mod[1], b, d)
    x2, main_o = _proj_odd(x1, acc, g2_prev, norm1_g[1][None, :], sh1, sc1, od_w_in[0].astype(BF16))
    o_f, o_b = _ret_scan(main_o)
    rw, rb = _router_params(router_w[1], router_b[1])
    x3, h2p, code, cnt = _merge_odd(o_f, o_b, main_o, ret_norm_g[0][None, :], od_w_out[0].astype(BF16),
                                    x2, g1, norm2_g[1][None, :], sh2, sc2, rw, rb)
    acc = _moe(h2p.reshape(n_tok * PACK, LANES), *_routing(code), cnt[0, :N_EXPERTS].astype(jnp.int32),
               1, moe_w_gu, moe_b_gu, moe_w_dn, moe_b_dn)
    return _final(x3, acc, g2, final_g[None, :], n_ctx // TM)
```

```python
import jax
import jax.numpy as jnp
import numpy as np
from jax import lax
from jax.experimental import pallas as pl
from jax.experimental.pallas import tpu as pltpu

F32 = jnp.float32
BF16 = jnp.bfloat16
U32 = jnp.uint32
HIGHEST = lax.Precision.HIGHEST

EPS = 1e-6
CH = 64
TM = 256
CPB = TM // CH
HD = 128
N_DIR = 2
GDN_H = 4
ML_H = 4
RET_H = 8
RET_DV = 256
CONV_W = 3
N_EXPERTS = 32
TOP_K = 4
SWIGLU_ALPHA = 1.702
SWIGLU_LIMIT = 7.0
MOE_BLOCK = 512
ROPE_BASE = 10000.0
NEG = -1e30
VMEM_LIMIT = 56 * 1024 * 1024


def _cparams(sem):
    return pltpu.CompilerParams(dimension_semantics=sem, vmem_limit_bytes=VMEM_LIMIT)


def _dot(a, b, precision=None):
    return jnp.dot(a, b, preferred_element_type=F32, precision=precision)


def _dot_nt(a, b):
    return lax.dot_general(a, b, (((1,), (1,)), ((), ())), preferred_element_type=F32)


def _dot_tn(a, b):
    return lax.dot_general(a, b, (((0,), (0,)), ((), ())), preferred_element_type=F32)


def _sigmoid(x):
    return 1.0 / (1.0 + jnp.exp(-x))


def _silu(x):
    return x * _sigmoid(x)


def _group_sum(x, w):
    outs = []
    for j in range(x.shape[1] // w):
        s = jnp.sum(x[:, j * w:(j + 1) * w], axis=-1, keepdims=True)
        outs.append(jnp.broadcast_to(s, (x.shape[0], w)))
    return outs[0] if len(outs) == 1 else jnp.concatenate(outs, axis=1)


def _rms_mod(x, g, sh, sc):
    ms = jnp.mean(x * x, axis=-1, keepdims=True)
    return (x * lax.rsqrt(ms + EPS) * g) * (1.0 + sc) + sh


LANES = 128
PACK = 4


def _store_row_packed(ref, x, lead=()):
    rows = x.shape[0]
    for g in range(x.shape[1] // LANES):
        ref[lead + (pl.ds(g, rows, stride=PACK), slice(None))] = x[:, g * LANES:(g + 1) * LANES]


def _load_row_packed(ref, rows, lead=()):
    return jnp.concatenate([ref[lead + (pl.ds(g, rows, stride=PACK), slice(None))] for g in range(PACK)], axis=1)


def _load_acc(acc_ref):
    rows = acc_ref.shape[1] // PACK
    return jnp.concatenate([_load_row_packed(acc_ref, rows, lead=(h,)) for h in range(acc_ref.shape[0])], axis=1)


def _acc_tile_spec(acc, tile_of):
    return pl.BlockSpec((acc.shape[0], TM * PACK, LANES), lambda b_, t_: (0, tile_of(b_, t_), 0))


def _adaln_kernel(c_ref, w_ref, b_ref, o_ref):
    c = c_ref[...]
    o_ref[...] = _dot(_silu(c), w_ref[...], precision=HIGHEST) + b_ref[...]


def _adaln(cond, mod_w, mod_b):
    depth, d, d6 = mod_w.shape
    n = d6 // d
    return pl.pallas_call(
        _adaln_kernel,
        out_shape=jax.ShapeDtypeStruct((depth, cond.shape[0], d6), F32),
        grid=(depth, n),
        in_specs=[pl.BlockSpec(cond.shape, lambda l, j: (0, 0)),
                  pl.BlockSpec((None, d, d), lambda l, j: (l, 0, j)),
                  pl.BlockSpec((None, 1, d), lambda l, j: (l, 0, j))],
        out_specs=pl.BlockSpec((None, cond.shape[0], d), lambda l, j: (l, 0, j)),
        compiler_params=_cparams(("arbitrary", "arbitrary")),
        name="adaln",
    )(cond, mod_w, mod_b.reshape(depth, 1, d6))


def _stream_tile(ctx_ref, x_ref):
    return jnp.where(pl.program_id(1) == 0, ctx_ref[...], x_ref[...])


def _stream_specs(d):
    return (pl.BlockSpec((None, TM, d), lambda b, t: (b, 0, 0)),
            pl.BlockSpec((None, TM, d), lambda b, t: (b, jnp.maximum(t - 1, 0), 0)))


def _proj_even_kernel(ctx_ref, x_ref, g_ref, sh_ref, sc_ref, w_ref, wg_ref, cw_ref, rate_ref, gb_ref,
                      main_ref, gates_ref):
    t = pl.program_id(1)
    h = _rms_mod(_stream_tile(ctx_ref, x_ref), g_ref[...], sh_ref[...], sc_ref[...])
    hb = h.astype(BF16)
    qk_w = GDN_H * HD

    row = lax.broadcasted_iota(jnp.int32, (TM, 1), 0)
    pos = jnp.where(t > 0, row & (CH - 1), row)
    last = jnp.where(t > 0, CH - 1, TM - 1)
    left_ok = pos != 0
    right_ok = pos != last
    for seg in range(3):
        sl = slice(seg * qk_w, (seg + 1) * qk_w)
        u = _dot(hb, w_ref[:, sl])
        um = jnp.where(left_ok, pltpu.roll(u, 1, 0), 0.0)
        up = jnp.where(right_ok, pltpu.roll(u, TM - 1, 0), 0.0)
        cv = _silu(um * cw_ref[0:1, sl] + u * cw_ref[1:2, sl] + up * cw_ref[2:3, sl])
        if seg < 2:
            ss = _group_sum(cv * cv, HD)
            cv = cv * lax.rsqrt(ss + EPS)
            if seg == 0:
                cv = cv * (HD ** -0.5)
        main_ref[:, sl] = cv.astype(BF16)
    for seg in range(3, 8):
        sl = slice(seg * qk_w, (seg + 1) * qk_w)
        u = _dot(hb, w_ref[:, sl])
        if seg == 5:
            u = u * (HD ** -0.5)
        main_ref[:, sl] = u.astype(BF16)

    z = _dot(hb, wg_ref[...]) + gb_ref[...]
    tl = jnp.log(1.0 + jnp.exp(-jnp.abs(z)))
    sp_pos = jnp.maximum(z, 0.0) + tl
    sp_neg = jnp.maximum(-z, 0.0) + tl
    lane = lax.broadcasted_iota(jnp.int32, z.shape, 1)
    ng = N_DIR * GDN_H
    res = jnp.where(lane < ng, -rate_ref[...] * sp_pos,
                    jnp.where(lane < 2 * ng, _sigmoid(z),
                              jnp.where(lane < 3 * ng, z, -sp_neg)))
    gates_ref[...] = res[:, :gates_ref.shape[-1]]


def _tile_mod_spec(d):
    return pl.BlockSpec((None, None, 1, d), lambda b, t: (b, jnp.minimum(t, 1), 0, 0))


def _proj_even(ctx, x, g, sh, sc, w_main, w_gate, conv_w, rate, gbias):
    b, s, d = x.shape
    t = ctx.shape[1] + s
    n = w_main.shape[1]
    ngl = 4 * N_DIR * GDN_H
    const = lambda shape: pl.BlockSpec(shape, lambda b_, t_: (0,) * len(shape))
    return pl.pallas_call(
        _proj_even_kernel,
        out_shape=(jax.ShapeDtypeStruct((b, t, n), BF16), jax.ShapeDtypeStruct((b, t, ngl), F32)),
        grid=(b, t // TM),
        in_specs=[*_stream_specs(d),
                  const((1, d)), _tile_mod_spec(d), _tile_mod_spec(d),
                  const(w_main.shape), const(w_gate.shape), const(conv_w.shape),
                  const(rate.shape), const(gbias.shape)],
        out_specs=(pl.BlockSpec((None, TM, n), lambda b_, t_: (b_, t_, 0)),
                   pl.BlockSpec((None, TM, ngl), lambda b_, t_: (b_, t_, 0))),
        compiler_params=_cparams(("arbitrary", "arbitrary")),
        name="proj_even",
    )(ctx, x, g, sh, sc, w_main, w_gate, conv_w, rate, gbias)


def _rev_tile(i, nt):
    return jnp.where(i == 0, 0, nt - i)


def _tri_masks():
    r = lax.broadcasted_iota(jnp.int32, (CH, CH), 0)
    c = lax.broadcasted_iota(jnp.int32, (CH, CH), 1)
    return r >= c, r > c, r <= c, r < c


SCAN_BG = 2
SOLVE_BASE = 8


def _scan_streams(heads):
    groups = [(bi, d) for bi in range(SCAN_BG) for d in range(N_DIR)]
    streams = [(gi, hh) for gi in range(len(groups)) for hh in range(heads)]
    return groups, streams


def _cumsum_both(groups, gcs, grs, lower, upper):
    tri = (lower.astype(F32), upper.astype(F32))
    cs_c = [_dot(tri[d], gc, precision=HIGHEST) for (_, d), gc in zip(groups, gcs)]
    cs_r = [_dot(gr, tri[1 - d], precision=HIGHEST) for (_, d), gr in zip(groups, grs)]
    return cs_c, cs_r


def _store_heads(refs, groups, rows, outs, heads):
    for gi, (bi, d) in enumerate(groups):
        o_ref = refs[d][-1]
        tile = jnp.concatenate(outs[gi * heads:(gi + 1) * heads], axis=1)
        o_ref[bi, rows[d], :] = tile.astype(o_ref.dtype)


class _GdnChunk:
    def __init__(self, refs, s_ref):
        self.refs, self.s_ref = refs, s_ref
        self.groups, self.streams = _scan_streams(GDN_H)

    def prelude(self, masks, rows, gcs, cs_c, cs_r):
        lower, lstrict, upper, ustrict = masks
        refs, groups = self.refs, self.groups
        ng = N_DIR * GDN_H
        st = []
        for gi, hh in self.streams:
            bi, d = groups[gi]
            ci = d * GDN_H + hh
            cols = slice(hh * HD, (hh + 1) * HD)
            incl, strict = (lower, lstrict) if d == 0 else (upper, ustrict)
            g_col = cs_c[gi][:, ci:ci + 1]
            g_row = cs_r[gi][ci:ci + 1, :]
            beta = gcs[gi][:, ng + ci:ng + ci + 1]
            tot = g_col[CH - 1:CH, :] if d == 0 else g_col[0:1, :]
            decay = jnp.where(incl, jnp.exp(jnp.where(incl, g_col - g_row, 0.0)), 0.0)
            q = refs[d][0][bi, rows[d], cols].astype(F32)
            k = refs[d][1][bi, rows[d], cols].astype(F32)
            v = refs[d][2][bi, rows[d], cols].astype(F32)
            kbeta = k * beta
            eg = jnp.exp(g_col)
            st.append(dict(
                strict=strict, decay=decay, kb=k.astype(BF16), kbetab=kbeta.astype(BF16), qb=q.astype(BF16),
                x=jnp.concatenate([v * beta, kbeta * eg], axis=1),
                qe=(q * eg).astype(BF16), kdec=(k * jnp.exp(tot - g_col)).astype(BF16), cd=jnp.exp(tot)))
        self.st = st

    def matmuls(self, rows):
        st, s_ref, streams = self.st, self.s_ref, self.streams
        kk = [_dot_nt(s["kbetab"], s["kb"]) for s in st]
        qk = [(_dot_nt(s["qb"], s["kb"]) * s["decay"]).astype(BF16) for s in st]
        r = lax.broadcasted_iota(jnp.int32, (CH, CH), 0)
        c = lax.broadcasted_iota(jnp.int32, (CH, CH), 1)
        eye = jnp.where(r == c, 1.0, 0.0)
        blocks_differ = lambda w: lax.shift_right_logical(r, w.bit_length() - 1) ^ \
            lax.shift_right_logical(c, w.bit_length() - 1)
        a = [jnp.where(s["strict"], m * s["decay"], 0.0) for s, m in zip(st, kk)]
        p = [-jnp.where(blocks_differ(SOLVE_BASE) == 0, m, 0.0) for m in a]
        t = [eye + m for m in p]
        w = 2
        while w < SOLVE_BASE:
            pb = [m.astype(BF16) for m in p]
            p = [_dot(m, m) for m in pb]
            t = [tt + _dot(tt.astype(BF16), m.astype(BF16)) for tt, m in zip(t, p)]
            w *= 2
        w = SOLVE_BASE
        while w < CH:
            off = blocks_differ(w) == 1
            tb = [tt.astype(BF16) for tt in t]
            ta = [_dot(tt, jnp.where(off, m, 0.0).astype(BF16)) for tt, m in zip(tb, a)]
            t = [tt - _dot(m.astype(BF16), tt2) for tt, m, tt2 in zip(t, ta, tb)]
            w *= 2
        x = [_dot(tt.astype(BF16), s["x"].astype(BF16)) for tt, s in zip(t, st)]
        s_old = [s_ref[si] for si in range(len(streams))]
        sb = [s.astype(BF16) for s in s_old]
        vnb = [(xx[:, :HD] - _dot(xx[:, HD:].astype(BF16), s)).astype(BF16) for xx, s in zip(x, sb)]
        o = [_dot(s["qe"], sbi) + _dot(m, vn) for s, sbi, m, vn in zip(st, sb, qk, vnb)]
        s_new = [so * s["cd"] + _dot_tn(s["kdec"], vn) for s, so, vn in zip(st, s_old, vnb)]
        for si, s in enumerate(s_new):
            s_ref[si] = s
        _store_heads(self.refs, self.groups, rows, o, GDN_H)


def _scan_specs(nt, width, col):
    fwd = pl.BlockSpec((SCAN_BG, TM, width), lambda b, i: (b, i, col))
    bwd = pl.BlockSpec((SCAN_BG, TM, width), lambda b, i: (b, _rev_tile(i, nt), col))
    return fwd, bwd


def _gate_specs(nt, shape):
    fwd = pl.BlockSpec((SCAN_BG, CPB) + shape, lambda b, i: (b, i, 0, 0))
    bwd = pl.BlockSpec((SCAN_BG, CPB) + shape, lambda b, i: (b, _rev_tile(i, nt), 0, 0))
    return fwd, bwd


class _MlstmChunk:
    def __init__(self, refs, c_ref, m_ref):
        self.refs, self.c_ref, self.m_ref = refs, c_ref, m_ref
        self.groups, self.streams = _scan_streams(ML_H)

    def prelude(self, masks, rows, gcs, grs, cs_c, cs_r):
        lower, _, upper, _ = masks
        refs, groups, streams, m_ref = self.refs, self.groups, self.streams, self.m_ref
        ng = N_DIR * GDN_H
        i_off = 2 * ng
        f_off = 2 * ng + N_DIR * ML_H
        lane = lax.broadcasted_iota(jnp.int32, (CH, HD), 1)
        ones_col = jnp.where(lane == 0, 1.0, 0.0).astype(BF16)
        ns = len(streams)
        dirs = [groups[gi][1] for gi, _ in streams]
        chan = [groups[gi][1] * ML_H + hh for gi, hh in streams]
        b_col = [cs_c[gi][:, f_off + c:f_off + c + 1] for (gi, _), c in zip(streams, chan)]
        b_row = [cs_r[gi][f_off + c:f_off + c + 1, :] for (gi, _), c in zip(streams, chan)]
        i_col = [gcs[gi][:, i_off + c:i_off + c + 1] for (gi, _), c in zip(streams, chan)]
        i_row = [grs[gi][i_off + c:i_off + c + 1, :] for (gi, _), c in zip(streams, chan)]
        b_tot = [bc[CH - 1:CH, :] if d == 0 else bc[0:1, :] for bc, d in zip(b_col, dirs)]
        m_rows = [m_ref[si] for si in range(ns)]
        m_old = [mr[:, 0:1] for mr in m_rows]
        d_in = [jnp.where(lower if d == 0 else upper, bc - br + ir, NEG)
                for d, bc, br, ir in zip(dirs, b_col, b_row, i_row)]
        d_end = [bt - br + ir for bt, br, ir in zip(b_tot, b_row, i_row)]
        mx_in = [jnp.max(a, axis=-1, keepdims=True) for a in d_in]
        mx_end = [jnp.max(a, axis=-1, keepdims=True) for a in d_end]
        d_carry = [bc + m for bc, m in zip(b_col, m_old)]
        m_t = [jnp.maximum(a, b_) for a, b_ in zip(d_carry, mx_in)]
        carry_end = [bt + m for bt, m in zip(b_tot, m_old)]
        m_new = [jnp.maximum(a, b_) for a, b_ in zip(carry_end, mx_end)]
        p_in = [jnp.exp(a - b_) for a, b_ in zip(d_in, m_t)]
        w_end = [jnp.exp(bt - bc + ic - mn) for bt, bc, ic, mn in zip(b_tot, b_col, i_col, m_new)]
        st = []
        for si, (gi, hh) in enumerate(streams):
            bi, d = groups[gi]
            cols = slice(hh * HD, (hh + 1) * HD)
            k = refs[d][1][bi, rows[d], cols]
            v = refs[d][2][bi, rows[d], cols]
            st.append(dict(
                q=refs[d][0][bi, rows[d], cols], k=k,
                v_aug=jnp.concatenate([v, ones_col], axis=1),
                p_in=p_in[si], w_carry=jnp.exp(d_carry[si] - m_t[si]), floor=jnp.exp(-m_t[si]),
                kw=(k.astype(F32) * w_end[si]).astype(BF16),
                f_end=jnp.exp(carry_end[si] - m_new[si]),
                m_new=jnp.broadcast_to(m_new[si], m_rows[si].shape)))
        self.st = st

    def matmuls(self, rows):
        st, c_ref, m_ref, streams = self.st, self.c_ref, self.m_ref, self.streams
        sc = [(_dot_nt(s["q"], s["k"]) * s["p_in"]).astype(BF16) for s in st]
        c_old = [c_ref[si] for si in range(len(streams))]
        qc = [_dot(s["q"], c.astype(BF16)) for s, c in zip(st, c_old)]
        nd = [s["w_carry"] * a + _dot(m, s["v_aug"]) for s, a, m in zip(st, qc, sc)]
        hout = [a[:, :HD] / jnp.maximum(jnp.abs(a[:, HD:HD + 1]), s["floor"]) for s, a in zip(st, nd)]
        c_new = [s["f_end"] * c + _dot_tn(s["kw"], s["v_aug"]) for s, c in zip(st, c_old)]
        for si, (s, c) in enumerate(zip(st, c_new)):
            c_ref[si] = c
            m_ref[si] = s["m_new"]
        _store_heads(self.refs, self.groups, rows, hout, ML_H)


def _even_scan_kernel(gqf, gkf, gvf, gqb, gkb, gvb, mqf, mkf, mvf, mqb, mkb, mvb, gcf, grf, gcb, grb,
                      ogf, ogb, omf, omb, s_ref, c_ref, m_ref):
    i = pl.program_id(1)

    @pl.when(i == 0)
    def _():
        s_ref[...] = jnp.zeros_like(s_ref)
        c_ref[...] = jnp.zeros_like(c_ref)
        m_ref[...] = jnp.zeros_like(m_ref)

    masks = _tri_masks()
    gdn = _GdnChunk(((gqf, gkf, gvf, ogf), (gqb, gkb, gvb, ogb)), s_ref)
    mls = _MlstmChunk(((mqf, mkf, mvf, omf), (mqb, mkb, mvb, omb)), c_ref, m_ref)
    gate_refs = ((gcf, grf), (gcb, grb))

    def chunk_body(cc, carry):
        cidx = (cc, CPB - 1 - cc)
        rows = tuple(pl.ds(pl.multiple_of(c * CH, CH), CH) for c in cidx)
        gcs = [gate_refs[d][0][bi, cidx[d]] for bi, d in gdn.groups]
        grs = [gate_refs[d][1][bi, cidx[d]] for bi, d in gdn.groups]
        cs_c, cs_r = _cumsum_both(gdn.groups, gcs, grs, masks[0], masks[2])
        gdn.prelude(masks, rows, gcs, cs_c, cs_r)
        mls.prelude(masks, rows, gcs, grs, cs_c, cs_r)
        gdn.matmuls(rows)
        mls.matmuls(rows)
        return carry

    lax.fori_loop(0, CPB, chunk_body, 0)


def _even_scan(main, gates_c, gates_r):
    b, t, _ = main.shape
    nt = t // TM
    w = GDN_H * HD
    ngl = gates_c.shape[-1]
    qkv = lambda cols: [_scan_specs(nt, w, c)[d] for d in range(N_DIR) for c in cols]
    gcf, gcb = _gate_specs(nt, (CH, ngl))
    grf, grb = _gate_specs(nt, (ngl, CH))
    of, ob = _scan_specs(nt, w, 0)
    out = jax.ShapeDtypeStruct((b, t, w), BF16)
    n_streams = SCAN_BG * N_DIR * GDN_H
    return pl.pallas_call(
        _even_scan_kernel,
        out_shape=(out, out, out, out),
        grid=(b // SCAN_BG, nt),
        in_specs=qkv((0, 1, 2)) + qkv((4, 5, 6)) + [gcf, grf, gcb, grb],
        out_specs=(of, ob, of, ob),
        scratch_shapes=[pltpu.VMEM((n_streams, HD, HD), F32), pltpu.VMEM((n_streams, HD, 2 * HD), F32),
                        pltpu.VMEM((n_streams, 1, HD), F32)],
        compiler_params=_cparams(("arbitrary", "arbitrary")),
        name="even_scan",
    )(*([main] * 12), gates_c, gates_r, gates_c, gates_r)


def _ret_kernel(qf, kf, vf, qb, kb, vb, intra_ref, cross_ref, tail_ref, cd_ref, of, ob, s_ref):
    i = pl.program_id(1)

    @pl.when(i == 0)
    def _():
        s_ref[...] = jnp.zeros_like(s_ref)

    refs = ((qf, kf, vf, of), (qb, kb, vb, ob))
    groups, streams = _scan_streams(RET_H)

    def chunk_body(cc, carry):
        cidx = (cc, CPB - 1 - cc)
        rows = tuple(pl.ds(pl.multiple_of(c * CH, CH), CH) for c in cidx)
        st = []
        for gi, hh in streams:
            bi, d = groups[gi]
            ti = d * RET_H + hh
            kcols = slice(hh * HD, (hh + 1) * HD)
            q = refs[d][0][bi, rows[d], kcols]
            k = refs[d][1][bi, rows[d], kcols]
            st.append(dict(
                ti=ti, q=q, k=k, v=refs[d][2][bi, rows[d], slice(hh * RET_DV, (hh + 1) * RET_DV)],
                qc=(q.astype(F32) * cross_ref[ti]).astype(BF16),
                kt=(k.astype(F32) * tail_ref[ti]).astype(BF16)))
        sc = [(_dot_nt(s["q"], s["k"]) * intra_ref[s["ti"]]).astype(BF16) for s in st]
        s_old = [s_ref[si] for si in range(len(streams))]
        o = [_dot(m, s["v"]) + _dot(s["qc"], so.astype(BF16)) for s, m, so in zip(st, sc, s_old)]
        s_new = [cd_ref[s["ti"]] * so + _dot_tn(s["kt"], s["v"]) for s, so in zip(st, s_old)]
        for si, s in enumerate(s_new):
            s_ref[si] = s
        _store_heads(refs, groups, rows, o, RET_H)
        return carry

    lax.fori_loop(0, CPB, chunk_body, 0)


def _ret_tables():
    pos = np.arange(CH, dtype=np.float64)
    intra, cross, tail, cd = [], [], [], []
    for d in range(N_DIR):
        expo = 5.0 + np.arange(RET_H, dtype=np.float64)
        if d == 1:
            expo = expo[::-1]
        lg = np.log1p(-np.exp2(-expo))
        p = pos if d == 0 else (CH - 1.0 - pos)
        diff = p[:, None] - p[None, :]
        for hh in range(RET_H):
            intra.append(np.where(diff >= 0, np.exp(np.where(diff >= 0, diff, 0.0) * lg[hh]), 0.0))
            cross.append(np.broadcast_to(np.exp((p + 1.0) * lg[hh])[:, None], (CH, HD)))
            tail.append(np.broadcast_to(np.exp((CH - 1.0 - p) * lg[hh])[:, None], (CH, HD)))
            cd.append(np.full((1, RET_DV), np.exp(CH * lg[hh])))
    f = lambda a: jnp.asarray(np.stack(a), F32)
    return f(intra), f(cross), f(tail), f(cd)


def _ret_scan(main):
    b, t, _ = main.shape
    nt = t // TM
    qw = RET_H * HD
    vw = RET_H * RET_DV
    qf, qb = _scan_specs(nt, qw, 0)
    kf, kb = _scan_specs(nt, qw, 1)
    vf, vb = _scan_specs(nt, vw, 1)
    of, ob = _scan_specs(nt, vw, 0)
    tabs = _ret_tables()
    const = lambda a: pl.BlockSpec(a.shape, lambda b_, i_: (0,) * a.ndim)
    return pl.pallas_call(
        _ret_kernel,
        out_shape=(jax.ShapeDtypeStruct((b, t, vw), BF16), jax.ShapeDtypeStruct((b, t, vw), BF16)),
        grid=(b // SCAN_BG, nt),
        in_specs=[qf, kf, vf, qb, kb, vb] + [const(a) for a in tabs],
        out_specs=(of, ob),
        scratch_shapes=[pltpu.VMEM((SCAN_BG * N_DIR * RET_H, HD, RET_DV), F32)],
        compiler_params=_cparams(("arbitrary", "arbitrary")),
        name="ret_scan",
    )(main, main, main, main, main, main, *tabs)


CODE_SHIFT = 17


def _route_and_pack(h2, rw_ref, rb_ref, h2p_ref, code_ref, cnt_ref, base_ref, skip_context):
    h2b = h2.astype(BF16)
    logits = _dot(h2b, rw_ref[...]) + rb_ref[...]
    lane = lax.broadcasted_iota(jnp.int32, logits.shape, 1)
    lane_f = lane.astype(F32)
    vals, idxs = [], []
    cur = logits
    for _ in range(TOP_K):
        m = jnp.max(cur, axis=-1, keepdims=True)
        ix = jnp.min(jnp.where(cur == m, lane_f, float(logits.shape[1])), axis=-1, keepdims=True)
        vals.append(m)
        idxs.append(ix)
        cur = jnp.where(lane_f == ix, NEG, cur)
    es = [jnp.exp(v - vals[0]) for v in vals]
    tot = es[0] + es[1] + es[2] + es[3]

    @pl.when((pl.program_id(0) == 0) & (pl.program_id(1) == 0))
    def _():
        base_ref[...] = jnp.zeros_like(base_ref)

    tm = logits.shape[0]
    routed = pl.program_id(1) > 0
    onehot = jnp.zeros(logits.shape, F32)
    for j in range(TOP_K):
        onehot = jnp.where(lane_f == idxs[j], 1.0, onehot)
    if skip_context:
        onehot = onehot * jnp.where(routed, 1.0, 0.0)
    r_i = lax.broadcasted_iota(jnp.int32, (tm, tm), 0)
    c_i = lax.broadcasted_iota(jnp.int32, (tm, tm), 1)
    before = _dot(jnp.where(r_i > c_i, 1.0, 0.0).astype(BF16), onehot.astype(BF16))
    base = base_ref[...]
    pos = before + base
    total = base + before[tm - 1:tm, :] + onehot[tm - 1:tm, :]
    base_ref[...] = total
    cnt_ref[...] = total

    code = jnp.zeros(logits.shape, F32)
    wgt = jnp.zeros(logits.shape, F32)
    for j in range(TOP_K):
        rank = jnp.sum(jnp.where(lane_f == idxs[j], pos, 0.0), axis=-1, keepdims=True)
        code = jnp.where(lane == j, idxs[j] * float(1 << CODE_SHIFT) + rank, code)
        wgt = jnp.where(lane == TOP_K + j, es[j] / tot, wgt)
    code_i = code.astype(jnp.int32)
    if skip_context:
        code_i = jnp.where(routed, code_i, -1)
    code_ref[...] = jnp.where(lane < TOP_K, code_i, pltpu.bitcast(wgt, jnp.int32))
    half = h2.shape[1] // 2
    r = h2b.astype(F32)
    lo = lax.shift_right_logical(pltpu.bitcast(r[:, :half], U32), jnp.uint32(16))
    hi = pltpu.bitcast(r[:, half:], U32) & jnp.uint32(0xFFFF0000)
    _store_row_packed(h2p_ref, hi | lo)


def _merge_even_kernel(ogf, ogb, omf, omb, z_ref, mo_ref, gg_ref, mg_ref, wo_ref, ctx_ref, x_ref, g1_ref,
                       n2_ref, sh_ref, sc_ref, rw_ref, rb_ref, x1_ref, h2p_ref, code_ref, cnt_ref, base_ref):
    og = ogf[...].astype(F32) + ogb[...].astype(F32)
    ms = _group_sum(og * og, HD) * (1.0 / HD)
    a = og * lax.rsqrt(ms + EPS) * gg_ref[...] * _silu(z_ref[...].astype(F32))
    om = omf[...].astype(F32) + omb[...].astype(F32)
    ms = _group_sum(om * om, HD) * (1.0 / HD)
    m = om * lax.rsqrt(ms + EPS) * mg_ref[...] * _sigmoid(mo_ref[...].astype(F32))
    cat = jnp.concatenate([a, m], axis=1).astype(BF16)
    y = _dot(cat, wo_ref[...])
    x1 = _stream_tile(ctx_ref, x_ref) + g1_ref[...] * y
    x1_ref[...] = x1
    h2 = _rms_mod(x1, n2_ref[...], sh_ref[...], sc_ref[...])
    _route_and_pack(h2, rw_ref, rb_ref, h2p_ref, code_ref, cnt_ref, base_ref, skip_context=False)


def _merge_odd_kernel(of, ob, gate_ref, ng_ref, wo_ref, x_ref, g1_ref,
                      n2_ref, sh_ref, sc_ref, rw_ref, rb_ref, x1_ref, h2p_ref, code_ref, cnt_ref, base_ref):
    o = of[...].astype(F32) + ob[...].astype(F32)
    o = o - _group_sum(o, RET_DV) * (1.0 / RET_DV)
    ms = _group_sum(o * o, RET_DV) * (1.0 / RET_DV)
    y = o * lax.rsqrt(ms + EPS) * ng_ref[...] * _silu(gate_ref[...].astype(F32))
    y = _dot(y.astype(BF16), wo_ref[...])
    x1 = x_ref[...] + g1_ref[...] * y
    x1_ref[...] = x1
    h2 = _rms_mod(x1, n2_ref[...], sh_ref[...], sc_ref[...])
    _route_and_pack(h2, rw_ref, rb_ref, h2p_ref, code_ref, cnt_ref, base_ref, skip_context=True)


def _merge_out(b, t, d):
    assert d // 2 == PACK * LANES
    shapes = (jax.ShapeDtypeStruct((b, t, d), F32), jax.ShapeDtypeStruct((b, t * PACK, LANES), U32),
              jax.ShapeDtypeStruct((b, t, LANES), jnp.int32), jax.ShapeDtypeStruct((1, LANES), F32))
    tile = lambda rows, width: pl.BlockSpec((None, rows, width), lambda b_, t_: (b_, t_, 0))
    specs = (tile(TM, d), tile(TM * PACK, LANES), tile(TM, LANES), pl.BlockSpec((1, LANES), lambda b_, t_: (0, 0)))
    return shapes, specs


def _merge_even(og_f, og_b, om_f, om_b, main, gdn_g, ml_g, w_out, ctx, x, g1, n2, sh2, sc2, rw, rb):
    b, s, d = x.shape
    t = ctx.shape[1] + s
    w = GDN_H * HD
    tile = lambda width, col: pl.BlockSpec((None, TM, width), lambda b_, t_: (b_, t_, col))
    const = lambda a: pl.BlockSpec(a.shape, lambda b_, t_: (0,) * a.ndim)
    shapes, specs = _merge_out(b, t, d)
    return pl.pallas_call(
        _merge_even_kernel,
        out_shape=shapes,
        grid=(b, t // TM),
        in_specs=[tile(w, 0), tile(w, 0), tile(w, 0), tile(w, 0), tile(w, 3), tile(w, 7),
                  const(gdn_g), const(ml_g), const(w_out), *_stream_specs(d), _tile_mod_spec(d),
                  const(n2), _tile_mod_spec(d), _tile_mod_spec(d), const(rw), const(rb)],
        out_specs=specs,
        scratch_shapes=[pltpu.VMEM((1, 128), F32)],
        compiler_params=_cparams(("arbitrary", "arbitrary")),
        name="merge_even",
    )(og_f, og_b, om_f, om_b, main, main, gdn_g, ml_g, w_out, ctx, x, g1, n2, sh2, sc2, rw, rb)


def _merge_odd(o_f, o_b, main, ret_g, w_out, x, g1, n2, sh2, sc2, rw, rb):
    b, t, d = x.shape
    vw = RET_H * RET_DV
    tile = lambda width, col: pl.BlockSpec((None, TM, width), lambda b_, t_: (b_, t_, col))
    const = lambda a: pl.BlockSpec(a.shape, lambda b_, t_: (0,) * a.ndim)
    shapes, specs = _merge_out(b, t, d)
    return pl.pallas_call(
        _merge_odd_kernel,
        out_shape=shapes,
        grid=(b, t // TM),
        in_specs=[tile(vw, 0), tile(vw, 0), tile(vw, 2), const(ret_g), const(w_out), tile(d, 0),
                  _tile_mod_spec(d), const(n2), _tile_mod_spec(d), _tile_mod_spec(d), const(rw), const(rb)],
        out_specs=specs,
        scratch_shapes=[pltpu.VMEM((1, 128), F32)],
        compiler_params=_cparams(("arbitrary", "arbitrary")),
        name="merge_odd",
    )(o_f, o_b, main, ret_g, w_out, x, g1, n2, sh2, sc2, rw, rb)


def _proj_odd_kernel(x_ref, acc_ref, g2_ref, g_ref, sh_ref, sc_ref, w_ref, cos_ref, sin_ref,
                     x2_ref, main_ref):
    x2 = x_ref[...] + g2_ref[...] * _load_acc(acc_ref)
    x2_ref[...] = x2
    hb = _rms_mod(x2, g_ref[...], sh_ref[...], sc_ref[...]).astype(BF16)
    qk_w = RET_H * HD
    cos = cos_ref[...]
    sin = sin_ref[...]
    n_seg = w_ref.shape[1] // qk_w
    for seg in range(n_seg):
        sl = slice(seg * qk_w, (seg + 1) * qk_w)
        u = _dot(hb, w_ref[:, sl])
        if seg < 2:
            parts = []
            for hh in range(RET_H):
                uh = u[:, hh * HD:(hh + 1) * HD]
                parts.append(uh * cos + pltpu.roll(uh, HD // 2, 1) * sin)
            u = jnp.concatenate(parts, axis=1)
            if seg == 1:
                u = u * (HD ** -0.5)
        main_ref[:, sl] = u.astype(BF16)


def _rope_tables(t):
    half = HD // 2
    freqs = ROPE_BASE ** (-jnp.arange(half, dtype=F32) / half)
    ang = jnp.arange(t, dtype=F32)[:, None] * freqs[None, :]
    cos, sin = jnp.cos(ang), jnp.sin(ang)
    return jnp.concatenate([cos, cos], axis=1), jnp.concatenate([-sin, sin], axis=1)


def _proj_odd(x, acc, g2, g, sh, sc, w):
    b, t, d = x.shape
    n = w.shape[1]
    cos, sin = _rope_tables(t)
    const = lambda a: pl.BlockSpec(a.shape, lambda b_, t_: (0,) * a.ndim)
    tile = lambda width: pl.BlockSpec((None, TM, width), lambda b_, t_: (b_, t_, 0))
    rope = pl.BlockSpec((TM, HD), lambda b_, t_: (t_, 0))
    nt = t // TM
    acc_tile = _acc_tile_spec(acc, lambda b_, t_: b_ * nt + t_)
    return pl.pallas_call(
        _proj_odd_kernel,
        out_shape=(jax.ShapeDtypeStruct((b, t, d), F32), jax.ShapeDtypeStruct((b, t, n), BF16)),
        grid=(b, nt),
        in_specs=[tile(d), acc_tile, _tile_mod_spec(d), const(g), _tile_mod_spec(d), _tile_mod_spec(d),
                  const(w), rope, rope],
        out_specs=(tile(d), tile(n)),
        compiler_params=_cparams(("arbitrary", "arbitrary")),
        name="proj_odd",
    )(x, acc, g2, g, sh, sc, w, cos, sin)


ROWS_PER_STEP = 8


def _packed_row(r):
    return pl.ds(pl.multiple_of(r * PACK, PACK), PACK)


def _move_rows(n_blocks):
    return MOE_BLOCK * next(k for k in (4, 2, 1) if n_blocks % k == 0)


def _moe_gather_kernel(src_ref, h_ref, xs_ref):
    def body(r8, carry):
        base = r8 * ROWS_PER_STEP
        rows = [h_ref[_packed_row(src_ref[0, base + j]), :] for j in range(ROWS_PER_STEP)]
        for j in range(ROWS_PER_STEP):
            xs_ref[_packed_row(base + j), :] = rows[j]
        return carry

    lax.fori_loop(0, src_ref.shape[1] // ROWS_PER_STEP, body, 0)


def _moe_gather(row_src, h2p, n_blocks):
    step = _move_rows(n_blocks)
    n_steps = n_blocks * MOE_BLOCK // step
    return pl.pallas_call(
        _moe_gather_kernel,
        out_shape=jax.ShapeDtypeStruct((n_blocks * MOE_BLOCK * PACK, LANES), U32),
        grid=(n_steps,),
        in_specs=[pl.BlockSpec((None, 1, step), lambda i: (i, 0, 0), memory_space=pltpu.SMEM),
                  pl.BlockSpec(h2p.shape, lambda i: (0, 0))],
        out_specs=pl.BlockSpec((step * PACK, LANES), lambda i: (i, 0)),
        compiler_params=_cparams(("arbitrary",)),
        name="moe_gather",
    )(row_src.reshape(n_steps, 1, step), h2p)


def _moe_mm_kernel(be_ref, nu_ref, xs_ref, rw_ref, wgu_ref, bgu_ref, wdn_ref, bdn_ref, ys_ref, wgu_s, wdn_s):
    i = pl.program_id(0)
    e = be_ref[i]
    half = wgu_ref.shape[0] // 2
    hl = LANES // 2

    @pl.when(i >= nu_ref[0])
    def _():
        ys_ref[...] = jnp.zeros_like(ys_ref)

    @pl.when(i < nu_ref[0])
    def _():
        @pl.when((i == 0) | (e != be_ref[jnp.maximum(i - 1, 0)]))
        def _():
            wgu_s[...] = wgu_ref[...].astype(BF16)
            for p in range(wdn_ref.shape[0] // LANES):
                first = wdn_ref[p * LANES:p * LANES + hl, :].astype(BF16).astype(F32)
                second = wdn_ref[p * LANES + hl:(p + 1) * LANES, :].astype(BF16).astype(F32)
                word = (pltpu.bitcast(second, U32) & jnp.uint32(0xFFFF0000)) | \
                    lax.shift_right_logical(pltpu.bitcast(first, U32), jnp.uint32(16))
                wdn_s[p * LANES:(p + 1) * LANES, :] = pltpu.bitcast(word, BF16)

        xu = _load_row_packed(xs_ref, MOE_BLOCK)
        lo = pltpu.bitcast(lax.shift_left(xu, jnp.uint32(16)), F32).astype(BF16)
        hi = pltpu.bitcast(xu & jnp.uint32(0xFFFF0000), F32).astype(BF16)
        gu = _dot(lo, wgu_s[:half, :]) + _dot(hi, wgu_s[half:, :]) + bgu_ref[...]
        even = (lax.broadcasted_iota(jnp.int32, (gu.shape[0], LANES), 1) & 1) == 0
        acts = []
        for p in range(gu.shape[1] // (2 * LANES)):
            a = gu[:, 2 * p * LANES:(2 * p + 1) * LANES]
            b = gu[:, (2 * p + 1) * LANES:(2 * p + 2) * LANES]
            gate = jnp.minimum(jnp.where(even, a, pltpu.roll(b, 1, 1)), SWIGLU_LIMIT)
            up = jnp.clip(jnp.where(even, pltpu.roll(a, LANES - 1, 1), b), -SWIGLU_LIMIT, SWIGLU_LIMIT)
            acts.append(((up + 1.0) * gate * _sigmoid(SWIGLU_ALPHA * gate)).astype(BF16))
        y = _dot(jnp.concatenate(acts, axis=1), wdn_s[...]) + bdn_ref[...]
        y = y * jnp.broadcast_to(rw_ref[...], (8, rw_ref.shape[1])).T[:, 0:1]
        cw = PACK * LANES
        for h in range(y.shape[1] // cw):
            _store_row_packed(ys_ref, y[:, h * cw:(h + 1) * cw], lead=(h,))


def _moe_mm(block_e, n_used, xs, row_w, layer, w_gu, b_gu, w_dn, b_dn):
    n_blocks = xs.shape[0] // (MOE_BLOCK * PACK)
    depth, n_e, d, f2 = w_gu.shape
    n_half = d // (PACK * LANES)
    ew = lambda shape: pl.BlockSpec((None, None) + shape, lambda i, be, nu: (layer, be[i], 0, 0))
    grid_spec = pltpu.PrefetchScalarGridSpec(
        num_scalar_prefetch=2,
        grid=(n_blocks,),
        in_specs=[pl.BlockSpec((MOE_BLOCK * PACK, LANES), lambda i, be, nu: (i, 0)),
                  pl.BlockSpec((None, 1, MOE_BLOCK), lambda i, be, nu: (i, 0, 0)),
                  ew((d, f2)), ew((1, f2)), ew((f2 // 2, d)), ew((1, d))],
        out_specs=pl.BlockSpec((n_half, MOE_BLOCK * PACK, LANES), lambda i, be, nu: (0, i, 0)),
        scratch_shapes=[pltpu.VMEM((d, f2), BF16), pltpu.VMEM((f2 // 2, d), BF16)],
    )
    return pl.pallas_call(
        _moe_mm_kernel,
        out_shape=jax.ShapeDtypeStruct((n_half, xs.shape[0], LANES), F32),
        grid_spec=grid_spec,
        compiler_params=_cparams(("arbitrary",)),
        name="moe_mm",
    )(block_e, n_used, xs, row_w.reshape(n_blocks, 1, MOE_BLOCK), w_gu, b_gu.reshape(depth, n_e, 1, f2),
      w_dn, b_dn.reshape(depth, n_e, 1, d))


def _moe_combine_kernel(nv_ref, dst_ref, ys_ref, acc_ref):
    i = pl.program_id(1)

    @pl.when(i == 0)
    def _():
        acc_ref[...] = jnp.zeros_like(acc_ref)

    def body(r8, carry):
        base = r8 * ROWS_PER_STEP
        dst = [_packed_row(dst_ref[0, base + j]) for j in range(ROWS_PER_STEP)]
        group = ROWS_PER_STEP * PACK
        ys = ys_ref[pl.ds(pl.multiple_of(r8 * group, group), group), :]
        new = [acc_ref[dst[j], :] + ys[j * PACK:(j + 1) * PACK, :] for j in range(ROWS_PER_STEP)]
        for j in range(ROWS_PER_STEP):
            acc_ref[dst[j], :] = new[j]
        return carry

    blocks_per_step = dst_ref.shape[1] // MOE_BLOCK
    groups_per_block = MOE_BLOCK // ROWS_PER_STEP
    for kb in range(blocks_per_step):
        groups = (nv_ref[i * blocks_per_step + kb] + ROWS_PER_STEP - 1) // ROWS_PER_STEP
        lax.fori_loop(kb * groups_per_block, kb * groups_per_block + groups, body, 0)


def _moe_combine(n_valid, row_dst, ys, n_rows_out):
    n_half, packed_rows, _ = ys.shape
    step = _move_rows(packed_rows // (MOE_BLOCK * PACK))
    n_blocks = packed_rows // (step * PACK)
    idx_spec = pl.BlockSpec((None, 1, step), lambda j, i, nu: (i, 0, 0), memory_space=pltpu.SMEM)
    grid_spec = pltpu.PrefetchScalarGridSpec(
        num_scalar_prefetch=1,
        grid=(n_half, n_blocks),
        in_specs=[idx_spec, pl.BlockSpec((None, step * PACK, LANES), lambda j, i, nu: (j, i, 0))],
        out_specs=pl.BlockSpec((None, n_rows_out * PACK, LANES), lambda j, i, nu: (j, 0, 0),
                               pipeline_mode=pl.Buffered(1)),
    )
    return pl.pallas_call(
        _moe_combine_kernel,
        out_shape=jax.ShapeDtypeStruct((n_half, n_rows_out * PACK, LANES), F32),
        grid_spec=grid_spec,
        compiler_params=_cparams(("arbitrary", "arbitrary")),
        name="moe_combine",
    )(n_valid, row_dst.reshape(n_blocks, 1, step), ys)


def _moe_inverse_kernel(pad_lo_ref, pad_hi_ref, dest_ref, inv_ref):
    def init(r, c):
        inv_ref[r] = -1
        return c

    def pad_range(k, carry):
        lax.fori_loop(pad_lo_ref[k], pad_hi_ref[k], init, 0)
        return carry

    n_ranges = pad_lo_ref.shape[0]
    lax.fori_loop(0, n_ranges - 1, pad_range, 0)

    def pad_block(blk, carry):
        lax.fori_loop(0, MOE_BLOCK, lambda r, c: init(blk * MOE_BLOCK + r, c), 0, unroll=8)
        return carry

    tail_end = pad_hi_ref[n_ranges - 1]
    lax.fori_loop(pad_lo_ref[n_ranges - 1] // MOE_BLOCK, tail_end // MOE_BLOCK, pad_block, 0)
    lax.fori_loop(tail_end // MOE_BLOCK * MOE_BLOCK, tail_end, init, 0)

    def body(a, carry):
        inv_ref[dest_ref[a]] = a
        return carry

    lax.fori_loop(0, dest_ref.shape[0], body, 0, unroll=8)


def _moe_inverse(pad_lo, pad_hi, dest, rows):
    smem = pl.BlockSpec(memory_space=pltpu.SMEM)
    return pl.pallas_call(
        _moe_inverse_kernel,
        out_shape=jax.ShapeDtypeStruct((rows,), jnp.int32),
        in_specs=[smem, smem, smem],
        out_specs=smem,
        name="moe_inverse",
    )(pad_lo, pad_hi, dest)


def _moe(h2p, codes, weights, counts, layer, w_gu, b_gu, w_dn, b_dn):
    n = h2p.shape[0] // PACK
    n_assign = n * TOP_K
    n_blocks = -(-n_assign // MOE_BLOCK) + N_EXPERTS
    rows = n_blocks * MOE_BLOCK
    padded = -(-counts // MOE_BLOCK) * MOE_BLOCK
    ends = jnp.cumsum(padded)
    offsets = ends - padded
    starts = jnp.arange(n_blocks, dtype=jnp.int32) * MOE_BLOCK
    block_e = jnp.minimum(jnp.sum(ends[None, :] <= starts[:, None], axis=1), N_EXPERTS - 1).astype(jnp.int32)
    n_used = (ends[-1:] // MOE_BLOCK).astype(jnp.int32)
    n_valid = jnp.clip(counts[block_e] - (starts - offsets[block_e]), 0, MOE_BLOCK).astype(jnp.int32)
    expert = lax.shift_right_logical(codes, CODE_SHIFT)
    which = expert[:, None] == jnp.arange(N_EXPERTS, dtype=jnp.int32)[None, :]
    dest = (codes & ((1 << CODE_SHIFT) - 1)) + jnp.sum(jnp.where(which, offsets[None, :], 0), axis=1)
    spare = 8
    dest = jnp.where(codes >= 0, dest, rows)
    pad_lo = jnp.concatenate([offsets + counts, ends[-1:]]).astype(jnp.int32)
    pad_hi = jnp.concatenate([ends, jnp.full((1,), rows + spare, ends.dtype)]).astype(jnp.int32)
    inv = _moe_inverse(pad_lo, pad_hi, dest.astype(jnp.int32), rows + spare)[:rows]
    tok = lax.shift_right_logical(inv, TOP_K.bit_length() - 1)
    row_src = jnp.where(inv >= 0, tok, 0)
    row_dst = jnp.where(inv >= 0, tok, n)
    row_w = jnp.where(inv >= 0, jnp.take(weights, jnp.maximum(inv, 0)), 0.0)
    xs = _moe_gather(row_src, h2p, n_blocks)
    ys = _moe_mm(block_e, n_used, xs, row_w, layer, w_gu, b_gu, w_dn, b_dn)
    return _moe_combine(n_valid, row_dst, ys, n + 8)


def _final_kernel(x_ref, acc_ref, g2_ref, g_ref, o_ref):
    x = x_ref[...] + g2_ref[...] * _load_acc(acc_ref)
    ms = jnp.mean(x * x, axis=-1, keepdims=True)
    o_ref[...] = x * lax.rsqrt(ms + EPS) * g_ref[...]


def _final(x, acc, g2, g, n_ctx_tiles):
    b, t, d = x.shape
    nt = t // TM - n_ctx_tiles
    tile_in = pl.BlockSpec((None, TM, d), lambda b_, t_: (b_, t_ + n_ctx_tiles, 0))
    acc_tile = _acc_tile_spec(acc, lambda b_, t_: b_ * (t // TM) + t_ + n_ctx_tiles)
    return pl.pallas_call(
        _final_kernel,
        out_shape=jax.ShapeDtypeStruct((b, nt * TM, d), F32),
        grid=(b, nt),
        in_specs=[tile_in, acc_tile,
                  pl.BlockSpec((None, None, 1, d), lambda b_, t_: (b_, 1, 0, 0)),
                  pl.BlockSpec((1, d), lambda b_, t_: (0, 0))],
        out_specs=pl.BlockSpec((None, TM, d), lambda b_, t_: (b_, t_, 0)),
        compiler_params=_cparams(("arbitrary", "arbitrary")),
        name="final_norm",
    )(x, acc, g2, g)


def _mod_tables(mod, b, d):
    outs = []
    for j in range(6):
        m = mod[:, j * d:(j + 1) * d]
        lat = m[:b]
        ctx = jnp.broadcast_to(m[b:b + 1], (b, d))
        outs.append(jnp.stack([ctx, lat], axis=1)[:, :, None, :])
    return outs


def _routing(code):
    codes = code[:, :, :TOP_K].reshape(-1)
    weights = lax.bitcast_convert_type(code[:, :, TOP_K:2 * TOP_K], F32).reshape(-1)
    return codes, weights


def _router_params(router_w, router_b):
    d, e = router_w.shape
    rw = jnp.zeros((d, 128), F32).at[:, :e].set(router_w).astype(BF16)
    rb = jnp.full((1, 128), NEG, F32).at[0, :e].set(router_b)
    return rw, rb


def kernel(x, c, ctx, c_ctx, mod_w, mod_b, norm1_g, norm2_g, ev_w_in, ev_conv_w, gdn_a_log, gdn_dt_bias,
           gdn_norm_g, ml_i_bias, ml_f_bias, ml_norm_g, ev_w_out, od_w_in, ret_norm_g, od_w_out,
           router_w, router_b, moe_w_gu, moe_b_gu, moe_w_dn, moe_b_dn, final_g):
    b, s, d = x.shape
    n_ctx = ctx.shape[1]
    depth = mod_w.shape[0]
    assert n_ctx == TM and s % TM == 0 and depth == 2 and b % SCAN_BG == 0
    t = n_ctx + s
    n_tok = b * t

    cond = jnp.concatenate([c, c_ctx[None, :], jnp.zeros((8 - b - 1, d), F32)], axis=0)
    mod = _adaln(cond, mod_w, mod_b)

    sh1, sc1, g1, sh2, sc2, g2 = _mod_tables(mod[0], b, d)
    qk_w = GDN_H * HD
    conv_ch = 3 * qk_w
    ng = N_DIR * GDN_H
    w_in = ev_w_in[0]
    o_z = conv_ch
    o_a = o_z + qk_w
    o_mq = o_a + 2 * ng
    o_i = o_mq + 4 * qk_w
    w_main = jnp.concatenate([w_in[:, :o_a], w_in[:, o_mq:o_i]], axis=1).astype(BF16)
    w_gate = jnp.concatenate([w_in[:, o_a:o_mq], w_in[:, o_i:o_i + 2 * ng],
                              jnp.zeros((d, 128 - 4 * ng), F32)], axis=1).astype(BF16)
    zeros_g = jnp.zeros((ng,), F32)
    rate = jnp.concatenate([jnp.exp(gdn_a_log[0].astype(F32)).reshape(-1), jnp.zeros((128 - ng,), F32)])[None, :]
    gbias = jnp.concatenate([gdn_dt_bias[0].reshape(-1), zeros_g, ml_i_bias[0].reshape(-1),
                             ml_f_bias[0].reshape(-1), jnp.zeros((128 - 4 * ng,), F32)])[None, :].astype(F32)
    main, gates = _proj_even(ctx, x, norm1_g[0][None, :], sh1, sc1, w_main, w_gate, ev_conv_w[0], rate, gbias)
    gates_c = gates.reshape(b, t // CH, CH, 4 * ng)
    gates_r = jnp.swapaxes(gates_c, 2, 3)
    og_f, og_b, om_f, om_b = _even_scan(main, gates_c, gates_r)
    rw, rb = _router_params(router_w[0], router_b[0])
    gdn_g = jnp.tile(gdn_norm_g[0], GDN_H)[None, :]
    x1, h2p, code, cnt = _merge_even(og_f, og_b, om_f, om_b, main, gdn_g, ml_norm_g[0][None, :],
                                     ev_w_out[0].astype(BF16), ctx, x, g1, norm2_g[0][None, :], sh2, sc2, rw, rb)
    acc = _moe(h2p.reshape(n_tok * PACK, LANES), *_routing(code), cnt[0, :N_EXPERTS].astype(jnp.int32),
               0, moe_w_gu, moe_b_gu, moe_w_dn, moe_b_dn)
    g2_prev = g2

    sh1, sc1, g1, sh2, sc2, g2 = _mod_tables(mod[1], b, d)
    x2, main_o = _proj_odd(x1, acc, g2_prev, norm1_g[1][None, :], sh1, sc1, od_w_in[0].astype(BF16))
    o_f, o_b = _ret_scan(main_o)
    rw, rb = _router_params(router_w[1], router_b[1])
    x3, h2p, code, cnt = _merge_odd(o_f, o_b, main_o, ret_norm_g[0][None, :], od_w_out[0].astype(BF16),
                                    x2, g1, norm2_g[1][None, :], sh2, sc2, rw, rb)
    acc = _moe(h2p.reshape(n_tok * PACK, LANES), *_routing(code), cnt[0, :N_EXPERTS].astype(jnp.int32),
               1, moe_w_gu, moe_b_gu, moe_w_dn, moe_b_dn)
    return _final(x3, acc, g2, final_g[None, :], n_ctx // TM)
```

```python
import jax
import jax.numpy as jnp
import numpy as np
from jax import lax
from jax.experimental import pallas as pl
from jax.experimental.pallas import tpu as pltpu

F32 = jnp.float32
BF16 = jnp.bfloat16
U32 = jnp.uint32
HIGHEST = lax.Precision.HIGHEST

EPS = 1e-6
CH = 64
TM = 256
CPB = TM // CH
HD = 128
N_DIR = 2
GDN_H = 4
ML_H = 4
RET_H = 8
RET_DV = 256
CONV_W = 3
N_EXPERTS = 32
TOP_K = 4
SWIGLU_ALPHA = 1.702
SWIGLU_LIMIT = 7.0
MOE_BLOCK = 512
ROPE_BASE = 10000.0
NEG = -1e30
VMEM_LIMIT = 56 * 1024 * 1024


def _cparams(sem):
    return pltpu.CompilerParams(dimension_semantics=sem, vmem_limit_bytes=VMEM_LIMIT)


def _dot(a, b, precision=None):
    return jnp.dot(a, b, preferred_element_type=F32, precision=precision)


def _dot_nt(a, b):
    return lax.dot_general(a, b, (((1,), (1,)), ((), ())), preferred_element_type=F32)


def _dot_tn(a, b):
    return lax.dot_general(a, b, (((0,), (0,)), ((), ())), preferred_element_type=F32)


def _sigmoid(x):
    return 1.0 / (1.0 + jnp.exp(-x))


def _silu(x):
    return x * _sigmoid(x)


def _group_sum(x, w):
    outs = []
    for j in range(x.shape[1] // w):
        s = jnp.sum(x[:, j * w:(j + 1) * w], axis=-1, keepdims=True)
        outs.append(jnp.broadcast_to(s, (x.shape[0], w)))
    return outs[0] if len(outs) == 1 else jnp.concatenate(outs, axis=1)


def _rms_mod(x, g, sh, sc):
    ms = jnp.mean(x * x, axis=-1, keepdims=True)
    return (x * lax.rsqrt(ms + EPS) * g) * (1.0 + sc) + sh


LANES = 128
PACK = 4


def _store_row_packed(ref, x, lead=()):
    rows = x.shape[0]
    for g in range(x.shape[1] // LANES):
        ref[lead + (pl.ds(g, rows, stride=PACK), slice(None))] = x[:, g * LANES:(g + 1) * LANES]


def _load_row_packed(ref, rows, lead=()):
    return jnp.concatenate([ref[lead + (pl.ds(g, rows, stride=PACK), slice(None))] for g in range(PACK)], axis=1)


def _load_acc(acc_ref):
    rows = acc_ref.shape[1] // PACK
    return jnp.concatenate([_load_row_packed(acc_ref, rows, lead=(h,)) for h in range(acc_ref.shape[0])], axis=1)


def _acc_tile_spec(acc, tile_of):
    return pl.BlockSpec((acc.shape[0], TM * PACK, LANES), lambda b_, t_: (0, tile_of(b_, t_), 0))


def _adaln_kernel(c_ref, w_ref, b_ref, o_ref):
    c = c_ref[...]
    o_ref[...] = _dot(_silu(c), w_ref[...], precision=HIGHEST) + b_ref[...]


def _adaln(cond, mod_w, mod_b):
    depth, d, d6 = mod_w.shape
    n = d6 // d
    return pl.pallas_call(
        _adaln_kernel,
        out_shape=jax.ShapeDtypeStruct((depth, cond.shape[0], d6), F32),
        grid=(depth, n),
        in_specs=[pl.BlockSpec(cond.shape, lambda l, j: (0, 0)),
                  pl.BlockSpec((None, d, d), lambda l, j: (l, 0, j)),
                  pl.BlockSpec((None, 1, d), lambda l, j: (l, 0, j))],
        out_specs=pl.BlockSpec((None, cond.shape[0], d), lambda l, j: (l, 0, j)),
        compiler_params=_cparams(("arbitrary", "arbitrary")),
        name="adaln",
    )(cond, mod_w, mod_b.reshape(depth, 1, d6))


def _stream_tile(ctx_ref, x_ref):
    return jnp.where(pl.program_id(1) == 0, ctx_ref[...], x_ref[...])


def _stream_specs(d):
    return (pl.BlockSpec((None, TM, d), lambda b, t: (b, 0, 0)),
            pl.BlockSpec((None, TM, d), lambda b, t: (b, jnp.maximum(t - 1, 0), 0)))


def _proj_even_kernel(ctx_ref, x_ref, g_ref, sh_ref, sc_ref, w_ref, wg_ref, cw_ref, rate_ref, gb_ref,
                      main_ref, gates_ref):
    t = pl.program_id(1)
    h = _rms_mod(_stream_tile(ctx_ref, x_ref), g_ref[...], sh_ref[...], sc_ref[...])
    hb = h.astype(BF16)
    qk_w = GDN_H * HD

    row = lax.broadcasted_iota(jnp.int32, (TM, 1), 0)
    pos = jnp.where(t > 0, row & (CH - 1), row)
    last = jnp.where(t > 0, CH - 1, TM - 1)
    left_ok = pos != 0
    right_ok = pos != last
    for seg in range(3):
        sl = slice(seg * qk_w, (seg + 1) * qk_w)
        u = _dot(hb, w_ref[:, sl])
        um = jnp.where(left_ok, pltpu.roll(u, 1, 0), 0.0)
        up = jnp.where(right_ok, pltpu.roll(u, TM - 1, 0), 0.0)
        cv = _silu(um * cw_ref[0:1, sl] + u * cw_ref[1:2, sl] + up * cw_ref[2:3, sl])
        if seg < 2:
            ss = _group_sum(cv * cv, HD)
            cv = cv * lax.rsqrt(ss + EPS)
            if seg == 0:
                cv = cv * (HD ** -0.5)
        main_ref[:, sl] = cv.astype(BF16)
    for seg in range(3, 8):
        sl = slice(seg * qk_w, (seg + 1) * qk_w)
        u = _dot(hb, w_ref[:, sl])
        if seg == 5:
            u = u * (HD ** -0.5)
        main_ref[:, sl] = u.astype(BF16)

    z = _dot(hb, wg_ref[...]) + gb_ref[...]
    tl = jnp.log(1.0 + jnp.exp(-jnp.abs(z)))
    sp_pos = jnp.maximum(z, 0.0) + tl
    sp_neg = jnp.maximum(-z, 0.0) + tl
    lane = lax.broadcasted_iota(jnp.int32, z.shape, 1)
    ng = N_DIR * GDN_H
    res = jnp.where(lane < ng, -rate_ref[...] * sp_pos,
                    jnp.where(lane < 2 * ng, _sigmoid(z),
                              jnp.where(lane < 3 * ng, z, -sp_neg)))
    gates_ref[...] = res[:, :gates_ref.shape[-1]]


def _tile_mod_spec(d):
    return pl.BlockSpec((None, None, 1, d), lambda b, t: (b, jnp.minimum(t, 1), 0, 0))


def _proj_even(ctx, x, g, sh, sc, w_main, w_gate, conv_w, rate, gbias):
    b, s, d = x.shape
    t = ctx.shape[1] + s
    n = w_main.shape[1]
    ngl = 4 * N_DIR * GDN_H
    const = lambda shape: pl.BlockSpec(shape, lambda b_, t_: (0,) * len(shape))
    return pl.pallas_call(
        _proj_even_kernel,
        out_shape=(jax.ShapeDtypeStruct((b, t, n), BF16), jax.ShapeDtypeStruct((b, t, ngl), F32)),
        grid=(b, t // TM),
        in_specs=[*_stream_specs(d),
                  const((1, d)), _tile_mod_spec(d), _tile_mod_spec(d),
                  const(w_main.shape), const(w_gate.shape), const(conv_w.shape),
                  const(rate.shape), const(gbias.shape)],
        out_specs=(pl.BlockSpec((None, TM, n), lambda b_, t_: (b_, t_, 0)),
                   pl.BlockSpec((None, TM, ngl), lambda b_, t_: (b_, t_, 0))),
        compiler_params=_cparams(("arbitrary", "arbitrary")),
        name="proj_even",
    )(ctx, x, g, sh, sc, w_main, w_gate, conv_w, rate, gbias)


def _rev_tile(i, nt):
    return jnp.where(i == 0, 0, nt - i)


def _tri_masks():
    r = lax.broadcasted_iota(jnp.int32, (CH, CH), 0)
    c = lax.broadcasted_iota(jnp.int32, (CH, CH), 1)
    return r >= c, r > c, r <= c, r < c


SCAN_BG = 2
SOLVE_BASE = 8


def _scan_streams(heads):
    groups = [(bi, d) for bi in range(SCAN_BG) for d in range(N_DIR)]
    streams = [(gi, hh) for gi in range(len(groups)) for hh in range(heads)]
    return groups, streams


def _cumsum_both(groups, gcs, grs, lower, upper):
    tri = (lower.astype(F32), upper.astype(F32))
    cs_c = [_dot(tri[d], gc, precision=HIGHEST) for (_, d), gc in zip(groups, gcs)]
    cs_r = [_dot(gr, tri[1 - d], precision=HIGHEST) for (_, d), gr in zip(groups, grs)]
    return cs_c, cs_r


def _store_heads(refs, groups, rows, outs, heads):
    for gi, (bi, d) in enumerate(groups):
        o_ref = refs[d][-1]
        tile = jnp.concatenate(outs[gi * heads:(gi + 1) * heads], axis=1)
        o_ref[bi, rows[d], :] = tile.astype(o_ref.dtype)


class _GdnChunk:
    def __init__(self, refs, s_ref):
        self.refs, self.s_ref = refs, s_ref
        self.groups, self.streams = _scan_streams(GDN_H)

    def prelude(self, masks, rows, gcs, cs_c, cs_r):
        lower, lstrict, upper, ustrict = masks
        refs, groups = self.refs, self.groups
        ng = N_DIR * GDN_H
        st = []
        for gi, hh in self.streams:
            bi, d = groups[gi]
            ci = d * GDN_H + hh
            cols = slice(hh * HD, (hh + 1) * HD)
            incl, strict = (lower, lstrict) if d == 0 else (upper, ustrict)
            g_col = cs_c[gi][:, ci:ci + 1]
            g_row = cs_r[gi][ci:ci + 1, :]
            beta = gcs[gi][:, ng + ci:ng + ci + 1]
            tot = g_col[CH - 1:CH, :] if d == 0 else g_col[0:1, :]
            decay = jnp.where(incl, jnp.exp(jnp.where(incl, g_col - g_row, 0.0)), 0.0)
            q = refs[d][0][bi, rows[d], cols].astype(F32)
            k = refs[d][1][bi, rows[d], cols].astype(F32)
            v = refs[d][2][bi, rows[d], cols].astype(F32)
            kbeta = k * beta
            eg = jnp.exp(g_col)
            st.append(dict(
                strict=strict, decay=decay, kb=k.astype(BF16), kbetab=kbeta.astype(BF16), qb=q.astype(BF16),
                x=jnp.concatenate([v * beta, kbeta * eg], axis=1),
                qe=(q * eg).astype(BF16), kdec=(k * jnp.exp(tot - g_col)).astype(BF16), cd=jnp.exp(tot)))
        self.st = st

    def matmuls(self, rows):
        st, s_ref, streams = self.st, self.s_ref, self.streams
        kk = [_dot_nt(s["kbetab"], s["kb"]) for s in st]
        qk = [(_dot_nt(s["qb"], s["kb"]) * s["decay"]).astype(BF16) for s in st]
        r = lax.broadcasted_iota(jnp.int32, (CH, CH), 0)
        c = lax.broadcasted_iota(jnp.int32, (CH, CH), 1)
        eye = jnp.where(r == c, 1.0, 0.0)
        blocks_differ = lambda w: lax.shift_right_logical(r, w.bit_length() - 1) ^ \
            lax.shift_right_logical(c, w.bit_length() - 1)
        a = [jnp.where(s["strict"], m * s["decay"], 0.0) for s, m in zip(st, kk)]
        p = [-jnp.where(blocks_differ(SOLVE_BASE) == 0, m, 0.0) for m in a]
        t = [eye + m for m in p]
        w = 2
        while w < SOLVE_BASE:
            pb = [m.astype(BF16) for m in p]
            p = [_dot(m, m) for m in pb]
            t = [tt + _dot(tt.astype(BF16), m.astype(BF16)) for tt, m in zip(t, p)]
            w *= 2
        w = SOLVE_BASE
        while w < CH:
            off = blocks_differ(w) == 1
            tb = [tt.astype(BF16) for tt in t]
            ta = [_dot(tt, jnp.where(off, m, 0.0).astype(BF16)) for tt, m in zip(tb, a)]
            t = [tt - _dot(m.astype(BF16), tt2) for tt, m, tt2 in zip(t, ta, tb)]
            w *= 2
        x = [_dot(tt.astype(BF16), s["x"].astype(BF16)) for tt, s in zip(t, st)]
        s_old = [s_ref[si] for si in range(len(streams))]
        sb = [s.astype(BF16) for s in s_old]
        vnb = [(xx[:, :HD] - _dot(xx[:, HD:].astype(BF16), s)).astype(BF16) for xx, s in zip(x, sb)]
        o = [_dot(s["qe"], sbi) + _dot(m, vn) for s, sbi, m, vn in zip(st, sb, qk, vnb)]
        s_new = [so * s["cd"] + _dot_tn(s["kdec"], vn) for s, so, vn in zip(st, s_old, vnb)]
        for si, s in enumerate(s_new):
            s_ref[si] = s
        _store_heads(self.refs, self.groups, rows, o, GDN_H)


def _scan_specs(nt, width, col):
    fwd = pl.BlockSpec((SCAN_BG, TM, width), lambda b, i: (b, i, col))
    bwd = pl.BlockSpec((SCAN_BG, TM, width), lambda b, i: (b, _rev_tile(i, nt), col))
    return fwd, bwd


def _gate_specs(nt, shape):
    fwd = pl.BlockSpec((SCAN_BG, CPB) + shape, lambda b, i: (b, i, 0, 0))
    bwd = pl.BlockSpec((SCAN_BG, CPB) + shape, lambda b, i: (b, _rev_tile(i, nt), 0, 0))
    return fwd, bwd


class _MlstmChunk:
    def __init__(self, refs, c_ref, m_ref):
        self.refs, self.c_ref, self.m_ref = refs, c_ref, m_ref
        self.groups, self.streams = _scan_streams(ML_H)

    def prelude(self, masks, rows, gcs, grs, cs_c, cs_r):
        lower, _, upper, _ = masks
        refs, groups, streams, m_ref = self.refs, self.groups, self.streams, self.m_ref
        ng = N_DIR * GDN_H
        i_off = 2 * ng
        f_off = 2 * ng + N_DIR * ML_H
        lane = lax.broadcasted_iota(jnp.int32, (CH, HD), 1)
        ones_col = jnp.where(lane == 0, 1.0, 0.0).astype(BF16)
        ns = len(streams)
        dirs = [groups[gi][1] for gi, _ in streams]
        chan = [groups[gi][1] * ML_H + hh for gi, hh in streams]
        b_col = [cs_c[gi][:, f_off + c:f_off + c + 1] for (gi, _), c in zip(streams, chan)]
        b_row = [cs_r[gi][f_off + c:f_off + c + 1, :] for (gi, _), c in zip(streams, chan)]
        i_col = [gcs[gi][:, i_off + c:i_off + c + 1] for (gi, _), c in zip(streams, chan)]
        i_row = [grs[gi][i_off + c:i_off + c + 1, :] for (gi, _), c in zip(streams, chan)]
        b_tot = [bc[CH - 1:CH, :] if d == 0 else bc[0:1, :] for bc, d in zip(b_col, dirs)]
        m_rows = [m_ref[si] for si in range(ns)]
        m_old = [mr[:, 0:1] for mr in m_rows]
        d_in = [jnp.where(lower if d == 0 else upper, bc - br + ir, NEG)
                for d, bc, br, ir in zip(dirs, b_col, b_row, i_row)]
        d_end = [bt - br + ir for bt, br, ir in zip(b_tot, b_row, i_row)]
        mx_in = [jnp.max(a, axis=-1, keepdims=True) for a in d_in]
        mx_end = [jnp.max(a, axis=-1, keepdims=True) for a in d_end]
        d_carry = [bc + m for bc, m in zip(b_col, m_old)]
        m_t = [jnp.maximum(a, b_) for a, b_ in zip(d_carry, mx_in)]
        carry_end = [bt + m for bt, m in zip(b_tot, m_old)]
        m_new = [jnp.maximum(a, b_) for a, b_ in zip(carry_end, mx_end)]
        p_in = [jnp.exp(a - b_) for a, b_ in zip(d_in, m_t)]
        w_end = [jnp.exp(bt - bc + ic - mn) for bt, bc, ic, mn in zip(b_tot, b_col, i_col, m_new)]
        st = []
        for si, (gi, hh) in enumerate(streams):
            bi, d = groups[gi]
            cols = slice(hh * HD, (hh + 1) * HD)
            k = refs[d][1][bi, rows[d], cols]
            v = refs[d][2][bi, rows[d], cols]
            st.append(dict(
                q=refs[d][0][bi, rows[d], cols], k=k,
                v_aug=jnp.concatenate([v, ones_col], axis=1),
                p_in=p_in[si], w_carry=jnp.exp(d_carry[si] - m_t[si]), floor=jnp.exp(-m_t[si]),
                kw=(k.astype(F32) * w_end[si]).astype(BF16),
                f_end=jnp.exp(carry_end[si] - m_new[si]),
                m_new=jnp.broadcast_to(m_new[si], m_rows[si].shape)))
        self.st = st

    def matmuls(self, rows):
        st, c_ref, m_ref, streams = self.st, self.c_ref, self.m_ref, self.streams
        sc = [(_dot_nt(s["q"], s["k"]) * s["p_in"]).astype(BF16) for s in st]
        c_old = [c_ref[si] for si in range(len(streams))]
        qc = [_dot(s["q"], c.astype(BF16)) for s, c in zip(st, c_old)]
        nd = [s["w_carry"] * a + _dot(m, s["v_aug"]) for s, a, m in zip(st, qc, sc)]
        hout = [a[:, :HD] / jnp.maximum(jnp.abs(a[:, HD:HD + 1]), s["floor"]) for s, a in zip(st, nd)]
        c_new = [s["f_end"] * c + _dot_tn(s["kw"], s["v_aug"]) for s, c in zip(st, c_old)]
        for si, (s, c) in enumerate(zip(st, c_new)):
            c_ref[si] = c
            m_ref[si] = s["m_new"]
        _store_heads(self.refs, self.groups, rows, hout, ML_H)


def _even_scan_kernel(gqf, gkf, gvf, gqb, gkb, gvb, mqf, mkf, mvf, mqb, mkb, mvb, gcf, grf, gcb, grb,
                      ogf, ogb, omf, omb, s_ref, c_ref, m_ref):
    i = pl.program_id(1)

    @pl.when(i == 0)
    def _():
        s_ref[...] = jnp.zeros_like(s_ref)
        c_ref[...] = jnp.zeros_like(c_ref)
        m_ref[...] = jnp.zeros_like(m_ref)

    masks = _tri_masks()
    gdn = _GdnChunk(((gqf, gkf, gvf, ogf), (gqb, gkb, gvb, ogb)), s_ref)
    mls = _MlstmChunk(((mqf, mkf, mvf, omf), (mqb, mkb, mvb, omb)), c_ref, m_ref)
    gate_refs = ((gcf, grf), (gcb, grb))

    def chunk_body(cc, carry):
        cidx = (cc, CPB - 1 - cc)
        rows = tuple(pl.ds(pl.multiple_of(c * CH, CH), CH) for c in cidx)
        gcs = [gate_refs[d][0][bi, cidx[d]] for bi, d in gdn.groups]
        grs = [gate_refs[d][1][bi, cidx[d]] for bi, d in gdn.groups]
        cs_c, cs_r = _cumsum_both(gdn.groups, gcs, grs, masks[0], masks[2])
        gdn.prelude(masks, rows, gcs, cs_c, cs_r)
        mls.prelude(masks, rows, gcs, grs, cs_c, cs_r)
        gdn.matmuls(rows)
        mls.matmuls(rows)
        return carry

    lax.fori_loop(0, CPB, chunk_body, 0)


def _even_scan(main, gates_c, gates_r):
    b, t, _ = main.shape
    nt = t // TM
    w = GDN_H * HD
    ngl = gates_c.shape[-1]
    qkv = lambda cols: [_scan_specs(nt, w, c)[d] for d in range(N_DIR) for c in cols]
    gcf, gcb = _gate_specs(nt, (CH, ngl))
    grf, grb = _gate_specs(nt, (ngl, CH))
    of, ob = _scan_specs(nt, w, 0)
    out = jax.ShapeDtypeStruct((b, t, w), BF16)
    n_streams = SCAN_BG * N_DIR * GDN_H
    return pl.pallas_call(
        _even_scan_kernel,
        out_shape=(out, out, out, out),
        grid=(b // SCAN_BG, nt),
        in_specs=qkv((0, 1, 2)) + qkv((4, 5, 6)) + [gcf, grf, gcb, grb],
        out_specs=(of, ob, of, ob),
        scratch_shapes=[pltpu.VMEM((n_streams, HD, HD), F32), pltpu.VMEM((n_streams, HD, 2 * HD), F32),
                        pltpu.VMEM((n_streams, 1, HD), F32)],
        compiler_params=_cparams(("arbitrary", "arbitrary")),
        name="even_scan",
    )(*([main] * 12), gates_c, gates_r, gates_c, gates_r)


def _ret_kernel(qf, kf, vf, qb, kb, vb, intra_ref, cross_ref, tail_ref, cd_ref, of, ob, s_ref):
    i = pl.program_id(1)

    @pl.when(i == 0)
    def _():
        s_ref[...] = jnp.zeros_like(s_ref)

    refs = ((qf, kf, vf, of), (qb, kb, vb, ob))
    groups, streams = _scan_streams(RET_H)

    def chunk_body(cc, carry):
        cidx = (cc, CPB - 1 - cc)
        rows = tuple(pl.ds(pl.multiple_of(c * CH, CH), CH) for c in cidx)
        st = []
        for gi, hh in streams:
            bi, d = groups[gi]
            ti = d * RET_H + hh
            kcols = slice(hh * HD, (hh + 1) * HD)
            q = refs[d][0][bi, rows[d], kcols]
            k = refs[d][1][bi, rows[d], kcols]
            st.append(dict(
                ti=ti, q=q, k=k, v=refs[d][2][bi, rows[d], slice(hh * RET_DV, (hh + 1) * RET_DV)],
                qc=(q.astype(F32) * cross_ref[ti]).astype(BF16),
                kt=(k.astype(F32) * tail_ref[ti]).astype(BF16)))
        sc = [(_dot_nt(s["q"], s["k"]) * intra_ref[s["ti"]]).astype(BF16) for s in st]
        s_old = [s_ref[si] for si in range(len(streams))]
        o = [_dot(m, s["v"]) + _dot(s["qc"], so.astype(BF16)) for s, m, so in zip(st, sc, s_old)]
        s_new = [cd_ref[s["ti"]] * so + _dot_tn(s["kt"], s["v"]) for s, so in zip(st, s_old)]
        for si, s in enumerate(s_new):
            s_ref[si] = s
        _store_heads(refs, groups, rows, o, RET_H)
        return carry

    lax.fori_loop(0, CPB, chunk_body, 0)


def _ret_tables():
    pos = np.arange(CH, dtype=np.float64)
    intra, cross, tail, cd = [], [], [], []
    for d in range(N_DIR):
        expo = 5.0 + np.arange(RET_H, dtype=np.float64)
        if d == 1:
            expo = expo[::-1]
        lg = np.log1p(-np.exp2(-expo))
        p = pos if d == 0 else (CH - 1.0 - pos)
        diff = p[:, None] - p[None, :]
        for hh in range(RET_H):
            intra.append(np.where(diff >= 0, np.exp(np.where(diff >= 0, diff, 0.0) * lg[hh]), 0.0))
            cross.append(np.broadcast_to(np.exp((p + 1.0) * lg[hh])[:, None], (CH, HD)))
            tail.append(np.broadcast_to(np.exp((CH - 1.0 - p) * lg[hh])[:, None], (CH, HD)))
            cd.append(np.full((1, RET_DV), np.exp(CH * lg[hh])))
    f = lambda a: jnp.asarray(np.stack(a), F32)
    return f(intra), f(cross), f(tail), f(cd)


def _ret_scan(main):
    b, t, _ = main.shape
    nt = t // TM
    qw = RET_H * HD
    vw = RET_H * RET_DV
    qf, qb = _scan_specs(nt, qw, 0)
    kf, kb = _scan_specs(nt, qw, 1)
    vf, vb = _scan_specs(nt, vw, 1)
    of, ob = _scan_specs(nt, vw, 0)
    tabs = _ret_tables()
    const = lambda a: pl.BlockSpec(a.shape, lambda b_, i_: (0,) * a.ndim)
    return pl.pallas_call(
        _ret_kernel,
        out_shape=(jax.ShapeDtypeStruct((b, t, vw), BF16), jax.ShapeDtypeStruct((b, t, vw), BF16)),
        grid=(b // SCAN_BG, nt),
        in_specs=[qf, kf, vf, qb, kb, vb] + [const(a) for a in tabs],
        out_specs=(of, ob),
        scratch_shapes=[pltpu.VMEM((SCAN_BG * N_DIR * RET_H, HD, RET_DV), F32)],
        compiler_params=_cparams(("arbitrary", "arbitrary")),
        name="ret_scan",
    )(main, main, main, main, main, main, *tabs)


CODE_SHIFT = 17


def _route_and_pack(h2, rw_ref, rb_ref, h2p_ref, code_ref, cnt_ref, base_ref, skip_context):
    h2b = h2.astype(BF16)
    logits = _dot(h2b, rw_ref[...]) + rb_ref[...]
    lane = lax.broadcasted_iota(jnp.int32, logits.shape, 1)
    lane_f = lane.astype(F32)
    vals, idxs = [], []
    cur = logits
    for _ in range(TOP_K):
        m = jnp.max(cur, axis=-1, keepdims=True)
        ix = jnp.min(jnp.where(cur == m, lane_f, float(logits.shape[1])), axis=-1, keepdims=True)
        vals.append(m)
        idxs.append(ix)
        cur = jnp.where(lane_f == ix, NEG, cur)
    es = [jnp.exp(v - vals[0]) for v in vals]
    tot = es[0] + es[1] + es[2] + es[3]

    @pl.when((pl.program_id(0) == 0) & (pl.program_id(1) == 0))
    def _():
        base_ref[...] = jnp.zeros_like(base_ref)

    tm = logits.shape[0]
    routed = pl.program_id(1) > 0
    onehot = jnp.zeros(logits.shape, F32)
    for j in range(TOP_K):
        onehot = jnp.where(lane_f == idxs[j], 1.0, onehot)
    if skip_context:
        onehot = onehot * jnp.where(routed, 1.0, 0.0)
    r_i = lax.broadcasted_iota(jnp.int32, (tm, tm), 0)
    c_i = lax.broadcasted_iota(jnp.int32, (tm, tm), 1)
    before = _dot(jnp.where(r_i > c_i, 1.0, 0.0).astype(BF16), onehot.astype(BF16))
    base = base_ref[...]
    pos = before + base
    total = base + before[tm - 1:tm, :] + onehot[tm - 1:tm, :]
    base_ref[...] = total
    cnt_ref[...] = total

    code = jnp.zeros(logits.shape, F32)
    wgt = jnp.zeros(logits.shape, F32)
    for j in range(TOP_K):
        rank = jnp.sum(jnp.where(lane_f == idxs[j], pos, 0.0), axis=-1, keepdims=True)
        code = jnp.where(lane == j, idxs[j] * float(1 << CODE_SHIFT) + rank, code)
        wgt = jnp.where(lane == TOP_K + j, es[j] / tot, wgt)
    code_i = code.astype(jnp.int32)
    if skip_context:
        code_i = jnp.where(routed, code_i, -1)
    code_ref[...] = jnp.where(lane < TOP_K, code_i, pltpu.bitcast(wgt, jnp.int32))
    half = h2.shape[1] // 2
    r = h2b.astype(F32)
    lo = lax.shift_right_logical(pltpu.bitcast(r[:, :half], U32), jnp.uint32(16))
    hi = pltpu.bitcast(r[:, half:], U32) & jnp.uint32(0xFFFF0000)
    _store_row_packed(h2p_ref, hi | lo)


def _merge_even_kernel(ogf, ogb, omf, omb, z_ref, mo_ref, gg_ref, mg_ref, wo_ref, ctx_ref, x_ref, g1_ref,
                       n2_ref, sh_ref, sc_ref, rw_ref, rb_ref, x1_ref, h2p_ref, code_ref, cnt_ref, base_ref):
    og = ogf[...].astype(F32) + ogb[...].astype(F32)
    ms = _group_sum(og * og, HD) * (1.0 / HD)
    a = og * lax.rsqrt(ms + EPS) * gg_ref[...] * _silu(z_ref[...].astype(F32))
    om = omf[...].astype(F32) + omb[...].astype(F32)
    ms = _group_sum(om * om, HD) * (1.0 / HD)
    m = om * lax.rsqrt(ms + EPS) * mg_ref[...] * _sigmoid(mo_ref[...].astype(F32))
    cat = jnp.concatenate([a, m], axis=1).astype(BF16)
    y = _dot(cat, wo_ref[...])
    x1 = _stream_tile(ctx_ref, x_ref) + g1_ref[...] * y
    x1_ref[...] = x1
    h2 = _rms_mod(x1, n2_ref[...], sh_ref[...], sc_ref[...])
    _route_and_pack(h2, rw_ref, rb_ref, h2p_ref, code_ref, cnt_ref, base_ref, skip_context=False)


def _merge_odd_kernel(of, ob, gate_ref, ng_ref, wo_ref, x_ref, g1_ref,
                      n2_ref, sh_ref, sc_ref, rw_ref, rb_ref, x1_ref, h2p_ref, code_ref, cnt_ref, base_ref):
    o = of[...].astype(F32) + ob[...].astype(F32)
    o = o - _group_sum(o, RET_DV) * (1.0 / RET_DV)
    ms = _group_sum(o * o, RET_DV) * (1.0 / RET_DV)
    y = o * lax.rsqrt(ms + EPS) * ng_ref[...] * _silu(gate_ref[...].astype(F32))
    y = _dot(y.astype(BF16), wo_ref[...])
    x1 = x_ref[...] + g1_ref[...] * y
    x1_ref[...] = x1
    h2 = _rms_mod(x1, n2_ref[...], sh_ref[...], sc_ref[...])
    _route_and_pack(h2, rw_ref, rb_ref, h2p_ref, code_ref, cnt_ref, base_ref, skip_context=True)


def _merge_out(b, t, d):
    assert d // 2 == PACK * LANES
    shapes = (jax.ShapeDtypeStruct((b, t, d), F32), jax.ShapeDtypeStruct((b, t * PACK, LANES), U32),
              jax.ShapeDtypeStruct((b, t, LANES), jnp.int32), jax.ShapeDtypeStruct((1, LANES), F32))
    tile = lambda rows, width: pl.BlockSpec((None, rows, width), lambda b_, t_: (b_, t_, 0))
    specs = (tile(TM, d), tile(TM * PACK, LANES), tile(TM, LANES), pl.BlockSpec((1, LANES), lambda b_, t_: (0, 0)))
    return shapes, specs


def _merge_even(og_f, og_b, om_f, om_b, main, gdn_g, ml_g, w_out, ctx, x, g1, n2, sh2, sc2, rw, rb):
    b, s, d = x.shape
    t = ctx.shape[1] + s
    w = GDN_H * HD
    tile = lambda width, col: pl.BlockSpec((None, TM, width), lambda b_, t_: (b_, t_, col))
    const = lambda a: pl.BlockSpec(a.shape, lambda b_, t_: (0,) * a.ndim)
    shapes, specs = _merge_out(b, t, d)
    return pl.pallas_call(
        _merge_even_kernel,
        out_shape=shapes,
        grid=(b, t // TM),
        in_specs=[tile(w, 0), tile(w, 0), tile(w, 0), tile(w, 0), tile(w, 3), tile(w, 7),
                  const(gdn_g), const(ml_g), const(w_out), *_stream_specs(d), _tile_mod_spec(d),
                  const(n2), _tile_mod_spec(d), _tile_mod_spec(d), const(rw), const(rb)],
        out_specs=specs,
        scratch_shapes=[pltpu.VMEM((1, 128), F32)],
        compiler_params=_cparams(("arbitrary", "arbitrary")),
        name="merge_even",
    )(og_f, og_b, om_f, om_b, main, main, gdn_g, ml_g, w_out, ctx, x, g1, n2, sh2, sc2, rw, rb)


def _merge_odd(o_f, o_b, main, ret_g, w_out, x, g1, n2, sh2, sc2, rw, rb):
    b, t, d = x.shape
    vw = RET_H * RET_DV
    tile = lambda width, col: pl.BlockSpec((None, TM, width), lambda b_, t_: (b_, t_, col))
    const = lambda a: pl.BlockSpec(a.shape, lambda b_, t_: (0,) * a.ndim)
    shapes, specs = _merge_out(b, t, d)
    return pl.pallas_call(
        _merge_odd_kernel,
        out_shape=shapes,
        grid=(b, t // TM),
        in_specs=[tile(vw, 0), tile(vw, 0), tile(vw, 2), const(ret_g), const(w_out), tile(d, 0),
                  _tile_mod_spec(d), const(n2), _tile_mod_spec(d), _tile_mod_spec(d), const(rw), const(rb)],
        out_specs=specs,
        scratch_shapes=[pltpu.VMEM((1, 128), F32)],
        compiler_params=_cparams(("arbitrary", "arbitrary")),
        name="merge_odd",
    )(o_f, o_b, main, ret_g, w_out, x, g1, n2, sh2, sc2, rw, rb)


def _proj_odd_kernel(x_ref, acc_ref, g2_ref, g_ref, sh_ref, sc_ref, w_ref, cos_ref, sin_ref,
                     x2_ref, main_ref):
    x2 = x_ref[...] + g2_ref[...] * _load_acc(acc_ref)
    x2_ref[...] = x2
    hb = _rms_mod(x2, g_ref[...], sh_ref[...], sc_ref[...]).astype(BF16)
    qk_w = RET_H * HD
    cos = cos_ref[...]
    sin = sin_ref[...]
    n_seg = w_ref.shape[1] // qk_w
    for seg in range(n_seg):
        sl = slice(seg * qk_w, (seg + 1) * qk_w)
        u = _dot(hb, w_ref[:, sl])
        if seg < 2:
            parts = []
            for hh in range(RET_H):
                uh = u[:, hh * HD:(hh + 1) * HD]
                parts.append(uh * cos + pltpu.roll(uh, HD // 2, 1) * sin)
            u = jnp.concatenate(parts, axis=1)
            if seg == 1:
                u = u * (HD ** -0.5)
        main_ref[:, sl] = u.astype(BF16)


def _rope_tables(t):
    half = HD // 2
    freqs = ROPE_BASE ** (-jnp.arange(half, dtype=F32) / half)
    ang = jnp.arange(t, dtype=F32)[:, None] * freqs[None, :]
    cos, sin = jnp.cos(ang), jnp.sin(ang)
    return jnp.concatenate([cos, cos], axis=1), jnp.concatenate([-sin, sin], axis=1)


def _proj_odd(x, acc, g2, g, sh, sc, w):
    b, t, d = x.shape
    n = w.shape[1]
    cos, sin = _rope_tables(t)
    const = lambda a: pl.BlockSpec(a.shape, lambda b_, t_: (0,) * a.ndim)
    tile = lambda width: pl.BlockSpec((None, TM, width), lambda b_, t_: (b_, t_, 0))
    rope = pl.BlockSpec((TM, HD), lambda b_, t_: (t_, 0))
    nt = t // TM
    acc_tile = _acc_tile_spec(acc, lambda b_, t_: b_ * nt + t_)
    return pl.pallas_call(
        _proj_odd_kernel,
        out_shape=(jax.ShapeDtypeStruct((b, t, d), F32), jax.ShapeDtypeStruct((b, t, n), BF16)),
        grid=(b, nt),
        in_specs=[tile(d), acc_tile, _tile_mod_spec(d), const(g), _tile_mod_spec(d), _tile_mod_spec(d),
                  const(w), rope, rope],
        out_specs=(tile(d), tile(n)),
        compiler_params=_cparams(("arbitrary", "arbitrary")),
        name="proj_odd",
    )(x, acc, g2, g, sh, sc, w, cos, sin)


ROWS_PER_STEP = 8


def _packed_row(r):
    return pl.ds(pl.multiple_of(r * PACK, PACK), PACK)


def _move_rows(n_blocks):
    return MOE_BLOCK * next(k for k in (4, 2, 1) if n_blocks % k == 0)


def _moe_gather_kernel(src_ref, h_ref, xs_ref):
    def body(r8, carry):
        base = r8 * ROWS_PER_STEP
        rows = [h_ref[_packed_row(src_ref[0, base + j]), :] for j in range(ROWS_PER_STEP)]
        for j in range(ROWS_PER_STEP):
            xs_ref[_packed_row(base + j), :] = rows[j]
        return carry

    lax.fori_loop(0, src_ref.shape[1] // ROWS_PER_STEP, body, 0)


def _moe_gather(row_src, h2p, n_blocks):
    step = _move_rows(n_blocks)
    n_steps = n_blocks * MOE_BLOCK // step
    return pl.pallas_call(
        _moe_gather_kernel,
        out_shape=jax.ShapeDtypeStruct((n_blocks * MOE_BLOCK * PACK, LANES), U32),
        grid=(n_steps,),
        in_specs=[pl.BlockSpec((None, 1, step), lambda i: (i, 0, 0), memory_space=pltpu.SMEM),
                  pl.BlockSpec(h2p.shape, lambda i: (0, 0))],
        out_specs=pl.BlockSpec((step * PACK, LANES), lambda i: (i, 0)),
        compiler_params=_cparams(("arbitrary",)),
        name="moe_gather",
    )(row_src.reshape(n_steps, 1, step), h2p)


def _moe_mm_kernel(be_ref, nu_ref, xs_ref, rw_ref, wgu_ref, bgu_ref, wdn_ref, bdn_ref, ys_ref, wgu_s, wdn_s):
    i = pl.program_id(0)
    e = be_ref[i]
    half = wgu_ref.shape[0] // 2
    hl = LANES // 2

    @pl.when(i >= nu_ref[0])
    def _():
        ys_ref[...] = jnp.zeros_like(ys_ref)

    @pl.when(i < nu_ref[0])
    def _():
        @pl.when((i == 0) | (e != be_ref[jnp.maximum(i - 1, 0)]))
        def _():
            wgu_s[...] = wgu_ref[...].astype(BF16)
            for p in range(wdn_ref.shape[0] // LANES):
                first = wdn_ref[p * LANES:p * LANES + hl, :].astype(BF16).astype(F32)
                second = wdn_ref[p * LANES + hl:(p + 1) * LANES, :].astype(BF16).astype(F32)
                word = (pltpu.bitcast(second, U32) & jnp.uint32(0xFFFF0000)) | \
                    lax.shift_right_logical(pltpu.bitcast(first, U32), jnp.uint32(16))
                wdn_s[p * LANES:(p + 1) * LANES, :] = pltpu.bitcast(word, BF16)

        xu = _load_row_packed(xs_ref, MOE_BLOCK)
        lo = pltpu.bitcast(lax.shift_left(xu, jnp.uint32(16)), F32).astype(BF16)
        hi = pltpu.bitcast(xu & jnp.uint32(0xFFFF0000), F32).astype(BF16)
        gu = _dot(lo, wgu_s[:half, :]) + _dot(hi, wgu_s[half:, :]) + bgu_ref[...]
        even = (lax.broadcasted_iota(jnp.int32, (gu.shape[0], LANES), 1) & 1) == 0
        acts = []
        for p in range(gu.shape[1] // (2 * LANES)):
            a = gu[:, 2 * p * LANES:(2 * p + 1) * LANES]
            b = gu[:, (2 * p + 1) * LANES:(2 * p + 2) * LANES]
            gate = jnp.minimum(jnp.where(even, a, pltpu.roll(b, 1, 1)), SWIGLU_LIMIT)
            up = jnp.clip(jnp.where(even, pltpu.roll(a, LANES - 1, 1), b), -SWIGLU_LIMIT, SWIGLU_LIMIT)
            acts.append(((up + 1.0) * gate * _sigmoid(SWIGLU_ALPHA * gate)).astype(BF16))
        y = _dot(jnp.concatenate(acts, axis=1), wdn_s[...]) + bdn_ref[...]
        y = y * jnp.broadcast_to(rw_ref[...], (8, rw_ref.shape[1])).T[:, 0:1]
        cw = PACK * LANES
        for h in range(y.shape[1] // cw):
            _store_row_packed(ys_ref, y[:, h * cw:(h + 1) * cw], lead=(h,))


def _moe_mm(block_e, n_used, xs, row_w, layer, w_gu, b_gu, w_dn, b_dn):
    n_blocks = xs.shape[0] // (MOE_BLOCK * PACK)
    depth, n_e, d, f2 = w_gu.shape
    n_half = d // (PACK * LANES)
    ew = lambda shape: pl.BlockSpec((None, None) + shape, lambda i, be, nu: (layer, be[i], 0, 0))
    grid_spec = pltpu.PrefetchScalarGridSpec(
        num_scalar_prefetch=2,
        grid=(n_blocks,),
        in_specs=[pl.BlockSpec((MOE_BLOCK * PACK, LANES), lambda i, be, nu: (i, 0)),
                  pl.BlockSpec((None, 1, MOE_BLOCK), lambda i, be, nu: (i, 0, 0)),
                  ew((d, f2)), ew((1, f2)), ew((f2 // 2, d)), ew((1, d))],
        out_specs=pl.BlockSpec((n_half, MOE_BLOCK * PACK, LANES), lambda i, be, nu: (0, i, 0)),
        scratch_shapes=[pltpu.VMEM((d, f2), BF16), pltpu.VMEM((f2 // 2, d), BF16)],
    )
    return pl.pallas_call(
        _moe_mm_kernel,
        out_shape=jax.ShapeDtypeStruct((n_half, xs.shape[0], LANES), F32),
        grid_spec=grid_spec,
        compiler_params=_cparams(("arbitrary",)),
        name="moe_mm",
    )(block_e, n_used, xs, row_w.reshape(n_blocks, 1, MOE_BLOCK), w_gu, b_gu.reshape(depth, n_e, 1, f2),
      w_dn, b_dn.reshape(depth, n_e, 1, d))


def _moe_combine_kernel(nv_ref, dst_ref, ys_ref, acc_ref):
    i = pl.program_id(1)

    @pl.when(i == 0)
    def _():
        acc_ref[...] = jnp.zeros_like(acc_ref)

    def body(r8, carry):
        base = r8 * ROWS_PER_STEP
        dst = [_packed_row(dst_ref[0, base + j]) for j in range(ROWS_PER_STEP)]
        group = ROWS_PER_STEP * PACK
        ys = ys_ref[pl.ds(pl.multiple_of(r8 * group, group), group), :]
        new = [acc_ref[dst[j], :] + ys[j * PACK:(j + 1) * PACK, :] for j in range(ROWS_PER_STEP)]
        for j in range(ROWS_PER_STEP):
            acc_ref[dst[j], :] = new[j]
        return carry

    blocks_per_step = dst_ref.shape[1] // MOE_BLOCK
    groups_per_block = MOE_BLOCK // ROWS_PER_STEP
    for kb in range(blocks_per_step):
        groups = (nv_ref[i * blocks_per_step + kb] + ROWS_PER_STEP - 1) // ROWS_PER_STEP
        lax.fori_loop(kb * groups_per_block, kb * groups_per_block + groups, body, 0)


def _moe_combine(n_valid, row_dst, ys, n_rows_out):
    n_half, packed_rows, _ = ys.shape
    step = _move_rows(packed_rows // (MOE_BLOCK * PACK))
    n_blocks = packed_rows // (step * PACK)
    idx_spec = pl.BlockSpec((None, 1, step), lambda j, i, nu: (i, 0, 0), memory_space=pltpu.SMEM)
    grid_spec = pltpu.PrefetchScalarGridSpec(
        num_scalar_prefetch=1,
        grid=(n_half, n_blocks),
        in_specs=[idx_spec, pl.BlockSpec((None, step * PACK, LANES), lambda j, i, nu: (j, i, 0))],
        out_specs=pl.BlockSpec((None, n_rows_out * PACK, LANES), lambda j, i, nu: (j, 0, 0),
                               pipeline_mode=pl.Buffered(1)),
    )
    return pl.pallas_call(
        _moe_combine_kernel,
        out_shape=jax.ShapeDtypeStruct((n_half, n_rows_out * PACK, LANES), F32),
        grid_spec=grid_spec,
        compiler_params=_cparams(("arbitrary", "arbitrary")),
        name="moe_combine",
    )(n_valid, row_dst.reshape(n_blocks, 1, step), ys)


def _moe_inverse_kernel(pad_lo_ref, pad_hi_ref, dest_ref, inv_ref):
    def init(r, c):
        inv_ref[r] = -1
        return c

    def pad_range(k, carry):
        lax.fori_loop(pad_lo_ref[k], pad_hi_ref[k], init, 0)
        return carry

    n_ranges = pad_lo_ref.shape[0]
    lax.fori_loop(0, n_ranges - 1, pad_range, 0)

    def pad_block(blk, carry):
        lax.fori_loop(0, MOE_BLOCK, lambda r, c: init(blk * MOE_BLOCK + r, c), 0, unroll=8)
        return carry

    tail_end = pad_hi_ref[n_ranges - 1]
    lax.fori_loop(pad_lo_ref[n_ranges - 1] // MOE_BLOCK, tail_end // MOE_BLOCK, pad_block, 0)
    lax.fori_loop(tail_end // MOE_BLOCK * MOE_BLOCK, tail_end, init, 0)

    def body(a, carry):
        inv_ref[dest_ref[a]] = a
        return carry

    lax.fori_loop(0, dest_ref.shape[0], body, 0, unroll=8)


def _moe_inverse(pad_lo, pad_hi, dest, rows):
    smem = pl.BlockSpec(memory_space=pltpu.SMEM)
    return pl.pallas_call(
        _moe_inverse_kernel,
        out_shape=jax.ShapeDtypeStruct((rows,), jnp.int32),
        in_specs=[smem, smem, smem],
        out_specs=smem,
        name="moe_inverse",
    )(pad_lo, pad_hi, dest)


def _moe(h2p, codes, weights, counts, layer, w_gu, b_gu, w_dn, b_dn):
    n = h2p.shape[0] // PACK
    n_assign = n * TOP_K
    n_blocks = -(-n_assign // MOE_BLOCK) + N_EXPERTS
    rows = n_blocks * MOE_BLOCK
    padded = -(-counts // MOE_BLOCK) * MOE_BLOCK
    ends = jnp.cumsum(padded)
    offsets = ends - padded
    starts = jnp.arange(n_blocks, dtype=jnp.int32) * MOE_BLOCK
    block_e = jnp.minimum(jnp.sum(ends[None, :] <= starts[:, None], axis=1), N_EXPERTS - 1).astype(jnp.int32)
    n_used = (ends[-1:] // MOE_BLOCK).astype(jnp.int32)
    of_block = block_e[:, None] == jnp.arange(N_EXPERTS, dtype=jnp.int32)[None, :]
    data_end = jnp.sum(jnp.where(of_block, (offsets + counts)[None, :], 0), axis=1)
    n_valid = jnp.clip(data_end - starts, 0, MOE_BLOCK).astype(jnp.int32)
    expert = lax.shift_right_logical(codes, CODE_SHIFT)
    which = expert[:, None] == jnp.arange(N_EXPERTS, dtype=jnp.int32)[None, :]
    dest = (codes & ((1 << CODE_SHIFT) - 1)) + jnp.sum(jnp.where(which, offsets[None, :], 0), axis=1)
    spare = 8
    dest = jnp.where(codes >= 0, dest, rows)
    pad_lo = jnp.concatenate([offsets + counts, ends[-1:]]).astype(jnp.int32)
    pad_hi = jnp.concatenate([ends, jnp.full((1,), rows + spare, ends.dtype)]).astype(jnp.int32)
    inv = _moe_inverse(pad_lo, pad_hi, dest.astype(jnp.int32), rows + spare)[:rows]
    tok = lax.shift_right_logical(inv, TOP_K.bit_length() - 1)
    row_src = jnp.where(inv >= 0, tok, 0)
    row_dst = jnp.where(inv >= 0, tok, n)
    row_w = jnp.where(inv >= 0, jnp.take(weights, jnp.maximum(inv, 0)), 0.0)
    xs = _moe_gather(row_src, h2p, n_blocks)
    ys = _moe_mm(block_e, n_used, xs, row_w, layer, w_gu, b_gu, w_dn, b_dn)
    return _moe_combine(n_valid, row_dst, ys, n + 8)


def _final_kernel(x_ref, acc_ref, g2_ref, g_ref, o_ref):
    x = x_ref[...] + g2_ref[...] * _load_acc(acc_ref)
    ms = jnp.mean(x * x, axis=-1, keepdims=True)
    o_ref[...] = x * lax.rsqrt(ms + EPS) * g_ref[...]


def _final(x, acc, g2, g, n_ctx_tiles):
    b, t, d = x.shape
    nt = t // TM - n_ctx_tiles
    tile_in = pl.BlockSpec((None, TM, d), lambda b_, t_: (b_, t_ + n_ctx_tiles, 0))
    acc_tile = _acc_tile_spec(acc, lambda b_, t_: b_ * (t // TM) + t_ + n_ctx_tiles)
    return pl.pallas_call(
        _final_kernel,
        out_shape=jax.ShapeDtypeStruct((b, nt * TM, d), F32),
        grid=(b, nt),
        in_specs=[tile_in, acc_tile,
                  pl.BlockSpec((None, None, 1, d), lambda b_, t_: (b_, 1, 0, 0)),
                  pl.BlockSpec((1, d), lambda b_, t_: (0, 0))],
        out_specs=pl.BlockSpec((None, TM, d), lambda b_, t_: (b_, t_, 0)),
        compiler_params=_cparams(("arbitrary", "arbitrary")),
        name="final_norm",
    )(x, acc, g2, g)


def _mod_tables(mod, b, d):
    outs = []
    for j in range(6):
        m = mod[:, j * d:(j + 1) * d]
        lat = m[:b]
        ctx = jnp.broadcast_to(m[b:b + 1], (b, d))
        outs.append(jnp.stack([ctx, lat], axis=1)[:, :, None, :])
    return outs


def _routing(code):
    codes = code[:, :, :TOP_K].reshape(-1)
    weights = lax.bitcast_convert_type(code[:, :, TOP_K:2 * TOP_K], F32).reshape(-1)
    return codes, weights


def _router_params(router_w, router_b):
    d, e = router_w.shape
    rw = jnp.zeros((d, 128), F32).at[:, :e].set(router_w).astype(BF16)
    rb = jnp.full((1, 128), NEG, F32).at[0, :e].set(router_b)
    return rw, rb


def kernel(x, c, ctx, c_ctx, mod_w, mod_b, norm1_g, norm2_g, ev_w_in, ev_conv_w, gdn_a_log, gdn_dt_bias,
           gdn_norm_g, ml_i_bias, ml_f_bias, ml_norm_g, ev_w_out, od_w_in, ret_norm_g, od_w_out,
           router_w, router_b, moe_w_gu, moe_b_gu, moe_w_dn, moe_b_dn, final_g):
    b, s, d = x.shape
    n_ctx = ctx.shape[1]
    depth = mod_w.shape[0]
    assert n_ctx == TM and s % TM == 0 and depth == 2 and b % SCAN_BG == 0
    t = n_ctx + s
    n_tok = b * t

    cond = jnp.concatenate([c, c_ctx[None, :], jnp.zeros((8 - b - 1, d), F32)], axis=0)
    mod = _adaln(cond, mod_w, mod_b)

    sh1, sc1, g1, sh2, sc2, g2 = _mod_tables(mod[0], b, d)
    qk_w = GDN_H * HD
    conv_ch = 3 * qk_w
    ng = N_DIR * GDN_H
    w_in = ev_w_in[0]
    o_z = conv_ch
    o_a = o_z + qk_w
    o_mq = o_a + 2 * ng
    o_i = o_mq + 4 * qk_w
    w_main = jnp.concatenate([w_in[:, :o_a], w_in[:, o_mq:o_i]], axis=1).astype(BF16)
    w_gate = jnp.concatenate([w_in[:, o_a:o_mq], w_in[:, o_i:o_i + 2 * ng],
                              jnp.zeros((d, 128 - 4 * ng), F32)], axis=1).astype(BF16)
    zeros_g = jnp.zeros((ng,), F32)
    rate = jnp.concatenate([jnp.exp(gdn_a_log[0].astype(F32)).reshape(-1), jnp.zeros((128 - ng,), F32)])[None, :]
    gbias = jnp.concatenate([gdn_dt_bias[0].reshape(-1), zeros_g, ml_i_bias[0].reshape(-1),
                             ml_f_bias[0].reshape(-1), jnp.zeros((128 - 4 * ng,), F32)])[None, :].astype(F32)
    main, gates = _proj_even(ctx, x, norm1_g[0][None, :], sh1, sc1, w_main, w_gate, ev_conv_w[0], rate, gbias)
    gates_c = gates.reshape(b, t // CH, CH, 4 * ng)
    gates_r = jnp.swapaxes(gates_c, 2, 3)
    og_f, og_b, om_f, om_b = _even_scan(main, gates_c, gates_r)
    rw, rb = _router_params(router_w[0], router_b[0])
    gdn_g = jnp.tile(gdn_norm_g[0], GDN_H)[None, :]
    x1, h2p, code, cnt = _merge_even(og_f, og_b, om_f, om_b, main, gdn_g, ml_norm_g[0][None, :],
                                     ev_w_out[0].astype(BF16), ctx, x, g1, norm2_g[0][None, :], sh2, sc2, rw, rb)
    acc = _moe(h2p.reshape(n_tok * PACK, LANES), *_routing(code), cnt[0, :N_EXPERTS].astype(jnp.int32),
               0, moe_w_gu, moe_b_gu, moe_w_dn, moe_b_dn)
    g2_prev = g2

    sh1, sc1, g1, sh2, sc2, g2 = _mod_tables(mod[1], b, d)
    x2, main_o = _proj_odd(x1, acc, g2_prev, norm1_g[1][None, :], sh1, sc1, od_w_in[0].astype(BF16))
    o_f, o_b = _ret_scan(main_o)
    rw, rb = _router_params(router_w[1], router_b[1])
    x3, h2p, code, cnt = _merge_odd(o_f, o_b, main_o, ret_norm_g[0][None, :], od_w_out[0].astype(BF16),
                                    x2, g1, norm2_g[1][None, :], sh2, sc2, rw, rb)
    acc = _moe(h2p.reshape(n_tok * PACK, LANES), *_routing(code), cnt[0, :N_EXPERTS].astype(jnp.int32),
               1, moe_w_gu, moe_b_gu, moe_w_dn, moe_b_dn)
    return _final(x3, acc, g2, final_g[None, :], n_ctx // TM)
```

```python
import jax
import jax.numpy as jnp
import numpy as np
from jax import lax
from jax.experimental import pallas as pl
from jax.experimental.pallas import tpu as pltpu

F32 = jnp.float32
BF16 = jnp.bfloat16
U32 = jnp.uint32
HIGHEST = lax.Precision.HIGHEST

EPS = 1e-6
CH = 64
TM = 256
CPB = TM // CH
HD = 128
N_DIR = 2
GDN_H = 4
ML_H = 4
RET_H = 8
RET_DV = 256
CONV_W = 3
N_EXPERTS = 32
TOP_K = 4
SWIGLU_ALPHA = 1.702
SWIGLU_LIMIT = 7.0
MOE_BLOCK = 512
ROPE_BASE = 10000.0
NEG = -1e30
VMEM_LIMIT = 56 * 1024 * 1024


def _cparams(sem):
    return pltpu.CompilerParams(dimension_semantics=sem, vmem_limit_bytes=VMEM_LIMIT)


def _dot(a, b, precision=None):
    return jnp.dot(a, b, preferred_element_type=F32, precision=precision)


def _dot_nt(a, b):
    return lax.dot_general(a, b, (((1,), (1,)), ((), ())), preferred_element_type=F32)


def _dot_tn(a, b):
    return lax.dot_general(a, b, (((0,), (0,)), ((), ())), preferred_element_type=F32)


def _sigmoid(x):
    return 1.0 / (1.0 + jnp.exp(-x))


def _silu(x):
    return x * _sigmoid(x)


def _group_sum(x, w):
    outs = []
    for j in range(x.shape[1] // w):
        s = jnp.sum(x[:, j * w:(j + 1) * w], axis=-1, keepdims=True)
        outs.append(jnp.broadcast_to(s, (x.shape[0], w)))
    return outs[0] if len(outs) == 1 else jnp.concatenate(outs, axis=1)


def _rms_mod(x, g, sh, sc):
    ms = jnp.mean(x * x, axis=-1, keepdims=True)
    return (x * lax.rsqrt(ms + EPS) * g) * (1.0 + sc) + sh


LANES = 128
PACK = 4


def _store_row_packed(ref, x, lead=()):
    rows = x.shape[0]
    for g in range(x.shape[1] // LANES):
        ref[lead + (pl.ds(g, rows, stride=PACK), slice(None))] = x[:, g * LANES:(g + 1) * LANES]


def _load_row_packed(ref, rows, lead=()):
    return jnp.concatenate([ref[lead + (pl.ds(g, rows, stride=PACK), slice(None))] for g in range(PACK)], axis=1)


def _load_acc(acc_ref):
    rows = acc_ref.shape[1] // PACK
    return jnp.concatenate([_load_row_packed(acc_ref, rows, lead=(h,)) for h in range(acc_ref.shape[0])], axis=1)


def _acc_tile_spec(acc, tile_of):
    return pl.BlockSpec((acc.shape[0], TM * PACK, LANES), lambda b_, t_: (0, tile_of(b_, t_), 0))


def _adaln_kernel(c_ref, w_ref, b_ref, o_ref):
    c = c_ref[...]
    o_ref[...] = _dot(_silu(c), w_ref[...], precision=HIGHEST) + b_ref[...]


def _adaln(cond, mod_w, mod_b):
    depth, d, d6 = mod_w.shape
    n = d6 // d
    return pl.pallas_call(
        _adaln_kernel,
        out_shape=jax.ShapeDtypeStruct((depth, cond.shape[0], d6), F32),
        grid=(depth, n),
        in_specs=[pl.BlockSpec(cond.shape, lambda l, j: (0, 0)),
                  pl.BlockSpec((None, d, d), lambda l, j: (l, 0, j)),
                  pl.BlockSpec((None, 1, d), lambda l, j: (l, 0, j))],
        out_specs=pl.BlockSpec((None, cond.shape[0], d), lambda l, j: (l, 0, j)),
        compiler_params=_cparams(("arbitrary", "arbitrary")),
        name="adaln",
    )(cond, mod_w, mod_b.reshape(depth, 1, d6))


def _stream_tile(ctx_ref, x_ref):
    return jnp.where(pl.program_id(1) == 0, ctx_ref[...], x_ref[...])


def _stream_specs(d):
    return (pl.BlockSpec((None, TM, d), lambda b, t: (b, 0, 0)),
            pl.BlockSpec((None, TM, d), lambda b, t: (b, jnp.maximum(t - 1, 0), 0)))


def _proj_even_kernel(ctx_ref, x_ref, g_ref, sh_ref, sc_ref, w_ref, wg_ref, cw_ref, rate_ref, gb_ref,
                      main_ref, gates_ref):
    t = pl.program_id(1)
    h = _rms_mod(_stream_tile(ctx_ref, x_ref), g_ref[...], sh_ref[...], sc_ref[...])
    hb = h.astype(BF16)
    qk_w = GDN_H * HD

    row = lax.broadcasted_iota(jnp.int32, (TM, 1), 0)
    pos = jnp.where(t > 0, row & (CH - 1), row)
    last = jnp.where(t > 0, CH - 1, TM - 1)
    left_ok = pos != 0
    right_ok = pos != last
    for seg in range(3):
        sl = slice(seg * qk_w, (seg + 1) * qk_w)
        u = _dot(hb, w_ref[:, sl])
        um = jnp.where(left_ok, pltpu.roll(u, 1, 0), 0.0)
        up = jnp.where(right_ok, pltpu.roll(u, TM - 1, 0), 0.0)
        cv = _silu(um * cw_ref[0:1, sl] + u * cw_ref[1:2, sl] + up * cw_ref[2:3, sl])
        if seg < 2:
            ss = _group_sum(cv * cv, HD)
            cv = cv * lax.rsqrt(ss + EPS)
            if seg == 0:
                cv = cv * (HD ** -0.5)
        main_ref[:, sl] = cv.astype(BF16)
    for seg in range(3, 8):
        sl = slice(seg * qk_w, (seg + 1) * qk_w)
        u = _dot(hb, w_ref[:, sl])
        if seg == 5:
            u = u * (HD ** -0.5)
        main_ref[:, sl] = u.astype(BF16)

    z = _dot(hb, wg_ref[...]) + gb_ref[...]
    tl = jnp.log(1.0 + jnp.exp(-jnp.abs(z)))
    sp_pos = jnp.maximum(z, 0.0) + tl
    sp_neg = jnp.maximum(-z, 0.0) + tl
    lane = lax.broadcasted_iota(jnp.int32, z.shape, 1)
    ng = N_DIR * GDN_H
    res = jnp.where(lane < ng, -rate_ref[...] * sp_pos,
                    jnp.where(lane < 2 * ng, _sigmoid(z),
                              jnp.where(lane < 3 * ng, z, -sp_neg)))
    gates_ref[...] = res[:, :gates_ref.shape[-1]]


def _tile_mod_spec(d):
    return pl.BlockSpec((None, None, 1, d), lambda b, t: (b, jnp.minimum(t, 1), 0, 0))


def _proj_even(ctx, x, g, sh, sc, w_main, w_gate, conv_w, rate, gbias):
    b, s, d = x.shape
    t = ctx.shape[1] + s
    n = w_main.shape[1]
    ngl = 4 * N_DIR * GDN_H
    const = lambda shape: pl.BlockSpec(shape, lambda b_, t_: (0,) * len(shape))
    return pl.pallas_call(
        _proj_even_kernel,
        out_shape=(jax.ShapeDtypeStruct((b, t, n), BF16), jax.ShapeDtypeStruct((b, t, ngl), F32)),
        grid=(b, t // TM),
        in_specs=[*_stream_specs(d),
                  const((1, d)), _tile_mod_spec(d), _tile_mod_spec(d),
                  const(w_main.shape), const(w_gate.shape), const(conv_w.shape),
                  const(rate.shape), const(gbias.shape)],
        out_specs=(pl.BlockSpec((None, TM, n), lambda b_, t_: (b_, t_, 0)),
                   pl.BlockSpec((None, TM, ngl), lambda b_, t_: (b_, t_, 0))),
        compiler_params=_cparams(("arbitrary", "arbitrary")),
        name="proj_even",
    )(ctx, x, g, sh, sc, w_main, w_gate, conv_w, rate, gbias)


def _rev_tile(i, nt):
    return jnp.where(i == 0, 0, nt - i)


def _tri_masks():
    r = lax.broadcasted_iota(jnp.int32, (CH, CH), 0)
    c = lax.broadcasted_iota(jnp.int32, (CH, CH), 1)
    return r >= c, r > c, r <= c, r < c


SCAN_BG = 2
SOLVE_BASE = 8


def _scan_streams(heads):
    groups = [(bi, d) for bi in range(SCAN_BG) for d in range(N_DIR)]
    streams = [(gi, hh) for gi in range(len(groups)) for hh in range(heads)]
    return groups, streams


def _cumsum_both(groups, gcs, grs, lower, upper):
    tri = (lower.astype(F32), upper.astype(F32))
    cs_c = [_dot(tri[d], gc, precision=HIGHEST) for (_, d), gc in zip(groups, gcs)]
    cs_r = [_dot(gr, tri[1 - d], precision=HIGHEST) for (_, d), gr in zip(groups, grs)]
    return cs_c, cs_r


def _store_heads(refs, groups, rows, outs, heads):
    for gi, (bi, d) in enumerate(groups):
        o_ref = refs[d][-1]
        tile = jnp.concatenate(outs[gi * heads:(gi + 1) * heads], axis=1)
        o_ref[bi, rows[d], :] = tile.astype(o_ref.dtype)


class _GdnChunk:
    def __init__(self, refs, s_ref):
        self.refs, self.s_ref = refs, s_ref
        self.groups, self.streams = _scan_streams(GDN_H)

    def prelude(self, masks, rows, gcs, cs_c, cs_r):
        lower, lstrict, upper, ustrict = masks
        refs, groups = self.refs, self.groups
        ng = N_DIR * GDN_H
        st = []
        for gi, hh in self.streams:
            bi, d = groups[gi]
            ci = d * GDN_H + hh
            cols = slice(hh * HD, (hh + 1) * HD)
            incl, strict = (lower, lstrict) if d == 0 else (upper, ustrict)
            g_col = cs_c[gi][:, ci:ci + 1]
            g_row = cs_r[gi][ci:ci + 1, :]
            beta = gcs[gi][:, ng + ci:ng + ci + 1]
            tot = g_col[CH - 1:CH, :] if d == 0 else g_col[0:1, :]
            decay = jnp.where(incl, jnp.exp(jnp.where(incl, g_col - g_row, 0.0)), 0.0)
            q = refs[d][0][bi, rows[d], cols].astype(F32)
            k = refs[d][1][bi, rows[d], cols].astype(F32)
            v = refs[d][2][bi, rows[d], cols].astype(F32)
            kbeta = k * beta
            eg = jnp.exp(g_col)
            st.append(dict(
                strict=strict, decay=decay, kb=k.astype(BF16), kbetab=kbeta.astype(BF16), qb=q.astype(BF16),
                x=jnp.concatenate([v * beta, kbeta * eg], axis=1),
                qe=(q * eg).astype(BF16), kdec=(k * jnp.exp(tot - g_col)).astype(BF16), cd=jnp.exp(tot)))
        self.st = st

    def matmuls(self, rows):
        st, s_ref, streams = self.st, self.s_ref, self.streams
        kk = [_dot_nt(s["kbetab"], s["kb"]) for s in st]
        qk = [(_dot_nt(s["qb"], s["kb"]) * s["decay"]).astype(BF16) for s in st]
        r = lax.broadcasted_iota(jnp.int32, (CH, CH), 0)
        c = lax.broadcasted_iota(jnp.int32, (CH, CH), 1)
        eye = jnp.where(r == c, 1.0, 0.0)
        blocks_differ = lambda w: lax.shift_right_logical(r, w.bit_length() - 1) ^ \
            lax.shift_right_logical(c, w.bit_length() - 1)
        a = [jnp.where(s["strict"], m * s["decay"], 0.0) for s, m in zip(st, kk)]
        p = [-jnp.where(blocks_differ(SOLVE_BASE) == 0, m, 0.0) for m in a]
        t = [eye + m for m in p]
        w = 2
        while w < SOLVE_BASE:
            pb = [m.astype(BF16) for m in p]
            p = [_dot(m, m) for m in pb]
            t = [tt + _dot(tt.astype(BF16), m.astype(BF16)) for tt, m in zip(t, p)]
            w *= 2
        w = SOLVE_BASE
        while w < CH:
            off = blocks_differ(w) == 1
            tb = [tt.astype(BF16) for tt in t]
            ta = [_dot(tt, jnp.where(off, m, 0.0).astype(BF16)) for tt, m in zip(tb, a)]
            t = [tt - _dot(m.astype(BF16), tt2) for tt, m, tt2 in zip(t, ta, tb)]
            w *= 2
        x = [_dot(tt.astype(BF16), s["x"].astype(BF16)) for tt, s in zip(t, st)]
        s_old = [s_ref[si] for si in range(len(streams))]
        sb = [s.astype(BF16) for s in s_old]
        vnb = [(xx[:, :HD] - _dot(xx[:, HD:].astype(BF16), s)).astype(BF16) for xx, s in zip(x, sb)]
        o = [_dot(s["qe"], sbi) + _dot(m, vn) for s, sbi, m, vn in zip(st, sb, qk, vnb)]
        s_new = [so * s["cd"] + _dot_tn(s["kdec"], vn) for s, so, vn in zip(st, s_old, vnb)]
        for si, s in enumerate(s_new):
            s_ref[si] = s
        _store_heads(self.refs, self.groups, rows, o, GDN_H)


def _scan_specs(nt, width, col):
    fwd = pl.BlockSpec((SCAN_BG, TM, width), lambda b, i: (b, i, col))
    bwd = pl.BlockSpec((SCAN_BG, TM, width), lambda b, i: (b, _rev_tile(i, nt), col))
    return fwd, bwd


def _gate_specs(nt, shape):
    fwd = pl.BlockSpec((SCAN_BG, CPB) + shape, lambda b, i: (b, i, 0, 0))
    bwd = pl.BlockSpec((SCAN_BG, CPB) + shape, lambda b, i: (b, _rev_tile(i, nt), 0, 0))
    return fwd, bwd


class _MlstmChunk:
    def __init__(self, refs, c_ref, m_ref):
        self.refs, self.c_ref, self.m_ref = refs, c_ref, m_ref
        self.groups, self.streams = _scan_streams(ML_H)

    def prelude(self, masks, rows, gcs, grs, cs_c, cs_r):
        lower, _, upper, _ = masks
        refs, groups, streams, m_ref = self.refs, self.groups, self.streams, self.m_ref
        ng = N_DIR * GDN_H
        i_off = 2 * ng
        f_off = 2 * ng + N_DIR * ML_H
        lane = lax.broadcasted_iota(jnp.int32, (CH, HD), 1)
        ones_col = jnp.where(lane == 0, 1.0, 0.0).astype(BF16)
        ns = len(streams)
        dirs = [groups[gi][1] for gi, _ in streams]
        chan = [groups[gi][1] * ML_H + hh for gi, hh in streams]
        b_col = [cs_c[gi][:, f_off + c:f_off + c + 1] for (gi, _), c in zip(streams, chan)]
        b_row = [cs_r[gi][f_off + c:f_off + c + 1, :] for (gi, _), c in zip(streams, chan)]
        i_col = [gcs[gi][:, i_off + c:i_off + c + 1] for (gi, _), c in zip(streams, chan)]
        i_row = [grs[gi][i_off + c:i_off + c + 1, :] for (gi, _), c in zip(streams, chan)]
        b_tot = [bc[CH - 1:CH, :] if d == 0 else bc[0:1, :] for bc, d in zip(b_col, dirs)]
        m_rows = [m_ref[si] for si in range(ns)]
        m_old = [mr[:, 0:1] for mr in m_rows]
        d_in = [jnp.where(lower if d == 0 else upper, bc - br + ir, NEG)
                for d, bc, br, ir in zip(dirs, b_col, b_row, i_row)]
        d_end = [bt - br + ir for bt, br, ir in zip(b_tot, b_row, i_row)]
        mx_in = [jnp.max(a, axis=-1, keepdims=True) for a in d_in]
        mx_end = [jnp.max(a, axis=-1, keepdims=True) for a in d_end]
        d_carry = [bc + m for bc, m in zip(b_col, m_old)]
        m_t = [jnp.maximum(a, b_) for a, b_ in zip(d_carry, mx_in)]
        carry_end = [bt + m for bt, m in zip(b_tot, m_old)]
        m_new = [jnp.maximum(a, b_) for a, b_ in zip(carry_end, mx_end)]
        p_in = [jnp.exp(a - b_) for a, b_ in zip(d_in, m_t)]
        w_end = [jnp.exp(bt - bc + ic - mn) for bt, bc, ic, mn in zip(b_tot, b_col, i_col, m_new)]
        st = []
        for si, (gi, hh) in enumerate(streams):
            bi, d = groups[gi]
            cols = slice(hh * HD, (hh + 1) * HD)
            k = refs[d][1][bi, rows[d], cols]
            v = refs[d][2][bi, rows[d], cols]
            st.append(dict(
                q=refs[d][0][bi, rows[d], cols], k=k,
                v_aug=jnp.concatenate([v, ones_col], axis=1),
                p_in=p_in[si], w_carry=jnp.exp(d_carry[si] - m_t[si]), floor=jnp.exp(-m_t[si]),
                kw=(k.astype(F32) * w_end[si]).astype(BF16),
                f_end=jnp.exp(carry_end[si] - m_new[si]),
                m_new=jnp.broadcast_to(m_new[si], m_rows[si].shape)))
        self.st = st

    def matmuls(self, rows):
        st, c_ref, m_ref, streams = self.st, self.c_ref, self.m_ref, self.streams
        sc = [(_dot_nt(s["q"], s["k"]) * s["p_in"]).astype(BF16) for s in st]
        c_old = [c_ref[si] for si in range(len(streams))]
        qc = [_dot(s["q"], c.astype(BF16)) for s, c in zip(st, c_old)]
        nd = [s["w_carry"] * a + _dot(m, s["v_aug"]) for s, a, m in zip(st, qc, sc)]
        hout = [a[:, :HD] / jnp.maximum(jnp.abs(a[:, HD:HD + 1]), s["floor"]) for s, a in zip(st, nd)]
        c_new = [s["f_end"] * c + _dot_tn(s["kw"], s["v_aug"]) for s, c in zip(st, c_old)]
        for si, (s, c) in enumerate(zip(st, c_new)):
            c_ref[si] = c
            m_ref[si] = s["m_new"]
        _store_heads(self.refs, self.groups, rows, hout, ML_H)


def _even_scan_kernel(gqf, gkf, gvf, gqb, gkb, gvb, mqf, mkf, mvf, mqb, mkb, mvb, gcf, grf, gcb, grb,
                      ogf, ogb, omf, omb, s_ref, c_ref, m_ref):
    i = pl.program_id(1)

    @pl.when(i == 0)
    def _():
        s_ref[...] = jnp.zeros_like(s_ref)
        c_ref[...] = jnp.zeros_like(c_ref)
        m_ref[...] = jnp.zeros_like(m_ref)

    masks = _tri_masks()
    gdn = _GdnChunk(((gqf, gkf, gvf, ogf), (gqb, gkb, gvb, ogb)), s_ref)
    mls = _MlstmChunk(((mqf, mkf, mvf, omf), (mqb, mkb, mvb, omb)), c_ref, m_ref)
    gate_refs = ((gcf, grf), (gcb, grb))

    def chunk_body(cc, carry):
        cidx = (cc, CPB - 1 - cc)
        rows = tuple(pl.ds(pl.multiple_of(c * CH, CH), CH) for c in cidx)
        gcs = [gate_refs[d][0][bi, cidx[d]] for bi, d in gdn.groups]
        grs = [gate_refs[d][1][bi, cidx[d]] for bi, d in gdn.groups]
        cs_c, cs_r = _cumsum_both(gdn.groups, gcs, grs, masks[0], masks[2])
        gdn.prelude(masks, rows, gcs, cs_c, cs_r)
        mls.prelude(masks, rows, gcs, grs, cs_c, cs_r)
        gdn.matmuls(rows)
        mls.matmuls(rows)
        return carry

    lax.fori_loop(0, CPB, chunk_body, 0)


def _even_scan(main, gates_c, gates_r):
    b, t, _ = main.shape
    nt = t // TM
    w = GDN_H * HD
    ngl = gates_c.shape[-1]
    qkv = lambda cols: [_scan_specs(nt, w, c)[d] for d in range(N_DIR) for c in cols]
    gcf, gcb = _gate_specs(nt, (CH, ngl))
    grf, grb = _gate_specs(nt, (ngl, CH))
    of, ob = _scan_specs(nt, w, 0)
    out = jax.ShapeDtypeStruct((b, t, w), BF16)
    n_streams = SCAN_BG * N_DIR * GDN_H
    return pl.pallas_call(
        _even_scan_kernel,
        out_shape=(out, out, out, out),
        grid=(b // SCAN_BG, nt),
        in_specs=qkv((0, 1, 2)) + qkv((4, 5, 6)) + [gcf, grf, gcb, grb],
        out_specs=(of, ob, of, ob),
        scratch_shapes=[pltpu.VMEM((n_streams, HD, HD), F32), pltpu.VMEM((n_streams, HD, 2 * HD), F32),
                        pltpu.VMEM((n_streams, 1, HD), F32)],
        compiler_params=_cparams(("arbitrary", "arbitrary")),
        name="even_scan",
    )(*([main] * 12), gates_c, gates_r, gates_c, gates_r)


def _ret_kernel(qf, kf, vf, qb, kb, vb, intra_ref, cross_ref, tail_ref, cd_ref, of, ob, s_ref):
    i = pl.program_id(1)

    @pl.when(i == 0)
    def _():
        s_ref[...] = jnp.zeros_like(s_ref)

    refs = ((qf, kf, vf, of), (qb, kb, vb, ob))
    groups, streams = _scan_streams(RET_H)

    def chunk_body(cc, carry):
        cidx = (cc, CPB - 1 - cc)
        rows = tuple(pl.ds(pl.multiple_of(c * CH, CH), CH) for c in cidx)
        st = []
        for gi, hh in streams:
            bi, d = groups[gi]
            ti = d * RET_H + hh
            kcols = slice(hh * HD, (hh + 1) * HD)
            q = refs[d][0][bi, rows[d], kcols]
            k = refs[d][1][bi, rows[d], kcols]
            st.append(dict(
                ti=ti, q=q, k=k, v=refs[d][2][bi, rows[d], slice(hh * RET_DV, (hh + 1) * RET_DV)],
                qc=(q.astype(F32) * cross_ref[ti]).astype(BF16),
                kt=(k.astype(F32) * tail_ref[ti]).astype(BF16)))
        sc = [(_dot_nt(s["q"], s["k"]) * intra_ref[s["ti"]]).astype(BF16) for s in st]
        s_old = [s_ref[si] for si in range(len(streams))]
        o = [_dot(m, s["v"]) + _dot(s["qc"], so.astype(BF16)) for s, m, so in zip(st, sc, s_old)]
        s_new = [cd_ref[s["ti"]] * so + _dot_tn(s["kt"], s["v"]) for s, so in zip(st, s_old)]
        for si, s in enumerate(s_new):
            s_ref[si] = s
        _store_heads(refs, groups, rows, o, RET_H)
        return carry

    lax.fori_loop(0, CPB, chunk_body, 0)


def _ret_tables():
    pos = np.arange(CH, dtype=np.float64)
    intra, cross, tail, cd = [], [], [], []
    for d in range(N_DIR):
        expo = 5.0 + np.arange(RET_H, dtype=np.float64)
        if d == 1:
            expo = expo[::-1]
        lg = np.log1p(-np.exp2(-expo))
        p = pos if d == 0 else (CH - 1.0 - pos)
        diff = p[:, None] - p[None, :]
        for hh in range(RET_H):
            intra.append(np.where(diff >= 0, np.exp(np.where(diff >= 0, diff, 0.0) * lg[hh]), 0.0))
            cross.append(np.broadcast_to(np.exp((p + 1.0) * lg[hh])[:, None], (CH, HD)))
            tail.append(np.broadcast_to(np.exp((CH - 1.0 - p) * lg[hh])[:, None], (CH, HD)))
            cd.append(np.full((1, RET_DV), np.exp(CH * lg[hh])))
    f = lambda a: jnp.asarray(np.stack(a), F32)
    return f(intra), f(cross), f(tail), f(cd)


def _ret_scan(main):
    b, t, _ = main.shape
    nt = t // TM
    qw = RET_H * HD
    vw = RET_H * RET_DV
    qf, qb = _scan_specs(nt, qw, 0)
    kf, kb = _scan_specs(nt, qw, 1)
    vf, vb = _scan_specs(nt, vw, 1)
    of, ob = _scan_specs(nt, vw, 0)
    tabs = _ret_tables()
    const = lambda a: pl.BlockSpec(a.shape, lambda b_, i_: (0,) * a.ndim)
    return pl.pallas_call(
        _ret_kernel,
        out_shape=(jax.ShapeDtypeStruct((b, t, vw), BF16), jax.ShapeDtypeStruct((b, t, vw), BF16)),
        grid=(b // SCAN_BG, nt),
        in_specs=[qf, kf, vf, qb, kb, vb] + [const(a) for a in tabs],
        out_specs=(of, ob),
        scratch_shapes=[pltpu.VMEM((SCAN_BG * N_DIR * RET_H, HD, RET_DV), F32)],
        compiler_params=_cparams(("arbitrary", "arbitrary")),
        name="ret_scan",
    )(main, main, main, main, main, main, *tabs)


CODE_SHIFT = 17


def _route_and_pack(h2, rw_ref, rb_ref, h2p_ref, code_ref, cnt_ref, base_ref, skip_context):
    h2b = h2.astype(BF16)
    logits = _dot(h2b, rw_ref[...]) + rb_ref[...]
    lane = lax.broadcasted_iota(jnp.int32, logits.shape, 1)
    lane_f = lane.astype(F32)
    vals, idxs = [], []
    cur = logits
    for _ in range(TOP_K):
        m = jnp.max(cur, axis=-1, keepdims=True)
        ix = jnp.min(jnp.where(cur == m, lane_f, float(logits.shape[1])), axis=-1, keepdims=True)
        vals.append(m)
        idxs.append(ix)
        cur = jnp.where(lane_f == ix, NEG, cur)
    es = [jnp.exp(v - vals[0]) for v in vals]
    tot = es[0] + es[1] + es[2] + es[3]

    @pl.when((pl.program_id(0) == 0) & (pl.program_id(1) == 0))
    def _():
        base_ref[...] = jnp.zeros_like(base_ref)

    tm = logits.shape[0]
    routed = pl.program_id(1) > 0
    onehot = jnp.zeros(logits.shape, F32)
    for j in range(TOP_K):
        onehot = jnp.where(lane_f == idxs[j], 1.0, onehot)
    if skip_context:
        onehot = onehot * jnp.where(routed, 1.0, 0.0)
    r_i = lax.broadcasted_iota(jnp.int32, (tm, tm), 0)
    c_i = lax.broadcasted_iota(jnp.int32, (tm, tm), 1)
    before = _dot(jnp.where(r_i > c_i, 1.0, 0.0).astype(BF16), onehot.astype(BF16))
    base = base_ref[...]
    pos = before + base
    total = base + before[tm - 1:tm, :] + onehot[tm - 1:tm, :]
    base_ref[...] = total
    cnt_ref[...] = total

    code = jnp.zeros(logits.shape, F32)
    wgt = jnp.zeros(logits.shape, F32)
    for j in range(TOP_K):
        rank = jnp.sum(jnp.where(lane_f == idxs[j], pos, 0.0), axis=-1, keepdims=True)
        code = jnp.where(lane == j, idxs[j] * float(1 << CODE_SHIFT) + rank, code)
        wgt = jnp.where(lane == TOP_K + j, es[j] / tot, wgt)
    code_i = code.astype(jnp.int32)
    if skip_context:
        code_i = jnp.where(routed, code_i, -1)
    code_ref[...] = jnp.where(lane < TOP_K, code_i, pltpu.bitcast(wgt, jnp.int32))
    half = h2.shape[1] // 2
    r = h2b.astype(F32)
    lo = lax.shift_right_logical(pltpu.bitcast(r[:, :half], U32), jnp.uint32(16))
    hi = pltpu.bitcast(r[:, half:], U32) & jnp.uint32(0xFFFF0000)
    _store_row_packed(h2p_ref, hi | lo)


def _merge_even_kernel(ogf, ogb, omf, omb, z_ref, mo_ref, gg_ref, mg_ref, wo_ref, ctx_ref, x_ref, g1_ref,
                       n2_ref, sh_ref, sc_ref, rw_ref, rb_ref, x1_ref, h2p_ref, code_ref, cnt_ref, base_ref):
    og = ogf[...].astype(F32) + ogb[...].astype(F32)
    ms = _group_sum(og * og, HD) * (1.0 / HD)
    a = og * lax.rsqrt(ms + EPS) * gg_ref[...] * _silu(z_ref[...].astype(F32))
    om = omf[...].astype(F32) + omb[...].astype(F32)
    ms = _group_sum(om * om, HD) * (1.0 / HD)
    m = om * lax.rsqrt(ms + EPS) * mg_ref[...] * _sigmoid(mo_ref[...].astype(F32))
    cat = jnp.concatenate([a, m], axis=1).astype(BF16)
    y = _dot(cat, wo_ref[...])
    x1 = _stream_tile(ctx_ref, x_ref) + g1_ref[...] * y
    x1_ref[...] = x1
    h2 = _rms_mod(x1, n2_ref[...], sh_ref[...], sc_ref[...])
    _route_and_pack(h2, rw_ref, rb_ref, h2p_ref, code_ref, cnt_ref, base_ref, skip_context=False)


def _merge_odd_kernel(of, ob, gate_ref, ng_ref, wo_ref, x_ref, g1_ref,
                      n2_ref, sh_ref, sc_ref, rw_ref, rb_ref, x1_ref, h2p_ref, code_ref, cnt_ref, base_ref):
    o = of[...].astype(F32) + ob[...].astype(F32)
    o = o - _group_sum(o, RET_DV) * (1.0 / RET_DV)
    ms = _group_sum(o * o, RET_DV) * (1.0 / RET_DV)
    y = o * lax.rsqrt(ms + EPS) * ng_ref[...] * _silu(gate_ref[...].astype(F32))
    y = _dot(y.astype(BF16), wo_ref[...])
    x1 = x_ref[...] + g1_ref[...] * y
    x1_ref[...] = x1
    h2 = _rms_mod(x1, n2_ref[...], sh_ref[...], sc_ref[...])
    _route_and_pack(h2, rw_ref, rb_ref, h2p_ref, code_ref, cnt_ref, base_ref, skip_context=True)


def _merge_out(b, t, d):
    assert d // 2 == PACK * LANES
    shapes = (jax.ShapeDtypeStruct((b, t, d), F32), jax.ShapeDtypeStruct((b, t * PACK, LANES), U32),
              jax.ShapeDtypeStruct((b, t, LANES), jnp.int32), jax.ShapeDtypeStruct((1, LANES), F32))
    tile = lambda rows, width: pl.BlockSpec((None, rows, width), lambda b_, t_: (b_, t_, 0))
    specs = (tile(TM, d), tile(TM * PACK, LANES), tile(TM, LANES), pl.BlockSpec((1, LANES), lambda b_, t_: (0, 0)))
    return shapes, specs


def _merge_even(og_f, og_b, om_f, om_b, main, gdn_g, ml_g, w_out, ctx, x, g1, n2, sh2, sc2, rw, rb):
    b, s, d = x.shape
    t = ctx.shape[1] + s
    w = GDN_H * HD
    tile = lambda width, col: pl.BlockSpec((None, TM, width), lambda b_, t_: (b_, t_, col))
    const = lambda a: pl.BlockSpec(a.shape, lambda b_, t_: (0,) * a.ndim)
    shapes, specs = _merge_out(b, t, d)
    return pl.pallas_call(
        _merge_even_kernel,
        out_shape=shapes,
        grid=(b, t // TM),
        in_specs=[tile(w, 0), tile(w, 0), tile(w, 0), tile(w, 0), tile(w, 3), tile(w, 7),
                  const(gdn_g), const(ml_g), const(w_out), *_stream_specs(d), _tile_mod_spec(d),
                  const(n2), _tile_mod_spec(d), _tile_mod_spec(d), const(rw), const(rb)],
        out_specs=specs,
        scratch_shapes=[pltpu.VMEM((1, LANES), F32)],
        compiler_params=_cparams(("arbitrary", "arbitrary")),
        name="merge_even",
    )(og_f, og_b, om_f, om_b, main, main, gdn_g, ml_g, w_out, ctx, x, g1, n2, sh2, sc2, rw, rb)


def _merge_odd(o_f, o_b, main, ret_g, w_out, x, g1, n2, sh2, sc2, rw, rb):
    b, t, d = x.shape
    vw = RET_H * RET_DV
    tile = lambda width, col: pl.BlockSpec((None, TM, width), lambda b_, t_: (b_, t_, col))
    const = lambda a: pl.BlockSpec(a.shape, lambda b_, t_: (0,) * a.ndim)
    shapes, specs = _merge_out(b, t, d)
    return pl.pallas_call(
        _merge_odd_kernel,
        out_shape=shapes,
        grid=(b, t // TM),
        in_specs=[tile(vw, 0), tile(vw, 0), tile(vw, 2), const(ret_g), const(w_out), tile(d, 0),
                  _tile_mod_spec(d), const(n2), _tile_mod_spec(d), _tile_mod_spec(d), const(rw), const(rb)],
        out_specs=specs,
        scratch_shapes=[pltpu.VMEM((1, LANES), F32)],
        compiler_params=_cparams(("arbitrary", "arbitrary")),
        name="merge_odd",
    )(o_f, o_b, main, ret_g, w_out, x, g1, n2, sh2, sc2, rw, rb)


def _proj_odd_kernel(x_ref, acc_ref, g2_ref, g_ref, sh_ref, sc_ref, w_ref, cos_ref, sin_ref,
                     x2_ref, main_ref):
    x2 = x_ref[...] + g2_ref[...] * _load_acc(acc_ref)
    x2_ref[...] = x2
    hb = _rms_mod(x2, g_ref[...], sh_ref[...], sc_ref[...]).astype(BF16)
    qk_w = RET_H * HD
    cos = cos_ref[...]
    sin = sin_ref[...]
    n_seg = w_ref.shape[1] // qk_w
    for seg in range(n_seg):
        sl = slice(seg * qk_w, (seg + 1) * qk_w)
        u = _dot(hb, w_ref[:, sl])
        if seg < 2:
            parts = []
            for hh in range(RET_H):
                uh = u[:, hh * HD:(hh + 1) * HD]
                parts.append(uh * cos + pltpu.roll(uh, HD // 2, 1) * sin)
            u = jnp.concatenate(parts, axis=1)
            if seg == 1:
                u = u * (HD ** -0.5)
        main_ref[:, sl] = u.astype(BF16)


def _rope_tables(t):
    half = HD // 2
    freqs = ROPE_BASE ** (-jnp.arange(half, dtype=F32) / half)
    ang = jnp.arange(t, dtype=F32)[:, None] * freqs[None, :]
    cos, sin = jnp.cos(ang), jnp.sin(ang)
    return jnp.concatenate([cos, cos], axis=1), jnp.concatenate([-sin, sin], axis=1)


def _proj_odd(x, acc, g2, g, sh, sc, w):
    b, t, d = x.shape
    n = w.shape[1]
    cos, sin = _rope_tables(t)
    const = lambda a: pl.BlockSpec(a.shape, lambda b_, t_: (0,) * a.ndim)
    tile = lambda width: pl.BlockSpec((None, TM, width), lambda b_, t_: (b_, t_, 0))
    rope = pl.BlockSpec((TM, HD), lambda b_, t_: (t_, 0))
    nt = t // TM
    acc_tile = _acc_tile_spec(acc, lambda b_, t_: b_ * nt + t_)
    return pl.pallas_call(
        _proj_odd_kernel,
        out_shape=(jax.ShapeDtypeStruct((b, t, d), F32), jax.ShapeDtypeStruct((b, t, n), BF16)),
        grid=(b, nt),
        in_specs=[tile(d), acc_tile, _tile_mod_spec(d), const(g), _tile_mod_spec(d), _tile_mod_spec(d),
                  const(w), rope, rope],
        out_specs=(tile(d), tile(n)),
        compiler_params=_cparams(("arbitrary", "arbitrary")),
        name="proj_odd",
    )(x, acc, g2, g, sh, sc, w, cos, sin)


ROWS_PER_STEP = 8


def _packed_row(r):
    return pl.ds(pl.multiple_of(r * PACK, PACK), PACK)


def _move_rows(n_blocks):
    return MOE_BLOCK * next(k for k in (4, 2, 1) if n_blocks % k == 0)


def _moe_gather_kernel(src_ref, h_ref, xs_ref):
    def body(r8, carry):
        base = r8 * ROWS_PER_STEP
        rows = [h_ref[_packed_row(src_ref[0, base + j]), :] for j in range(ROWS_PER_STEP)]
        for j in range(ROWS_PER_STEP):
            xs_ref[_packed_row(base + j), :] = rows[j]
        return carry

    lax.fori_loop(0, src_ref.shape[1] // ROWS_PER_STEP, body, 0)


def _moe_gather(row_src, h2p, n_blocks):
    step = _move_rows(n_blocks)
    n_steps = n_blocks * MOE_BLOCK // step
    return pl.pallas_call(
        _moe_gather_kernel,
        out_shape=jax.ShapeDtypeStruct((n_blocks * MOE_BLOCK * PACK, LANES), U32),
        grid=(n_steps,),
        in_specs=[pl.BlockSpec((None, 1, step), lambda i: (i, 0, 0), memory_space=pltpu.SMEM),
                  pl.BlockSpec(h2p.shape, lambda i: (0, 0))],
        out_specs=pl.BlockSpec((step * PACK, LANES), lambda i: (i, 0)),
        compiler_params=_cparams(("arbitrary",)),
        name="moe_gather",
    )(row_src.reshape(n_steps, 1, step), h2p)


def _moe_mm_kernel(be_ref, nu_ref, xs_ref, rw_ref, wgu_ref, bgu_ref, wdn_ref, bdn_ref, ys_ref, wgu_s, wdn_s):
    i = pl.program_id(0)
    e = be_ref[i]
    half = wgu_ref.shape[0] // 2
    hl = LANES // 2

    @pl.when(i >= nu_ref[0])
    def _():
        ys_ref[...] = jnp.zeros_like(ys_ref)

    @pl.when(i < nu_ref[0])
    def _():
        @pl.when((i == 0) | (e != be_ref[jnp.maximum(i - 1, 0)]))
        def _():
            wgu_s[...] = wgu_ref[...].astype(BF16)
            for p in range(wdn_ref.shape[0] // LANES):
                first = wdn_ref[p * LANES:p * LANES + hl, :].astype(BF16).astype(F32)
                second = wdn_ref[p * LANES + hl:(p + 1) * LANES, :].astype(BF16).astype(F32)
                word = (pltpu.bitcast(second, U32) & jnp.uint32(0xFFFF0000)) | \
                    lax.shift_right_logical(pltpu.bitcast(first, U32), jnp.uint32(16))
                wdn_s[p * LANES:(p + 1) * LANES, :] = pltpu.bitcast(word, BF16)

        xu = _load_row_packed(xs_ref, MOE_BLOCK)
        lo = pltpu.bitcast(lax.shift_left(xu, jnp.uint32(16)), F32).astype(BF16)
        hi = pltpu.bitcast(xu & jnp.uint32(0xFFFF0000), F32).astype(BF16)
        gu = _dot(lo, wgu_s[:half, :]) + _dot(hi, wgu_s[half:, :]) + bgu_ref[...]
        even = (lax.broadcasted_iota(jnp.int32, (gu.shape[0], LANES), 1) & 1) == 0
        acts = []
        for p in range(gu.shape[1] // (2 * LANES)):
            a = gu[:, 2 * p * LANES:(2 * p + 1) * LANES]
            b = gu[:, (2 * p + 1) * LANES:(2 * p + 2) * LANES]
            gate = jnp.minimum(jnp.where(even, a, pltpu.roll(b, 1, 1)), SWIGLU_LIMIT)
            up = jnp.clip(jnp.where(even, pltpu.roll(a, LANES - 1, 1), b), -SWIGLU_LIMIT, SWIGLU_LIMIT)
            acts.append(((up + 1.0) * gate * _sigmoid(SWIGLU_ALPHA * gate)).astype(BF16))
        y = _dot(jnp.concatenate(acts, axis=1), wdn_s[...]) + bdn_ref[...]
        y = y * jnp.broadcast_to(rw_ref[...], (8, rw_ref.shape[1])).T[:, 0:1]
        cw = PACK * LANES
        for h in range(y.shape[1] // cw):
            _store_row_packed(ys_ref, y[:, h * cw:(h + 1) * cw], lead=(h,))


def _moe_mm(block_e, n_used, xs, row_w, layer, w_gu, b_gu, w_dn, b_dn):
    n_blocks = xs.shape[0] // (MOE_BLOCK * PACK)
    depth, n_e, d, f2 = w_gu.shape
    n_half = d // (PACK * LANES)
    ew = lambda shape: pl.BlockSpec((None, None) + shape, lambda i, be, nu: (layer, be[i], 0, 0))
    grid_spec = pltpu.PrefetchScalarGridSpec(
        num_scalar_prefetch=2,
        grid=(n_blocks,),
        in_specs=[pl.BlockSpec((MOE_BLOCK * PACK, LANES), lambda i, be, nu: (i, 0)),
                  pl.BlockSpec((None, 1, MOE_BLOCK), lambda i, be, nu: (i, 0, 0)),
                  ew((d, f2)), ew((1, f2)), ew((f2 // 2, d)), ew((1, d))],
        out_specs=pl.BlockSpec((n_half, MOE_BLOCK * PACK, LANES), lambda i, be, nu: (0, i, 0)),
        scratch_shapes=[pltpu.VMEM((d, f2), BF16), pltpu.VMEM((f2 // 2, d), BF16)],
    )
    return pl.pallas_call(
        _moe_mm_kernel,
        out_shape=jax.ShapeDtypeStruct((n_half, xs.shape[0], LANES), F32),
        grid_spec=grid_spec,
        compiler_params=_cparams(("arbitrary",)),
        name="moe_mm",
    )(block_e, n_used, xs, row_w.reshape(n_blocks, 1, MOE_BLOCK), w_gu, b_gu.reshape(depth, n_e, 1, f2),
      w_dn, b_dn.reshape(depth, n_e, 1, d))


def _moe_combine_kernel(nv_ref, dst_ref, ys_ref, acc_ref):
    i = pl.program_id(1)

    @pl.when(i == 0)
    def _():
        acc_ref[...] = jnp.zeros_like(acc_ref)

    def body(r8, carry):
        base = r8 * ROWS_PER_STEP
        dst = [_packed_row(dst_ref[0, base + j]) for j in range(ROWS_PER_STEP)]
        group = ROWS_PER_STEP * PACK
        ys = ys_ref[pl.ds(pl.multiple_of(r8 * group, group), group), :]
        new = [acc_ref[dst[j], :] + ys[j * PACK:(j + 1) * PACK, :] for j in range(ROWS_PER_STEP)]
        for j in range(ROWS_PER_STEP):
            acc_ref[dst[j], :] = new[j]
        return carry

    blocks_per_step = dst_ref.shape[1] // MOE_BLOCK
    groups_per_block = MOE_BLOCK // ROWS_PER_STEP
    for kb in range(blocks_per_step):
        groups = (nv_ref[i * blocks_per_step + kb] + ROWS_PER_STEP - 1) // ROWS_PER_STEP
        lax.fori_loop(kb * groups_per_block, kb * groups_per_block + groups, body, 0)


def _moe_combine(n_valid, row_dst, ys, n_rows_out):
    n_half, packed_rows, _ = ys.shape
    step = _move_rows(packed_rows // (MOE_BLOCK * PACK))
    n_blocks = packed_rows // (step * PACK)
    idx_spec = pl.BlockSpec((None, 1, step), lambda j, i, nu: (i, 0, 0), memory_space=pltpu.SMEM)
    grid_spec = pltpu.PrefetchScalarGridSpec(
        num_scalar_prefetch=1,
        grid=(n_half, n_blocks),
        in_specs=[idx_spec, pl.BlockSpec((None, step * PACK, LANES), lambda j, i, nu: (j, i, 0))],
        out_specs=pl.BlockSpec((None, n_rows_out * PACK, LANES), lambda j, i, nu: (j, 0, 0),
                               pipeline_mode=pl.Buffered(1)),
    )
    return pl.pallas_call(
        _moe_combine_kernel,
        out_shape=jax.ShapeDtypeStruct((n_half, n_rows_out * PACK, LANES), F32),
        grid_spec=grid_spec,
        compiler_params=_cparams(("arbitrary", "arbitrary")),
        name="moe_combine",
    )(n_valid, row_dst.reshape(n_blocks, 1, step), ys)


def _moe_inverse_kernel(pad_lo_ref, pad_hi_ref, dest_ref, inv_ref):
    def init(r, c):
        inv_ref[r] = -1
        return c

    def pad_range(k, carry):
        lax.fori_loop(pad_lo_ref[k], pad_hi_ref[k], init, 0)
        return carry

    n_ranges = pad_lo_ref.shape[0]
    lax.fori_loop(0, n_ranges - 1, pad_range, 0)

    def pad_block(blk, carry):
        lax.fori_loop(0, MOE_BLOCK, lambda r, c: init(blk * MOE_BLOCK + r, c), 0, unroll=8)
        return carry

    tail_end = pad_hi_ref[n_ranges - 1]
    lax.fori_loop(pad_lo_ref[n_ranges - 1] // MOE_BLOCK, tail_end // MOE_BLOCK, pad_block, 0)
    lax.fori_loop(tail_end // MOE_BLOCK * MOE_BLOCK, tail_end, init, 0)

    def body(a, carry):
        inv_ref[dest_ref[a]] = a
        return carry

    lax.fori_loop(0, dest_ref.shape[0], body, 0, unroll=8)


def _moe_inverse(pad_lo, pad_hi, dest, rows):
    smem = pl.BlockSpec(memory_space=pltpu.SMEM)
    return pl.pallas_call(
        _moe_inverse_kernel,
        out_shape=jax.ShapeDtypeStruct((rows,), jnp.int32),
        in_specs=[smem, smem, smem],
        out_specs=smem,
        name="moe_inverse",
    )(pad_lo, pad_hi, dest)


def _moe(h2p, codes, weights, counts, layer, w_gu, b_gu, w_dn, b_dn):
    n = h2p.shape[0] // PACK
    n_assign = n * TOP_K
    n_blocks = -(-n_assign // MOE_BLOCK) + N_EXPERTS
    rows = n_blocks * MOE_BLOCK
    padded = -(-counts // MOE_BLOCK) * MOE_BLOCK
    ends = jnp.cumsum(padded)
    offsets = ends - padded
    starts = jnp.arange(n_blocks, dtype=jnp.int32) * MOE_BLOCK
    block_e = jnp.minimum(jnp.sum(ends[None, :] <= starts[:, None], axis=1), N_EXPERTS - 1).astype(jnp.int32)
    n_used = (ends[-1:] // MOE_BLOCK).astype(jnp.int32)
    of_block = block_e[:, None] == jnp.arange(N_EXPERTS, dtype=jnp.int32)[None, :]
    data_end = jnp.sum(jnp.where(of_block, (offsets + counts)[None, :], 0), axis=1)
    n_valid = jnp.clip(data_end - starts, 0, MOE_BLOCK).astype(jnp.int32)
    expert = lax.shift_right_logical(codes, CODE_SHIFT)
    which = expert[:, None] == jnp.arange(N_EXPERTS, dtype=jnp.int32)[None, :]
    dest = (codes & ((1 << CODE_SHIFT) - 1)) + jnp.sum(jnp.where(which, offsets[None, :], 0), axis=1)
    spare = 8
    dest = jnp.where(codes >= 0, dest, rows)
    pad_lo = jnp.concatenate([offsets + counts, ends[-1:]]).astype(jnp.int32)
    pad_hi = jnp.concatenate([ends, jnp.full((1,), rows + spare, ends.dtype)]).astype(jnp.int32)
    inv = _moe_inverse(pad_lo, pad_hi, dest.astype(jnp.int32), rows + spare)[:rows]
    tok = lax.shift_right_logical(inv, TOP_K.bit_length() - 1)
    row_src = jnp.where(inv >= 0, tok, 0)
    row_dst = jnp.where(inv >= 0, tok, n)
    row_w = jnp.where(inv >= 0, jnp.take(weights, jnp.maximum(inv, 0)), 0.0)
    xs = _moe_gather(row_src, h2p, n_blocks)
    ys = _moe_mm(block_e, n_used, xs, row_w, layer, w_gu, b_gu, w_dn, b_dn)
    return _moe_combine(n_valid, row_dst, ys, n + 8)


def _final_kernel(x_ref, acc_ref, g2_ref, g_ref, o_ref):
    x = x_ref[...] + g2_ref[...] * _load_acc(acc_ref)
    ms = jnp.mean(x * x, axis=-1, keepdims=True)
    o_ref[...] = x * lax.rsqrt(ms + EPS) * g_ref[...]


def _final(x, acc, g2, g, n_ctx_tiles):
    b, t, d = x.shape
    nt = t // TM - n_ctx_tiles
    tile_in = pl.BlockSpec((None, TM, d), lambda b_, t_: (b_, t_ + n_ctx_tiles, 0))
    acc_tile = _acc_tile_spec(acc, lambda b_, t_: b_ * (t // TM) + t_ + n_ctx_tiles)
    return pl.pallas_call(
        _final_kernel,
        out_shape=jax.ShapeDtypeStruct((b, nt * TM, d), F32),
        grid=(b, nt),
        in_specs=[tile_in, acc_tile,
                  pl.BlockSpec((None, None, 1, d), lambda b_, t_: (b_, 1, 0, 0)),
                  pl.BlockSpec((1, d), lambda b_, t_: (0, 0))],
        out_specs=pl.BlockSpec((None, TM, d), lambda b_, t_: (b_, t_, 0)),
        compiler_params=_cparams(("arbitrary", "arbitrary")),
        name="final_norm",
    )(x, acc, g2, g)


def _mod_tables(mod, b, d):
    outs = []
    for j in range(6):
        m = mod[:, j * d:(j + 1) * d]
        lat = m[:b]
        ctx = jnp.broadcast_to(m[b:b + 1], (b, d))
        outs.append(jnp.stack([ctx, lat], axis=1)[:, :, None, :])
    return outs


def _routing(code):
    codes = code[:, :, :TOP_K].reshape(-1)
    weights = lax.bitcast_convert_type(code[:, :, TOP_K:2 * TOP_K], F32).reshape(-1)
    return codes, weights


def _router_params(router_w, router_b):
    d, e = router_w.shape
    rw = jnp.zeros((d, LANES), F32).at[:, :e].set(router_w).astype(BF16)
    rb = jnp.full((1, LANES), NEG, F32).at[0, :e].set(router_b)
    return rw, rb


def kernel(x, c, ctx, c_ctx, mod_w, mod_b, norm1_g, norm2_g, ev_w_in, ev_conv_w, gdn_a_log, gdn_dt_bias,
           gdn_norm_g, ml_i_bias, ml_f_bias, ml_norm_g, ev_w_out, od_w_in, ret_norm_g, od_w_out,
           router_w, router_b, moe_w_gu, moe_b_gu, moe_w_dn, moe_b_dn, final_g):
    b, s, d = x.shape
    n_ctx = ctx.shape[1]
    depth = mod_w.shape[0]
    assert n_ctx == TM and s % TM == 0 and depth == 2 and b % SCAN_BG == 0
    assert ev_conv_w.shape[1] == CONV_W
    t = n_ctx + s
    n_tok = b * t

    cond = jnp.concatenate([c, c_ctx[None, :], jnp.zeros((8 - b - 1, d), F32)], axis=0)
    mod = _adaln(cond, mod_w, mod_b)

    sh1, sc1, g1, sh2, sc2, g2 = _mod_tables(mod[0], b, d)
    qk_w = GDN_H * HD
    conv_ch = 3 * qk_w
    ng = N_DIR * GDN_H
    w_in = ev_w_in[0]
    o_z = conv_ch
    o_a = o_z + qk_w
    o_mq = o_a + 2 * ng
    o_i = o_mq + 4 * qk_w
    w_main = jnp.concatenate([w_in[:, :o_a], w_in[:, o_mq:o_i]], axis=1).astype(BF16)
    w_gate = jnp.concatenate([w_in[:, o_a:o_mq], w_in[:, o_i:o_i + 2 * ng],
                              jnp.zeros((d, LANES - 4 * ng), F32)], axis=1).astype(BF16)
    zeros_g = jnp.zeros((ng,), F32)
    rate = jnp.concatenate([jnp.exp(gdn_a_log[0].astype(F32)).reshape(-1), jnp.zeros((LANES - ng,), F32)])[None, :]
    gbias = jnp.concatenate([gdn_dt_bias[0].reshape(-1), zeros_g, ml_i_bias[0].reshape(-1),
                             ml_f_bias[0].reshape(-1), jnp.zeros((LANES - 4 * ng,), F32)])[None, :].astype(F32)
    main, gates = _proj_even(ctx, x, norm1_g[0][None, :], sh1, sc1, w_main, w_gate, ev_conv_w[0], rate, gbias)
    gates_c = gates.reshape(b, t // CH, CH, 4 * ng)
    gates_r = jnp.swapaxes(gates_c, 2, 3)
    og_f, og_b, om_f, om_b = _even_scan(main, gates_c, gates_r)
    rw, rb = _router_params(router_w[0], router_b[0])
    gdn_g = jnp.tile(gdn_norm_g[0], GDN_H)[None, :]
    x1, h2p, code, cnt = _merge_even(og_f, og_b, om_f, om_b, main, gdn_g, ml_norm_g[0][None, :],
                                     ev_w_out[0].astype(BF16), ctx, x, g1, norm2_g[0][None, :], sh2, sc2, rw, rb)
    acc = _moe(h2p.reshape(n_tok * PACK, LANES), *_routing(code), cnt[0, :N_EXPERTS].astype(jnp.int32),
               0, moe_w_gu, moe_b_gu, moe_w_dn, moe_b_dn)
    g2_prev = g2

    sh1, sc1, g1, sh2, sc2, g2 = _mod_tables(mod[1], b, d)
    x2, main_o = _proj_odd(x1, acc, g2_prev, norm1_g[1][None, :], sh1, sc1, od_w_in[0].astype(BF16))
    o_f, o_b = _ret_scan(main_o)
    rw, rb = _router_params(router_w[1], router_b[1])
    x3, h2p, code, cnt = _merge_odd(o_f, o_b, main_o, ret_norm_g[0][None, :], od_w_out[0].astype(BF16),
                                    x2, g1, norm2_g[1][None, :], sh2, sc2, rw, rb)
    acc = _moe(h2p.reshape(n_tok * PACK, LANES), *_routing(code), cnt[0, :N_EXPERTS].astype(jnp.int32),
               1, moe_w_gu, moe_b_gu, moe_w_dn, moe_b_dn)
    return _final(x3, acc, g2, final_g[None, :], n_ctx // TM)
```

```python
import jax
import jax.numpy as jnp
import numpy as np
from jax import lax
from jax.experimental import pallas as pl
from jax.experimental.pallas import tpu as pltpu

F32 = jnp.float32
BF16 = jnp.bfloat16
U32 = jnp.uint32
HIGHEST = lax.Precision.HIGHEST

EPS = 1e-6
CH = 64
TM = 256
CPB = TM // CH
HD = 128
N_DIR = 2
GDN_H = 4
ML_H = 4
RET_H = 8
RET_DV = 256
CONV_W = 3
N_EXPERTS = 32
TOP_K = 4
SWIGLU_ALPHA = 1.702
SWIGLU_LIMIT = 7.0
MOE_BLOCK = 512
ROPE_BASE = 10000.0
NEG = -1e30
VMEM_LIMIT = 56 * 1024 * 1024


def _cparams(sem):
    return pltpu.CompilerParams(dimension_semantics=sem, vmem_limit_bytes=VMEM_LIMIT)


def _dot(a, b, precision=None):
    return jnp.dot(a, b, preferred_element_type=F32, precision=precision)


def _dot_nt(a, b):
    return lax.dot_general(a, b, (((1,), (1,)), ((), ())), preferred_element_type=F32)


def _dot_tn(a, b):
    return lax.dot_general(a, b, (((0,), (0,)), ((), ())), preferred_element_type=F32)


def _sigmoid(x):
    return 1.0 / (1.0 + jnp.exp(-x))


def _silu(x):
    return x * _sigmoid(x)


def _group_sum(x, w):
    outs = []
    for j in range(x.shape[1] // w):
        s = jnp.sum(x[:, j * w:(j + 1) * w], axis=-1, keepdims=True)
        outs.append(jnp.broadcast_to(s, (x.shape[0], w)))
    return outs[0] if len(outs) == 1 else jnp.concatenate(outs, axis=1)


def _rms_mod(x, g, sh, sc):
    ms = jnp.mean(x * x, axis=-1, keepdims=True)
    return (x * lax.rsqrt(ms + EPS) * g) * (1.0 + sc) + sh


LANES = 128
PACK = 4


def _store_row_packed(ref, x, lead=()):
    rows = x.shape[0]
    for g in range(x.shape[1] // LANES):
        ref[lead + (pl.ds(g, rows, stride=PACK), slice(None))] = x[:, g * LANES:(g + 1) * LANES]


def _load_row_packed(ref, rows, lead=()):
    return jnp.concatenate([ref[lead + (pl.ds(g, rows, stride=PACK), slice(None))] for g in range(PACK)], axis=1)


def _load_acc(acc_ref):
    rows = acc_ref.shape[1] // PACK
    return jnp.concatenate([_load_row_packed(acc_ref, rows, lead=(h,)) for h in range(acc_ref.shape[0])], axis=1)


def _acc_tile_spec(acc, tile_of):
    return pl.BlockSpec((acc.shape[0], TM * PACK, LANES), lambda b_, t_: (0, tile_of(b_, t_), 0))


def _adaln_kernel(c_ref, w_ref, b_ref, o_ref):
    c = c_ref[...]
    o_ref[...] = _dot(_silu(c), w_ref[...], precision=HIGHEST) + b_ref[...]


def _adaln(cond, mod_w, mod_b):
    depth, d, d6 = mod_w.shape
    n = d6 // d
    return pl.pallas_call(
        _adaln_kernel,
        out_shape=jax.ShapeDtypeStruct((depth, cond.shape[0], d6), F32),
        grid=(depth, n),
        in_specs=[pl.BlockSpec(cond.shape, lambda l, j: (0, 0)),
                  pl.BlockSpec((None, d, d), lambda l, j: (l, 0, j)),
                  pl.BlockSpec((None, 1, d), lambda l, j: (l, 0, j))],
        out_specs=pl.BlockSpec((None, cond.shape[0], d), lambda l, j: (l, 0, j)),
        compiler_params=_cparams(("arbitrary", "arbitrary")),
        name="adaln",
    )(cond, mod_w, mod_b.reshape(depth, 1, d6))


def _stream_tile(ctx_ref, x_ref):
    return jnp.where(pl.program_id(1) == 0, ctx_ref[...], x_ref[...])


def _stream_specs(d):
    return (pl.BlockSpec((None, TM, d), lambda b, t: (b, 0, 0)),
            pl.BlockSpec((None, TM, d), lambda b, t: (b, jnp.maximum(t - 1, 0), 0)))


def _proj_even_kernel(ctx_ref, x_ref, g_ref, sh_ref, sc_ref, w_ref, wg_ref, cw_ref, rate_ref, gb_ref,
                      main_ref, gates_ref):
    t = pl.program_id(1)
    h = _rms_mod(_stream_tile(ctx_ref, x_ref), g_ref[...], sh_ref[...], sc_ref[...])
    hb = h.astype(BF16)
    qk_w = GDN_H * HD

    row = lax.broadcasted_iota(jnp.int32, (TM, 1), 0)
    pos = jnp.where(t > 0, row & (CH - 1), row)
    last = jnp.where(t > 0, CH - 1, TM - 1)
    left_ok = pos != 0
    right_ok = pos != last
    for seg in range(3):
        sl = slice(seg * qk_w, (seg + 1) * qk_w)
        u = _dot(hb, w_ref[:, sl])
        um = jnp.where(left_ok, pltpu.roll(u, 1, 0), 0.0)
        up = jnp.where(right_ok, pltpu.roll(u, TM - 1, 0), 0.0)
        cv = _silu(um * cw_ref[0:1, sl] + u * cw_ref[1:2, sl] + up * cw_ref[2:3, sl])
        if seg < 2:
            ss = _group_sum(cv * cv, HD)
            cv = cv * lax.rsqrt(ss + EPS)
            if seg == 0:
                cv = cv * (HD ** -0.5)
        main_ref[:, sl] = cv.astype(BF16)
    for seg in range(3, 8):
        sl = slice(seg * qk_w, (seg + 1) * qk_w)
        u = _dot(hb, w_ref[:, sl])
        if seg == 5:
            u = u * (HD ** -0.5)
        main_ref[:, sl] = u.astype(BF16)

    z = _dot(hb, wg_ref[...]) + gb_ref[...]
    tl = jnp.log(1.0 + jnp.exp(-jnp.abs(z)))
    sp_pos = jnp.maximum(z, 0.0) + tl
    sp_neg = jnp.maximum(-z, 0.0) + tl
    lane = lax.broadcasted_iota(jnp.int32, z.shape, 1)
    ng = N_DIR * GDN_H
    res = jnp.where(lane < ng, -rate_ref[...] * sp_pos,
                    jnp.where(lane < 2 * ng, _sigmoid(z),
                              jnp.where(lane < 3 * ng, z, -sp_neg)))
    gates_ref[...] = res[:, :gates_ref.shape[-1]]


def _tile_mod_spec(d):
    return pl.BlockSpec((None, None, 1, d), lambda b, t: (b, jnp.minimum(t, 1), 0, 0))


def _proj_even(ctx, x, g, sh, sc, w_main, w_gate, conv_w, rate, gbias):
    b, s, d = x.shape
    t = ctx.shape[1] + s
    n = w_main.shape[1]
    ngl = 4 * N_DIR * GDN_H
    const = lambda shape: pl.BlockSpec(shape, lambda b_, t_: (0,) * len(shape))
    return pl.pallas_call(
        _proj_even_kernel,
        out_shape=(jax.ShapeDtypeStruct((b, t, n), BF16), jax.ShapeDtypeStruct((b, t, ngl), F32)),
        grid=(b, t // TM),
        in_specs=[*_stream_specs(d),
                  const((1, d)), _tile_mod_spec(d), _tile_mod_spec(d),
                  const(w_main.shape), const(w_gate.shape), const(conv_w.shape),
                  const(rate.shape), const(gbias.shape)],
        out_specs=(pl.BlockSpec((None, TM, n), lambda b_, t_: (b_, t_, 0)),
                   pl.BlockSpec((None, TM, ngl), lambda b_, t_: (b_, t_, 0))),
        compiler_params=_cparams(("arbitrary", "arbitrary")),
        name="proj_even",
    )(ctx, x, g, sh, sc, w_main, w_gate, conv_w, rate, gbias)


def _rev_tile(i, nt):
    return jnp.where(i == 0, 0, nt - i)


def _tri_masks():
    r = lax.broadcasted_iota(jnp.int32, (CH, CH), 0)
    c = lax.broadcasted_iota(jnp.int32, (CH, CH), 1)
    return r >= c, r > c, r <= c, r < c


SCAN_BG = 2
SOLVE_BASE = 8


def _scan_streams(heads):
    groups = [(bi, d) for bi in range(SCAN_BG) for d in range(N_DIR)]
    streams = [(gi, hh) for gi in range(len(groups)) for hh in range(heads)]
    return groups, streams


def _cumsum_both(groups, gcs, grs, lower, upper):
    tri = (lower.astype(F32), upper.astype(F32))
    cs_c = [_dot(tri[d], gc, precision=HIGHEST) for (_, d), gc in zip(groups, gcs)]
    cs_r = [_dot(gr, tri[1 - d], precision=HIGHEST) for (_, d), gr in zip(groups, grs)]
    return cs_c, cs_r


def _store_heads(refs, groups, rows, outs, heads):
    for gi, (bi, d) in enumerate(groups):
        o_ref = refs[d][-1]
        tile = jnp.concatenate(outs[gi * heads:(gi + 1) * heads], axis=1)
        o_ref[bi, rows[d], :] = tile.astype(o_ref.dtype)


class _GdnChunk:
    def __init__(self, refs, s_ref):
        self.refs, self.s_ref = refs, s_ref
        self.groups, self.streams = _scan_streams(GDN_H)

    def prelude(self, masks, rows, gcs, cs_c, cs_r):
        lower, lstrict, upper, ustrict = masks
        refs, groups = self.refs, self.groups
        ng = N_DIR * GDN_H
        st = []
        for gi, hh in self.streams:
            bi, d = groups[gi]
            ci = d * GDN_H + hh
            cols = slice(hh * HD, (hh + 1) * HD)
            incl, strict = (lower, lstrict) if d == 0 else (upper, ustrict)
            g_col = cs_c[gi][:, ci:ci + 1]
            g_row = cs_r[gi][ci:ci + 1, :]
            beta = gcs[gi][:, ng + ci:ng + ci + 1]
            tot = g_col[CH - 1:CH, :] if d == 0 else g_col[0:1, :]
            decay = jnp.where(incl, jnp.exp(jnp.where(incl, g_col - g_row, 0.0)), 0.0)
            q = refs[d][0][bi, rows[d], cols].astype(F32)
            k = refs[d][1][bi, rows[d], cols].astype(F32)
            v = refs[d][2][bi, rows[d], cols].astype(F32)
            kbeta = k * beta
            eg = jnp.exp(g_col)
            st.append(dict(
                strict=strict, decay=decay, kb=k.astype(BF16), kbetab=kbeta.astype(BF16), qb=q.astype(BF16),
                x=jnp.concatenate([v * beta, kbeta * eg], axis=1),
                qe=(q * eg).astype(BF16), kdec=(k * jnp.exp(tot - g_col)).astype(BF16), cd=jnp.exp(tot)))
        self.st = st

    def matmuls(self, rows):
        st, s_ref, streams = self.st, self.s_ref, self.streams
        kk = [_dot_nt(s["kbetab"], s["kb"]) for s in st]
        qk = [(_dot_nt(s["qb"], s["kb"]) * s["decay"]).astype(BF16) for s in st]
        r = lax.broadcasted_iota(jnp.int32, (CH, CH), 0)
        c = lax.broadcasted_iota(jnp.int32, (CH, CH), 1)
        eye = jnp.where(r == c, 1.0, 0.0)
        blocks_differ = lambda w: lax.shift_right_logical(r, w.bit_length() - 1) ^ \
            lax.shift_right_logical(c, w.bit_length() - 1)
        a = [jnp.where(s["strict"], m * s["decay"], 0.0) for s, m in zip(st, kk)]
        p = [-jnp.where(blocks_differ(SOLVE_BASE) == 0, m, 0.0) for m in a]
        t = [eye + m for m in p]
        w = 2
        while w < SOLVE_BASE:
            pb = [m.astype(BF16) for m in p]
            p = [_dot(m, m) for m in pb]
            t = [tt + _dot(tt.astype(BF16), m.astype(BF16)) for tt, m in zip(t, p)]
            w *= 2
        w = SOLVE_BASE
        while w < CH:
            off = blocks_differ(w) == 1
            tb = [tt.astype(BF16) for tt in t]
            ta = [_dot(tt, jnp.where(off, m, 0.0).astype(BF16)) for tt, m in zip(tb, a)]
            t = [tt - _dot(m.astype(BF16), tt2) for tt, m, tt2 in zip(t, ta, tb)]
            w *= 2
        x = [_dot(tt.astype(BF16), s["x"].astype(BF16)) for tt, s in zip(t, st)]
        s_old = [s_ref[si] for si in range(len(streams))]
        sb = [s.astype(BF16) for s in s_old]
        vnb = [(xx[:, :HD] - _dot(xx[:, HD:].astype(BF16), s)).astype(BF16) for xx, s in zip(x, sb)]
        o = [_dot(s["qe"], sbi) + _dot(m, vn) for s, sbi, m, vn in zip(st, sb, qk, vnb)]
        s_new = [so * s["cd"] + _dot_tn(s["kdec"], vn) for s, so, vn in zip(st, s_old, vnb)]
        for si, s in enumerate(s_new):
            s_ref[si] = s
        _store_heads(self.refs, self.groups, rows, o, GDN_H)


def _scan_specs(nt, width, col):
    fwd = pl.BlockSpec((SCAN_BG, TM, width), lambda b, i: (b, i, col))
    bwd = pl.BlockSpec((SCAN_BG, TM, width), lambda b, i: (b, _rev_tile(i, nt), col))
    return fwd, bwd


def _gate_specs(nt, shape):
    fwd = pl.BlockSpec((SCAN_BG, CPB) + shape, lambda b, i: (b, i, 0, 0))
    bwd = pl.BlockSpec((SCAN_BG, CPB) + shape, lambda b, i: (b, _rev_tile(i, nt), 0, 0))
    return fwd, bwd


class _MlstmChunk:
    def __init__(self, refs, c_ref, m_ref):
        self.refs, self.c_ref, self.m_ref = refs, c_ref, m_ref
        self.groups, self.streams = _scan_streams(ML_H)

    def prelude(self, masks, rows, gcs, grs, cs_c, cs_r):
        lower, _, upper, _ = masks
        refs, groups, streams, m_ref = self.refs, self.groups, self.streams, self.m_ref
        ng = N_DIR * GDN_H
        i_off = 2 * ng
        f_off = 2 * ng + N_DIR * ML_H
        lane = lax.broadcasted_iota(jnp.int32, (CH, HD), 1)
        ones_col = jnp.where(lane == 0, 1.0, 0.0).astype(BF16)
        ns = len(streams)
        dirs = [groups[gi][1] for gi, _ in streams]
        chan = [groups[gi][1] * ML_H + hh for gi, hh in streams]
        b_col = [cs_c[gi][:, f_off + c:f_off + c + 1] for (gi, _), c in zip(streams, chan)]
        b_row = [cs_r[gi][f_off + c:f_off + c + 1, :] for (gi, _), c in zip(streams, chan)]
        i_col = [gcs[gi][:, i_off + c:i_off + c + 1] for (gi, _), c in zip(streams, chan)]
        i_row = [grs[gi][i_off + c:i_off + c + 1, :] for (gi, _), c in zip(streams, chan)]
        b_tot = [bc[CH - 1:CH, :] if d == 0 else bc[0:1, :] for bc, d in zip(b_col, dirs)]
        m_rows = [m_ref[si] for si in range(ns)]
        m_old = [mr[:, 0:1] for mr in m_rows]
        d_in = [jnp.where(lower if d == 0 else upper, bc - br + ir, NEG)
                for d, bc, br, ir in zip(dirs, b_col, b_row, i_row)]
        d_end = [bt - br + ir for bt, br, ir in zip(b_tot, b_row, i_row)]
        mx_in = [jnp.max(a, axis=-1, keepdims=True) for a in d_in]
        mx_end = [jnp.max(a, axis=-1, keepdims=True) for a in d_end]
        d_carry = [bc + m for bc, m in zip(b_col, m_old)]
        m_t = [jnp.maximum(a, b_) for a, b_ in zip(d_carry, mx_in)]
        carry_end = [bt + m for bt, m in zip(b_tot, m_old)]
        m_new = [jnp.maximum(a, b_) for a, b_ in zip(carry_end, mx_end)]
        p_in = [jnp.exp(a - b_) for a, b_ in zip(d_in, m_t)]
        w_end = [jnp.exp(bt - bc + ic - mn) for bt, bc, ic, mn in zip(b_tot, b_col, i_col, m_new)]
        st = []
        for si, (gi, hh) in enumerate(streams):
            bi, d = groups[gi]
            cols = slice(hh * HD, (hh + 1) * HD)
            k = refs[d][1][bi, rows[d], cols]
            v = refs[d][2][bi, rows[d], cols]
            st.append(dict(
                q=refs[d][0][bi, rows[d], cols], k=k,
                v_aug=jnp.concatenate([v, ones_col], axis=1),
                p_in=p_in[si], w_carry=jnp.exp(d_carry[si] - m_t[si]), floor=jnp.exp(-m_t[si]),
                kw=(k.astype(F32) * w_end[si]).astype(BF16),
                f_end=jnp.exp(carry_end[si] - m_new[si]),
                m_new=jnp.broadcast_to(m_new[si], m_rows[si].shape)))
        self.st = st

    def matmuls(self, rows):
        st, c_ref, m_ref, streams = self.st, self.c_ref, self.m_ref, self.streams
        sc = [(_dot_nt(s["q"], s["k"]) * s["p_in"]).astype(BF16) for s in st]
        c_old = [c_ref[si] for si in range(len(streams))]
        qc = [_dot(s["q"], c.astype(BF16)) for s, c in zip(st, c_old)]
        nd = [s["w_carry"] * a + _dot(m, s["v_aug"]) for s, a, m in zip(st, qc, sc)]
        hout = [a[:, :HD] / jnp.maximum(jnp.abs(a[:, HD:HD + 1]), s["floor"]) for s, a in zip(st, nd)]
        c_new = [s["f_end"] * c + _dot_tn(s["kw"], s["v_aug"]) for s, c in zip(st, c_old)]
        for si, (s, c) in enumerate(zip(st, c_new)):
            c_ref[si] = c
            m_ref[si] = s["m_new"]
        _store_heads(self.refs, self.groups, rows, hout, ML_H)


def _even_scan_kernel(gqf, gkf, gvf, gqb, gkb, gvb, mqf, mkf, mvf, mqb, mkb, mvb, gcf, grf, gcb, grb,
                      ogf, ogb, omf, omb, s_ref, c_ref, m_ref):
    i = pl.program_id(1)

    @pl.when(i == 0)
    def _():
        s_ref[...] = jnp.zeros_like(s_ref)
        c_ref[...] = jnp.zeros_like(c_ref)
        m_ref[...] = jnp.zeros_like(m_ref)

    masks = _tri_masks()
    gdn = _GdnChunk(((gqf, gkf, gvf, ogf), (gqb, gkb, gvb, ogb)), s_ref)
    mls = _MlstmChunk(((mqf, mkf, mvf, omf), (mqb, mkb, mvb, omb)), c_ref, m_ref)
    gate_refs = ((gcf, grf), (gcb, grb))

    def chunk_body(cc, carry):
        cidx = (cc, CPB - 1 - cc)
        rows = tuple(pl.ds(pl.multiple_of(c * CH, CH), CH) for c in cidx)
        gcs = [gate_refs[d][0][bi, cidx[d]] for bi, d in gdn.groups]
        grs = [gate_refs[d][1][bi, cidx[d]] for bi, d in gdn.groups]
        cs_c, cs_r = _cumsum_both(gdn.groups, gcs, grs, masks[0], masks[2])
        gdn.prelude(masks, rows, gcs, cs_c, cs_r)
        mls.prelude(masks, rows, gcs, grs, cs_c, cs_r)
        gdn.matmuls(rows)
        mls.matmuls(rows)
        return carry

    lax.fori_loop(0, CPB, chunk_body, 0)


def _even_scan(main, gates_c, gates_r):
    b, t, _ = main.shape
    nt = t // TM
    w = GDN_H * HD
    ngl = gates_c.shape[-1]
    qkv = lambda cols: [_scan_specs(nt, w, c)[d] for d in range(N_DIR) for c in cols]
    gcf, gcb = _gate_specs(nt, (CH, ngl))
    grf, grb = _gate_specs(nt, (ngl, CH))
    of, ob = _scan_specs(nt, w, 0)
    out = jax.ShapeDtypeStruct((b, t, w), BF16)
    n_streams = SCAN_BG * N_DIR * GDN_H
    return pl.pallas_call(
        _even_scan_kernel,
        out_shape=(out, out, out, out),
        grid=(b // SCAN_BG, nt),
        in_specs=qkv((0, 1, 2)) + qkv((4, 5, 6)) + [gcf, grf, gcb, grb],
        out_specs=(of, ob, of, ob),
        scratch_shapes=[pltpu.VMEM((n_streams, HD, HD), F32), pltpu.VMEM((n_streams, HD, 2 * HD), F32),
                        pltpu.VMEM((n_streams, 1, HD), F32)],
        compiler_params=_cparams(("arbitrary", "arbitrary")),
        name="even_scan",
    )(*([main] * 12), gates_c, gates_r, gates_c, gates_r)


def _ret_kernel(qf, kf, vf, qb, kb, vb, intra_ref, cross_ref, tail_ref, cd_ref, of, ob, s_ref):
    i = pl.program_id(1)

    @pl.when(i == 0)
    def _():
        s_ref[...] = jnp.zeros_like(s_ref)

    refs = ((qf, kf, vf, of), (qb, kb, vb, ob))
    groups, streams = _scan_streams(RET_H)

    def chunk_body(cc, carry):
        cidx = (cc, CPB - 1 - cc)
        rows = tuple(pl.ds(pl.multiple_of(c * CH, CH), CH) for c in cidx)
        st = []
        for gi, hh in streams:
            bi, d = groups[gi]
            ti = d * RET_H + hh
            kcols = slice(hh * HD, (hh + 1) * HD)
            q = refs[d][0][bi, rows[d], kcols]
            k = refs[d][1][bi, rows[d], kcols]
            st.append(dict(
                ti=ti, q=q, k=k, v=refs[d][2][bi, rows[d], slice(hh * RET_DV, (hh + 1) * RET_DV)],
                qc=(q.astype(F32) * cross_ref[ti]).astype(BF16),
                kt=(k.astype(F32) * tail_ref[ti]).astype(BF16)))
        sc = [(_dot_nt(s["q"], s["k"]) * intra_ref[s["ti"]]).astype(BF16) for s in st]
        s_old = [s_ref[si] for si in range(len(streams))]
        o = [_dot(m, s["v"]) + _dot(s["qc"], so.astype(BF16)) for s, m, so in zip(st, sc, s_old)]
        s_new = [cd_ref[s["ti"]] * so + _dot_tn(s["kt"], s["v"]) for s, so in zip(st, s_old)]
        for si, s in enumerate(s_new):
            s_ref[si] = s
        _store_heads(refs, groups, rows, o, RET_H)
        return carry

    lax.fori_loop(0, CPB, chunk_body, 0)


def _ret_tables():
    pos = np.arange(CH, dtype=np.float64)
    intra, cross, tail, cd = [], [], [], []
    for d in range(N_DIR):
        expo = 5.0 + np.arange(RET_H, dtype=np.float64)
        if d == 1:
            expo = expo[::-1]
        lg = np.log1p(-np.exp2(-expo))
        p = pos if d == 0 else (CH - 1.0 - pos)
        diff = p[:, None] - p[None, :]
        for hh in range(RET_H):
            intra.append(np.where(diff >= 0, np.exp(np.where(diff >= 0, diff, 0.0) * lg[hh]), 0.0))
            cross.append(np.broadcast_to(np.exp((p + 1.0) * lg[hh])[:, None], (CH, HD)))
            tail.append(np.broadcast_to(np.exp((CH - 1.0 - p) * lg[hh])[:, None], (CH, HD)))
            cd.append(np.full((1, RET_DV), np.exp(CH * lg[hh])))
    f = lambda a: jnp.asarray(np.stack(a), F32)
    return f(intra), f(cross), f(tail), f(cd)


def _ret_scan(main):
    b, t, _ = main.shape
    nt = t // TM
    qw = RET_H * HD
    vw = RET_H * RET_DV
    qf, qb = _scan_specs(nt, qw, 0)
    kf, kb = _scan_specs(nt, qw, 1)
    vf, vb = _scan_specs(nt, vw, 1)
    of, ob = _scan_specs(nt, vw, 0)
    tabs = _ret_tables()
    const = lambda a: pl.BlockSpec(a.shape, lambda b_, i_: (0,) * a.ndim)
    return pl.pallas_call(
        _ret_kernel,
        out_shape=(jax.ShapeDtypeStruct((b, t, vw), BF16), jax.ShapeDtypeStruct((b, t, vw), BF16)),
        grid=(b // SCAN_BG, nt),
        in_specs=[qf, kf, vf, qb, kb, vb] + [const(a) for a in tabs],
        out_specs=(of, ob),
        scratch_shapes=[pltpu.VMEM((SCAN_BG * N_DIR * RET_H, HD, RET_DV), F32)],
        compiler_params=_cparams(("arbitrary", "arbitrary")),
        name="ret_scan",
    )(main, main, main, main, main, main, *tabs)


CODE_SHIFT = 17


def _route_and_pack(h2, rw_ref, rb_ref, h2p_ref, code_ref, cnt_ref, base_ref, skip_context):
    h2b = h2.astype(BF16)
    logits = _dot(h2b, rw_ref[...]) + rb_ref[...]
    lane = lax.broadcasted_iota(jnp.int32, logits.shape, 1)
    lane_f = lane.astype(F32)
    vals, idxs = [], []
    cur = logits
    for _ in range(TOP_K):
        m = jnp.max(cur, axis=-1, keepdims=True)
        ix = jnp.min(jnp.where(cur == m, lane_f, float(logits.shape[1])), axis=-1, keepdims=True)
        vals.append(m)
        idxs.append(ix)
        cur = jnp.where(lane_f == ix, NEG, cur)
    es = [jnp.exp(v - vals[0]) for v in vals]
    tot = es[0] + es[1] + es[2] + es[3]

    @pl.when((pl.program_id(0) == 0) & (pl.program_id(1) == 0))
    def _():
        base_ref[...] = jnp.zeros_like(base_ref)

    tm = logits.shape[0]
    routed = pl.program_id(1) > 0
    onehot = jnp.zeros(logits.shape, F32)
    for j in range(TOP_K):
        onehot = jnp.where(lane_f == idxs[j], 1.0, onehot)
    if skip_context:
        onehot = onehot * jnp.where(routed, 1.0, 0.0)
    r_i = lax.broadcasted_iota(jnp.int32, (tm, tm), 0)
    c_i = lax.broadcasted_iota(jnp.int32, (tm, tm), 1)
    before = _dot(jnp.where(r_i > c_i, 1.0, 0.0).astype(BF16), onehot.astype(BF16))
    base = base_ref[...]
    pos = before + base
    total = base + before[tm - 1:tm, :] + onehot[tm - 1:tm, :]
    base_ref[...] = total
    cnt_ref[...] = total

    code = jnp.zeros(logits.shape, F32)
    wgt = jnp.zeros(logits.shape, F32)
    for j in range(TOP_K):
        rank = jnp.sum(jnp.where(lane_f == idxs[j], pos, 0.0), axis=-1, keepdims=True)
        code = jnp.where(lane == j, idxs[j] * float(1 << CODE_SHIFT) + rank, code)
        wgt = jnp.where(lane == TOP_K + j, es[j] / tot, wgt)
    code_i = code.astype(jnp.int32)
    if skip_context:
        code_i = jnp.where(routed, code_i, -1)
    code_ref[...] = jnp.where(lane < TOP_K, code_i, pltpu.bitcast(wgt, jnp.int32))
    half = h2.shape[1] // 2
    r = h2b.astype(F32)
    lo = lax.shift_right_logical(pltpu.bitcast(r[:, :half], U32), jnp.uint32(16))
    hi = pltpu.bitcast(r[:, half:], U32) & jnp.uint32(0xFFFF0000)
    _store_row_packed(h2p_ref, hi | lo)


def _merge_even_kernel(ogf, ogb, omf, omb, z_ref, mo_ref, gg_ref, mg_ref, wo_ref, ctx_ref, x_ref, g1_ref,
                       n2_ref, sh_ref, sc_ref, rw_ref, rb_ref, x1_ref, h2p_ref, code_ref, cnt_ref, base_ref):
    og = ogf[...].astype(F32) + ogb[...].astype(F32)
    ms = _group_sum(og * og, HD) * (1.0 / HD)
    a = og * lax.rsqrt(ms + EPS) * gg_ref[...] * _silu(z_ref[...].astype(F32))
    om = omf[...].astype(F32) + omb[...].astype(F32)
    ms = _group_sum(om * om, HD) * (1.0 / HD)
    m = om * lax.rsqrt(ms + EPS) * mg_ref[...] * _sigmoid(mo_ref[...].astype(F32))
    cat = jnp.concatenate([a, m], axis=1).astype(BF16)
    y = _dot(cat, wo_ref[...])
    x1 = _stream_tile(ctx_ref, x_ref) + g1_ref[...] * y
    x1_ref[...] = x1
    h2 = _rms_mod(x1, n2_ref[...], sh_ref[...], sc_ref[...])
    _route_and_pack(h2, rw_ref, rb_ref, h2p_ref, code_ref, cnt_ref, base_ref, skip_context=False)


def _merge_odd_kernel(of, ob, gate_ref, ng_ref, wo_ref, x_ref, g1_ref,
                      n2_ref, sh_ref, sc_ref, rw_ref, rb_ref, x1_ref, h2p_ref, code_ref, cnt_ref, base_ref):
    o = of[...].astype(F32) + ob[...].astype(F32)
    o = o - _group_sum(o, RET_DV) * (1.0 / RET_DV)
    ms = _group_sum(o * o, RET_DV) * (1.0 / RET_DV)
    y = o * lax.rsqrt(ms + EPS) * ng_ref[...] * _silu(gate_ref[...].astype(F32))
    y = _dot(y.astype(BF16), wo_ref[...])
    x1 = x_ref[...] + g1_ref[...] * y
    x1_ref[...] = x1
    h2 = _rms_mod(x1, n2_ref[...], sh_ref[...], sc_ref[...])
    _route_and_pack(h2, rw_ref, rb_ref, h2p_ref, code_ref, cnt_ref, base_ref, skip_context=True)


def _merge_out(b, t, d):
    assert d // 2 == PACK * LANES
    shapes = (jax.ShapeDtypeStruct((b, t, d), F32), jax.ShapeDtypeStruct((b, t * PACK, LANES), U32),
              jax.ShapeDtypeStruct((b, t, LANES), jnp.int32), jax.ShapeDtypeStruct((1, LANES), F32))
    tile = lambda rows, width: pl.BlockSpec((None, rows, width), lambda b_, t_: (b_, t_, 0))
    specs = (tile(TM, d), tile(TM * PACK, LANES), tile(TM, LANES), pl.BlockSpec((1, LANES), lambda b_, t_: (0, 0)))
    return shapes, specs


def _merge_even(og_f, og_b, om_f, om_b, main, gdn_g, ml_g, w_out, ctx, x, g1, n2, sh2, sc2, rw, rb):
    b, s, d = x.shape
    t = ctx.shape[1] + s
    w = GDN_H * HD
    tile = lambda width, col: pl.BlockSpec((None, TM, width), lambda b_, t_: (b_, t_, col))
    const = lambda a: pl.BlockSpec(a.shape, lambda b_, t_: (0,) * a.ndim)
    shapes, specs = _merge_out(b, t, d)
    return pl.pallas_call(
        _merge_even_kernel,
        out_shape=shapes,
        grid=(b, t // TM),
        in_specs=[tile(w, 0), tile(w, 0), tile(w, 0), tile(w, 0), tile(w, 3), tile(w, 7),
                  const(gdn_g), const(ml_g), const(w_out), *_stream_specs(d), _tile_mod_spec(d),
                  const(n2), _tile_mod_spec(d), _tile_mod_spec(d), const(rw), const(rb)],
        out_specs=specs,
        scratch_shapes=[pltpu.VMEM((1, LANES), F32)],
        compiler_params=_cparams(("arbitrary", "arbitrary")),
        name="merge_even",
    )(og_f, og_b, om_f, om_b, main, main, gdn_g, ml_g, w_out, ctx, x, g1, n2, sh2, sc2, rw, rb)


def _merge_odd(o_f, o_b, main, ret_g, w_out, x, g1, n2, sh2, sc2, rw, rb):
    b, t, d = x.shape
    vw = RET_H * RET_DV
    tile = lambda width, col: pl.BlockSpec((None, TM, width), lambda b_, t_: (b_, t_, col))
    const = lambda a: pl.BlockSpec(a.shape, lambda b_, t_: (0,) * a.ndim)
    shapes, specs = _merge_out(b, t, d)
    return pl.pallas_call(
        _merge_odd_kernel,
        out_shape=shapes,
        grid=(b, t // TM),
        in_specs=[tile(vw, 0), tile(vw, 0), tile(vw, 2), const(ret_g), const(w_out), tile(d, 0),
                  _tile_mod_spec(d), const(n2), _tile_mod_spec(d), _tile_mod_spec(d), const(rw), const(rb)],
        out_specs=specs,
        scratch_shapes=[pltpu.VMEM((1, LANES), F32)],
        compiler_params=_cparams(("arbitrary", "arbitrary")),
        name="merge_odd",
    )(o_f, o_b, main, ret_g, w_out, x, g1, n2, sh2, sc2, rw, rb)


def _proj_odd_kernel(x_ref, acc_ref, g2_ref, g_ref, sh_ref, sc_ref, w_ref, cos_ref, sin_ref,
                     x2_ref, main_ref):
    x2 = x_ref[...] + g2_ref[...] * _load_acc(acc_ref)
    x2_ref[...] = x2
    hb = _rms_mod(x2, g_ref[...], sh_ref[...], sc_ref[...]).astype(BF16)
    qk_w = RET_H * HD
    cos = cos_ref[...]
    sin = sin_ref[...]
    n_seg = w_ref.shape[1] // qk_w
    for seg in range(n_seg):
        sl = slice(seg * qk_w, (seg + 1) * qk_w)
        u = _dot(hb, w_ref[:, sl])
        if seg < 2:
            parts = []
            for hh in range(RET_H):
                uh = u[:, hh * HD:(hh + 1) * HD]
                parts.append(uh * cos + pltpu.roll(uh, HD // 2, 1) * sin)
            u = jnp.concatenate(parts, axis=1)
            if seg == 1:
                u = u * (HD ** -0.5)
        main_ref[:, sl] = u.astype(BF16)


def _rope_tables(t):
    half = HD // 2
    freqs = ROPE_BASE ** (-jnp.arange(half, dtype=F32) / half)
    ang = jnp.arange(t, dtype=F32)[:, None] * freqs[None, :]
    cos, sin = jnp.cos(ang), jnp.sin(ang)
    return jnp.concatenate([cos, cos], axis=1), jnp.concatenate([-sin, sin], axis=1)


def _proj_odd(x, acc, g2, g, sh, sc, w):
    b, t, d = x.shape
    n = w.shape[1]
    cos, sin = _rope_tables(t)
    const = lambda a: pl.BlockSpec(a.shape, lambda b_, t_: (0,) * a.ndim)
    tile = lambda width: pl.BlockSpec((None, TM, width), lambda b_, t_: (b_, t_, 0))
    rope = pl.BlockSpec((TM, HD), lambda b_, t_: (t_, 0))
    nt = t // TM
    acc_tile = _acc_tile_spec(acc, lambda b_, t_: b_ * nt + t_)
    return pl.pallas_call(
        _proj_odd_kernel,
        out_shape=(jax.ShapeDtypeStruct((b, t, d), F32), jax.ShapeDtypeStruct((b, t, n), BF16)),
        grid=(b, nt),
        in_specs=[tile(d), acc_tile, _tile_mod_spec(d), const(g), _tile_mod_spec(d), _tile_mod_spec(d),
                  const(w), rope, rope],
        out_specs=(tile(d), tile(n)),
        compiler_params=_cparams(("arbitrary", "arbitrary")),
        name="proj_odd",
    )(x, acc, g2, g, sh, sc, w, cos, sin)


ROWS_PER_STEP = 8


def _packed_row(r):
    return pl.ds(pl.multiple_of(r * PACK, PACK), PACK)


def _move_rows(n_blocks):
    return MOE_BLOCK * next(k for k in (8, 4, 2, 1) if n_blocks % k == 0)


def _moe_gather_kernel(src_ref, h_ref, xs_ref):
    def body(r8, carry):
        base = r8 * ROWS_PER_STEP
        rows = [h_ref[_packed_row(src_ref[0, base + j]), :] for j in range(ROWS_PER_STEP)]
        for j in range(ROWS_PER_STEP):
            xs_ref[_packed_row(base + j), :] = rows[j]
        return carry

    lax.fori_loop(0, src_ref.shape[1] // ROWS_PER_STEP, body, 0)


def _moe_gather(row_src, h2p, n_blocks):
    step = _move_rows(n_blocks)
    n_steps = n_blocks * MOE_BLOCK // step
    return pl.pallas_call(
        _moe_gather_kernel,
        out_shape=jax.ShapeDtypeStruct((n_blocks * MOE_BLOCK * PACK, LANES), U32),
        grid=(n_steps,),
        in_specs=[pl.BlockSpec((None, 1, step), lambda i: (i, 0, 0), memory_space=pltpu.SMEM),
                  pl.BlockSpec(h2p.shape, lambda i: (0, 0))],
        out_specs=pl.BlockSpec((step * PACK, LANES), lambda i: (i, 0)),
        compiler_params=_cparams(("arbitrary",)),
        name="moe_gather",
    )(row_src.reshape(n_steps, 1, step), h2p)


def _moe_mm_kernel(be_ref, nu_ref, xs_ref, rw_ref, wgu_ref, bgu_ref, wdn_ref, bdn_ref, ys_ref, wgu_s, wdn_s):
    i = pl.program_id(0)
    e = be_ref[i]
    half = wgu_ref.shape[0] // 2
    hl = LANES // 2

    @pl.when(i >= nu_ref[0])
    def _():
        ys_ref[...] = jnp.zeros_like(ys_ref)

    @pl.when(i < nu_ref[0])
    def _():
        @pl.when((i == 0) | (e != be_ref[jnp.maximum(i - 1, 0)]))
        def _():
            wgu_s[...] = wgu_ref[...].astype(BF16)
            for p in range(wdn_ref.shape[0] // LANES):
                first = wdn_ref[p * LANES:p * LANES + hl, :].astype(BF16).astype(F32)
                second = wdn_ref[p * LANES + hl:(p + 1) * LANES, :].astype(BF16).astype(F32)
                word = (pltpu.bitcast(second, U32) & jnp.uint32(0xFFFF0000)) | \
                    lax.shift_right_logical(pltpu.bitcast(first, U32), jnp.uint32(16))
                wdn_s[p * LANES:(p + 1) * LANES, :] = pltpu.bitcast(word, BF16)

        xu = _load_row_packed(xs_ref, MOE_BLOCK)
        lo = pltpu.bitcast(lax.shift_left(xu, jnp.uint32(16)), F32).astype(BF16)
        hi = pltpu.bitcast(xu & jnp.uint32(0xFFFF0000), F32).astype(BF16)
        gu = _dot(lo, wgu_s[:half, :]) + _dot(hi, wgu_s[half:, :]) + bgu_ref[...]
        even = (lax.broadcasted_iota(jnp.int32, (gu.shape[0], LANES), 1) & 1) == 0
        acts = []
        for p in range(gu.shape[1] // (2 * LANES)):
            a = gu[:, 2 * p * LANES:(2 * p + 1) * LANES]
            b = gu[:, (2 * p + 1) * LANES:(2 * p + 2) * LANES]
            gate = jnp.minimum(jnp.where(even, a, pltpu.roll(b, 1, 1)), SWIGLU_LIMIT)
            up = jnp.clip(jnp.where(even, pltpu.roll(a, LANES - 1, 1), b), -SWIGLU_LIMIT, SWIGLU_LIMIT)
            acts.append(((up + 1.0) * gate * _sigmoid(SWIGLU_ALPHA * gate)).astype(BF16))
        y = _dot(jnp.concatenate(acts, axis=1), wdn_s[...]) + bdn_ref[...]
        y = y * jnp.broadcast_to(rw_ref[...], (8, rw_ref.shape[1])).T[:, 0:1]
        cw = PACK * LANES
        for h in range(y.shape[1] // cw):
            _store_row_packed(ys_ref, y[:, h * cw:(h + 1) * cw], lead=(h,))


def _moe_mm(block_e, n_used, xs, row_w, layer, w_gu, b_gu, w_dn, b_dn):
    n_blocks = xs.shape[0] // (MOE_BLOCK * PACK)
    depth, n_e, d, f2 = w_gu.shape
    n_half = d // (PACK * LANES)
    ew = lambda shape: pl.BlockSpec((None, None) + shape, lambda i, be, nu: (layer, be[i], 0, 0))
    grid_spec = pltpu.PrefetchScalarGridSpec(
        num_scalar_prefetch=2,
        grid=(n_blocks,),
        in_specs=[pl.BlockSpec((MOE_BLOCK * PACK, LANES), lambda i, be, nu: (i, 0)),
                  pl.BlockSpec((None, 1, MOE_BLOCK), lambda i, be, nu: (i, 0, 0)),
                  ew((d, f2)), ew((1, f2)), ew((f2 // 2, d)), ew((1, d))],
        out_specs=pl.BlockSpec((n_half, MOE_BLOCK * PACK, LANES), lambda i, be, nu: (0, i, 0)),
        scratch_shapes=[pltpu.VMEM((d, f2), BF16), pltpu.VMEM((f2 // 2, d), BF16)],
    )
    return pl.pallas_call(
        _moe_mm_kernel,
        out_shape=jax.ShapeDtypeStruct((n_half, xs.shape[0], LANES), F32),
        grid_spec=grid_spec,
        compiler_params=_cparams(("arbitrary",)),
        name="moe_mm",
    )(block_e, n_used, xs, row_w.reshape(n_blocks, 1, MOE_BLOCK), w_gu, b_gu.reshape(depth, n_e, 1, f2),
      w_dn, b_dn.reshape(depth, n_e, 1, d))


def _moe_combine_kernel(nv_ref, dst_ref, ys_ref, acc_ref):
    i = pl.program_id(1)

    @pl.when(i == 0)
    def _():
        acc_ref[...] = jnp.zeros_like(acc_ref)

    def body(r8, carry):
        base = r8 * ROWS_PER_STEP
        dst = [_packed_row(dst_ref[0, base + j]) for j in range(ROWS_PER_STEP)]
        group = ROWS_PER_STEP * PACK
        ys = ys_ref[pl.ds(pl.multiple_of(r8 * group, group), group), :]
        new = [acc_ref[dst[j], :] + ys[j * PACK:(j + 1) * PACK, :] for j in range(ROWS_PER_STEP)]
        for j in range(ROWS_PER_STEP):
            acc_ref[dst[j], :] = new[j]
        return carry

    blocks_per_step = dst_ref.shape[1] // MOE_BLOCK
    groups_per_block = MOE_BLOCK // ROWS_PER_STEP
    for kb in range(blocks_per_step):
        groups = (nv_ref[i * blocks_per_step + kb] + ROWS_PER_STEP - 1) // ROWS_PER_STEP
        lax.fori_loop(kb * groups_per_block, kb * groups_per_block + groups, body, 0)


def _moe_combine(n_valid, row_dst, ys, n_rows_out):
    n_half, packed_rows, _ = ys.shape
    step = _move_rows(packed_rows // (MOE_BLOCK * PACK))
    n_blocks = packed_rows // (step * PACK)
    idx_spec = pl.BlockSpec((None, 1, step), lambda j, i, nu: (i, 0, 0), memory_space=pltpu.SMEM)
    grid_spec = pltpu.PrefetchScalarGridSpec(
        num_scalar_prefetch=1,
        grid=(n_half, n_blocks),
        in_specs=[idx_spec, pl.BlockSpec((None, step * PACK, LANES), lambda j, i, nu: (j, i, 0))],
        out_specs=pl.BlockSpec((None, n_rows_out * PACK, LANES), lambda j, i, nu: (j, 0, 0),
                               pipeline_mode=pl.Buffered(1)),
    )
    return pl.pallas_call(
        _moe_combine_kernel,
        out_shape=jax.ShapeDtypeStruct((n_half, n_rows_out * PACK, LANES), F32),
        grid_spec=grid_spec,
        compiler_params=_cparams(("arbitrary", "arbitrary")),
        name="moe_combine",
    )(n_valid, row_dst.reshape(n_blocks, 1, step), ys)


def _moe_inverse_kernel(pad_lo_ref, pad_hi_ref, dest_ref, inv_ref):
    def init(r, c):
        inv_ref[r] = -1
        return c

    def pad_range(k, carry):
        lax.fori_loop(pad_lo_ref[k], pad_hi_ref[k], init, 0)
        return carry

    n_ranges = pad_lo_ref.shape[0]
    lax.fori_loop(0, n_ranges - 1, pad_range, 0)

    def pad_block(blk, carry):
        lax.fori_loop(0, MOE_BLOCK, lambda r, c: init(blk * MOE_BLOCK + r, c), 0, unroll=8)
        return carry

    tail_end = pad_hi_ref[n_ranges - 1]
    lax.fori_loop(pad_lo_ref[n_ranges - 1] // MOE_BLOCK, tail_end // MOE_BLOCK, pad_block, 0)
    lax.fori_loop(tail_end // MOE_BLOCK * MOE_BLOCK, tail_end, init, 0)

    def body(a, carry):
        inv_ref[dest_ref[a]] = a
        return carry

    lax.fori_loop(0, dest_ref.shape[0], body, 0, unroll=8)


def _moe_inverse(pad_lo, pad_hi, dest, rows):
    smem = pl.BlockSpec(memory_space=pltpu.SMEM)
    return pl.pallas_call(
        _moe_inverse_kernel,
        out_shape=jax.ShapeDtypeStruct((rows,), jnp.int32),
        in_specs=[smem, smem, smem],
        out_specs=smem,
        name="moe_inverse",
    )(pad_lo, pad_hi, dest)


def _moe(h2p, codes, weights, counts, layer, w_gu, b_gu, w_dn, b_dn):
    n = h2p.shape[0] // PACK
    n_assign = n * TOP_K
    n_blocks = -(-n_assign // MOE_BLOCK) + N_EXPERTS
    rows = n_blocks * MOE_BLOCK
    padded = -(-counts // MOE_BLOCK) * MOE_BLOCK
    ends = jnp.cumsum(padded)
    offsets = ends - padded
    starts = jnp.arange(n_blocks, dtype=jnp.int32) * MOE_BLOCK
    block_e = jnp.minimum(jnp.sum(ends[None, :] <= starts[:, None], axis=1), N_EXPERTS - 1).astype(jnp.int32)
    n_used = (ends[-1:] // MOE_BLOCK).astype(jnp.int32)
    of_block = block_e[:, None] == jnp.arange(N_EXPERTS, dtype=jnp.int32)[None, :]
    data_end = jnp.sum(jnp.where(of_block, (offsets + counts)[None, :], 0), axis=1)
    n_valid = jnp.clip(data_end - starts, 0, MOE_BLOCK).astype(jnp.int32)
    expert = lax.shift_right_logical(codes, CODE_SHIFT)
    which = expert[:, None] == jnp.arange(N_EXPERTS, dtype=jnp.int32)[None, :]
    dest = (codes & ((1 << CODE_SHIFT) - 1)) + jnp.sum(jnp.where(which, offsets[None, :], 0), axis=1)
    spare = 8
    dest = jnp.where(codes >= 0, dest, rows)
    pad_lo = jnp.concatenate([offsets + counts, ends[-1:]]).astype(jnp.int32)
    pad_hi = jnp.concatenate([ends, jnp.full((1,), rows + spare, ends.dtype)]).astype(jnp.int32)
    inv = _moe_inverse(pad_lo, pad_hi, dest.astype(jnp.int32), rows + spare)[:rows]
    tok = lax.shift_right_logical(inv, TOP_K.bit_length() - 1)
    row_src = jnp.where(inv >= 0, tok, 0)
    row_dst = jnp.where(inv >= 0, tok, n)
    row_w = jnp.where(inv >= 0, jnp.take(weights, jnp.maximum(inv, 0)), 0.0)
    xs = _moe_gather(row_src, h2p, n_blocks)
    ys = _moe_mm(block_e, n_used, xs, row_w, layer, w_gu, b_gu, w_dn, b_dn)
    return _moe_combine(n_valid, row_dst, ys, n + 8)


def _final_kernel(x_ref, acc_ref, g2_ref, g_ref, o_ref):
    x = x_ref[...] + g2_ref[...] * _load_acc(acc_ref)
    ms = jnp.mean(x * x, axis=-1, keepdims=True)
    o_ref[...] = x * lax.rsqrt(ms + EPS) * g_ref[...]


def _final(x, acc, g2, g, n_ctx_tiles):
    b, t, d = x.shape
    nt = t // TM - n_ctx_tiles
    tile_in = pl.BlockSpec((None, TM, d), lambda b_, t_: (b_, t_ + n_ctx_tiles, 0))
    acc_tile = _acc_tile_spec(acc, lambda b_, t_: b_ * (t // TM) + t_ + n_ctx_tiles)
    return pl.pallas_call(
        _final_kernel,
        out_shape=jax.ShapeDtypeStruct((b, nt * TM, d), F32),
        grid=(b, nt),
        in_specs=[tile_in, acc_tile,
                  pl.BlockSpec((None, None, 1, d), lambda b_, t_: (b_, 1, 0, 0)),
                  pl.BlockSpec((1, d), lambda b_, t_: (0, 0))],
        out_specs=pl.BlockSpec((None, TM, d), lambda b_, t_: (b_, t_, 0)),
        compiler_params=_cparams(("arbitrary", "arbitrary")),
        name="final_norm",
    )(x, acc, g2, g)


def _mod_tables(mod, b, d):
    outs = []
    for j in range(6):
        m = mod[:, j * d:(j + 1) * d]
        lat = m[:b]
        ctx = jnp.broadcast_to(m[b:b + 1], (b, d))
        outs.append(jnp.stack([ctx, lat], axis=1)[:, :, None, :])
    return outs


def _routing(code):
    codes = code[:, :, :TOP_K].reshape(-1)
    weights = lax.bitcast_convert_type(code[:, :, TOP_K:2 * TOP_K], F32).reshape(-1)
    return codes, weights


def _router_params(router_w, router_b):
    d, e = router_w.shape
    rw = jnp.zeros((d, LANES), F32).at[:, :e].set(router_w).astype(BF16)
    rb = jnp.full((1, LANES), NEG, F32).at[0, :e].set(router_b)
    return rw, rb


def kernel(x, c, ctx, c_ctx, mod_w, mod_b, norm1_g, norm2_g, ev_w_in, ev_conv_w, gdn_a_log, gdn_dt_bias,
           gdn_norm_g, ml_i_bias, ml_f_bias, ml_norm_g, ev_w_out, od_w_in, ret_norm_g, od_w_out,
           router_w, router_b, moe_w_gu, moe_b_gu, moe_w_dn, moe_b_dn, final_g):
    b, s, d = x.shape
    n_ctx = ctx.shape[1]
    depth = mod_w.shape[0]
    assert n_ctx == TM and s % TM == 0 and depth == 2 and b % SCAN_BG == 0
    assert ev_conv_w.shape[1] == CONV_W
    t = n_ctx + s
    n_tok = b * t

    cond = jnp.concatenate([c, c_ctx[None, :], jnp.zeros((8 - b - 1, d), F32)], axis=0)
    mod = _adaln(cond, mod_w, mod_b)

    sh1, sc1, g1, sh2, sc2, g2 = _mod_tables(mod[0], b, d)
    qk_w = GDN_H * HD
    conv_ch = 3 * qk_w
    ng = N_DIR * GDN_H
    w_in = ev_w_in[0]
    o_z = conv_ch
    o_a = o_z + qk_w
    o_mq = o_a + 2 * ng
    o_i = o_mq + 4 * qk_w
    w_main = jnp.concatenate([w_in[:, :o_a], w_in[:, o_mq:o_i]], axis=1).astype(BF16)
    w_gate = jnp.concatenate([w_in[:, o_a:o_mq], w_in[:, o_i:o_i + 2 * ng],
                              jnp.zeros((d, LANES - 4 * ng), F32)], axis=1).astype(BF16)
    zeros_g = jnp.zeros((ng,), F32)
    rate = jnp.concatenate([jnp.exp(gdn_a_log[0].astype(F32)).reshape(-1), jnp.zeros((LANES - ng,), F32)])[None, :]
    gbias = jnp.concatenate([gdn_dt_bias[0].reshape(-1), zeros_g, ml_i_bias[0].reshape(-1),
                             ml_f_bias[0].reshape(-1), jnp.zeros((LANES - 4 * ng,), F32)])[None, :].astype(F32)
    main, gates = _proj_even(ctx, x, norm1_g[0][None, :], sh1, sc1, w_main, w_gate, ev_conv_w[0], rate, gbias)
    gates_c = gates.reshape(b, t // CH, CH, 4 * ng)
    gates_r = jnp.swapaxes(gates_c, 2, 3)
    og_f, og_b, om_f, om_b = _even_scan(main, gates_c, gates_r)
    rw, rb = _router_params(router_w[0], router_b[0])
    gdn_g = jnp.tile(gdn_norm_g[0], GDN_H)[None, :]
    x1, h2p, code, cnt = _merge_even(og_f, og_b, om_f, om_b, main, gdn_g, ml_norm_g[0][None, :],
                                     ev_w_out[0].astype(BF16), ctx, x, g1, norm2_g[0][None, :], sh2, sc2, rw, rb)
    acc = _moe(h2p.reshape(n_tok * PACK, LANES), *_routing(code), cnt[0, :N_EXPERTS].astype(jnp.int32),
               0, moe_w_gu, moe_b_gu, moe_w_dn, moe_b_dn)
    g2_prev = g2

    sh1, sc1, g1, sh2, sc2, g2 = _mod_tables(mod[1], b, d)
    x2, main_o = _proj_odd(x1, acc, g2_prev, norm1_g[1][None, :], sh1, sc1, od_w_in[0].astype(BF16))
    o_f, o_b = _ret_scan(main_o)
    rw, rb = _router_params(router_w[1], router_b[1])
    x3, h2p, code, cnt = _merge_odd(o_f, o_b, main_o, ret_norm_g[0][None, :], od_w_out[0].astype(BF16),
                                    x2, g1, norm2_g[1][None, :], sh2, sc2, rw, rb)
    acc = _moe(h2p.reshape(n_tok * PACK, LANES), *_routing(code), cnt[0, :N_EXPERTS].astype(jnp.int32),
               1, moe_w_gu, moe_b_gu, moe_w_dn, moe_b_dn)
    return _final(x3, acc, g2, final_g[None, :], n_ctx // TM)
```
